```python
import math
import jax, jax.numpy as jnp
from jax import lax
import numpy as np

D_MODEL = 2048
BATCH = 4
SEQ = 4096
DEPTH = 1

HEAD_DIM = 64
N_Q_HEADS = 16
N_KV_HEADS = 4
Q_PER_KV = N_Q_HEADS // N_KV_HEADS
ATTN_WIDTH = N_Q_HEADS * HEAD_DIM
KV_WIDTH = N_KV_HEADS * HEAD_DIM
WINDOW = 128
BLOCK = 128
SSM_WIDTH = D_MODEL - ATTN_WIDTH
SSM_GROUP = 16
SSM_GROUPS = SSM_WIDTH // SSM_GROUP
SSM_STATE = 64
MIX_WIDTH = ATTN_WIDTH + SSM_WIDTH
IN_WIDTH = ATTN_WIDTH + 2 * KV_WIDTH + SSM_WIDTH
FF_HIDDEN = -(-8 * D_MODEL // (3 * 256)) * 256
REL_BUCKETS = 32
REL_MAX_DISTANCE = 128
EPS = 1e-6

kernel_name = "hymba_s5_swa_sink_hybrid"


def _rmsnorm(x, g):
    xf = x.astype(jnp.float32)
    y = xf * lax.rsqrt(jnp.mean(xf * xf, axis=-1, keepdims=True) + EPS)
    return (y * g.astype(jnp.float32)).astype(x.dtype)


def _t5_bucket(dist):
    n = np.maximum(dist, 0)
    max_exact = REL_BUCKETS // 2
    nf = np.maximum(n, 1).astype(np.float32)
    large = max_exact + (np.log(nf / max_exact) / math.log(REL_MAX_DISTANCE / max_exact)
                         * (REL_BUCKETS - max_exact)).astype(np.int32)
    large = np.minimum(large, REL_BUCKETS - 1)
    return np.where(n < max_exact, n, large).astype(np.int32)


def _sliding_window_attention(q, k, v, sinks, rel_bias):
    bsz, L = q.shape[0], q.shape[1]
    nb = L // BLOCK
    qb = q.reshape(bsz, nb, BLOCK, N_KV_HEADS, Q_PER_KV, HEAD_DIM)
    pad = ((0, 0), (BLOCK, 0), (0, 0), (0, 0))
    kp = jnp.pad(k, pad).reshape(bsz, nb + 1, BLOCK, N_KV_HEADS, HEAD_DIM)
    vp = jnp.pad(v, pad).reshape(bsz, nb + 1, BLOCK, N_KV_HEADS, HEAD_DIM)
    kb = jnp.concatenate([kp[:, :-1], kp[:, 1:]], axis=2)
    vb = jnp.concatenate([vp[:, :-1], vp[:, 1:]], axis=2)
    logits = jnp.einsum('bnqkgd,bnskd->bnkgqs', qb, kb).astype(jnp.float32) * (HEAD_DIM ** -0.5)

    qi = np.arange(BLOCK)[:, None]
    sj = np.arange(2 * BLOCK)[None, :]
    dist = qi + BLOCK - sj
    bucket = _t5_bucket(dist)
    bias = jnp.transpose(rel_bias[bucket].astype(jnp.float32), (2, 0, 1))
    bias = bias.reshape(N_KV_HEADS, Q_PER_KV, BLOCK, 2 * BLOCK)
    in_window = (dist >= 0) & (dist < WINDOW)
    key_pos = np.arange(nb)[:, None] * BLOCK - BLOCK + np.arange(2 * BLOCK)[None, :]
    valid = in_window[None] & (key_pos >= 0)[:, None, :]
    logits = jnp.where(valid[None, :, None, None], logits + bias, -jnp.inf)

    sink = sinks.astype(jnp.float32).reshape(N_KV_HEADS, Q_PER_KV, 1, 1)
    m = jnp.maximum(jnp.max(logits, axis=-1, keepdims=True), sink)
    p = jnp.exp(logits - m)
    w = p / (jnp.sum(p, axis=-1, keepdims=True) + jnp.exp(sink - m))
    out = jnp.einsum('bnkgqs,bnskd->bnqkgd', w.astype(v.dtype), vb)
    return out.reshape(bsz, L, ATTN_WIDTH)


def _scan_combine(a, b):
    a_re, a_im, x_re, x_im = a
    b_re, b_im, y_re, y_im = b
    n_re = b_re * a_re - b_im * a_im
    n_im = b_re * a_im + b_im * a_re
    o_re = b_re * x_re - b_im * x_im + y_re
    o_im = b_re * x_im + b_im * x_re + y_im
    return (n_re, n_im, o_re, o_im)


def _s5_mixer(u, a_re, a_im, log_dt, b_re, b_im, c_re, c_im, d, w_glu):
    bsz, L = u.shape[0], u.shape[1]
    f32 = jnp.float32
    uf = u.reshape(bsz, L, SSM_GROUPS, SSM_GROUP).astype(f32)
    a_re = a_re.astype(f32)
    a_im = a_im.astype(f32)
    dt = jnp.exp(log_dt.astype(f32))[:, None]
    mag = jnp.exp(a_re * dt)
    ang = a_im * dt
    lb_re, lb_im = mag * jnp.cos(ang), mag * jnp.sin(ang)
    nr, ni = lb_re - 1.0, lb_im
    den = a_re * a_re + a_im * a_im
    f_re = (nr * a_re + ni * a_im) / den
    f_im = (ni * a_re - nr * a_im) / den
    b_re = b_re.astype(f32)
    b_im = b_im.astype(f32)
    bb_re = f_re[..., None] * b_re - f_im[..., None] * b_im
    bb_im = f_re[..., None] * b_im + f_im[..., None] * b_re
    bu_re = jnp.einsum('blgp,gnp->blgn', uf, bb_re)
    bu_im = jnp.einsum('blgp,gnp->blgn', uf, bb_im)
    shape_a = (1, L, SSM_GROUPS, SSM_STATE)
    elems = (jnp.broadcast_to(lb_re, shape_a), jnp.broadcast_to(lb_im, shape_a), bu_re, bu_im)
    _, _, h_re, h_im = lax.associative_scan(_scan_combine, elems, axis=1)
    y = (jnp.einsum('blgn,gpn->blgp', h_re, c_re.astype(f32))
         - jnp.einsum('blgn,gpn->blgp', h_im, c_im.astype(f32))
         + d.astype(f32) * uf)
    y = jax.nn.gelu(y.reshape(bsz, L, SSM_WIDTH)).astype(u.dtype)
    return y * jax.nn.sigmoid(y @ w_glu)


def _layer(x, rel_bias, ln1_g, w_in, q_norm_g, k_norm_g, attn_sinks, ssm_a_re, ssm_a_im,
           ssm_log_dt, ssm_b_re, ssm_b_im, ssm_c_re, ssm_c_im, ssm_d, w_glu,
           attn_out_g, ssm_out_g, w_out, ln2_g, w_ff_gate, w_ff_up, w_ff_down):
    bsz, L = x.shape[0], x.shape[1]
    h = _rmsnorm(x, ln1_g)
    proj = h @ w_in
    q, k, v, u = jnp.split(proj, [ATTN_WIDTH, ATTN_WIDTH + KV_WIDTH, ATTN_WIDTH + 2 * KV_WIDTH], axis=-1)
    q = _rmsnorm(q.reshape(bsz, L, N_Q_HEADS, HEAD_DIM), q_norm_g)
    k = _rmsnorm(k.reshape(bsz, L, N_KV_HEADS, HEAD_DIM), k_norm_g)
    v = v.reshape(bsz, L, N_KV_HEADS, HEAD_DIM)
    y_attn = _sliding_window_attention(q, k, v, attn_sinks, rel_bias)
    y_ssm = _s5_mixer(u, ssm_a_re, ssm_a_im, ssm_log_dt, ssm_b_re, ssm_b_im,
                      ssm_c_re, ssm_c_im, ssm_d, w_glu)
    mixed = jnp.concatenate([_rmsnorm(y_attn, attn_out_g), _rmsnorm(y_ssm, ssm_out_g)], axis=-1)
    x = x + mixed @ w_out
    h2 = _rmsnorm(x, ln2_g)
    ff = (jax.nn.silu(h2 @ w_ff_gate) * (h2 @ w_ff_up)) @ w_ff_down
    return x + ff


def setup_inputs(seed: int = 0) -> dict:
    key = jax.random.key(seed)
    ks = jax.random.split(key, 24)
    f32 = jnp.float32
    nrm = lambda k, s, sc: jax.random.normal(k, s, f32) * sc
    Dp = DEPTH
    n_idx = jnp.arange(SSM_STATE, dtype=f32)
    return {
        "x": nrm(ks[0], (BATCH, SEQ, D_MODEL), 1.0),
        "rel_bias": nrm(ks[1], (REL_BUCKETS, N_Q_HEADS), 0.5),
        "ln1_g": 1.0 + nrm(ks[2], (Dp, D_MODEL), 0.02),
        "w_in": nrm(ks[3], (Dp, D_MODEL, IN_WIDTH), D_MODEL ** -0.5),
        "q_norm_g": 1.0 + nrm(ks[4], (Dp, HEAD_DIM), 0.02),
        "k_norm_g": 1.0 + nrm(ks[5], (Dp, HEAD_DIM), 0.02),
        "attn_sinks": nrm(ks[6], (Dp, N_Q_HEADS), 0.5),
        "ssm_a_re": -0.5 + nrm(ks[7], (Dp, SSM_GROUPS, SSM_STATE), 0.01),
        "ssm_a_im": math.pi * n_idx + nrm(ks[8], (Dp, SSM_GROUPS, SSM_STATE), 0.01),
        "ssm_log_dt": jax.random.uniform(ks[9], (Dp, SSM_GROUPS), f32, math.log(1e-3), math.log(1e-1)),
        "ssm_b_re": nrm(ks[10], (Dp, SSM_GROUPS, SSM_STATE, SSM_GROUP), (2 * SSM_GROUP) ** -0.5),
        "ssm_b_im": nrm(ks[11], (Dp, SSM_GROUPS, SSM_STATE, SSM_GROUP), (2 * SSM_GROUP) ** -0.5),
        "ssm_c_re": nrm(ks[12], (Dp, SSM_GROUPS, SSM_GROUP, SSM_STATE), (2 * SSM_STATE) ** -0.5),
        "ssm_c_im": nrm(ks[13], (Dp, SSM_GROUPS, SSM_GROUP, SSM_STATE), (2 * SSM_STATE) ** -0.5),
        "ssm_d": nrm(ks[14], (Dp, SSM_GROUPS, SSM_GROUP), 1.0),
        "w_glu": nrm(ks[15], (Dp, SSM_WIDTH, SSM_WIDTH), SSM_WIDTH ** -0.5),
        "attn_out_g": 1.0 + nrm(ks[16], (Dp, ATTN_WIDTH), 0.02),
        "ssm_out_g": 1.0 + nrm(ks[17], (Dp, SSM_WIDTH), 0.02),
        "w_out": nrm(ks[18], (Dp, MIX_WIDTH, D_MODEL), MIX_WIDTH ** -0.5),
        "ln2_g": 1.0 + nrm(ks[19], (Dp, D_MODEL), 0.02),
        "w_ff_gate": nrm(ks[20], (Dp, D_MODEL, FF_HIDDEN), D_MODEL ** -0.5),
        "w_ff_up": nrm(ks[21], (Dp, D_MODEL, FF_HIDDEN), D_MODEL ** -0.5),
        "w_ff_down": nrm(ks[22], (Dp, FF_HIDDEN, D_MODEL), FF_HIDDEN ** -0.5),
    }


def reference(x, rel_bias, ln1_g, w_in, q_norm_g, k_norm_g, attn_sinks, ssm_a_re, ssm_a_im,
              ssm_log_dt, ssm_b_re, ssm_b_im, ssm_c_re, ssm_c_im, ssm_d, w_glu,
              attn_out_g, ssm_out_g, w_out, ln2_g, w_ff_gate, w_ff_up, w_ff_down):
    for l in range(DEPTH):
        x = _layer(x, rel_bias, ln1_g[l], w_in[l], q_norm_g[l], k_norm_g[l], attn_sinks[l],
                   ssm_a_re[l], ssm_a_im[l], ssm_log_dt[l], ssm_b_re[l], ssm_b_im[l],
                   ssm_c_re[l], ssm_c_im[l], ssm_d[l], w_glu[l], attn_out_g[l], ssm_out_g[l],
                   w_out[l], ln2_g[l], w_ff_gate[l], w_ff_up[l], w_ff_down[l])
    return x
```

```python
import functools
import math

import jax
import jax.numpy as jnp
import numpy as np
from jax import lax
from jax.experimental import pallas as pl
from jax.experimental.pallas import tpu as pltpu

D_MODEL = 2048
BATCH = 4
SEQ = 4096
HEAD_DIM = 64
N_Q_HEADS = 16
N_KV_HEADS = 4
Q_PER_KV = N_Q_HEADS // N_KV_HEADS
ATTN_WIDTH = N_Q_HEADS * HEAD_DIM
KV_WIDTH = N_KV_HEADS * HEAD_DIM
WINDOW = 128
BLOCK = 128
SSM_WIDTH = D_MODEL - ATTN_WIDTH
SSM_GROUP = 16
SSM_GROUPS = SSM_WIDTH // SSM_GROUP
SSM_STATE = 64
FF_HIDDEN = 5632
REL_BUCKETS = 32
REL_MAX_DISTANCE = 128
EPS = 1e-6

V7X_LANES = 128
V7X_SUBLANES = 8
V7X_VMEM_BYTES = 64 * 1024 * 1024
VMEM_LIMIT = V7X_VMEM_BYTES - 8 * 1024 * 1024

MASK_VALUE = -1e30
KV_SLAB = Q_PER_KV * HEAD_DIM
LANE_CHUNKS = SSM_WIDTH // V7X_LANES
PAIRS_PER_CHUNK = 4
CHUNK_COLS = PAIRS_PER_CHUNK * 2 * V7X_LANES

IN_TM = 512
FFN_TM = 512
FFN_TF = 512
SSM_T = 64
SSM_ROWS = SSM_T * BATCH

f32 = jnp.float32
bf16 = jnp.bfloat16


def _dot(a, b):
    return jnp.dot(a, b, preferred_element_type=f32)


def _rms_scale(x):
    return lax.rsqrt(jnp.mean(x * x, axis=-1, keepdims=True) + EPS)


def _in_proj_kernel(x_ref, g_ref, wq_ref, wk_ref, wv_ref, wu_ref, seg_ref, rep_ref, gq_ref, gk_ref,
                    q_ref, k_ref, v_ref, u_ref):
    xf = x_ref[...]
    h = (xf * _rms_scale(xf) * g_ref[...]).astype(bf16)
    seg = seg_ref[...]

    def head_norm(p, gain):
        sq = p * p
        hi = sq.astype(bf16)
        lo = (sq - hi.astype(f32)).astype(bf16)
        ss = _dot(hi, seg) + _dot(lo, seg)
        return p * lax.rsqrt(ss * (1.0 / HEAD_DIM) + EPS) * gain

    for c in range(ATTN_WIDTH // KV_SLAB):
        cols = slice(c * KV_SLAB, (c + 1) * KV_SLAB)
        qc = _dot(h, wq_ref[:, cols])
        q_ref[:, cols] = head_norm(qc, gq_ref[:, cols]).astype(bf16)
    kc = head_norm(_dot(h, wk_ref[...]), gk_ref[...]).astype(bf16)
    k_ref[...] = _dot(kc, rep_ref[...]).astype(bf16)
    vc = _dot(h, wv_ref[...]).astype(bf16)
    v_ref[...] = _dot(vc, rep_ref[...]).astype(bf16)
    u_ref[...] = _dot(h, wu_ref[...]).astype(bf16)


def _const_spec(shape):
    nd = len(shape)
    return pl.BlockSpec(shape, lambda *_: (0,) * nd)


def _in_proj(x, ln1_g, wq, wk, wv, wu, seg, rep, gq, gk):
    nt = SEQ // IN_TM
    row_spec = lambda w: pl.BlockSpec((None, IN_TM, w), lambda b, t: (b, t, 0))
    return pl.pallas_call(
        _in_proj_kernel,
        grid=(BATCH, nt),
        in_specs=[
            row_spec(D_MODEL),
            _const_spec((1, D_MODEL)),
            _const_spec((D_MODEL, ATTN_WIDTH)),
            _const_spec((D_MODEL, KV_WIDTH)),
            _const_spec((D_MODEL, KV_WIDTH)),
            _const_spec((D_MODEL, SSM_WIDTH)),
            _const_spec((KV_SLAB, KV_SLAB)),
            _const_spec((KV_WIDTH, N_KV_HEADS * KV_SLAB)),
            _const_spec((1, ATTN_WIDTH)),
            _const_spec((1, KV_WIDTH)),
        ],
        out_specs=[
            row_spec(ATTN_WIDTH),
            row_spec(N_KV_HEADS * KV_SLAB),
            row_spec(N_KV_HEADS * KV_SLAB),
            pl.BlockSpec((IN_TM, SSM_WIDTH), lambda b, t: (t, b)),
        ],
        out_shape=[
            jax.ShapeDtypeStruct((BATCH, SEQ, ATTN_WIDTH), bf16),
            jax.ShapeDtypeStruct((BATCH, SEQ, N_KV_HEADS * KV_SLAB), bf16),
            jax.ShapeDtypeStruct((BATCH, SEQ, N_KV_HEADS * KV_SLAB), bf16),
            jax.ShapeDtypeStruct((SEQ, BATCH * SSM_WIDTH), bf16),
        ],
        compiler_params=pltpu.CompilerParams(
            dimension_semantics=("arbitrary", "arbitrary"), vmem_limit_bytes=VMEM_LIMIT),
        name="in_proj",
    )(x, ln1_g, wq, wk, wv, wu, seg, rep, gq, gk)


def _attn_kernel(sink_ref, q_ref, kc_ref, kp_ref, vc_ref, vp_ref, bias_ref, qmask_ref, g_ref,
                 o_ref, y_scr):
    lane_head = lax.broadcasted_iota(jnp.int32, (BLOCK, KV_SLAB), 1) // HEAD_DIM
    for kh in range(N_KV_HEADS):
        cols = slice(kh * KV_SLAB, (kh + 1) * KV_SLAB)
        qs = q_ref[:, cols]
        q4 = jnp.concatenate([qs * qmask_ref[g] for g in range(Q_PER_KV)], axis=0)
        keys = jnp.concatenate([kp_ref[:, cols], kc_ref[:, cols]], axis=0)
        vals = jnp.concatenate([vp_ref[:, cols], vc_ref[:, cols]], axis=0)
        s4 = lax.dot_general(q4, keys, (((1,), (1,)), ((), ())), preferred_element_type=f32)
        probs, inv_den = [], []
        for g in range(Q_PER_KV):
            head = kh * Q_PER_KV + g
            s = s4[g * BLOCK:(g + 1) * BLOCK] + bias_ref[0, head]
            sink = sink_ref[head]
            m = jnp.maximum(jnp.max(s, axis=-1, keepdims=True), sink)
            p = jnp.exp(s - m)
            den = jnp.sum(p, axis=-1, keepdims=True) + jnp.exp(sink - m)
            probs.append(p.astype(bf16))
            inv_den.append(1.0 / den)
        o4 = _dot(jnp.concatenate(probs, axis=0), vals)
        acc = jnp.zeros((BLOCK, KV_SLAB), f32)
        for g in range(Q_PER_KV):
            acc = acc + jnp.where(lane_head == g, o4[g * BLOCK:(g + 1) * BLOCK] * inv_den[g], 0.0)
        y_scr[:, cols] = acc
    y = y_scr[...]
    o_ref[...] = (y * _rms_scale(y) * g_ref[...]).astype(bf16)


def _attention(sinks, q, k_rep, v_rep, bias2, qmask, attn_out_g):
    nb = SEQ // BLOCK
    w = N_KV_HEADS * KV_SLAB
    cur = lambda b, n: (b, n, 0)
    prev = lambda b, n: (b, jnp.maximum(n - 1, 0), 0)
    blk = lambda im: pl.BlockSpec((None, BLOCK, w), im)
    return pl.pallas_call(
        _attn_kernel,
        grid=(BATCH, nb),
        in_specs=[
            pl.BlockSpec(memory_space=pltpu.SMEM),
            blk(cur), blk(cur), blk(prev), blk(cur), blk(prev),
            pl.BlockSpec((1, N_Q_HEADS, BLOCK, 2 * BLOCK), lambda b, n: (jnp.minimum(n, 1), 0, 0, 0)),
            _const_spec((Q_PER_KV, BLOCK, KV_SLAB)),
            _const_spec((1, ATTN_WIDTH)),
        ],
        out_specs=blk(cur),
        out_shape=jax.ShapeDtypeStruct((BATCH, SEQ, ATTN_WIDTH), bf16),
        scratch_shapes=[pltpu.VMEM((BLOCK, ATTN_WIDTH), f32)],
        compiler_params=pltpu.CompilerParams(
            dimension_semantics=("arbitrary", "arbitrary"), vmem_limit_bytes=VMEM_LIMIT),
        name="attn",
    )(sinks, q, k_rep, k_rep, v_rep, v_rep, bias2, qmask, attn_out_g)


def _gelu_tanh(x):
    return 0.5 * x * (1.0 + jnp.tanh(math.sqrt(2.0 / math.pi) * (x + 0.044715 * (x * x * x))))


def _sigmoid(x):
    return 1.0 / (1.0 + jnp.exp(-x))


def _ssm_kernel(u_ref, wb_ref, lam1_ref, lam2_ref, wc_ref, d_ref, wglu_ref, g_ref,
                o_ref, bu_scr, y_scr, st_scr, prev_scr):
    @pl.when(pl.program_id(0) == 0)
    def _():
        st_scr[...] = jnp.zeros_like(st_scr)
        prev_scr[...] = jnp.zeros_like(prev_scr)

    for c in range(LANE_CHUNKS):
        bu_scr[c] = _dot(u_ref[:, c * V7X_LANES:(c + 1) * V7X_LANES], wb_ref[c])

    first_step = lax.broadcasted_iota(jnp.int32, (V7X_SUBLANES, V7X_LANES), 0) < BATCH
    n_tiles = SSM_ROWS // V7X_SUBLANES
    for c in range(LANE_CHUNKS):
        for q0 in range(0, PAIRS_PER_CHUNK, 2):
            sls = []
            for q in (q0, q0 + 1):
                base = q * 2 * V7X_LANES
                sls.append((slice(base, base + V7X_LANES), slice(base + V7X_LANES, base + 2 * V7X_LANES)))
            lams = [(lam1_ref[c, :, cr], lam1_ref[c, :, ci], lam2_ref[c, :, cr], lam2_ref[c, :, ci])
                    for cr, ci in sls]
            init = tuple((st_scr[c, :, cr], st_scr[c, :, ci], prev_scr[c, :, cr], prev_scr[c, :, ci])
                         for cr, ci in sls)

            def body(v, carry, c=c, sls=sls, lams=lams):
                rows = pl.ds(pl.multiple_of(v * V7X_SUBLANES, V7X_SUBLANES), V7X_SUBLANES)
                out = []
                for (cr, ci), (l1r, l1i, l2r, l2i), (s_re, s_im, pr_re, pr_im) in zip(sls, lams, carry):
                    a_re = bu_scr[c, rows, cr]
                    a_im = bu_scr[c, rows, ci]
                    r_re = pltpu.roll(a_re, BATCH, 0)
                    r_im = pltpu.roll(a_im, BATCH, 0)
                    p_re = jnp.where(first_step, pr_re, r_re)
                    p_im = jnp.where(first_step, pr_im, r_im)
                    x_re = a_re + (l1r * p_re - l1i * p_im)
                    x_im = a_im + (l1r * p_im + l1i * p_re)
                    n_re = (l2r * s_re - l2i * s_im) + x_re
                    n_im = (l2r * s_im + l2i * s_re) + x_im
                    bu_scr[c, rows, cr] = n_re
                    bu_scr[c, rows, ci] = n_im
                    out.append((n_re, n_im, r_re, r_im))
                return tuple(out)

            fin = lax.fori_loop(0, n_tiles, body, init, unroll=4)
            for (cr, ci), (s_re, s_im, pr_re, pr_im) in zip(sls, fin):
                st_scr[c, :, cr] = s_re
                st_scr[c, :, ci] = s_im
                prev_scr[c, :, cr] = pr_re
                prev_scr[c, :, ci] = pr_im

    for c in range(LANE_CHUNKS):
        cols = slice(c * V7X_LANES, (c + 1) * V7X_LANES)
        yc = _dot(bu_scr[c].astype(bf16), wc_ref[c])
        yc = yc + d_ref[:, cols] * u_ref[:, cols].astype(f32)
        y_scr[:, cols] = _gelu_tanh(yc)
    y = y_scr[...]
    out = y * _sigmoid(_dot(y.astype(bf16), wglu_ref[...]))
    o_ref[...] = (out * _rms_scale(out) * g_ref[...]).astype(bf16)


def _ssm(u_tb, wb, lam1, lam2, wc, d, wglu, ssm_out_g):
    n_rows = SEQ * BATCH
    return pl.pallas_call(
        _ssm_kernel,
        grid=(n_rows // SSM_ROWS,),
        in_specs=[
            pl.BlockSpec((SSM_ROWS, SSM_WIDTH), lambda i: (i, 0)),
            _const_spec((LANE_CHUNKS, V7X_LANES, CHUNK_COLS)),
            _const_spec((LANE_CHUNKS, V7X_SUBLANES, CHUNK_COLS)),
            _const_spec((LANE_CHUNKS, V7X_SUBLANES, CHUNK_COLS)),
            _const_spec((LANE_CHUNKS, CHUNK_COLS, V7X_LANES)),
            _const_spec((1, SSM_WIDTH)),
            _const_spec((SSM_WIDTH, SSM_WIDTH)),
            _const_spec((1, SSM_WIDTH)),
        ],
        out_specs=pl.BlockSpec((SSM_ROWS, SSM_WIDTH), lambda i: (i, 0)),
        out_shape=jax.ShapeDtypeStruct((n_rows, SSM_WIDTH), bf16),
        scratch_shapes=[
            pltpu.VMEM((LANE_CHUNKS, SSM_ROWS, CHUNK_COLS), f32),
            pltpu.VMEM((SSM_ROWS, SSM_WIDTH), f32),
            pltpu.VMEM((LANE_CHUNKS, V7X_SUBLANES, CHUNK_COLS), f32),
            pltpu.VMEM((LANE_CHUNKS, V7X_SUBLANES, CHUNK_COLS), f32),
        ],
        compiler_params=pltpu.CompilerParams(
            dimension_semantics=("arbitrary",), vmem_limit_bytes=VMEM_LIMIT),
        name="ssm",
    )(u_tb, wb, lam1, lam2, wc, d, wglu, ssm_out_g)


def _out_proj_kernel(a_ref, s_ref, x_ref, wa_ref, ws_ref, o_ref):
    o_ref[...] = x_ref[...] + _dot(a_ref[...], wa_ref[...]) + _dot(s_ref[...], ws_ref[...])


def _out_proj(a, s_tb, x, wa, ws):
    nt = SEQ // IN_TM
    return pl.pallas_call(
        _out_proj_kernel,
        grid=(BATCH, nt),
        in_specs=[
            pl.BlockSpec((None, IN_TM, ATTN_WIDTH), lambda b, t: (b, t, 0)),
            pl.BlockSpec((IN_TM, SSM_WIDTH), lambda b, t: (t, b)),
            pl.BlockSpec((None, IN_TM, D_MODEL), lambda b, t: (b, t, 0)),
            _const_spec((ATTN_WIDTH, D_MODEL)),
            _const_spec((SSM_WIDTH, D_MODEL)),
        ],
        out_specs=pl.BlockSpec((None, IN_TM, D_MODEL), lambda b, t: (b, t, 0)),
        out_shape=jax.ShapeDtypeStruct((BATCH, SEQ, D_MODEL), f32),
        compiler_params=pltpu.CompilerParams(
            dimension_semantics=("arbitrary", "arbitrary"), vmem_limit_bytes=VMEM_LIMIT),
        name="out_proj",
    )(a, s_tb, x, wa, ws)


def _ffn_kernel(x_ref, g_ref, wg_ref, wu_ref, wd_ref, o_ref, h_scr):
    @pl.when(pl.program_id(1) == 0)
    def _():
        xf = x_ref[...]
        h_scr[...] = (xf * _rms_scale(xf) * g_ref[...]).astype(bf16)
        o_ref[...] = xf

    h = h_scr[...]
    gate = _dot(h, wg_ref[...])
    up = _dot(h, wu_ref[...])
    act = (gate * _sigmoid(gate) * up).astype(bf16)
    o_ref[...] += _dot(act, wd_ref[...])


def _ffn(x, ln2_g, wg, wu, wd):
    n_rows = BATCH * SEQ
    return pl.pallas_call(
        _ffn_kernel,
        grid=(n_rows // FFN_TM, FF_HIDDEN // FFN_TF),
        in_specs=[
            pl.BlockSpec((FFN_TM, D_MODEL), lambda i, f: (i, 0)),
            _const_spec((1, D_MODEL)),
            pl.BlockSpec((D_MODEL, FFN_TF), lambda i, f: (0, f)),
            pl.BlockSpec((D_MODEL, FFN_TF), lambda i, f: (0, f)),
            pl.BlockSpec((FFN_TF, D_MODEL), lambda i, f: (f, 0)),
        ],
        out_specs=pl.BlockSpec((FFN_TM, D_MODEL), lambda i, f: (i, 0)),
        out_shape=jax.ShapeDtypeStruct((n_rows, D_MODEL), f32),
        scratch_shapes=[pltpu.VMEM((FFN_TM, D_MODEL), bf16)],
        compiler_params=pltpu.CompilerParams(
            dimension_semantics=("arbitrary", "arbitrary"), vmem_limit_bytes=VMEM_LIMIT),
        name="ffn",
    )(x, ln2_g, wg, wu, wd)


def _t5_bucket(dist):
    n = np.maximum(dist, 0)
    max_exact = REL_BUCKETS // 2
    nf = np.maximum(n, 1).astype(np.float32)
    large = max_exact + (np.log(nf / max_exact) / math.log(REL_MAX_DISTANCE / max_exact)
                         * (REL_BUCKETS - max_exact)).astype(np.int32)
    large = np.minimum(large, REL_BUCKETS - 1)
    return np.where(n < max_exact, n, large).astype(np.int32)


def _bias_tables(rel_bias):
    qi = np.arange(BLOCK)[:, None]
    sj = np.arange(2 * BLOCK)[None, :]
    dist = qi + BLOCK - sj
    in_window = (dist >= 0) & (dist < WINDOW)
    bias = jnp.transpose(rel_bias[_t5_bucket(dist)].astype(f32), (2, 0, 1))
    normal = jnp.where(in_window[None], bias, MASK_VALUE)
    first = jnp.where((in_window & (sj >= BLOCK))[None], bias, MASK_VALUE)
    return jnp.stack([first, normal])


def _ssm_params(a_re, a_im, log_dt, b_re, b_im, c_re, c_im):
    dt = jnp.exp(log_dt)[:, None]
    mag = jnp.exp(a_re * dt)
    ang = a_im * dt
    lb_re, lb_im = mag * jnp.cos(ang), mag * jnp.sin(ang)
    nr, ni = lb_re - 1.0, lb_im
    den = a_re * a_re + a_im * a_im
    f_re = (nr * a_re + ni * a_im) / den
    f_im = (ni * a_re - nr * a_im) / den
    bb_re = f_re[..., None] * b_re - f_im[..., None] * b_im
    bb_im = f_re[..., None] * b_im + f_im[..., None] * b_re

    sel = np.zeros((8, PAIRS_PER_CHUNK, 2), np.float32)
    for q in range(PAIRS_PER_CHUNK):
        for gl in range(2):
            sel[2 * q + gl, q, gl] = 1.0
    split = lambda t: t.reshape((t.shape[0], LANE_CHUNKS, PAIRS_PER_CHUNK, 2) + t.shape[2:])
    bbs = split(jnp.stack([bb_re, bb_im]))
    wb = jnp.einsum('jqg,acqgnp->cjpqagn', sel, bbs).reshape(LANE_CHUNKS, V7X_LANES, CHUNK_COLS)
    cs = split(jnp.stack([c_re, -c_im]))
    wc = jnp.einsum('jqg,acqgpn->cqagnjp', sel, cs).reshape(LANE_CHUNKS, CHUNK_COLS, V7X_LANES)

    def lam_table(re, im):
        t = split(jnp.stack([re, im]))
        t = jnp.transpose(t, (1, 2, 0, 3, 4)).reshape(LANE_CHUNKS, 1, CHUNK_COLS)
        return jnp.broadcast_to(t, (LANE_CHUNKS, V7X_SUBLANES, CHUNK_COLS))

    lam1 = lam_table(lb_re, lb_im)
    lam2 = lam_table(lb_re * lb_re - lb_im * lb_im, 2.0 * lb_re * lb_im)
    return wb.astype(bf16), wc.astype(bf16), lam1, lam2


def _layer(x, rel_bias, ln1_g, w_in, q_norm_g, k_norm_g, attn_sinks, ssm_a_re, ssm_a_im,
           ssm_log_dt, ssm_b_re, ssm_b_im, ssm_c_re, ssm_c_im, ssm_d, w_glu,
           attn_out_g, ssm_out_g, w_out, ln2_g, w_ff_gate, w_ff_up, w_ff_down):
    row = lambda v: v.reshape(1, -1).astype(f32)
    o_q, o_k, o_v = ATTN_WIDTH, ATTN_WIDTH + KV_WIDTH, ATTN_WIDTH + 2 * KV_WIDTH
    wq, wk = w_in[:, :o_q].astype(bf16), w_in[:, o_q:o_k].astype(bf16)
    wv, wu = w_in[:, o_k:o_v].astype(bf16), w_in[:, o_v:].astype(bf16)

    lanes = np.arange(KV_SLAB)
    seg = jnp.asarray(lanes[:, None] // HEAD_DIM == lanes[None, :] // HEAD_DIM, bf16)
    src = np.arange(KV_WIDTH)
    dst = np.arange(N_KV_HEADS * KV_SLAB)
    rep = jnp.asarray((src[:, None] // HEAD_DIM == dst[None, :] // KV_SLAB)
                      & (src[:, None] % HEAD_DIM == dst[None, :] % HEAD_DIM), bf16)
    gq = row(jnp.tile(q_norm_g, N_Q_HEADS)) * (HEAD_DIM ** -0.5)
    gk = row(jnp.tile(k_norm_g, N_KV_HEADS))
    qmask = jnp.asarray(np.broadcast_to(
        (lanes[None, None, :] // HEAD_DIM) == np.arange(Q_PER_KV)[:, None, None],
        (Q_PER_KV, BLOCK, KV_SLAB)), bf16)

    q, k_rep, v_rep, u_tb = _in_proj(x, row(ln1_g), wq, wk, wv, wu, seg, rep, gq, gk)

    y_attn = _attention(attn_sinks.astype(f32), q, k_rep, v_rep, _bias_tables(rel_bias), qmask,
                        row(attn_out_g))

    wb, wc, lam1, lam2 = _ssm_params(ssm_a_re.astype(f32), ssm_a_im.astype(f32), ssm_log_dt.astype(f32),
                                     ssm_b_re.astype(f32), ssm_b_im.astype(f32),
                                     ssm_c_re.astype(f32), ssm_c_im.astype(f32))
    y_ssm = _ssm(u_tb.reshape(SEQ * BATCH, SSM_WIDTH), wb, lam1, lam2, wc, row(ssm_d),
                 w_glu.astype(bf16), row(ssm_out_g))

    x1 = _out_proj(y_attn, y_ssm.reshape(SEQ, BATCH * SSM_WIDTH), x,
                   w_out[:ATTN_WIDTH].astype(bf16), w_out[ATTN_WIDTH:].astype(bf16))

    out = _ffn(x1.reshape(BATCH * SEQ, D_MODEL), row(ln2_g), w_ff_gate.astype(bf16),
               w_ff_up.astype(bf16), w_ff_down.astype(bf16))
    return out.reshape(BATCH, SEQ, D_MODEL)


def kernel(x, rel_bias, ln1_g, w_in, q_norm_g, k_norm_g, attn_sinks, ssm_a_re, ssm_a_im, ssm_log_dt, ssm_b_re, ssm_b_im, ssm_c_re, ssm_c_im, ssm_d, w_glu, attn_out_g, ssm_out_g, w_out, ln2_g, w_ff_gate, w_ff_up, w_ff_down):
    for l in range(ln1_g.shape[0]):
        x = _layer(x, rel_bias, ln1_g[l], w_in[l], q_norm_g[l], k_norm_g[l], attn_sinks[l],
                   ssm_a_re[l], ssm_a_im[l], ssm_log_dt[l], ssm_b_re[l], ssm_b_im[l],
                   ssm_c_re[l], ssm_c_im[l], ssm_d[l], w_glu[l], attn_out_g[l], ssm_out_g[l],
                   w_out[l], ln2_g[l], w_ff_gate[l], w_ff_up[l], w_ff_down[l])
    return x
```

```python
import functools
import math

import jax
import jax.numpy as jnp
import numpy as np
from jax import lax
from jax.experimental import pallas as pl
from jax.experimental.pallas import tpu as pltpu

D_MODEL = 2048
BATCH = 4
SEQ = 4096
HEAD_DIM = 64
N_Q_HEADS = 16
N_KV_HEADS = 4
Q_PER_KV = N_Q_HEADS // N_KV_HEADS
ATTN_WIDTH = N_Q_HEADS * HEAD_DIM
KV_WIDTH = N_KV_HEADS * HEAD_DIM
WINDOW = 128
BLOCK = 128
SSM_WIDTH = D_MODEL - ATTN_WIDTH
SSM_GROUP = 16
SSM_GROUPS = SSM_WIDTH // SSM_GROUP
SSM_STATE = 64
FF_HIDDEN = 5632
REL_BUCKETS = 32
REL_MAX_DISTANCE = 128
EPS = 1e-6

V7X_LANES = 128
V7X_SUBLANES = 8
V7X_VMEM_BYTES = 64 * 1024 * 1024
VMEM_LIMIT = V7X_VMEM_BYTES - 8 * 1024 * 1024

MASK_VALUE = -1e30
KV_SLAB = Q_PER_KV * HEAD_DIM
LANE_CHUNKS = SSM_WIDTH // V7X_LANES
PAIRS_PER_CHUNK = 4
CHUNK_COLS = PAIRS_PER_CHUNK * 2 * V7X_LANES

IN_TM = 512
FFN_TM = 512
FFN_TF = 512
SSM_T = 64
SSM_ROWS = SSM_T * BATCH

f32 = jnp.float32
bf16 = jnp.bfloat16


def _dot(a, b):
    return jnp.dot(a, b, preferred_element_type=f32)


def _rms_scale(x):
    return lax.rsqrt(jnp.mean(x * x, axis=-1, keepdims=True) + EPS)


def _in_proj_kernel(x_ref, g_ref, wq_ref, wk_ref, wv_ref, wu_ref, seg_ref, rep_ref, gq_ref, gk_ref,
                    q_ref, k_ref, v_ref, u_ref):
    xf = x_ref[...]
    h = (xf * _rms_scale(xf) * g_ref[...]).astype(bf16)
    seg = seg_ref[...]

    def head_norm(p, gain):
        sq = p * p
        hi = sq.astype(bf16)
        lo = (sq - hi.astype(f32)).astype(bf16)
        ss = _dot(hi, seg) + _dot(lo, seg)
        return p * lax.rsqrt(ss * (1.0 / HEAD_DIM) + EPS) * gain

    for c in range(ATTN_WIDTH // KV_SLAB):
        cols = slice(c * KV_SLAB, (c + 1) * KV_SLAB)
        qc = _dot(h, wq_ref[:, cols])
        q_ref[:, cols] = head_norm(qc, gq_ref[:, cols]).astype(bf16)
    kc = head_norm(_dot(h, wk_ref[...]), gk_ref[...]).astype(bf16)
    k_ref[...] = _dot(kc, rep_ref[...]).astype(bf16)
    vc = _dot(h, wv_ref[...]).astype(bf16)
    v_ref[...] = _dot(vc, rep_ref[...]).astype(bf16)
    u_ref[...] = _dot(h, wu_ref[...]).astype(bf16)


def _const_spec(shape):
    nd = len(shape)
    return pl.BlockSpec(shape, lambda *_: (0,) * nd)


def _in_proj(x, ln1_g, wq, wk, wv, wu, seg, rep, gq, gk):
    nt = SEQ // IN_TM
    row_spec = lambda w: pl.BlockSpec((None, IN_TM, w), lambda b, t: (b, t, 0))
    return pl.pallas_call(
        _in_proj_kernel,
        grid=(BATCH, nt),
        in_specs=[
            row_spec(D_MODEL),
            _const_spec((1, D_MODEL)),
            _const_spec((D_MODEL, ATTN_WIDTH)),
            _const_spec((D_MODEL, KV_WIDTH)),
            _const_spec((D_MODEL, KV_WIDTH)),
            _const_spec((D_MODEL, SSM_WIDTH)),
            _const_spec((KV_SLAB, KV_SLAB)),
            _const_spec((KV_WIDTH, N_KV_HEADS * KV_SLAB)),
            _const_spec((1, ATTN_WIDTH)),
            _const_spec((1, KV_WIDTH)),
        ],
        out_specs=[
            row_spec(ATTN_WIDTH),
            row_spec(N_KV_HEADS * KV_SLAB),
            row_spec(N_KV_HEADS * KV_SLAB),
            row_spec(SSM_WIDTH),
        ],
        out_shape=[
            jax.ShapeDtypeStruct((BATCH, SEQ, ATTN_WIDTH), bf16),
            jax.ShapeDtypeStruct((BATCH, SEQ, N_KV_HEADS * KV_SLAB), bf16),
            jax.ShapeDtypeStruct((BATCH, SEQ, N_KV_HEADS * KV_SLAB), bf16),
            jax.ShapeDtypeStruct((BATCH, SEQ, SSM_WIDTH), bf16),
        ],
        compiler_params=pltpu.CompilerParams(
            dimension_semantics=("arbitrary", "arbitrary"), vmem_limit_bytes=VMEM_LIMIT),
        name="in_proj",
    )(x, ln1_g, wq, wk, wv, wu, seg, rep, gq, gk)


def _attn_kernel(sink_ref, q_ref, kc_ref, kp_ref, vc_ref, vp_ref, bias_ref, qmask_ref, g_ref,
                 o_ref, y_scr):
    lane_head = lax.broadcasted_iota(jnp.int32, (BLOCK, KV_SLAB), 1) // HEAD_DIM
    for kh in range(N_KV_HEADS):
        cols = slice(kh * KV_SLAB, (kh + 1) * KV_SLAB)
        qs = q_ref[:, cols]
        q4 = jnp.concatenate([qs * qmask_ref[g] for g in range(Q_PER_KV)], axis=0)
        keys = jnp.concatenate([kp_ref[:, cols], kc_ref[:, cols]], axis=0)
        vals = jnp.concatenate([vp_ref[:, cols], vc_ref[:, cols]], axis=0)
        s4 = lax.dot_general(q4, keys, (((1,), (1,)), ((), ())), preferred_element_type=f32)
        probs, inv_den = [], []
        for g in range(Q_PER_KV):
            head = kh * Q_PER_KV + g
            s = s4[g * BLOCK:(g + 1) * BLOCK] + bias_ref[0, head]
            sink = sink_ref[head]
            m = jnp.maximum(jnp.max(s, axis=-1, keepdims=True), sink)
            p = jnp.exp(s - m)
            den = jnp.sum(p, axis=-1, keepdims=True) + jnp.exp(sink - m)
            probs.append(p.astype(bf16))
            inv_den.append(1.0 / den)
        o4 = _dot(jnp.concatenate(probs, axis=0), vals)
        acc = jnp.zeros((BLOCK, KV_SLAB), f32)
        for g in range(Q_PER_KV):
            acc = acc + jnp.where(lane_head == g, o4[g * BLOCK:(g + 1) * BLOCK] * inv_den[g], 0.0)
        y_scr[:, cols] = acc
    y = y_scr[...]
    o_ref[...] = (y * _rms_scale(y) * g_ref[...]).astype(bf16)


def _attention(sinks, q, k_rep, v_rep, bias2, qmask, attn_out_g):
    nb = SEQ // BLOCK
    w = N_KV_HEADS * KV_SLAB
    cur = lambda b, n: (b, n, 0)
    prev = lambda b, n: (b, jnp.maximum(n - 1, 0), 0)
    blk = lambda im: pl.BlockSpec((None, BLOCK, w), im)
    return pl.pallas_call(
        _attn_kernel,
        grid=(BATCH, nb),
        in_specs=[
            pl.BlockSpec(memory_space=pltpu.SMEM),
            blk(cur), blk(cur), blk(prev), blk(cur), blk(prev),
            pl.BlockSpec((1, N_Q_HEADS, BLOCK, 2 * BLOCK), lambda b, n: (jnp.minimum(n, 1), 0, 0, 0)),
            _const_spec((Q_PER_KV, BLOCK, KV_SLAB)),
            _const_spec((1, ATTN_WIDTH)),
        ],
        out_specs=blk(cur),
        out_shape=jax.ShapeDtypeStruct((BATCH, SEQ, ATTN_WIDTH), bf16),
        scratch_shapes=[pltpu.VMEM((BLOCK, ATTN_WIDTH), f32)],
        compiler_params=pltpu.CompilerParams(
            dimension_semantics=("arbitrary", "arbitrary"), vmem_limit_bytes=VMEM_LIMIT),
        name="attn",
    )(sinks, q, k_rep, k_rep, v_rep, v_rep, bias2, qmask, attn_out_g)


def _gelu_tanh(x):
    return 0.5 * x * (1.0 + jnp.tanh(math.sqrt(2.0 / math.pi) * (x + 0.044715 * (x * x * x))))


def _sigmoid(x):
    return 1.0 / (1.0 + jnp.exp(-x))


def _ssm_kernel(u_ref, perm_ref, wb_ref, lam1_ref, lam2_ref, wc_ref, d_ref, wglu_ref, g_ref,
                o_ref, u_scr, bu_scr, y_scr, st_scr, prev_scr):
    @pl.when(pl.program_id(0) == 0)
    def _():
        st_scr[...] = jnp.zeros_like(st_scr)
        prev_scr[...] = jnp.zeros_like(prev_scr)

    u_bt = u_ref[...].reshape(SSM_ROWS, SSM_WIDTH)
    u_scr[...] = _dot(perm_ref[0], u_bt).astype(bf16)

    for c in range(LANE_CHUNKS):
        bu_scr[c] = _dot(u_scr[:, c * V7X_LANES:(c + 1) * V7X_LANES], wb_ref[c])

    first_step = lax.broadcasted_iota(jnp.int32, (V7X_SUBLANES, V7X_LANES), 0) < BATCH
    n_tiles = SSM_ROWS // V7X_SUBLANES
    for c in range(LANE_CHUNKS):
        for q0 in range(0, PAIRS_PER_CHUNK, 2):
            sls = []
            for q in (q0, q0 + 1):
                base = q * 2 * V7X_LANES
                sls.append((slice(base, base + V7X_LANES), slice(base + V7X_LANES, base + 2 * V7X_LANES)))
            lams = [(lam1_ref[c, :, cr], lam1_ref[c, :, ci], lam2_ref[c, :, cr], lam2_ref[c, :, ci])
                    for cr, ci in sls]
            init = tuple((st_scr[c, :, cr], st_scr[c, :, ci], prev_scr[c, :, cr], prev_scr[c, :, ci])
                         for cr, ci in sls)

            def body(v, carry, c=c, sls=sls, lams=lams):
                rows = pl.ds(pl.multiple_of(v * V7X_SUBLANES, V7X_SUBLANES), V7X_SUBLANES)
                out = []
                for (cr, ci), (l1r, l1i, l2r, l2i), (s_re, s_im, pr_re, pr_im) in zip(sls, lams, carry):
                    a_re = bu_scr[c, rows, cr]
                    a_im = bu_scr[c, rows, ci]
                    r_re = pltpu.roll(a_re, BATCH, 0)
                    r_im = pltpu.roll(a_im, BATCH, 0)
                    p_re = jnp.where(first_step, pr_re, r_re)
                    p_im = jnp.where(first_step, pr_im, r_im)
                    x_re = a_re + (l1r * p_re - l1i * p_im)
                    x_im = a_im + (l1r * p_im + l1i * p_re)
                    n_re = (l2r * s_re - l2i * s_im) + x_re
                    n_im = (l2r * s_im + l2i * s_re) + x_im
                    bu_scr[c, rows, cr] = n_re
                    bu_scr[c, rows, ci] = n_im
                    out.append((n_re, n_im, r_re, r_im))
                return tuple(out)

            fin = lax.fori_loop(0, n_tiles, body, init, unroll=4)
            for (cr, ci), (s_re, s_im, pr_re, pr_im) in zip(sls, fin):
                st_scr[c, :, cr] = s_re
                st_scr[c, :, ci] = s_im
                prev_scr[c, :, cr] = pr_re
                prev_scr[c, :, ci] = pr_im

    for c in range(LANE_CHUNKS):
        cols = slice(c * V7X_LANES, (c + 1) * V7X_LANES)
        yc = _dot(bu_scr[c].astype(bf16), wc_ref[c])
        yc = yc + d_ref[:, cols] * u_scr[:, cols].astype(f32)
        y_scr[:, cols] = _gelu_tanh(yc)
    y = y_scr[...]
    out = y * _sigmoid(_dot(y.astype(bf16), wglu_ref[...]))
    out_tb = (out * _rms_scale(out) * g_ref[...]).astype(bf16)
    o_ref[...] = _dot(perm_ref[1], out_tb).astype(bf16).reshape(BATCH, SSM_T, SSM_WIDTH)


def _ssm(u, perm, wb, lam1, lam2, wc, d, wglu, ssm_out_g):
    return pl.pallas_call(
        _ssm_kernel,
        grid=(SEQ // SSM_T,),
        in_specs=[
            pl.BlockSpec((BATCH, SSM_T, SSM_WIDTH), lambda i: (0, i, 0)),
            _const_spec((2, SSM_ROWS, SSM_ROWS)),
            _const_spec((LANE_CHUNKS, V7X_LANES, CHUNK_COLS)),
            _const_spec((LANE_CHUNKS, V7X_SUBLANES, CHUNK_COLS)),
            _const_spec((LANE_CHUNKS, V7X_SUBLANES, CHUNK_COLS)),
            _const_spec((LANE_CHUNKS, CHUNK_COLS, V7X_LANES)),
            _const_spec((1, SSM_WIDTH)),
            _const_spec((SSM_WIDTH, SSM_WIDTH)),
            _const_spec((1, SSM_WIDTH)),
        ],
        out_specs=pl.BlockSpec((BATCH, SSM_T, SSM_WIDTH), lambda i: (0, i, 0)),
        out_shape=jax.ShapeDtypeStruct((BATCH, SEQ, SSM_WIDTH), bf16),
        scratch_shapes=[
            pltpu.VMEM((SSM_ROWS, SSM_WIDTH), bf16),
            pltpu.VMEM((LANE_CHUNKS, SSM_ROWS, CHUNK_COLS), f32),
            pltpu.VMEM((SSM_ROWS, SSM_WIDTH), f32),
            pltpu.VMEM((LANE_CHUNKS, V7X_SUBLANES, CHUNK_COLS), f32),
            pltpu.VMEM((LANE_CHUNKS, V7X_SUBLANES, CHUNK_COLS), f32),
        ],
        compiler_params=pltpu.CompilerParams(
            dimension_semantics=("arbitrary",), vmem_limit_bytes=VMEM_LIMIT),
        name="ssm",
    )(u, perm, wb, lam1, lam2, wc, d, wglu, ssm_out_g)


def _out_proj_kernel(a_ref, s_ref, x_ref, wa_ref, ws_ref, o_ref):
    o_ref[...] = x_ref[...] + _dot(a_ref[...], wa_ref[...]) + _dot(s_ref[...], ws_ref[...])


def _out_proj(a, s, x, wa, ws):
    nt = SEQ // IN_TM
    return pl.pallas_call(
        _out_proj_kernel,
        grid=(BATCH, nt),
        in_specs=[
            pl.BlockSpec((None, IN_TM, ATTN_WIDTH), lambda b, t: (b, t, 0)),
            pl.BlockSpec((None, IN_TM, SSM_WIDTH), lambda b, t: (b, t, 0)),
            pl.BlockSpec((None, IN_TM, D_MODEL), lambda b, t: (b, t, 0)),
            _const_spec((ATTN_WIDTH, D_MODEL)),
            _const_spec((SSM_WIDTH, D_MODEL)),
        ],
        out_specs=pl.BlockSpec((None, IN_TM, D_MODEL), lambda b, t: (b, t, 0)),
        out_shape=jax.ShapeDtypeStruct((BATCH, SEQ, D_MODEL), f32),
        compiler_params=pltpu.CompilerParams(
            dimension_semantics=("arbitrary", "arbitrary"), vmem_limit_bytes=VMEM_LIMIT),
        name="out_proj",
    )(a, s, x, wa, ws)


def _ffn_kernel(x_ref, g_ref, wg_ref, wu_ref, wd_ref, o_ref, h_scr):
    @pl.when(pl.program_id(1) == 0)
    def _():
        xf = x_ref[...]
        h_scr[...] = (xf * _rms_scale(xf) * g_ref[...]).astype(bf16)
        o_ref[...] = xf

    h = h_scr[...]
    gate = _dot(h, wg_ref[...])
    up = _dot(h, wu_ref[...])
    act = (gate * _sigmoid(gate) * up).astype(bf16)
    o_ref[...] += _dot(act, wd_ref[...])


def _ffn(x, ln2_g, wg, wu, wd):
    n_rows = BATCH * SEQ
    return pl.pallas_call(
        _ffn_kernel,
        grid=(n_rows // FFN_TM, FF_HIDDEN // FFN_TF),
        in_specs=[
            pl.BlockSpec((FFN_TM, D_MODEL), lambda i, f: (i, 0)),
            _const_spec((1, D_MODEL)),
            pl.BlockSpec((D_MODEL, FFN_TF), lambda i, f: (0, f)),
            pl.BlockSpec((D_MODEL, FFN_TF), lambda i, f: (0, f)),
            pl.BlockSpec((FFN_TF, D_MODEL), lambda i, f: (f, 0)),
        ],
        out_specs=pl.BlockSpec((FFN_TM, D_MODEL), lambda i, f: (i, 0)),
        out_shape=jax.ShapeDtypeStruct((n_rows, D_MODEL), f32),
        scratch_shapes=[pltpu.VMEM((FFN_TM, D_MODEL), bf16)],
        compiler_params=pltpu.CompilerParams(
            dimension_semantics=("arbitrary", "arbitrary"), vmem_limit_bytes=VMEM_LIMIT),
        name="ffn",
    )(x, ln2_g, wg, wu, wd)


def _t5_bucket(dist):
    n = np.maximum(dist, 0)
    max_exact = REL_BUCKETS // 2
    nf = np.maximum(n, 1).astype(np.float32)
    large = max_exact + (np.log(nf / max_exact) / math.log(REL_MAX_DISTANCE / max_exact)
                         * (REL_BUCKETS - max_exact)).astype(np.int32)
    large = np.minimum(large, REL_BUCKETS - 1)
    return np.where(n < max_exact, n, large).astype(np.int32)


def _bias_kernel(rb_ref, bucket_ref, valid_ref, o_ref):
    head = pl.program_id(1)
    bucket = bucket_ref[...]
    acc = jnp.zeros((BLOCK, 2 * BLOCK), f32)
    for k in range(REL_BUCKETS):
        acc = jnp.where(bucket == k, rb_ref[k, head], acc)
    o_ref[0, 0] = jnp.where(valid_ref[0] != 0, acc, MASK_VALUE)


def _bias_tables(rel_bias):
    qi = np.arange(BLOCK)[:, None]
    sj = np.arange(2 * BLOCK)[None, :]
    dist = qi + BLOCK - sj
    in_window = (dist >= 0) & (dist < WINDOW)
    valid = np.stack([in_window & (sj >= BLOCK), in_window]).astype(np.int32)
    return pl.pallas_call(
        _bias_kernel,
        grid=(2, N_Q_HEADS),
        in_specs=[
            pl.BlockSpec(memory_space=pltpu.SMEM),
            _const_spec((BLOCK, 2 * BLOCK)),
            pl.BlockSpec((1, BLOCK, 2 * BLOCK), lambda t, h: (t, 0, 0)),
        ],
        out_specs=pl.BlockSpec((1, 1, BLOCK, 2 * BLOCK), lambda t, h: (t, h, 0, 0)),
        out_shape=jax.ShapeDtypeStruct((2, N_Q_HEADS, BLOCK, 2 * BLOCK), f32),
        name="bias_table",
    )(rel_bias.astype(f32), jnp.asarray(_t5_bucket(dist)), jnp.asarray(valid))


def _ssm_params(a_re, a_im, log_dt, b_re, b_im, c_re, c_im):
    dt = jnp.exp(log_dt)[:, None]
    mag = jnp.exp(a_re * dt)
    ang = a_im * dt
    lb_re, lb_im = mag * jnp.cos(ang), mag * jnp.sin(ang)
    nr, ni = lb_re - 1.0, lb_im
    den = a_re * a_re + a_im * a_im
    f_re = (nr * a_re + ni * a_im) / den
    f_im = (ni * a_re - nr * a_im) / den
    bb_re = f_re[..., None] * b_re - f_im[..., None] * b_im
    bb_im = f_re[..., None] * b_im + f_im[..., None] * b_re

    sel = np.zeros((8, PAIRS_PER_CHUNK, 2), np.float32)
    for q in range(PAIRS_PER_CHUNK):
        for gl in range(2):
            sel[2 * q + gl, q, gl] = 1.0
    split = lambda t: t.reshape((t.shape[0], LANE_CHUNKS, PAIRS_PER_CHUNK, 2) + t.shape[2:])
    bbs = split(jnp.stack([bb_re, bb_im]))
    wb = jnp.einsum('jqg,acqgnp->cjpqagn', sel, bbs).reshape(LANE_CHUNKS, V7X_LANES, CHUNK_COLS)
    cs = split(jnp.stack([c_re, -c_im]))
    wc = jnp.einsum('jqg,acqgpn->cqagnjp', sel, cs).reshape(LANE_CHUNKS, CHUNK_COLS, V7X_LANES)

    def lam_table(re, im):
        t = split(jnp.stack([re, im]))
        t = jnp.transpose(t, (1, 2, 0, 3, 4)).reshape(LANE_CHUNKS, 1, CHUNK_COLS)
        return jnp.broadcast_to(t, (LANE_CHUNKS, V7X_SUBLANES, CHUNK_COLS))

    lam1 = lam_table(lb_re, lb_im)
    lam2 = lam_table(lb_re * lb_re - lb_im * lb_im, 2.0 * lb_re * lb_im)
    return wb.astype(bf16), wc.astype(bf16), lam1, lam2


def _layer(x, rel_bias, ln1_g, w_in, q_norm_g, k_norm_g, attn_sinks, ssm_a_re, ssm_a_im,
           ssm_log_dt, ssm_b_re, ssm_b_im, ssm_c_re, ssm_c_im, ssm_d, w_glu,
           attn_out_g, ssm_out_g, w_out, ln2_g, w_ff_gate, w_ff_up, w_ff_down):
    row = lambda v: v.reshape(1, -1).astype(f32)
    o_q, o_k, o_v = ATTN_WIDTH, ATTN_WIDTH + KV_WIDTH, ATTN_WIDTH + 2 * KV_WIDTH
    wq, wk = w_in[:, :o_q].astype(bf16), w_in[:, o_q:o_k].astype(bf16)
    wv, wu = w_in[:, o_k:o_v].astype(bf16), w_in[:, o_v:].astype(bf16)

    lanes = np.arange(KV_SLAB)
    seg = jnp.asarray(lanes[:, None] // HEAD_DIM == lanes[None, :] // HEAD_DIM, bf16)
    src = np.arange(KV_WIDTH)
    dst = np.arange(N_KV_HEADS * KV_SLAB)
    rep = jnp.asarray((src[:, None] // HEAD_DIM == dst[None, :] // KV_SLAB)
                      & (src[:, None] % HEAD_DIM == dst[None, :] % HEAD_DIM), bf16)
    gq = row(jnp.tile(q_norm_g, N_Q_HEADS)) * (HEAD_DIM ** -0.5)
    gk = row(jnp.tile(k_norm_g, N_KV_HEADS))
    qmask = jnp.asarray(np.broadcast_to(
        (lanes[None, None, :] // HEAD_DIM) == np.arange(Q_PER_KV)[:, None, None],
        (Q_PER_KV, BLOCK, KV_SLAB)), bf16)

    q, k_rep, v_rep, u = _in_proj(x, row(ln1_g), wq, wk, wv, wu, seg, rep, gq, gk)

    y_attn = _attention(attn_sinks.astype(f32), q, k_rep, v_rep, _bias_tables(rel_bias), qmask,
                        row(attn_out_g))

    wb, wc, lam1, lam2 = _ssm_params(ssm_a_re.astype(f32), ssm_a_im.astype(f32), ssm_log_dt.astype(f32),
                                     ssm_b_re.astype(f32), ssm_b_im.astype(f32),
                                     ssm_c_re.astype(f32), ssm_c_im.astype(f32))
    r_tb = np.arange(SSM_ROWS)
    to_tb = r_tb[None, :] == (r_tb[:, None] % BATCH) * SSM_T + r_tb[:, None] // BATCH
    perm = jnp.asarray(np.stack([to_tb, to_tb.T]), bf16)
    y_ssm = _ssm(u, perm, wb, lam1, lam2, wc, row(ssm_d), w_glu.astype(bf16), row(ssm_out_g))

    x1 = _out_proj(y_attn, y_ssm, x,
                   w_out[:ATTN_WIDTH].astype(bf16), w_out[ATTN_WIDTH:].astype(bf16))

    out = _ffn(x1.reshape(BATCH * SEQ, D_MODEL), row(ln2_g), w_ff_gate.astype(bf16),
               w_ff_up.astype(bf16), w_ff_down.astype(bf16))
    return out.reshape(BATCH, SEQ, D_MODEL)


def kernel(x, rel_bias, ln1_g, w_in, q_norm_g, k_norm_g, attn_sinks, ssm_a_re, ssm_a_im, ssm_log_dt, ssm_b_re, ssm_b_im, ssm_c_re, ssm_c_im, ssm_d, w_glu, attn_out_g, ssm_out_g, w_out, ln2_g, w_ff_gate, w_ff_up, w_ff_down):
    for l in range(ln1_g.shape[0]):
        x = _layer(x, rel_bias, ln1_g[l], w_in[l], q_norm_g[l], k_norm_g[l], attn_sinks[l],
                   ssm_a_re[l], ssm_a_im[l], ssm_log_dt[l], ssm_b_re[l], ssm_b_im[l],
                   ssm_c_re[l], ssm_c_im[l], ssm_d[l], w_glu[l], attn_out_g[l], ssm_out_g[l],
                   w_out[l], ln2_g[l], w_ff_gate[l], w_ff_up[l], w_ff_down[l])
    return x
```

```python
import functools
import math

import jax
import jax.numpy as jnp
import numpy as np
from jax import lax
from jax.experimental import pallas as pl
from jax.experimental.pallas import tpu as pltpu

D_MODEL = 2048
BATCH = 4
SEQ = 4096
HEAD_DIM = 64
N_Q_HEADS = 16
N_KV_HEADS = 4
Q_PER_KV = N_Q_HEADS // N_KV_HEADS
ATTN_WIDTH = N_Q_HEADS * HEAD_DIM
KV_WIDTH = N_KV_HEADS * HEAD_DIM
WINDOW = 128
BLOCK = 128
SSM_WIDTH = D_MODEL - ATTN_WIDTH
SSM_GROUP = 16
SSM_GROUPS = SSM_WIDTH // SSM_GROUP
SSM_STATE = 64
FF_HIDDEN = 5632
REL_BUCKETS = 32
REL_MAX_DISTANCE = 128
EPS = 1e-6

V7X_LANES = 128
V7X_SUBLANES = 8
V7X_VMEM_BYTES = 64 * 1024 * 1024
VMEM_LIMIT = V7X_VMEM_BYTES - 8 * 1024 * 1024

MASK_VALUE = -1e30
KV_SLAB = Q_PER_KV * HEAD_DIM
LANE_CHUNKS = SSM_WIDTH // V7X_LANES
SLOTS = 2
SLOT_COLS = SLOTS * 2 * V7X_LANES

IN_TM = 512
FFN_TM = 512
FFN_TF = 512
SSM_T = 64
SSM_ROWS = SSM_T * BATCH
SSM_ROWS2 = 2 * SSM_ROWS

f32 = jnp.float32
bf16 = jnp.bfloat16


def _dot(a, b):
    return jnp.dot(a, b, preferred_element_type=f32)


def _rms_scale(x):
    return lax.rsqrt(jnp.mean(x * x, axis=-1, keepdims=True) + EPS)


def _in_proj_kernel(x_ref, g_ref, wq_ref, wk_ref, wv_ref, wu_ref, seg_ref, rep_ref, gq_ref, gk_ref,
                    q_ref, k_ref, v_ref, u_ref):
    xf = x_ref[...]
    h = (xf * _rms_scale(xf) * g_ref[...]).astype(bf16)
    seg = seg_ref[...]

    def head_norm(p, gain):
        sq = p * p
        hi = sq.astype(bf16)
        lo = (sq - hi.astype(f32)).astype(bf16)
        ss = _dot(hi, seg) + _dot(lo, seg)
        return p * lax.rsqrt(ss * (1.0 / HEAD_DIM) + EPS) * gain

    for c in range(ATTN_WIDTH // KV_SLAB):
        cols = slice(c * KV_SLAB, (c + 1) * KV_SLAB)
        qc = _dot(h, wq_ref[:, cols])
        q_ref[:, cols] = head_norm(qc, gq_ref[:, cols]).astype(bf16)
    kc = head_norm(_dot(h, wk_ref[...]), gk_ref[...]).astype(bf16)
    k_ref[...] = _dot(kc, rep_ref[...]).astype(bf16)
    vc = _dot(h, wv_ref[...]).astype(bf16)
    v_ref[...] = _dot(vc, rep_ref[...]).astype(bf16)
    u_ref[...] = _dot(h, wu_ref[...]).astype(bf16)


def _const_spec(shape):
    nd = len(shape)
    return pl.BlockSpec(shape, lambda *_: (0,) * nd)


def _in_proj(x, ln1_g, wq, wk, wv, wu, seg, rep, gq, gk):
    nt = SEQ // IN_TM
    row_spec = lambda w: pl.BlockSpec((None, IN_TM, w), lambda b, t: (b, t, 0))
    return pl.pallas_call(
        _in_proj_kernel,
        grid=(BATCH, nt),
        in_specs=[
            row_spec(D_MODEL),
            _const_spec((1, D_MODEL)),
            _const_spec((D_MODEL, ATTN_WIDTH)),
            _const_spec((D_MODEL, KV_WIDTH)),
            _const_spec((D_MODEL, KV_WIDTH)),
            _const_spec((D_MODEL, SSM_WIDTH)),
            _const_spec((KV_SLAB, KV_SLAB)),
            _const_spec((KV_WIDTH, N_KV_HEADS * KV_SLAB)),
            _const_spec((1, ATTN_WIDTH)),
            _const_spec((1, KV_WIDTH)),
        ],
        out_specs=[
            row_spec(ATTN_WIDTH),
            row_spec(N_KV_HEADS * KV_SLAB),
            row_spec(N_KV_HEADS * KV_SLAB),
            row_spec(SSM_WIDTH),
        ],
        out_shape=[
            jax.ShapeDtypeStruct((BATCH, SEQ, ATTN_WIDTH), bf16),
            jax.ShapeDtypeStruct((BATCH, SEQ, N_KV_HEADS * KV_SLAB), bf16),
            jax.ShapeDtypeStruct((BATCH, SEQ, N_KV_HEADS * KV_SLAB), bf16),
            jax.ShapeDtypeStruct((BATCH, SEQ, SSM_WIDTH), bf16),
        ],
        compiler_params=pltpu.CompilerParams(
            dimension_semantics=("arbitrary", "arbitrary"), vmem_limit_bytes=VMEM_LIMIT),
        name="in_proj",
    )(x, ln1_g, wq, wk, wv, wu, seg, rep, gq, gk)


def _attn_kernel(sink_ref, q_ref, kc_ref, kp_ref, vc_ref, vp_ref, bias_ref, qmask_ref, g_ref,
                 o_ref, y_scr):
    lane_head = lax.broadcasted_iota(jnp.int32, (BLOCK, KV_SLAB), 1) // HEAD_DIM
    for kh in range(N_KV_HEADS):
        cols = slice(kh * KV_SLAB, (kh + 1) * KV_SLAB)
        qs = q_ref[:, cols]
        q4 = jnp.concatenate([qs * qmask_ref[g] for g in range(Q_PER_KV)], axis=0)
        keys = jnp.concatenate([kp_ref[:, cols], kc_ref[:, cols]], axis=0)
        vals = jnp.concatenate([vp_ref[:, cols], vc_ref[:, cols]], axis=0)
        s4 = lax.dot_general(q4, keys, (((1,), (1,)), ((), ())), preferred_element_type=f32)
        probs, inv_den = [], []
        for g in range(Q_PER_KV):
            head = kh * Q_PER_KV + g
            s = s4[g * BLOCK:(g + 1) * BLOCK] + bias_ref[0, head]
            sink = sink_ref[head]
            m = jnp.maximum(jnp.max(s, axis=-1, keepdims=True), sink)
            p = jnp.exp(s - m)
            den = jnp.sum(p, axis=-1, keepdims=True) + jnp.exp(sink - m)
            probs.append(p.astype(bf16))
            inv_den.append(1.0 / den)
        o4 = _dot(jnp.concatenate(probs, axis=0), vals)
        acc = jnp.zeros((BLOCK, KV_SLAB), f32)
        for g in range(Q_PER_KV):
            acc = acc + jnp.where(lane_head == g, o4[g * BLOCK:(g + 1) * BLOCK] * inv_den[g], 0.0)
        y_scr[:, cols] = acc
    y = y_scr[...]
    o_ref[...] = (y * _rms_scale(y) * g_ref[...]).astype(bf16)


def _attention(sinks, q, k_rep, v_rep, bias2, qmask, attn_out_g):
    nb = SEQ // BLOCK
    w = N_KV_HEADS * KV_SLAB
    cur = lambda b, n: (b, n, 0)
    prev = lambda b, n: (b, jnp.maximum(n - 1, 0), 0)
    blk = lambda im: pl.BlockSpec((None, BLOCK, w), im)
    return pl.pallas_call(
        _attn_kernel,
        grid=(BATCH, nb),
        in_specs=[
            pl.BlockSpec(memory_space=pltpu.SMEM),
            blk(cur), blk(cur), blk(prev), blk(cur), blk(prev),
            pl.BlockSpec((1, N_Q_HEADS, BLOCK, 2 * BLOCK), lambda b, n: (jnp.minimum(n, 1), 0, 0, 0)),
            _const_spec((Q_PER_KV, BLOCK, KV_SLAB)),
            _const_spec((1, ATTN_WIDTH)),
        ],
        out_specs=blk(cur),
        out_shape=jax.ShapeDtypeStruct((BATCH, SEQ, ATTN_WIDTH), bf16),
        scratch_shapes=[pltpu.VMEM((BLOCK, ATTN_WIDTH), f32)],
        compiler_params=pltpu.CompilerParams(
            dimension_semantics=("arbitrary", "arbitrary"), vmem_limit_bytes=VMEM_LIMIT),
        name="attn",
    )(sinks, q, k_rep, k_rep, v_rep, v_rep, bias2, qmask, attn_out_g)


def _gelu_tanh(x):
    return 0.5 * x * (1.0 + jnp.tanh(math.sqrt(2.0 / math.pi) * (x + 0.044715 * (x * x * x))))


def _sigmoid(x):
    return 1.0 / (1.0 + jnp.exp(-x))


def _ssm_kernel(u_ref, spread_ref, gather_ref, wb_ref, lam_ref, wc_ref, d_ref, wglu_ref, g_ref,
                o_ref, u2_scr, bu_scr, h_scr, y2_scr, y_scr, st_scr):
    @pl.when(pl.program_id(0) == 0)
    def _():
        st_scr[...] = jnp.zeros_like(st_scr)

    u_bt = u_ref[...].reshape(SSM_ROWS, SSM_WIDTH)
    u2 = _dot(spread_ref[...], u_bt)
    row_par = lax.broadcasted_iota(jnp.int32, (SSM_ROWS2, V7X_LANES), 0) % 2
    pair_par = (lax.broadcasted_iota(jnp.int32, (SSM_ROWS2, V7X_LANES), 1) // (2 * SSM_GROUP)) % 2
    own_pair = row_par == pair_par

    for c in range(LANE_CHUNKS):
        uc = u2[:, c * V7X_LANES:(c + 1) * V7X_LANES]
        u2_scr[c] = uc
        bu_scr[c] = _dot(jnp.where(own_pair, uc, 0.0).astype(bf16), wb_ref[c])

    half = LANE_CHUNKS // 2
    for c0 in (0, half):
        chains = [(c, s) for c in range(c0, c0 + half) for s in range(SLOTS)]

        def cols(s):
            base = s * 2 * V7X_LANES
            return slice(base, base + V7X_LANES), slice(base + V7X_LANES, base + 2 * V7X_LANES)

        lams = [(lam_ref[c, :, cols(s)[0]], lam_ref[c, :, cols(s)[1]]) for c, s in chains]
        init = tuple((st_scr[c, :, cols(s)[0]], st_scr[c, :, cols(s)[1]]) for c, s in chains)

        def body(t, carry, chains=chains, lams=lams, cols=cols):
            rows = pl.ds(pl.multiple_of(t * V7X_SUBLANES, V7X_SUBLANES), V7X_SUBLANES)
            out = []
            for (c, s), (l_re, l_im), (s_re, s_im) in zip(chains, lams, carry):
                cr, ci = cols(s)
                n_re = (l_re * s_re - l_im * s_im) + bu_scr[c, rows, cr]
                n_im = (l_re * s_im + l_im * s_re) + bu_scr[c, rows, ci]
                h_scr[c, rows, cr] = n_re
                h_scr[c, rows, ci] = n_im
                out.append((n_re, n_im))
            return tuple(out)

        fin = lax.fori_loop(0, SSM_T, body, init, unroll=2)
        for (c, s), (s_re, s_im) in zip(chains, fin):
            st_scr[c, :, cols(s)[0]] = s_re
            st_scr[c, :, cols(s)[1]] = s_im

    even = pl.ds(0, SSM_ROWS, stride=2)
    odd = pl.ds(1, SSM_ROWS, stride=2)
    for c in range(LANE_CHUNKS):
        cols_c = slice(c * V7X_LANES, (c + 1) * V7X_LANES)
        y2 = _dot(h_scr[c].astype(bf16), wc_ref[c])
        y2_scr[2 * c] = y2[:, :V7X_LANES]
        y2_scr[2 * c + 1] = y2[:, V7X_LANES:]
        yc = y2_scr.at[2 * c][even, :] + y2_scr.at[2 * c + 1][odd, :]
        yc = yc + d_ref[:, cols_c] * u2_scr.at[c][even, :]
        y_scr[:, cols_c] = _gelu_tanh(yc)
    y = y_scr[...]
    out = y * _sigmoid(_dot(y.astype(bf16), wglu_ref[...]))
    out_tb = (out * _rms_scale(out) * g_ref[...]).astype(bf16)
    o_ref[...] = _dot(gather_ref[...], out_tb).astype(bf16).reshape(BATCH, SSM_T, SSM_WIDTH)


def _ssm(u, spread, gather, wb, lam, wc, d, wglu, ssm_out_g):
    return pl.pallas_call(
        _ssm_kernel,
        grid=(SEQ // SSM_T,),
        in_specs=[
            pl.BlockSpec((BATCH, SSM_T, SSM_WIDTH), lambda i: (0, i, 0)),
            _const_spec((SSM_ROWS2, SSM_ROWS)),
            _const_spec((SSM_ROWS, SSM_ROWS)),
            _const_spec((LANE_CHUNKS, V7X_LANES, SLOT_COLS)),
            _const_spec((LANE_CHUNKS, V7X_SUBLANES, SLOT_COLS)),
            _const_spec((LANE_CHUNKS, SLOT_COLS, 2 * V7X_LANES)),
            _const_spec((1, SSM_WIDTH)),
            _const_spec((SSM_WIDTH, SSM_WIDTH)),
            _const_spec((1, SSM_WIDTH)),
        ],
        out_specs=pl.BlockSpec((BATCH, SSM_T, SSM_WIDTH), lambda i: (0, i, 0)),
        out_shape=jax.ShapeDtypeStruct((BATCH, SEQ, SSM_WIDTH), bf16),
        scratch_shapes=[
            pltpu.VMEM((LANE_CHUNKS, SSM_ROWS2, V7X_LANES), f32),
            pltpu.VMEM((LANE_CHUNKS, SSM_ROWS2, SLOT_COLS), f32),
            pltpu.VMEM((LANE_CHUNKS, SSM_ROWS2, SLOT_COLS), f32),
            pltpu.VMEM((2 * LANE_CHUNKS, SSM_ROWS2, V7X_LANES), f32),
            pltpu.VMEM((SSM_ROWS, SSM_WIDTH), f32),
            pltpu.VMEM((LANE_CHUNKS, V7X_SUBLANES, SLOT_COLS), f32),
        ],
        compiler_params=pltpu.CompilerParams(
            dimension_semantics=("arbitrary",), vmem_limit_bytes=VMEM_LIMIT),
        name="ssm",
    )(u, spread, gather, wb, lam, wc, d, wglu, ssm_out_g)


def _out_proj_kernel(a_ref, s_ref, x_ref, wa_ref, ws_ref, o_ref):
    o_ref[...] = x_ref[...] + _dot(a_ref[...], wa_ref[...]) + _dot(s_ref[...], ws_ref[...])


def _out_proj(a, s, x, wa, ws):
    nt = SEQ // IN_TM
    return pl.pallas_call(
        _out_proj_kernel,
        grid=(BATCH, nt),
        in_specs=[
            pl.BlockSpec((None, IN_TM, ATTN_WIDTH), lambda b, t: (b, t, 0)),
            pl.BlockSpec((None, IN_TM, SSM_WIDTH), lambda b, t: (b, t, 0)),
            pl.BlockSpec((None, IN_TM, D_MODEL), lambda b, t: (b, t, 0)),
            _const_spec((ATTN_WIDTH, D_MODEL)),
            _const_spec((SSM_WIDTH, D_MODEL)),
        ],
        out_specs=pl.BlockSpec((None, IN_TM, D_MODEL), lambda b, t: (b, t, 0)),
        out_shape=jax.ShapeDtypeStruct((BATCH, SEQ, D_MODEL), f32),
        compiler_params=pltpu.CompilerParams(
            dimension_semantics=("arbitrary", "arbitrary"), vmem_limit_bytes=VMEM_LIMIT),
        name="out_proj",
    )(a, s, x, wa, ws)


def _ffn_kernel(x_ref, g_ref, wg_ref, wu_ref, wd_ref, o_ref, h_scr):
    @pl.when(pl.program_id(1) == 0)
    def _():
        xf = x_ref[...]
        h_scr[...] = (xf * _rms_scale(xf) * g_ref[...]).astype(bf16)
        o_ref[...] = xf

    h = h_scr[...]
    gate = _dot(h, wg_ref[...])
    up = _dot(h, wu_ref[...])
    act = (gate * _sigmoid(gate) * up).astype(bf16)
    o_ref[...] += _dot(act, wd_ref[...])


def _ffn(x, ln2_g, wg, wu, wd):
    n_rows = BATCH * SEQ
    return pl.pallas_call(
        _ffn_kernel,
        grid=(n_rows // FFN_TM, FF_HIDDEN // FFN_TF),
        in_specs=[
            pl.BlockSpec((FFN_TM, D_MODEL), lambda i, f: (i, 0)),
            _const_spec((1, D_MODEL)),
            pl.BlockSpec((D_MODEL, FFN_TF), lambda i, f: (0, f)),
            pl.BlockSpec((D_MODEL, FFN_TF), lambda i, f: (0, f)),
            pl.BlockSpec((FFN_TF, D_MODEL), lambda i, f: (f, 0)),
        ],
        out_specs=pl.BlockSpec((FFN_TM, D_MODEL), lambda i, f: (i, 0)),
        out_shape=jax.ShapeDtypeStruct((n_rows, D_MODEL), f32),
        scratch_shapes=[pltpu.VMEM((FFN_TM, D_MODEL), bf16)],
        compiler_params=pltpu.CompilerParams(
            dimension_semantics=("arbitrary", "arbitrary"), vmem_limit_bytes=VMEM_LIMIT),
        name="ffn",
    )(x, ln2_g, wg, wu, wd)


def _t5_bucket(dist):
    n = np.maximum(dist, 0)
    max_exact = REL_BUCKETS // 2
    nf = np.maximum(n, 1).astype(np.float32)
    large = max_exact + (np.log(nf / max_exact) / math.log(REL_MAX_DISTANCE / max_exact)
                         * (REL_BUCKETS - max_exact)).astype(np.int32)
    large = np.minimum(large, REL_BUCKETS - 1)
    return np.where(n < max_exact, n, large).astype(np.int32)


def _bias_kernel(rb_ref, bucket_ref, valid_ref, o_ref):
    head = pl.program_id(1)
    bucket = bucket_ref[...]
    acc = jnp.zeros((BLOCK, 2 * BLOCK), f32)
    for k in range(REL_BUCKETS):
        acc = jnp.where(bucket == k, rb_ref[k, head], acc)
    o_ref[0, 0] = jnp.where(valid_ref[0] != 0, acc, MASK_VALUE)


def _bias_tables(rel_bias):
    qi = np.arange(BLOCK)[:, None]
    sj = np.arange(2 * BLOCK)[None, :]
    dist = qi + BLOCK - sj
    in_window = (dist >= 0) & (dist < WINDOW)
    valid = np.stack([in_window & (sj >= BLOCK), in_window]).astype(np.int32)
    return pl.pallas_call(
        _bias_kernel,
        grid=(2, N_Q_HEADS),
        in_specs=[
            pl.BlockSpec(memory_space=pltpu.SMEM),
            _const_spec((BLOCK, 2 * BLOCK)),
            pl.BlockSpec((1, BLOCK, 2 * BLOCK), lambda t, h: (t, 0, 0)),
        ],
        out_specs=pl.BlockSpec((1, 1, BLOCK, 2 * BLOCK), lambda t, h: (t, h, 0, 0)),
        out_shape=jax.ShapeDtypeStruct((2, N_Q_HEADS, BLOCK, 2 * BLOCK), f32),
        name="bias_table",
    )(rel_bias.astype(f32), jnp.asarray(_t5_bucket(dist)), jnp.asarray(valid))


def _ssm_params(a_re, a_im, log_dt, b_re, b_im, c_re, c_im):
    dt = jnp.exp(log_dt)[:, None]
    mag = jnp.exp(a_re * dt)
    ang = a_im * dt
    lb_re, lb_im = mag * jnp.cos(ang), mag * jnp.sin(ang)
    nr, ni = lb_re - 1.0, lb_im
    den = a_re * a_re + a_im * a_im
    f_re = (nr * a_re + ni * a_im) / den
    f_im = (ni * a_re - nr * a_im) / den
    bb_re = f_re[..., None] * b_re - f_im[..., None] * b_im
    bb_im = f_re[..., None] * b_im + f_im[..., None] * b_re

    eye = np.eye(2, dtype=np.float32)
    split = lambda t: t.reshape((2, LANE_CHUNKS, SLOTS, 2, 2) + t.shape[2:])
    bbs = split(jnp.stack([bb_re, bb_im]))
    wb = jnp.einsum('us,gh,acsjgnp->cujgpsahn', eye, eye, bbs).reshape(LANE_CHUNKS, V7X_LANES, SLOT_COLS)
    cs = split(jnp.stack([c_re, -c_im]))
    wc = jnp.einsum('vj,us,gh,acsjgpn->csagnvujhp', eye, eye, eye, cs).reshape(
        LANE_CHUNKS, SLOT_COLS, 2 * V7X_LANES)
    lam = jnp.transpose(split(jnp.stack([lb_re, lb_im])), (1, 3, 2, 0, 4, 5)).reshape(LANE_CHUNKS, 1, 2, SLOT_COLS)
    lam = jnp.broadcast_to(lam, (LANE_CHUNKS, BATCH, 2, SLOT_COLS)).reshape(LANE_CHUNKS, V7X_SUBLANES, SLOT_COLS)
    return wb.astype(bf16), wc.astype(bf16), lam


def _layer(x, rel_bias, ln1_g, w_in, q_norm_g, k_norm_g, attn_sinks, ssm_a_re, ssm_a_im,
           ssm_log_dt, ssm_b_re, ssm_b_im, ssm_c_re, ssm_c_im, ssm_d, w_glu,
           attn_out_g, ssm_out_g, w_out, ln2_g, w_ff_gate, w_ff_up, w_ff_down):
    row = lambda v: v.reshape(1, -1).astype(f32)
    o_q, o_k, o_v = ATTN_WIDTH, ATTN_WIDTH + KV_WIDTH, ATTN_WIDTH + 2 * KV_WIDTH
    wq, wk = w_in[:, :o_q].astype(bf16), w_in[:, o_q:o_k].astype(bf16)
    wv, wu = w_in[:, o_k:o_v].astype(bf16), w_in[:, o_v:].astype(bf16)

    lanes = np.arange(KV_SLAB)
    seg = jnp.asarray(lanes[:, None] // HEAD_DIM == lanes[None, :] // HEAD_DIM, bf16)
    src = np.arange(KV_WIDTH)
    dst = np.arange(N_KV_HEADS * KV_SLAB)
    rep = jnp.asarray((src[:, None] // HEAD_DIM == dst[None, :] // KV_SLAB)
                      & (src[:, None] % HEAD_DIM == dst[None, :] % HEAD_DIM), bf16)
    gq = row(jnp.tile(q_norm_g, N_Q_HEADS)) * (HEAD_DIM ** -0.5)
    gk = row(jnp.tile(k_norm_g, N_KV_HEADS))
    qmask = jnp.asarray(np.broadcast_to(
        (lanes[None, None, :] // HEAD_DIM) == np.arange(Q_PER_KV)[:, None, None],
        (Q_PER_KV, BLOCK, KV_SLAB)), bf16)

    q, k_rep, v_rep, u = _in_proj(x, row(ln1_g), wq, wk, wv, wu, seg, rep, gq, gk)

    y_attn = _attention(attn_sinks.astype(f32), q, k_rep, v_rep, _bias_tables(rel_bias), qmask,
                        row(attn_out_g))

    wb, wc, lam = _ssm_params(ssm_a_re.astype(f32), ssm_a_im.astype(f32), ssm_log_dt.astype(f32),
                                     ssm_b_re.astype(f32), ssm_b_im.astype(f32),
                                     ssm_c_re.astype(f32), ssm_c_im.astype(f32))
    r_bt = np.arange(SSM_ROWS)
    r_tb = (r_bt % SSM_T) * BATCH + r_bt // SSM_T
    gather = r_tb[:, None] == np.arange(SSM_ROWS)[None, :]
    spread = np.arange(SSM_ROWS2)[:, None] // 2 == r_tb[None, :]
    y_ssm = _ssm(u, jnp.asarray(spread, bf16), jnp.asarray(gather, bf16), wb, lam, wc, row(ssm_d),
                 w_glu.astype(bf16), row(ssm_out_g))

    x1 = _out_proj(y_attn, y_ssm, x,
                   w_out[:ATTN_WIDTH].astype(bf16), w_out[ATTN_WIDTH:].astype(bf16))

    out = _ffn(x1.reshape(BATCH * SEQ, D_MODEL), row(ln2_g), w_ff_gate.astype(bf16),
               w_ff_up.astype(bf16), w_ff_down.astype(bf16))
    return out.reshape(BATCH, SEQ, D_MODEL)


def kernel(x, rel_bias, ln1_g, w_in, q_norm_g, k_norm_g, attn_sinks, ssm_a_re, ssm_a_im, ssm_log_dt, ssm_b_re, ssm_b_im, ssm_c_re, ssm_c_im, ssm_d, w_glu, attn_out_g, ssm_out_g, w_out, ln2_g, w_ff_gate, w_ff_up, w_ff_down):
    for l in range(ln1_g.shape[0]):
        x = _layer(x, rel_bias, ln1_g[l], w_in[l], q_norm_g[l], k_norm_g[l], attn_sinks[l],
                   ssm_a_re[l], ssm_a_im[l], ssm_log_dt[l], ssm_b_re[l], ssm_b_im[l],
                   ssm_c_re[l], ssm_c_im[l], ssm_d[l], w_glu[l], attn_out_g[l], ssm_out_g[l],
                   w_out[l], ln2_g[l], w_ff_gate[l], w_ff_up[l], w_ff_down[l])
    return x
```

```python
import functools
import math

import jax
import jax.numpy as jnp
import numpy as np
from jax import lax
from jax.experimental import pallas as pl
from jax.experimental.pallas import tpu as pltpu

D_MODEL = 2048
BATCH = 4
SEQ = 4096
HEAD_DIM = 64
N_Q_HEADS = 16
N_KV_HEADS = 4
Q_PER_KV = N_Q_HEADS // N_KV_HEADS
ATTN_WIDTH = N_Q_HEADS * HEAD_DIM
KV_WIDTH = N_KV_HEADS * HEAD_DIM
WINDOW = 128
BLOCK = 128
SSM_WIDTH = D_MODEL - ATTN_WIDTH
IN_WIDTH = ATTN_WIDTH + 2 * KV_WIDTH + SSM_WIDTH
SSM_GROUP = 16
SSM_GROUPS = SSM_WIDTH // SSM_GROUP
SSM_STATE = 64
FF_HIDDEN = 5632
REL_BUCKETS = 32
REL_MAX_DISTANCE = 128
EPS = 1e-6

V7X_LANES = 128
V7X_SUBLANES = 8
V7X_VMEM_BYTES = 64 * 1024 * 1024
VMEM_LIMIT = V7X_VMEM_BYTES - 8 * 1024 * 1024

MASK_VALUE = -1e30
KV_SLAB = Q_PER_KV * HEAD_DIM
LANE_CHUNKS = SSM_WIDTH // V7X_LANES
SLOTS = 2
SLOT_COLS = SLOTS * 2 * V7X_LANES

IN_TM = 512
FFN_TM = 512
FFN_TF = 512
SSM_T = 64
SSM_ROWS = SSM_T * BATCH
SSM_ROWS2 = 2 * SSM_ROWS

f32 = jnp.float32
bf16 = jnp.bfloat16


def _dot(a, b):
    return jnp.dot(a, b, preferred_element_type=f32)


def _rms_scale(x):
    return lax.rsqrt(jnp.mean(x * x, axis=-1, keepdims=True) + EPS)


def _in_proj_kernel(x_ref, g_ref, w_ref, seg_ref, rep_ref, gqk_ref, q_ref, k_ref, v_ref, u_ref, proj_scr):
    xf = x_ref[...]
    h = (xf * _rms_scale(xf) * g_ref[...]).astype(bf16)
    proj_scr[...] = _dot(h, w_ref[...])

    n_slabs = (ATTN_WIDTH + KV_WIDTH) // KV_SLAB
    slab = lambda c: proj_scr[:, c * KV_SLAB:(c + 1) * KV_SLAB]
    sq = jnp.concatenate([slab(c) * slab(c) for c in range(n_slabs)], axis=0)
    hi = sq.astype(bf16)
    lo = (sq - hi.astype(f32)).astype(bf16)
    ss = _dot(jnp.concatenate([hi, lo], axis=0), seg_ref[...])
    ss = ss[:n_slabs * IN_TM] + ss[n_slabs * IN_TM:]
    normed = []
    for c in range(n_slabs):
        cols = slice(c * KV_SLAB, (c + 1) * KV_SLAB)
        scale = lax.rsqrt(ss[c * IN_TM:(c + 1) * IN_TM] * (1.0 / HEAD_DIM) + EPS)
        normed.append((slab(c) * scale * gqk_ref[:, cols]).astype(bf16))
    for c in range(n_slabs - 1):
        q_ref[:, c * KV_SLAB:(c + 1) * KV_SLAB] = normed[c]
    v_cols = slice(ATTN_WIDTH + KV_WIDTH, ATTN_WIDTH + 2 * KV_WIDTH)
    kv = _dot(jnp.concatenate([normed[-1], proj_scr[:, v_cols].astype(bf16)], axis=0), rep_ref[...])
    k_ref[...] = kv[:IN_TM].astype(bf16)
    v_ref[...] = kv[IN_TM:].astype(bf16)
    u_ref[...] = proj_scr[:, ATTN_WIDTH + 2 * KV_WIDTH:].astype(bf16)


def _const_spec(shape):
    nd = len(shape)
    return pl.BlockSpec(shape, lambda *_: (0,) * nd)


def _in_proj(x, ln1_g, w, seg, rep, gqk):
    nt = SEQ // IN_TM
    row_spec = lambda w: pl.BlockSpec((None, IN_TM, w), lambda b, t: (b, t, 0))
    return pl.pallas_call(
        _in_proj_kernel,
        grid=(BATCH, nt),
        in_specs=[
            row_spec(D_MODEL),
            _const_spec((1, D_MODEL)),
            _const_spec((D_MODEL, IN_WIDTH)),
            _const_spec((KV_SLAB, KV_SLAB)),
            _const_spec((KV_WIDTH, N_KV_HEADS * KV_SLAB)),
            _const_spec((1, ATTN_WIDTH + KV_WIDTH)),
        ],
        out_specs=[
            row_spec(ATTN_WIDTH),
            row_spec(N_KV_HEADS * KV_SLAB),
            row_spec(N_KV_HEADS * KV_SLAB),
            row_spec(SSM_WIDTH),
        ],
        out_shape=[
            jax.ShapeDtypeStruct((BATCH, SEQ, ATTN_WIDTH), bf16),
            jax.ShapeDtypeStruct((BATCH, SEQ, N_KV_HEADS * KV_SLAB), bf16),
            jax.ShapeDtypeStruct((BATCH, SEQ, N_KV_HEADS * KV_SLAB), bf16),
            jax.ShapeDtypeStruct((BATCH, SEQ, SSM_WIDTH), bf16),
        ],
        scratch_shapes=[pltpu.VMEM((IN_TM, IN_WIDTH), f32)],
        compiler_params=pltpu.CompilerParams(
            dimension_semantics=("arbitrary", "arbitrary"), vmem_limit_bytes=VMEM_LIMIT),
        name="in_proj",
    )(x, ln1_g, w, seg, rep, gqk)


def _attn_kernel(sink_ref, q_ref, kc_ref, kp_ref, vc_ref, vp_ref, bias_ref, qmask_ref, g_ref,
                 o_ref, y_scr):
    lane_head = lax.broadcasted_iota(jnp.int32, (BLOCK, KV_SLAB), 1) // HEAD_DIM
    for kh in range(N_KV_HEADS):
        cols = slice(kh * KV_SLAB, (kh + 1) * KV_SLAB)
        qs = q_ref[:, cols]
        q4 = jnp.concatenate([qs * qmask_ref[g] for g in range(Q_PER_KV)], axis=0)
        keys = jnp.concatenate([kp_ref[:, cols], kc_ref[:, cols]], axis=0)
        vals = jnp.concatenate([vp_ref[:, cols], vc_ref[:, cols]], axis=0)
        s4 = lax.dot_general(q4, keys, (((1,), (1,)), ((), ())), preferred_element_type=f32)
        probs, inv_den = [], []
        for g in range(Q_PER_KV):
            head = kh * Q_PER_KV + g
            s = s4[g * BLOCK:(g + 1) * BLOCK] + bias_ref[0, head]
            sink = sink_ref[head]
            m = jnp.maximum(jnp.max(s, axis=-1, keepdims=True), sink)
            p = jnp.exp(s - m)
            den = jnp.sum(p, axis=-1, keepdims=True) + jnp.exp(sink - m)
            probs.append(p.astype(bf16))
            inv_den.append(1.0 / den)
        o4 = _dot(jnp.concatenate(probs, axis=0), vals)
        acc = jnp.zeros((BLOCK, KV_SLAB), f32)
        for g in range(Q_PER_KV):
            acc = acc + jnp.where(lane_head == g, o4[g * BLOCK:(g + 1) * BLOCK] * inv_den[g], 0.0)
        y_scr[:, cols] = acc
    y = y_scr[...]
    o_ref[...] = (y * _rms_scale(y) * g_ref[...]).astype(bf16)


def _attention(sinks, q, k_rep, v_rep, bias2, qmask, attn_out_g):
    nb = SEQ // BLOCK
    w = N_KV_HEADS * KV_SLAB
    cur = lambda b, n: (b, n, 0)
    prev = lambda b, n: (b, jnp.maximum(n - 1, 0), 0)
    blk = lambda im: pl.BlockSpec((None, BLOCK, w), im)
    return pl.pallas_call(
        _attn_kernel,
        grid=(BATCH, nb),
        in_specs=[
            pl.BlockSpec(memory_space=pltpu.SMEM),
            blk(cur), blk(cur), blk(prev), blk(cur), blk(prev),
            pl.BlockSpec((1, N_Q_HEADS, BLOCK, 2 * BLOCK), lambda b, n: (jnp.minimum(n, 1), 0, 0, 0)),
            _const_spec((Q_PER_KV, BLOCK, KV_SLAB)),
            _const_spec((1, ATTN_WIDTH)),
        ],
        out_specs=blk(cur),
        out_shape=jax.ShapeDtypeStruct((BATCH, SEQ, ATTN_WIDTH), bf16),
        scratch_shapes=[pltpu.VMEM((BLOCK, ATTN_WIDTH), f32)],
        compiler_params=pltpu.CompilerParams(
            dimension_semantics=("arbitrary", "arbitrary"), vmem_limit_bytes=VMEM_LIMIT),
        name="attn",
    )(sinks, q, k_rep, k_rep, v_rep, v_rep, bias2, qmask, attn_out_g)


def _gelu_tanh(x):
    return 0.5 * x * (1.0 + jnp.tanh(math.sqrt(2.0 / math.pi) * (x + 0.044715 * (x * x * x))))


def _sigmoid(x):
    return 1.0 / (1.0 + jnp.exp(-x))


def _ssm_kernel(u_ref, spread_ref, gather_ref, wb_ref, lam_ref, wc_ref, d_ref, wglu_ref, g_ref,
                o_ref, u2_scr, bu_scr, h_scr, y2_scr, y_scr, st_scr):
    @pl.when(pl.program_id(0) == 0)
    def _():
        st_scr[...] = jnp.zeros_like(st_scr)

    u_bt = u_ref[...].reshape(SSM_ROWS, SSM_WIDTH)
    u2 = _dot(spread_ref[...], u_bt)
    row_par = lax.broadcasted_iota(jnp.int32, (SSM_ROWS2, V7X_LANES), 0) % 2
    pair_par = (lax.broadcasted_iota(jnp.int32, (SSM_ROWS2, V7X_LANES), 1) // (2 * SSM_GROUP)) % 2
    own_pair = row_par == pair_par

    for c in range(LANE_CHUNKS):
        uc = u2[:, c * V7X_LANES:(c + 1) * V7X_LANES]
        u2_scr[c] = uc
        bu_scr[c] = _dot(jnp.where(own_pair, uc, 0.0).astype(bf16), wb_ref[c])

    half = LANE_CHUNKS // 2
    for c0 in (0, half):
        chains = [(c, s) for c in range(c0, c0 + half) for s in range(SLOTS)]

        def cols(s):
            base = s * 2 * V7X_LANES
            return slice(base, base + V7X_LANES), slice(base + V7X_LANES, base + 2 * V7X_LANES)

        lams = [(lam_ref[c, :, cols(s)[0]], lam_ref[c, :, cols(s)[1]]) for c, s in chains]
        init = tuple((st_scr[c, :, cols(s)[0]], st_scr[c, :, cols(s)[1]]) for c, s in chains)

        def body(t, carry, chains=chains, lams=lams, cols=cols):
            rows = pl.ds(pl.multiple_of(t * V7X_SUBLANES, V7X_SUBLANES), V7X_SUBLANES)
            out = []
            for (c, s), (l_re, l_im), (s_re, s_im) in zip(chains, lams, carry):
                cr, ci = cols(s)
                n_re = (l_re * s_re - l_im * s_im) + bu_scr[c, rows, cr]
                n_im = (l_re * s_im + l_im * s_re) + bu_scr[c, rows, ci]
                h_scr[c, rows, cr] = n_re
                h_scr[c, rows, ci] = n_im
                out.append((n_re, n_im))
            return tuple(out)

        fin = lax.fori_loop(0, SSM_T, body, init, unroll=2)
        for (c, s), (s_re, s_im) in zip(chains, fin):
            st_scr[c, :, cols(s)[0]] = s_re
            st_scr[c, :, cols(s)[1]] = s_im

    even = pl.ds(0, SSM_ROWS, stride=2)
    odd = pl.ds(1, SSM_ROWS, stride=2)
    for c in range(LANE_CHUNKS):
        cols_c = slice(c * V7X_LANES, (c + 1) * V7X_LANES)
        y2 = _dot(h_scr[c].astype(bf16), wc_ref[c])
        y2_scr[2 * c] = y2[:, :V7X_LANES]
        y2_scr[2 * c + 1] = y2[:, V7X_LANES:]
        yc = y2_scr.at[2 * c][even, :] + y2_scr.at[2 * c + 1][odd, :]
        yc = yc + d_ref[:, cols_c] * u2_scr.at[c][even, :]
        y_scr[:, cols_c] = _gelu_tanh(yc)
    y = y_scr[...]
    out = y * _sigmoid(_dot(y.astype(bf16), wglu_ref[...]))
    out_tb = (out * _rms_scale(out) * g_ref[...]).astype(bf16)
    o_ref[...] = _dot(gather_ref[...], out_tb).astype(bf16).reshape(BATCH, SSM_T, SSM_WIDTH)


def _ssm(u, spread, gather, wb, lam, wc, d, wglu, ssm_out_g):
    return pl.pallas_call(
        _ssm_kernel,
        grid=(SEQ // SSM_T,),
        in_specs=[
            pl.BlockSpec((BATCH, SSM_T, SSM_WIDTH), lambda i: (0, i, 0)),
            _const_spec((SSM_ROWS2, SSM_ROWS)),
            _const_spec((SSM_ROWS, SSM_ROWS)),
            _const_spec((LANE_CHUNKS, V7X_LANES, SLOT_COLS)),
            _const_spec((LANE_CHUNKS, V7X_SUBLANES, SLOT_COLS)),
            _const_spec((LANE_CHUNKS, SLOT_COLS, 2 * V7X_LANES)),
            _const_spec((1, SSM_WIDTH)),
            _const_spec((SSM_WIDTH, SSM_WIDTH)),
            _const_spec((1, SSM_WIDTH)),
        ],
        out_specs=pl.BlockSpec((BATCH, SSM_T, SSM_WIDTH), lambda i: (0, i, 0)),
        out_shape=jax.ShapeDtypeStruct((BATCH, SEQ, SSM_WIDTH), bf16),
        scratch_shapes=[
            pltpu.VMEM((LANE_CHUNKS, SSM_ROWS2, V7X_LANES), f32),
            pltpu.VMEM((LANE_CHUNKS, SSM_ROWS2, SLOT_COLS), f32),
            pltpu.VMEM((LANE_CHUNKS, SSM_ROWS2, SLOT_COLS), f32),
            pltpu.VMEM((2 * LANE_CHUNKS, SSM_ROWS2, V7X_LANES), f32),
            pltpu.VMEM((SSM_ROWS, SSM_WIDTH), f32),
            pltpu.VMEM((LANE_CHUNKS, V7X_SUBLANES, SLOT_COLS), f32),
        ],
        compiler_params=pltpu.CompilerParams(
            dimension_semantics=("arbitrary",), vmem_limit_bytes=VMEM_LIMIT),
        name="ssm",
    )(u, spread, gather, wb, lam, wc, d, wglu, ssm_out_g)


def _out_proj_kernel(a_ref, s_ref, x_ref, wa_ref, ws_ref, o_ref):
    o_ref[...] = x_ref[...] + _dot(a_ref[...], wa_ref[...]) + _dot(s_ref[...], ws_ref[...])


def _out_proj(a, s, x, wa, ws):
    nt = SEQ // IN_TM
    return pl.pallas_call(
        _out_proj_kernel,
        grid=(BATCH, nt),
        in_specs=[
            pl.BlockSpec((None, IN_TM, ATTN_WIDTH), lambda b, t: (b, t, 0)),
            pl.BlockSpec((None, IN_TM, SSM_WIDTH), lambda b, t: (b, t, 0)),
            pl.BlockSpec((None, IN_TM, D_MODEL), lambda b, t: (b, t, 0)),
            _const_spec((ATTN_WIDTH, D_MODEL)),
            _const_spec((SSM_WIDTH, D_MODEL)),
        ],
        out_specs=pl.BlockSpec((None, IN_TM, D_MODEL), lambda b, t: (b, t, 0)),
        out_shape=jax.ShapeDtypeStruct((BATCH, SEQ, D_MODEL), f32),
        compiler_params=pltpu.CompilerParams(
            dimension_semantics=("arbitrary", "arbitrary"), vmem_limit_bytes=VMEM_LIMIT),
        name="out_proj",
    )(a, s, x, wa, ws)


def _ffn_kernel(x_ref, g_ref, wg_ref, wu_ref, wd_ref, o_ref, h_scr):
    @pl.when(pl.program_id(1) == 0)
    def _():
        xf = x_ref[...]
        h_scr[...] = (xf * _rms_scale(xf) * g_ref[...]).astype(bf16)
        o_ref[...] = xf

    h = h_scr[...]
    gate = _dot(h, wg_ref[...])
    up = _dot(h, wu_ref[...])
    act = (gate * _sigmoid(gate) * up).astype(bf16)
    o_ref[...] += _dot(act, wd_ref[...])


def _ffn(x, ln2_g, wg, wu, wd):
    n_rows = BATCH * SEQ
    return pl.pallas_call(
        _ffn_kernel,
        grid=(n_rows // FFN_TM, FF_HIDDEN // FFN_TF),
        in_specs=[
            pl.BlockSpec((FFN_TM, D_MODEL), lambda i, f: (i, 0)),
            _const_spec((1, D_MODEL)),
            pl.BlockSpec((D_MODEL, FFN_TF), lambda i, f: (0, f)),
            pl.BlockSpec((D_MODEL, FFN_TF), lambda i, f: (0, f)),
            pl.BlockSpec((FFN_TF, D_MODEL), lambda i, f: (f, 0)),
        ],
        out_specs=pl.BlockSpec((FFN_TM, D_MODEL), lambda i, f: (i, 0)),
        out_shape=jax.ShapeDtypeStruct((n_rows, D_MODEL), f32),
        scratch_shapes=[pltpu.VMEM((FFN_TM, D_MODEL), bf16)],
        compiler_params=pltpu.CompilerParams(
            dimension_semantics=("arbitrary", "arbitrary"), vmem_limit_bytes=VMEM_LIMIT),
        name="ffn",
    )(x, ln2_g, wg, wu, wd)


def _t5_bucket(dist):
    n = np.maximum(dist, 0)
    max_exact = REL_BUCKETS // 2
    nf = np.maximum(n, 1).astype(np.float32)
    large = max_exact + (np.log(nf / max_exact) / math.log(REL_MAX_DISTANCE / max_exact)
                         * (REL_BUCKETS - max_exact)).astype(np.int32)
    large = np.minimum(large, REL_BUCKETS - 1)
    return np.where(n < max_exact, n, large).astype(np.int32)


def _bias_kernel(rb_ref, bucket_ref, valid_ref, o_ref):
    head = pl.program_id(1)
    bucket = bucket_ref[...]
    acc = jnp.zeros((BLOCK, 2 * BLOCK), f32)
    for k in range(REL_BUCKETS):
        acc = jnp.where(bucket == k, rb_ref[k, head], acc)
    o_ref[0, 0] = jnp.where(valid_ref[0] != 0, acc, MASK_VALUE)


def _bias_tables(rel_bias):
    qi = np.arange(BLOCK)[:, None]
    sj = np.arange(2 * BLOCK)[None, :]
    dist = qi + BLOCK - sj
    in_window = (dist >= 0) & (dist < WINDOW)
    valid = np.stack([in_window & (sj >= BLOCK), in_window]).astype(np.int32)
    return pl.pallas_call(
        _bias_kernel,
        grid=(2, N_Q_HEADS),
        in_specs=[
            pl.BlockSpec(memory_space=pltpu.SMEM),
            _const_spec((BLOCK, 2 * BLOCK)),
            pl.BlockSpec((1, BLOCK, 2 * BLOCK), lambda t, h: (t, 0, 0)),
        ],
        out_specs=pl.BlockSpec((1, 1, BLOCK, 2 * BLOCK), lambda t, h: (t, h, 0, 0)),
        out_shape=jax.ShapeDtypeStruct((2, N_Q_HEADS, BLOCK, 2 * BLOCK), f32),
        name="bias_table",
    )(rel_bias.astype(f32), jnp.asarray(_t5_bucket(dist)), jnp.asarray(valid))


def _ssm_params(a_re, a_im, log_dt, b_re, b_im, c_re, c_im):
    dt = jnp.exp(log_dt)[:, None]
    mag = jnp.exp(a_re * dt)
    ang = a_im * dt
    lb_re, lb_im = mag * jnp.cos(ang), mag * jnp.sin(ang)
    nr, ni = lb_re - 1.0, lb_im
    den = a_re * a_re + a_im * a_im
    f_re = (nr * a_re + ni * a_im) / den
    f_im = (ni * a_re - nr * a_im) / den
    bb_re = f_re[..., None] * b_re - f_im[..., None] * b_im
    bb_im = f_re[..., None] * b_im + f_im[..., None] * b_re

    eye = np.eye(2, dtype=np.float32)
    split = lambda t: t.reshape((2, LANE_CHUNKS, SLOTS, 2, 2) + t.shape[2:])
    bbs = split(jnp.stack([bb_re, bb_im]))
    wb = jnp.einsum('us,gh,acsjgnp->cujgpsahn', eye, eye, bbs).reshape(LANE_CHUNKS, V7X_LANES, SLOT_COLS)
    cs = split(jnp.stack([c_re, -c_im]))
    wc = jnp.einsum('vj,us,gh,acsjgpn->csagnvujhp', eye, eye, eye, cs).reshape(
        LANE_CHUNKS, SLOT_COLS, 2 * V7X_LANES)
    lam = jnp.transpose(split(jnp.stack([lb_re, lb_im])), (1, 3, 2, 0, 4, 5)).reshape(LANE_CHUNKS, 1, 2, SLOT_COLS)
    lam = jnp.broadcast_to(lam, (LANE_CHUNKS, BATCH, 2, SLOT_COLS)).reshape(LANE_CHUNKS, V7X_SUBLANES, SLOT_COLS)
    return wb.astype(bf16), wc.astype(bf16), lam


def _layer(x, rel_bias, ln1_g, w_in, q_norm_g, k_norm_g, attn_sinks, ssm_a_re, ssm_a_im,
           ssm_log_dt, ssm_b_re, ssm_b_im, ssm_c_re, ssm_c_im, ssm_d, w_glu,
           attn_out_g, ssm_out_g, w_out, ln2_g, w_ff_gate, w_ff_up, w_ff_down):
    row = lambda v: v.reshape(1, -1).astype(f32)

    lanes = np.arange(KV_SLAB)
    seg = jnp.asarray(lanes[:, None] // HEAD_DIM == lanes[None, :] // HEAD_DIM, bf16)
    src = np.arange(KV_WIDTH)
    dst = np.arange(N_KV_HEADS * KV_SLAB)
    rep = jnp.asarray((src[:, None] // HEAD_DIM == dst[None, :] // KV_SLAB)
                      & (src[:, None] % HEAD_DIM == dst[None, :] % HEAD_DIM), bf16)
    gqk = row(jnp.concatenate([jnp.tile(q_norm_g, N_Q_HEADS) * (HEAD_DIM ** -0.5),
                               jnp.tile(k_norm_g, N_KV_HEADS)]))
    qmask = jnp.asarray(np.broadcast_to(
        (lanes[None, None, :] // HEAD_DIM) == np.arange(Q_PER_KV)[:, None, None],
        (Q_PER_KV, BLOCK, KV_SLAB)), bf16)

    q, k_rep, v_rep, u = _in_proj(x, row(ln1_g), w_in.astype(bf16), seg, rep, gqk)

    y_attn = _attention(attn_sinks.astype(f32), q, k_rep, v_rep, _bias_tables(rel_bias), qmask,
                        row(attn_out_g))

    wb, wc, lam = _ssm_params(ssm_a_re.astype(f32), ssm_a_im.astype(f32), ssm_log_dt.astype(f32),
                                     ssm_b_re.astype(f32), ssm_b_im.astype(f32),
                                     ssm_c_re.astype(f32), ssm_c_im.astype(f32))
    r_bt = np.arange(SSM_ROWS)
    r_tb = (r_bt % SSM_T) * BATCH + r_bt // SSM_T
    gather = r_tb[:, None] == np.arange(SSM_ROWS)[None, :]
    spread = np.arange(SSM_ROWS2)[:, None] // 2 == r_tb[None, :]
    y_ssm = _ssm(u, jnp.asarray(spread, bf16), jnp.asarray(gather, bf16), wb, lam, wc, row(ssm_d),
                 w_glu.astype(bf16), row(ssm_out_g))

    x1 = _out_proj(y_attn, y_ssm, x,
                   w_out[:ATTN_WIDTH].astype(bf16), w_out[ATTN_WIDTH:].astype(bf16))

    out = _ffn(x1.reshape(BATCH * SEQ, D_MODEL), row(ln2_g), w_ff_gate.astype(bf16),
               w_ff_up.astype(bf16), w_ff_down.astype(bf16))
    return out.reshape(BATCH, SEQ, D_MODEL)


def kernel(x, rel_bias, ln1_g, w_in, q_norm_g, k_norm_g, attn_sinks, ssm_a_re, ssm_a_im, ssm_log_dt, ssm_b_re, ssm_b_im, ssm_c_re, ssm_c_im, ssm_d, w_glu, attn_out_g, ssm_out_g, w_out, ln2_g, w_ff_gate, w_ff_up, w_ff_down):
    for l in range(ln1_g.shape[0]):
        x = _layer(x, rel_bias, ln1_g[l], w_in[l], q_norm_g[l], k_norm_g[l], attn_sinks[l],
                   ssm_a_re[l], ssm_a_im[l], ssm_log_dt[l], ssm_b_re[l], ssm_b_im[l],
                   ssm_c_re[l], ssm_c_im[l], ssm_d[l], w_glu[l], attn_out_g[l], ssm_out_g[l],
                   w_out[l], ln2_g[l], w_ff_gate[l], w_ff_up[l], w_ff_down[l])
    return x
```

```python
import functools
import math

import jax
import jax.numpy as jnp
import numpy as np
from jax import lax
from jax.experimental import pallas as pl
from jax.experimental.pallas import tpu as pltpu

D_MODEL = 2048
BATCH = 4
SEQ = 4096
HEAD_DIM = 64
N_Q_HEADS = 16
N_KV_HEADS = 4
Q_PER_KV = N_Q_HEADS // N_KV_HEADS
ATTN_WIDTH = N_Q_HEADS * HEAD_DIM
KV_WIDTH = N_KV_HEADS * HEAD_DIM
WINDOW = 128
BLOCK = 128
SSM_WIDTH = D_MODEL - ATTN_WIDTH
IN_WIDTH = ATTN_WIDTH + 2 * KV_WIDTH + SSM_WIDTH
SSM_GROUP = 16
SSM_GROUPS = SSM_WIDTH // SSM_GROUP
SSM_STATE = 64
FF_HIDDEN = 5632
REL_BUCKETS = 32
REL_MAX_DISTANCE = 128
EPS = 1e-6

V7X_LANES = 128
V7X_SUBLANES = 8
V7X_VMEM_BYTES = 64 * 1024 * 1024
VMEM_LIMIT = V7X_VMEM_BYTES - 8 * 1024 * 1024

MASK_VALUE = -1e30
KV_SLAB = Q_PER_KV * HEAD_DIM
LANE_CHUNKS = SSM_WIDTH // V7X_LANES
SLOTS = 2
SLOT_COLS = SLOTS * 2 * V7X_LANES

IN_TM = 512
FFN_TM = 512
FFN_TF = 512
SSM_T = 64
SSM_ROWS = SSM_T * BATCH
SSM_ROWS2 = 2 * SSM_ROWS

f32 = jnp.float32
bf16 = jnp.bfloat16


def _dot(a, b):
    return jnp.dot(a, b, preferred_element_type=f32)


def _rms_scale(x):
    return lax.rsqrt(jnp.mean(x * x, axis=-1, keepdims=True) + EPS)


def _in_proj_kernel(x_ref, g_ref, w_ref, seg_ref, rep_ref, gqk_ref, q_ref, k_ref, v_ref, u_ref, proj_scr):
    xf = x_ref[...]
    h = (xf * _rms_scale(xf) * g_ref[...]).astype(bf16)
    proj_scr[...] = _dot(h, w_ref[...])

    n_slabs = (ATTN_WIDTH + KV_WIDTH) // KV_SLAB
    slab = lambda c: proj_scr[:, c * KV_SLAB:(c + 1) * KV_SLAB]
    sq = jnp.concatenate([slab(c) * slab(c) for c in range(n_slabs)], axis=0)
    hi = sq.astype(bf16)
    lo = (sq - hi.astype(f32)).astype(bf16)
    ss = _dot(jnp.concatenate([hi, lo], axis=0), seg_ref[...])
    ss = ss[:n_slabs * IN_TM] + ss[n_slabs * IN_TM:]
    normed = []
    for c in range(n_slabs):
        cols = slice(c * KV_SLAB, (c + 1) * KV_SLAB)
        scale = lax.rsqrt(ss[c * IN_TM:(c + 1) * IN_TM] * (1.0 / HEAD_DIM) + EPS)
        normed.append((slab(c) * scale * gqk_ref[:, cols]).astype(bf16))
    for c in range(n_slabs - 1):
        q_ref[:, c * KV_SLAB:(c + 1) * KV_SLAB] = normed[c]
    v_cols = slice(ATTN_WIDTH + KV_WIDTH, ATTN_WIDTH + 2 * KV_WIDTH)
    kv = _dot(jnp.concatenate([normed[-1], proj_scr[:, v_cols].astype(bf16)], axis=0), rep_ref[...])
    k_ref[...] = kv[:IN_TM].astype(bf16)
    v_ref[...] = kv[IN_TM:].astype(bf16)
    u_ref[...] = proj_scr[:, ATTN_WIDTH + 2 * KV_WIDTH:].astype(bf16)


def _const_spec(shape):
    nd = len(shape)
    return pl.BlockSpec(shape, lambda *_: (0,) * nd)


def _in_proj(x, ln1_g, w, seg, rep, gqk):
    nt = SEQ // IN_TM
    row_spec = lambda w: pl.BlockSpec((None, IN_TM, w), lambda b, t: (b, t, 0))
    return pl.pallas_call(
        _in_proj_kernel,
        grid=(BATCH, nt),
        in_specs=[
            row_spec(D_MODEL),
            _const_spec((1, D_MODEL)),
            _const_spec((D_MODEL, IN_WIDTH)),
            _const_spec((KV_SLAB, KV_SLAB)),
            _const_spec((KV_WIDTH, N_KV_HEADS * KV_SLAB)),
            _const_spec((1, ATTN_WIDTH + KV_WIDTH)),
        ],
        out_specs=[
            row_spec(ATTN_WIDTH),
            row_spec(N_KV_HEADS * KV_SLAB),
            row_spec(N_KV_HEADS * KV_SLAB),
            row_spec(SSM_WIDTH),
        ],
        out_shape=[
            jax.ShapeDtypeStruct((BATCH, SEQ, ATTN_WIDTH), bf16),
            jax.ShapeDtypeStruct((BATCH, SEQ, N_KV_HEADS * KV_SLAB), bf16),
            jax.ShapeDtypeStruct((BATCH, SEQ, N_KV_HEADS * KV_SLAB), bf16),
            jax.ShapeDtypeStruct((BATCH, SEQ, SSM_WIDTH), bf16),
        ],
        scratch_shapes=[pltpu.VMEM((IN_TM, IN_WIDTH), f32)],
        compiler_params=pltpu.CompilerParams(
            dimension_semantics=("arbitrary", "arbitrary"), vmem_limit_bytes=VMEM_LIMIT),
        name="in_proj",
    )(x, ln1_g, w, seg, rep, gqk)


def _attn_kernel(sink_ref, q_ref, kc_ref, kp_ref, vc_ref, vp_ref, bias_ref, qmask_ref, g_ref,
                 o_ref, y_scr):
    lane_head = lax.broadcasted_iota(jnp.int32, (BLOCK, KV_SLAB), 1) // HEAD_DIM
    for kh in range(N_KV_HEADS):
        cols = slice(kh * KV_SLAB, (kh + 1) * KV_SLAB)
        qs = q_ref[:, cols]
        q4 = jnp.concatenate([qs * qmask_ref[g] for g in range(Q_PER_KV)], axis=0)
        keys = jnp.concatenate([kp_ref[:, cols], kc_ref[:, cols]], axis=0)
        vals = jnp.concatenate([vp_ref[:, cols], vc_ref[:, cols]], axis=0)
        s4 = lax.dot_general(q4, keys, (((1,), (1,)), ((), ())), preferred_element_type=f32)
        probs, inv_den = [], []
        for g in range(Q_PER_KV):
            head = kh * Q_PER_KV + g
            s = s4[g * BLOCK:(g + 1) * BLOCK] + bias_ref[0, head]
            sink = sink_ref[head]
            m = jnp.maximum(jnp.max(s, axis=-1, keepdims=True), sink)
            p = jnp.exp(s - m)
            den = jnp.sum(p, axis=-1, keepdims=True) + jnp.exp(sink - m)
            probs.append(p.astype(bf16))
            inv_den.append(1.0 / den)
        o4 = _dot(jnp.concatenate(probs, axis=0), vals)
        acc = jnp.zeros((BLOCK, KV_SLAB), f32)
        for g in range(Q_PER_KV):
            acc = acc + jnp.where(lane_head == g, o4[g * BLOCK:(g + 1) * BLOCK] * inv_den[g], 0.0)
        y_scr[:, cols] = acc
    y = y_scr[...]
    o_ref[...] = (y * _rms_scale(y) * g_ref[...]).astype(bf16)


def _attention(sinks, q, k_rep, v_rep, bias2, qmask, attn_out_g):
    nb = SEQ // BLOCK
    w = N_KV_HEADS * KV_SLAB
    cur = lambda b, n: (b, n, 0)
    prev = lambda b, n: (b, jnp.maximum(n - 1, 0), 0)
    blk = lambda im: pl.BlockSpec((None, BLOCK, w), im)
    return pl.pallas_call(
        _attn_kernel,
        grid=(BATCH, nb),
        in_specs=[
            pl.BlockSpec(memory_space=pltpu.SMEM),
            blk(cur), blk(cur), blk(prev), blk(cur), blk(prev),
            pl.BlockSpec((1, N_Q_HEADS, BLOCK, 2 * BLOCK), lambda b, n: (jnp.minimum(n, 1), 0, 0, 0)),
            _const_spec((Q_PER_KV, BLOCK, KV_SLAB)),
            _const_spec((1, ATTN_WIDTH)),
        ],
        out_specs=blk(cur),
        out_shape=jax.ShapeDtypeStruct((BATCH, SEQ, ATTN_WIDTH), bf16),
        scratch_shapes=[pltpu.VMEM((BLOCK, ATTN_WIDTH), f32)],
        compiler_params=pltpu.CompilerParams(
            dimension_semantics=("arbitrary", "arbitrary"), vmem_limit_bytes=VMEM_LIMIT),
        name="attn",
    )(sinks, q, k_rep, k_rep, v_rep, v_rep, bias2, qmask, attn_out_g)


def _gelu_tanh(x):
    return 0.5 * x * (1.0 + jnp.tanh(math.sqrt(2.0 / math.pi) * (x + 0.044715 * (x * x * x))))


def _sigmoid(x):
    return 1.0 / (1.0 + jnp.exp(-x))


def _ssm_kernel(u_ref, spread_ref, gather_ref, wb_ref, lam_ref, wc_ref, d_ref, wglu_ref, g_ref,
                o_ref, u2_scr, bu_scr, h_scr, y2_scr, y_scr, st_scr):
    @pl.when(pl.program_id(0) == 0)
    def _():
        st_scr[...] = jnp.zeros_like(st_scr)

    u_bt = u_ref[...].reshape(SSM_ROWS, SSM_WIDTH)
    u2 = _dot(spread_ref[...], u_bt)
    row_par = lax.broadcasted_iota(jnp.int32, (SSM_ROWS2, V7X_LANES), 0) % 2
    pair_par = (lax.broadcasted_iota(jnp.int32, (SSM_ROWS2, V7X_LANES), 1) // (2 * SSM_GROUP)) % 2
    own_pair = row_par == pair_par

    for c in range(LANE_CHUNKS):
        uc = u2[:, c * V7X_LANES:(c + 1) * V7X_LANES]
        u2_scr[c] = uc
        bu_scr[c] = _dot(jnp.where(own_pair, uc, 0.0).astype(bf16), wb_ref[c])

    half = LANE_CHUNKS // 2
    for c0 in (0, half):
        chains = [(c, s) for c in range(c0, c0 + half) for s in range(SLOTS)]

        def cols(s):
            base = s * 2 * V7X_LANES
            return slice(base, base + V7X_LANES), slice(base + V7X_LANES, base + 2 * V7X_LANES)

        lams = [(lam_ref[c, :, cols(s)[0]], lam_ref[c, :, cols(s)[1]]) for c, s in chains]
        init = tuple((st_scr[c, :, cols(s)[0]], st_scr[c, :, cols(s)[1]]) for c, s in chains)

        def body(t, carry, chains=chains, lams=lams, cols=cols):
            rows = pl.ds(t * V7X_SUBLANES, V7X_SUBLANES)
            out = []
            for (c, s), (l_re, l_im), (s_re, s_im) in zip(chains, lams, carry):
                cr, ci = cols(s)
                n_re = (l_re * s_re - l_im * s_im) + bu_scr[c, rows, cr]
                n_im = (l_re * s_im + l_im * s_re) + bu_scr[c, rows, ci]
                h_scr[c, rows, cr] = n_re
                h_scr[c, rows, ci] = n_im
                out.append((n_re, n_im))
            return tuple(out)

        fin = init
        for t in range(SSM_T):
            fin = body(t, fin)
        for (c, s), (s_re, s_im) in zip(chains, fin):
            st_scr[c, :, cols(s)[0]] = s_re
            st_scr[c, :, cols(s)[1]] = s_im

    even = pl.ds(0, SSM_ROWS, stride=2)
    odd = pl.ds(1, SSM_ROWS, stride=2)
    for c in range(LANE_CHUNKS):
        cols_c = slice(c * V7X_LANES, (c + 1) * V7X_LANES)
        y2 = _dot(h_scr[c].astype(bf16), wc_ref[c])
        y2_scr[2 * c] = y2[:, :V7X_LANES]
        y2_scr[2 * c + 1] = y2[:, V7X_LANES:]
        yc = y2_scr.at[2 * c][even, :] + y2_scr.at[2 * c + 1][odd, :]
        yc = yc + d_ref[:, cols_c] * u2_scr.at[c][even, :]
        y_scr[:, cols_c] = _gelu_tanh(yc)
    y = y_scr[...]
    out = y * _sigmoid(_dot(y.astype(bf16), wglu_ref[...]))
    out_tb = (out * _rms_scale(out) * g_ref[...]).astype(bf16)
    o_ref[...] = _dot(gather_ref[...], out_tb).astype(bf16).reshape(BATCH, SSM_T, SSM_WIDTH)


def _ssm(u, spread, gather, wb, lam, wc, d, wglu, ssm_out_g):
    return pl.pallas_call(
        _ssm_kernel,
        grid=(SEQ // SSM_T,),
        in_specs=[
            pl.BlockSpec((BATCH, SSM_T, SSM_WIDTH), lambda i: (0, i, 0)),
            _const_spec((SSM_ROWS2, SSM_ROWS)),
            _const_spec((SSM_ROWS, SSM_ROWS)),
            _const_spec((LANE_CHUNKS, V7X_LANES, SLOT_COLS)),
            _const_spec((LANE_CHUNKS, V7X_SUBLANES, SLOT_COLS)),
            _const_spec((LANE_CHUNKS, SLOT_COLS, 2 * V7X_LANES)),
            _const_spec((1, SSM_WIDTH)),
            _const_spec((SSM_WIDTH, SSM_WIDTH)),
            _const_spec((1, SSM_WIDTH)),
        ],
        out_specs=pl.BlockSpec((BATCH, SSM_T, SSM_WIDTH), lambda i: (0, i, 0)),
        out_shape=jax.ShapeDtypeStruct((BATCH, SEQ, SSM_WIDTH), bf16),
        scratch_shapes=[
            pltpu.VMEM((LANE_CHUNKS, SSM_ROWS2, V7X_LANES), f32),
            pltpu.VMEM((LANE_CHUNKS, SSM_ROWS2, SLOT_COLS), f32),
            pltpu.VMEM((LANE_CHUNKS, SSM_ROWS2, SLOT_COLS), f32),
            pltpu.VMEM((2 * LANE_CHUNKS, SSM_ROWS2, V7X_LANES), f32),
            pltpu.VMEM((SSM_ROWS, SSM_WIDTH), f32),
            pltpu.VMEM((LANE_CHUNKS, V7X_SUBLANES, SLOT_COLS), f32),
        ],
        compiler_params=pltpu.CompilerParams(
            dimension_semantics=("arbitrary",), vmem_limit_bytes=VMEM_LIMIT),
        name="ssm",
    )(u, spread, gather, wb, lam, wc, d, wglu, ssm_out_g)


def _out_proj_kernel(a_ref, s_ref, x_ref, wa_ref, ws_ref, o_ref):
    o_ref[...] = x_ref[...] + _dot(a_ref[...], wa_ref[...]) + _dot(s_ref[...], ws_ref[...])


def _out_proj(a, s, x, wa, ws):
    nt = SEQ // IN_TM
    return pl.pallas_call(
        _out_proj_kernel,
        grid=(BATCH, nt),
        in_specs=[
            pl.BlockSpec((None, IN_TM, ATTN_WIDTH), lambda b, t: (b, t, 0)),
            pl.BlockSpec((None, IN_TM, SSM_WIDTH), lambda b, t: (b, t, 0)),
            pl.BlockSpec((None, IN_TM, D_MODEL), lambda b, t: (b, t, 0)),
            _const_spec((ATTN_WIDTH, D_MODEL)),
            _const_spec((SSM_WIDTH, D_MODEL)),
        ],
        out_specs=pl.BlockSpec((None, IN_TM, D_MODEL), lambda b, t: (b, t, 0)),
        out_shape=jax.ShapeDtypeStruct((BATCH, SEQ, D_MODEL), f32),
        compiler_params=pltpu.CompilerParams(
            dimension_semantics=("arbitrary", "arbitrary"), vmem_limit_bytes=VMEM_LIMIT),
        name="out_proj",
    )(a, s, x, wa, ws)


def _ffn_kernel(x_ref, g_ref, wg_ref, wu_ref, wd_ref, o_ref, h_scr):
    @pl.when(pl.program_id(1) == 0)
    def _():
        xf = x_ref[...]
        h_scr[...] = (xf * _rms_scale(xf) * g_ref[...]).astype(bf16)
        o_ref[...] = xf

    h = h_scr[...]
    gate = _dot(h, wg_ref[...])
    up = _dot(h, wu_ref[...])
    act = (gate * _sigmoid(gate) * up).astype(bf16)
    o_ref[...] += _dot(act, wd_ref[...])


def _ffn(x, ln2_g, wg, wu, wd):
    n_rows = BATCH * SEQ
    return pl.pallas_call(
        _ffn_kernel,
        grid=(n_rows // FFN_TM, FF_HIDDEN // FFN_TF),
        in_specs=[
            pl.BlockSpec((FFN_TM, D_MODEL), lambda i, f: (i, 0)),
            _const_spec((1, D_MODEL)),
            pl.BlockSpec((D_MODEL, FFN_TF), lambda i, f: (0, f)),
            pl.BlockSpec((D_MODEL, FFN_TF), lambda i, f: (0, f)),
            pl.BlockSpec((FFN_TF, D_MODEL), lambda i, f: (f, 0)),
        ],
        out_specs=pl.BlockSpec((FFN_TM, D_MODEL), lambda i, f: (i, 0)),
        out_shape=jax.ShapeDtypeStruct((n_rows, D_MODEL), f32),
        scratch_shapes=[pltpu.VMEM((FFN_TM, D_MODEL), bf16)],
        compiler_params=pltpu.CompilerParams(
            dimension_semantics=("arbitrary", "arbitrary"), vmem_limit_bytes=VMEM_LIMIT),
        name="ffn",
    )(x, ln2_g, wg, wu, wd)


def _t5_bucket(dist):
    n = np.maximum(dist, 0)
    max_exact = REL_BUCKETS // 2
    nf = np.maximum(n, 1).astype(np.float32)
    large = max_exact + (np.log(nf / max_exact) / math.log(REL_MAX_DISTANCE / max_exact)
                         * (REL_BUCKETS - max_exact)).astype(np.int32)
    large = np.minimum(large, REL_BUCKETS - 1)
    return np.where(n < max_exact, n, large).astype(np.int32)


def _bias_kernel(rb_ref, bucket_ref, valid_ref, o_ref):
    head = pl.program_id(1)
    bucket = bucket_ref[...]
    acc = jnp.zeros((BLOCK, 2 * BLOCK), f32)
    for k in range(REL_BUCKETS):
        acc = jnp.where(bucket == k, rb_ref[k, head], acc)
    o_ref[0, 0] = jnp.where(valid_ref[0] != 0, acc, MASK_VALUE)


def _bias_tables(rel_bias):
    qi = np.arange(BLOCK)[:, None]
    sj = np.arange(2 * BLOCK)[None, :]
    dist = qi + BLOCK - sj
    in_window = (dist >= 0) & (dist < WINDOW)
    valid = np.stack([in_window & (sj >= BLOCK), in_window]).astype(np.int32)
    return pl.pallas_call(
        _bias_kernel,
        grid=(2, N_Q_HEADS),
        in_specs=[
            pl.BlockSpec(memory_space=pltpu.SMEM),
            _const_spec((BLOCK, 2 * BLOCK)),
            pl.BlockSpec((1, BLOCK, 2 * BLOCK), lambda t, h: (t, 0, 0)),
        ],
        out_specs=pl.BlockSpec((1, 1, BLOCK, 2 * BLOCK), lambda t, h: (t, h, 0, 0)),
        out_shape=jax.ShapeDtypeStruct((2, N_Q_HEADS, BLOCK, 2 * BLOCK), f32),
        name="bias_table",
    )(rel_bias.astype(f32), jnp.asarray(_t5_bucket(dist)), jnp.asarray(valid))


def _ssm_params(a_re, a_im, log_dt, b_re, b_im, c_re, c_im):
    dt = jnp.exp(log_dt)[:, None]
    mag = jnp.exp(a_re * dt)
    ang = a_im * dt
    lb_re, lb_im = mag * jnp.cos(ang), mag * jnp.sin(ang)
    nr, ni = lb_re - 1.0, lb_im
    den = a_re * a_re + a_im * a_im
    f_re = (nr * a_re + ni * a_im) / den
    f_im = (ni * a_re - nr * a_im) / den
    bb_re = f_re[..., None] * b_re - f_im[..., None] * b_im
    bb_im = f_re[..., None] * b_im + f_im[..., None] * b_re

    eye = np.eye(2, dtype=np.float32)
    split = lambda t: t.reshape((2, LANE_CHUNKS, SLOTS, 2, 2) + t.shape[2:])
    bbs = split(jnp.stack([bb_re, bb_im]))
    wb = jnp.einsum('us,gh,acsjgnp->cujgpsahn', eye, eye, bbs).reshape(LANE_CHUNKS, V7X_LANES, SLOT_COLS)
    cs = split(jnp.stack([c_re, -c_im]))
    wc = jnp.einsum('vj,us,gh,acsjgpn->csagnvujhp', eye, eye, eye, cs).reshape(
        LANE_CHUNKS, SLOT_COLS, 2 * V7X_LANES)
    lam = jnp.transpose(split(jnp.stack([lb_re, lb_im])), (1, 3, 2, 0, 4, 5)).reshape(LANE_CHUNKS, 1, 2, SLOT_COLS)
    lam = jnp.broadcast_to(lam, (LANE_CHUNKS, BATCH, 2, SLOT_COLS)).reshape(LANE_CHUNKS, V7X_SUBLANES, SLOT_COLS)
    return wb.astype(bf16), wc.astype(bf16), lam


def _layer(x, rel_bias, ln1_g, w_in, q_norm_g, k_norm_g, attn_sinks, ssm_a_re, ssm_a_im,
           ssm_log_dt, ssm_b_re, ssm_b_im, ssm_c_re, ssm_c_im, ssm_d, w_glu,
           attn_out_g, ssm_out_g, w_out, ln2_g, w_ff_gate, w_ff_up, w_ff_down):
    row = lambda v: v.reshape(1, -1).astype(f32)

    lanes = np.arange(KV_SLAB)
    seg = jnp.asarray(lanes[:, None] // HEAD_DIM == lanes[None, :] // HEAD_DIM, bf16)
    src = np.arange(KV_WIDTH)
    dst = np.arange(N_KV_HEADS * KV_SLAB)
    rep = jnp.asarray((src[:, None] // HEAD_DIM == dst[None, :] // KV_SLAB)
                      & (src[:, None] % HEAD_DIM == dst[None, :] % HEAD_DIM), bf16)
    gqk = row(jnp.concatenate([jnp.tile(q_norm_g, N_Q_HEADS) * (HEAD_DIM ** -0.5),
                               jnp.tile(k_norm_g, N_KV_HEADS)]))
    qmask = jnp.asarray(np.broadcast_to(
        (lanes[None, None, :] // HEAD_DIM) == np.arange(Q_PER_KV)[:, None, None],
        (Q_PER_KV, BLOCK, KV_SLAB)), bf16)

    q, k_rep, v_rep, u = _in_proj(x, row(ln1_g), w_in.astype(bf16), seg, rep, gqk)

    y_attn = _attention(attn_sinks.astype(f32), q, k_rep, v_rep, _bias_tables(rel_bias), qmask,
                        row(attn_out_g))

    wb, wc, lam = _ssm_params(ssm_a_re.astype(f32), ssm_a_im.astype(f32), ssm_log_dt.astype(f32),
                                     ssm_b_re.astype(f32), ssm_b_im.astype(f32),
                                     ssm_c_re.astype(f32), ssm_c_im.astype(f32))
    r_bt = np.arange(SSM_ROWS)
    r_tb = (r_bt % SSM_T) * BATCH + r_bt // SSM_T
    gather = r_tb[:, None] == np.arange(SSM_ROWS)[None, :]
    spread = np.arange(SSM_ROWS2)[:, None] // 2 == r_tb[None, :]
    y_ssm = _ssm(u, jnp.asarray(spread, bf16), jnp.asarray(gather, bf16), wb, lam, wc, row(ssm_d),
                 w_glu.astype(bf16), row(ssm_out_g))

    x1 = _out_proj(y_attn, y_ssm, x,
                   w_out[:ATTN_WIDTH].astype(bf16), w_out[ATTN_WIDTH:].astype(bf16))

    out = _ffn(x1.reshape(BATCH * SEQ, D_MODEL), row(ln2_g), w_ff_gate.astype(bf16),
               w_ff_up.astype(bf16), w_ff_down.astype(bf16))
    return out.reshape(BATCH, SEQ, D_MODEL)


def kernel(x, rel_bias, ln1_g, w_in, q_norm_g, k_norm_g, attn_sinks, ssm_a_re, ssm_a_im, ssm_log_dt, ssm_b_re, ssm_b_im, ssm_c_re, ssm_c_im, ssm_d, w_glu, attn_out_g, ssm_out_g, w_out, ln2_g, w_ff_gate, w_ff_up, w_ff_down):
    for l in range(ln1_g.shape[0]):
        x = _layer(x, rel_bias, ln1_g[l], w_in[l], q_norm_g[l], k_norm_g[l], attn_sinks[l],
                   ssm_a_re[l], ssm_a_im[l], ssm_log_dt[l], ssm_b_re[l], ssm_b_im[l],
                   ssm_c_re[l], ssm_c_im[l], ssm_d[l], w_glu[l], attn_out_g[l], ssm_out_g[l],
                   w_out[l], ln2_g[l], w_ff_gate[l], w_ff_up[l], w_ff_down[l])
    return x
```

```python
import functools
import math

import jax
import jax.numpy as jnp
import numpy as np
from jax import lax
from jax.experimental import pallas as pl
from jax.experimental.pallas import tpu as pltpu

D_MODEL = 2048
BATCH = 4
SEQ = 4096
HEAD_DIM = 64
N_Q_HEADS = 16
N_KV_HEADS = 4
Q_PER_KV = N_Q_HEADS // N_KV_HEADS
ATTN_WIDTH = N_Q_HEADS * HEAD_DIM
KV_WIDTH = N_KV_HEADS * HEAD_DIM
WINDOW = 128
BLOCK = 128
SSM_WIDTH = D_MODEL - ATTN_WIDTH
IN_WIDTH = ATTN_WIDTH + 2 * KV_WIDTH + SSM_WIDTH
SSM_GROUP = 16
SSM_GROUPS = SSM_WIDTH // SSM_GROUP
SSM_STATE = 64
FF_HIDDEN = 5632
REL_BUCKETS = 32
REL_MAX_DISTANCE = 128
EPS = 1e-6

V7X_LANES = 128
V7X_SUBLANES = 8
V7X_VMEM_BYTES = 64 * 1024 * 1024
VMEM_LIMIT = V7X_VMEM_BYTES - 8 * 1024 * 1024

MASK_VALUE = -1e30
KV_SLAB = Q_PER_KV * HEAD_DIM
LANE_CHUNKS = SSM_WIDTH // V7X_LANES
SLOTS = 2
SLOT_COLS = SLOTS * 2 * V7X_LANES

IN_TM = 512
FFN_TM = 1024
FFN_TF = 512
SSM_T = 64
SSM_ROWS = SSM_T * BATCH
SSM_ROWS2 = 2 * SSM_ROWS

f32 = jnp.float32
bf16 = jnp.bfloat16


def _dot(a, b):
    return jnp.dot(a, b, preferred_element_type=f32)


def _rms_scale(x):
    return lax.rsqrt(jnp.mean(x * x, axis=-1, keepdims=True) + EPS)


def _in_proj_kernel(x_ref, g_ref, w_ref, seg_ref, rep_ref, gqk_ref, q_ref, k_ref, v_ref, u_ref, proj_scr):
    xf = x_ref[...]
    h = (xf * _rms_scale(xf) * g_ref[...]).astype(bf16)
    proj_scr[...] = _dot(h, w_ref[...])

    n_slabs = (ATTN_WIDTH + KV_WIDTH) // KV_SLAB
    slab = lambda c: proj_scr[:, c * KV_SLAB:(c + 1) * KV_SLAB]
    sq = jnp.concatenate([slab(c) * slab(c) for c in range(n_slabs)], axis=0)
    hi = sq.astype(bf16)
    lo = (sq - hi.astype(f32)).astype(bf16)
    ss = _dot(jnp.concatenate([hi, lo], axis=0), seg_ref[...])
    ss = ss[:n_slabs * IN_TM] + ss[n_slabs * IN_TM:]
    normed = []
    for c in range(n_slabs):
        cols = slice(c * KV_SLAB, (c + 1) * KV_SLAB)
        scale = lax.rsqrt(ss[c * IN_TM:(c + 1) * IN_TM] * (1.0 / HEAD_DIM) + EPS)
        normed.append((slab(c) * scale * gqk_ref[:, cols]).astype(bf16))
    for c in range(n_slabs - 1):
        q_ref[:, c * KV_SLAB:(c + 1) * KV_SLAB] = normed[c]
    v_cols = slice(ATTN_WIDTH + KV_WIDTH, ATTN_WIDTH + 2 * KV_WIDTH)
    kv = _dot(jnp.concatenate([normed[-1], proj_scr[:, v_cols].astype(bf16)], axis=0), rep_ref[...])
    k_ref[...] = kv[:IN_TM].astype(bf16)
    v_ref[...] = kv[IN_TM:].astype(bf16)
    u_ref[...] = proj_scr[:, ATTN_WIDTH + 2 * KV_WIDTH:].astype(bf16)


def _const_spec(shape):
    nd = len(shape)
    return pl.BlockSpec(shape, lambda *_: (0,) * nd)


def _in_proj(x, ln1_g, w, seg, rep, gqk):
    nt = SEQ // IN_TM
    row_spec = lambda w: pl.BlockSpec((None, IN_TM, w), lambda b, t: (b, t, 0))
    return pl.pallas_call(
        _in_proj_kernel,
        grid=(BATCH, nt),
        in_specs=[
            row_spec(D_MODEL),
            _const_spec((1, D_MODEL)),
            _const_spec((D_MODEL, IN_WIDTH)),
            _const_spec((KV_SLAB, KV_SLAB)),
            _const_spec((KV_WIDTH, N_KV_HEADS * KV_SLAB)),
            _const_spec((1, ATTN_WIDTH + KV_WIDTH)),
        ],
        out_specs=[
            row_spec(ATTN_WIDTH),
            row_spec(N_KV_HEADS * KV_SLAB),
            row_spec(N_KV_HEADS * KV_SLAB),
            row_spec(SSM_WIDTH),
        ],
        out_shape=[
            jax.ShapeDtypeStruct((BATCH, SEQ, ATTN_WIDTH), bf16),
            jax.ShapeDtypeStruct((BATCH, SEQ, N_KV_HEADS * KV_SLAB), bf16),
            jax.ShapeDtypeStruct((BATCH, SEQ, N_KV_HEADS * KV_SLAB), bf16),
            jax.ShapeDtypeStruct((BATCH, SEQ, SSM_WIDTH), bf16),
        ],
        scratch_shapes=[pltpu.VMEM((IN_TM, IN_WIDTH), f32)],
        compiler_params=pltpu.CompilerParams(
            dimension_semantics=("arbitrary", "arbitrary"), vmem_limit_bytes=VMEM_LIMIT),
        name="in_proj",
    )(x, ln1_g, w, seg, rep, gqk)


def _attn_kernel(sink_ref, q_ref, kc_ref, kp_ref, vc_ref, vp_ref, bias_ref, qmask_ref, g_ref,
                 o_ref, y_scr):
    lane_head = lax.broadcasted_iota(jnp.int32, (BLOCK, KV_SLAB), 1) // HEAD_DIM
    for kh in range(N_KV_HEADS):
        cols = slice(kh * KV_SLAB, (kh + 1) * KV_SLAB)
        qs = q_ref[:, cols]
        q4 = jnp.concatenate([qs * qmask_ref[g] for g in range(Q_PER_KV)], axis=0)
        keys = jnp.concatenate([kp_ref[:, cols], kc_ref[:, cols]], axis=0)
        vals = jnp.concatenate([vp_ref[:, cols], vc_ref[:, cols]], axis=0)
        s4 = lax.dot_general(q4, keys, (((1,), (1,)), ((), ())), preferred_element_type=f32)
        probs, inv_den = [], []
        for g in range(Q_PER_KV):
            head = kh * Q_PER_KV + g
            s = s4[g * BLOCK:(g + 1) * BLOCK] + bias_ref[0, head]
            sink = sink_ref[head]
            m = jnp.maximum(jnp.max(s, axis=-1, keepdims=True), sink)
            p = jnp.exp(s - m)
            den = jnp.sum(p, axis=-1, keepdims=True) + jnp.exp(sink - m)
            probs.append(p.astype(bf16))
            inv_den.append(1.0 / den)
        o4 = _dot(jnp.concatenate(probs, axis=0), vals)
        acc = jnp.zeros((BLOCK, KV_SLAB), f32)
        for g in range(Q_PER_KV):
            acc = acc + jnp.where(lane_head == g, o4[g * BLOCK:(g + 1) * BLOCK] * inv_den[g], 0.0)
        y_scr[:, cols] = acc
    y = y_scr[...]
    o_ref[...] = (y * _rms_scale(y) * g_ref[...]).astype(bf16)


def _attention(sinks, q, k_rep, v_rep, bias2, qmask, attn_out_g):
    nb = SEQ // BLOCK
    w = N_KV_HEADS * KV_SLAB
    cur = lambda b, n: (b, n, 0)
    prev = lambda b, n: (b, jnp.maximum(n - 1, 0), 0)
    blk = lambda im: pl.BlockSpec((None, BLOCK, w), im)
    return pl.pallas_call(
        _attn_kernel,
        grid=(BATCH, nb),
        in_specs=[
            pl.BlockSpec(memory_space=pltpu.SMEM),
            blk(cur), blk(cur), blk(prev), blk(cur), blk(prev),
            pl.BlockSpec((1, N_Q_HEADS, BLOCK, 2 * BLOCK), lambda b, n: (jnp.minimum(n, 1), 0, 0, 0)),
            _const_spec((Q_PER_KV, BLOCK, KV_SLAB)),
            _const_spec((1, ATTN_WIDTH)),
        ],
        out_specs=blk(cur),
        out_shape=jax.ShapeDtypeStruct((BATCH, SEQ, ATTN_WIDTH), bf16),
        scratch_shapes=[pltpu.VMEM((BLOCK, ATTN_WIDTH), f32)],
        compiler_params=pltpu.CompilerParams(
            dimension_semantics=("arbitrary", "arbitrary"), vmem_limit_bytes=VMEM_LIMIT),
        name="attn",
    )(sinks, q, k_rep, k_rep, v_rep, v_rep, bias2, qmask, attn_out_g)


def _gelu_tanh(x):
    return 0.5 * x * (1.0 + jnp.tanh(math.sqrt(2.0 / math.pi) * (x + 0.044715 * (x * x * x))))


def _sigmoid(x):
    return 1.0 / (1.0 + jnp.exp(-x))


def _ssm_kernel(u_ref, spread_ref, gather_ref, wb_ref, lam_ref, wc_ref, d_ref, wglu_ref, g_ref,
                o_ref, u2_scr, bu_scr, h_scr, y2_scr, y_scr, st_scr):
    @pl.when(pl.program_id(0) == 0)
    def _():
        st_scr[...] = jnp.zeros_like(st_scr)

    u_bt = u_ref[...].reshape(SSM_ROWS, SSM_WIDTH)
    u2 = _dot(spread_ref[...], u_bt)
    row_par = lax.broadcasted_iota(jnp.int32, (SSM_ROWS2, V7X_LANES), 0) % 2
    pair_par = (lax.broadcasted_iota(jnp.int32, (SSM_ROWS2, V7X_LANES), 1) // (2 * SSM_GROUP)) % 2
    own_pair = row_par == pair_par

    for c in range(LANE_CHUNKS):
        uc = u2[:, c * V7X_LANES:(c + 1) * V7X_LANES]
        u2_scr[c] = uc
        bu_scr[c] = _dot(jnp.where(own_pair, uc, 0.0).astype(bf16), wb_ref[c])

    half = LANE_CHUNKS // 2
    for c0 in (0, half):
        chains = [(c, s) for c in range(c0, c0 + half) for s in range(SLOTS)]

        def cols(s):
            base = s * 2 * V7X_LANES
            return slice(base, base + V7X_LANES), slice(base + V7X_LANES, base + 2 * V7X_LANES)

        lams = [(lam_ref[c, :, cols(s)[0]], lam_ref[c, :, cols(s)[1]]) for c, s in chains]
        init = tuple((st_scr[c, :, cols(s)[0]], st_scr[c, :, cols(s)[1]]) for c, s in chains)

        def body(t, carry, chains=chains, lams=lams, cols=cols):
            rows = pl.ds(t * V7X_SUBLANES, V7X_SUBLANES)
            out = []
            for (c, s), (l_re, l_im), (s_re, s_im) in zip(chains, lams, carry):
                cr, ci = cols(s)
                n_re = (l_re * s_re - l_im * s_im) + bu_scr[c, rows, cr]
                n_im = (l_re * s_im + l_im * s_re) + bu_scr[c, rows, ci]
                h_scr[c, rows, cr] = n_re
                h_scr[c, rows, ci] = n_im
                out.append((n_re, n_im))
            return tuple(out)

        fin = init
        for t in range(SSM_T):
            fin = body(t, fin)
        for (c, s), (s_re, s_im) in zip(chains, fin):
            st_scr[c, :, cols(s)[0]] = s_re
            st_scr[c, :, cols(s)[1]] = s_im

    even = pl.ds(0, SSM_ROWS, stride=2)
    odd = pl.ds(1, SSM_ROWS, stride=2)
    for c in range(LANE_CHUNKS):
        cols_c = slice(c * V7X_LANES, (c + 1) * V7X_LANES)
        y2 = _dot(h_scr[c].astype(bf16), wc_ref[c])
        y2_scr[2 * c] = y2[:, :V7X_LANES]
        y2_scr[2 * c + 1] = y2[:, V7X_LANES:]
        yc = y2_scr.at[2 * c][even, :] + y2_scr.at[2 * c + 1][odd, :]
        yc = yc + d_ref[:, cols_c] * u2_scr.at[c][even, :]
        y_scr[:, cols_c] = _gelu_tanh(yc)
    y = y_scr[...]
    out = y * _sigmoid(_dot(y.astype(bf16), wglu_ref[...]))
    out_tb = (out * _rms_scale(out) * g_ref[...]).astype(bf16)
    o_ref[...] = _dot(gather_ref[...], out_tb).astype(bf16).reshape(BATCH, SSM_T, SSM_WIDTH)


def _ssm(u, spread, gather, wb, lam, wc, d, wglu, ssm_out_g):
    return pl.pallas_call(
        _ssm_kernel,
        grid=(SEQ // SSM_T,),
        in_specs=[
            pl.BlockSpec((BATCH, SSM_T, SSM_WIDTH), lambda i: (0, i, 0)),
            _const_spec((SSM_ROWS2, SSM_ROWS)),
            _const_spec((SSM_ROWS, SSM_ROWS)),
            _const_spec((LANE_CHUNKS, V7X_LANES, SLOT_COLS)),
            _const_spec((LANE_CHUNKS, V7X_SUBLANES, SLOT_COLS)),
            _const_spec((LANE_CHUNKS, SLOT_COLS, 2 * V7X_LANES)),
            _const_spec((1, SSM_WIDTH)),
            _const_spec((SSM_WIDTH, SSM_WIDTH)),
            _const_spec((1, SSM_WIDTH)),
        ],
        out_specs=pl.BlockSpec((BATCH, SSM_T, SSM_WIDTH), lambda i: (0, i, 0)),
        out_shape=jax.ShapeDtypeStruct((BATCH, SEQ, SSM_WIDTH), bf16),
        scratch_shapes=[
            pltpu.VMEM((LANE_CHUNKS, SSM_ROWS2, V7X_LANES), f32),
            pltpu.VMEM((LANE_CHUNKS, SSM_ROWS2, SLOT_COLS), f32),
            pltpu.VMEM((LANE_CHUNKS, SSM_ROWS2, SLOT_COLS), f32),
            pltpu.VMEM((2 * LANE_CHUNKS, SSM_ROWS2, V7X_LANES), f32),
            pltpu.VMEM((SSM_ROWS, SSM_WIDTH), f32),
            pltpu.VMEM((LANE_CHUNKS, V7X_SUBLANES, SLOT_COLS), f32),
        ],
        compiler_params=pltpu.CompilerParams(
            dimension_semantics=("arbitrary",), vmem_limit_bytes=VMEM_LIMIT),
        name="ssm",
    )(u, spread, gather, wb, lam, wc, d, wglu, ssm_out_g)


def _out_proj_kernel(a_ref, s_ref, x_ref, wa_ref, ws_ref, g_ref, o_ref, h_ref):
    x1 = x_ref[...] + _dot(a_ref[...], wa_ref[...]) + _dot(s_ref[...], ws_ref[...])
    o_ref[...] = x1
    h_ref[...] = (x1 * _rms_scale(x1) * g_ref[...]).astype(bf16)


def _out_proj(a, s, x, wa, ws, ln2_g):
    nt = SEQ // IN_TM
    row_spec = lambda w: pl.BlockSpec((None, IN_TM, w), lambda b, t: (b, t, 0))
    return pl.pallas_call(
        _out_proj_kernel,
        grid=(BATCH, nt),
        in_specs=[
            row_spec(ATTN_WIDTH),
            row_spec(SSM_WIDTH),
            row_spec(D_MODEL),
            _const_spec((ATTN_WIDTH, D_MODEL)),
            _const_spec((SSM_WIDTH, D_MODEL)),
            _const_spec((1, D_MODEL)),
        ],
        out_specs=[row_spec(D_MODEL), row_spec(D_MODEL)],
        out_shape=[jax.ShapeDtypeStruct((BATCH, SEQ, D_MODEL), f32),
                   jax.ShapeDtypeStruct((BATCH, SEQ, D_MODEL), bf16)],
        compiler_params=pltpu.CompilerParams(
            dimension_semantics=("arbitrary", "arbitrary"), vmem_limit_bytes=VMEM_LIMIT),
        name="out_proj",
    )(a, s, x, wa, ws, ln2_g)


def _ffn_kernel(x_hbm, h_ref, wg_ref, wu_ref, wd_ref, o_ref, sem):
    first = pl.program_id(1) == 0
    rows = pl.ds(pl.multiple_of(pl.program_id(0) * FFN_TM, FFN_TM), FFN_TM)
    residual_copy = pltpu.make_async_copy(x_hbm.at[rows, :], o_ref, sem)

    @pl.when(first)
    def _():
        residual_copy.start()

    h = h_ref[...]
    gate = _dot(h, wg_ref[...])
    up = _dot(h, wu_ref[...])
    act = (gate * _sigmoid(gate) * up).astype(bf16)

    @pl.when(first)
    def _():
        residual_copy.wait()

    for n in range(D_MODEL // FFN_TF):
        cols = slice(n * FFN_TF, (n + 1) * FFN_TF)
        o_ref[:, cols] += _dot(act, wd_ref[:, cols])


def _ffn(x, h, wg, wu, wd):
    n_rows = BATCH * SEQ
    return pl.pallas_call(
        _ffn_kernel,
        grid=(n_rows // FFN_TM, FF_HIDDEN // FFN_TF),
        in_specs=[
            pl.BlockSpec(memory_space=pl.ANY),
            pl.BlockSpec((FFN_TM, D_MODEL), lambda i, f: (i, 0)),
            pl.BlockSpec((D_MODEL, FFN_TF), lambda i, f: (0, f)),
            pl.BlockSpec((D_MODEL, FFN_TF), lambda i, f: (0, f)),
            pl.BlockSpec((FFN_TF, D_MODEL), lambda i, f: (f, 0)),
        ],
        out_specs=pl.BlockSpec((FFN_TM, D_MODEL), lambda i, f: (i, 0)),
        out_shape=jax.ShapeDtypeStruct((n_rows, D_MODEL), f32),
        scratch_shapes=[pltpu.SemaphoreType.DMA(())],
        compiler_params=pltpu.CompilerParams(
            dimension_semantics=("arbitrary", "arbitrary"), vmem_limit_bytes=VMEM_LIMIT),
        name="ffn",
    )(x, h, wg, wu, wd)


def _t5_bucket(dist):
    n = np.maximum(dist, 0)
    max_exact = REL_BUCKETS // 2
    nf = np.maximum(n, 1).astype(np.float32)
    large = max_exact + (np.log(nf / max_exact) / math.log(REL_MAX_DISTANCE / max_exact)
                         * (REL_BUCKETS - max_exact)).astype(np.int32)
    large = np.minimum(large, REL_BUCKETS - 1)
    return np.where(n < max_exact, n, large).astype(np.int32)


def _bias_kernel(rb_ref, bucket_ref, valid_ref, o_ref):
    head = pl.program_id(1)
    bucket = bucket_ref[...]
    acc = jnp.zeros((BLOCK, 2 * BLOCK), f32)
    for k in range(REL_BUCKETS):
        acc = jnp.where(bucket == k, rb_ref[k, head], acc)
    o_ref[0, 0] = jnp.where(valid_ref[0] != 0, acc, MASK_VALUE)


def _bias_tables(rel_bias):
    qi = np.arange(BLOCK)[:, None]
    sj = np.arange(2 * BLOCK)[None, :]
    dist = qi + BLOCK - sj
    in_window = (dist >= 0) & (dist < WINDOW)
    valid = np.stack([in_window & (sj >= BLOCK), in_window]).astype(np.int32)
    return pl.pallas_call(
        _bias_kernel,
        grid=(2, N_Q_HEADS),
        in_specs=[
            pl.BlockSpec(memory_space=pltpu.SMEM),
            _const_spec((BLOCK, 2 * BLOCK)),
            pl.BlockSpec((1, BLOCK, 2 * BLOCK), lambda t, h: (t, 0, 0)),
        ],
        out_specs=pl.BlockSpec((1, 1, BLOCK, 2 * BLOCK), lambda t, h: (t, h, 0, 0)),
        out_shape=jax.ShapeDtypeStruct((2, N_Q_HEADS, BLOCK, 2 * BLOCK), f32),
        name="bias_table",
    )(rel_bias.astype(f32), jnp.asarray(_t5_bucket(dist)), jnp.asarray(valid))


def _ssm_params(a_re, a_im, log_dt, b_re, b_im, c_re, c_im):
    dt = jnp.exp(log_dt)[:, None]
    mag = jnp.exp(a_re * dt)
    ang = a_im * dt
    lb_re, lb_im = mag * jnp.cos(ang), mag * jnp.sin(ang)
    nr, ni = lb_re - 1.0, lb_im
    den = a_re * a_re + a_im * a_im
    f_re = (nr * a_re + ni * a_im) / den
    f_im = (ni * a_re - nr * a_im) / den
    bb_re = f_re[..., None] * b_re - f_im[..., None] * b_im
    bb_im = f_re[..., None] * b_im + f_im[..., None] * b_re

    eye = np.eye(2, dtype=np.float32)
    split = lambda t: t.reshape((2, LANE_CHUNKS, SLOTS, 2, 2) + t.shape[2:])
    bbs = split(jnp.stack([bb_re, bb_im]))
    wb = jnp.einsum('us,gh,acsjgnp->cujgpsahn', eye, eye, bbs).reshape(LANE_CHUNKS, V7X_LANES, SLOT_COLS)
    cs = split(jnp.stack([c_re, -c_im]))
    wc = jnp.einsum('vj,us,gh,acsjgpn->csagnvujhp', eye, eye, eye, cs).reshape(
        LANE_CHUNKS, SLOT_COLS, 2 * V7X_LANES)
    lam = jnp.transpose(split(jnp.stack([lb_re, lb_im])), (1, 3, 2, 0, 4, 5)).reshape(LANE_CHUNKS, 1, 2, SLOT_COLS)
    lam = jnp.broadcast_to(lam, (LANE_CHUNKS, BATCH, 2, SLOT_COLS)).reshape(LANE_CHUNKS, V7X_SUBLANES, SLOT_COLS)
    return wb.astype(bf16), wc.astype(bf16), lam


def _layer(x, rel_bias, ln1_g, w_in, q_norm_g, k_norm_g, attn_sinks, ssm_a_re, ssm_a_im,
           ssm_log_dt, ssm_b_re, ssm_b_im, ssm_c_re, ssm_c_im, ssm_d, w_glu,
           attn_out_g, ssm_out_g, w_out, ln2_g, w_ff_gate, w_ff_up, w_ff_down):
    row = lambda v: v.reshape(1, -1).astype(f32)

    lanes = np.arange(KV_SLAB)
    seg = jnp.asarray(lanes[:, None] // HEAD_DIM == lanes[None, :] // HEAD_DIM, bf16)
    src = np.arange(KV_WIDTH)
    dst = np.arange(N_KV_HEADS * KV_SLAB)
    rep = jnp.asarray((src[:, None] // HEAD_DIM == dst[None, :] // KV_SLAB)
                      & (src[:, None] % HEAD_DIM == dst[None, :] % HEAD_DIM), bf16)
    gqk = row(jnp.concatenate([jnp.tile(q_norm_g, N_Q_HEADS) * (HEAD_DIM ** -0.5),
                               jnp.tile(k_norm_g, N_KV_HEADS)]))
    qmask = jnp.asarray(np.broadcast_to(
        (lanes[None, None, :] // HEAD_DIM) == np.arange(Q_PER_KV)[:, None, None],
        (Q_PER_KV, BLOCK, KV_SLAB)), bf16)

    q, k_rep, v_rep, u = _in_proj(x, row(ln1_g), w_in.astype(bf16), seg, rep, gqk)

    y_attn = _attention(attn_sinks.astype(f32), q, k_rep, v_rep, _bias_tables(rel_bias), qmask,
                        row(attn_out_g))

    wb, wc, lam = _ssm_params(ssm_a_re.astype(f32), ssm_a_im.astype(f32), ssm_log_dt.astype(f32),
                                     ssm_b_re.astype(f32), ssm_b_im.astype(f32),
                                     ssm_c_re.astype(f32), ssm_c_im.astype(f32))
    r_bt = np.arange(SSM_ROWS)
    r_tb = (r_bt % SSM_T) * BATCH + r_bt // SSM_T
    gather = r_tb[:, None] == np.arange(SSM_ROWS)[None, :]
    spread = np.arange(SSM_ROWS2)[:, None] // 2 == r_tb[None, :]
    y_ssm = _ssm(u, jnp.asarray(spread, bf16), jnp.asarray(gather, bf16), wb, lam, wc, row(ssm_d),
                 w_glu.astype(bf16), row(ssm_out_g))

    x1, h2 = _out_proj(y_attn, y_ssm, x, w_out[:ATTN_WIDTH].astype(bf16),
                       w_out[ATTN_WIDTH:].astype(bf16), row(ln2_g))

    out = _ffn(x1.reshape(BATCH * SEQ, D_MODEL), h2.reshape(BATCH * SEQ, D_MODEL),
               w_ff_gate.astype(bf16), w_ff_up.astype(bf16), w_ff_down.astype(bf16))
    return out.reshape(BATCH, SEQ, D_MODEL)


def kernel(x, rel_bias, ln1_g, w_in, q_norm_g, k_norm_g, attn_sinks, ssm_a_re, ssm_a_im, ssm_log_dt, ssm_b_re, ssm_b_im, ssm_c_re, ssm_c_im, ssm_d, w_glu, attn_out_g, ssm_out_g, w_out, ln2_g, w_ff_gate, w_ff_up, w_ff_down):
    for l in range(ln1_g.shape[0]):
        x = _layer(x, rel_bias, ln1_g[l], w_in[l], q_norm_g[l], k_norm_g[l], attn_sinks[l],
                   ssm_a_re[l], ssm_a_im[l], ssm_log_dt[l], ssm_b_re[l], ssm_b_im[l],
                   ssm_c_re[l], ssm_c_im[l], ssm_d[l], w_glu[l], attn_out_g[l], ssm_out_g[l],
                   w_out[l], ln2_g[l], w_ff_gate[l], w_ff_up[l], w_ff_down[l])
    return x
```

```python
import functools
import math

import jax
import jax.numpy as jnp
import numpy as np
from jax import lax
from jax.experimental import pallas as pl
from jax.experimental.pallas import tpu as pltpu

D_MODEL = 2048
BATCH = 4
SEQ = 4096
HEAD_DIM = 64
N_Q_HEADS = 16
N_KV_HEADS = 4
Q_PER_KV = N_Q_HEADS // N_KV_HEADS
ATTN_WIDTH = N_Q_HEADS * HEAD_DIM
KV_WIDTH = N_KV_HEADS * HEAD_DIM
WINDOW = 128
BLOCK = 128
SSM_WIDTH = D_MODEL - ATTN_WIDTH
IN_WIDTH = ATTN_WIDTH + 2 * KV_WIDTH + SSM_WIDTH
SSM_GROUP = 16
SSM_GROUPS = SSM_WIDTH // SSM_GROUP
SSM_STATE = 64
FF_HIDDEN = 5632
REL_BUCKETS = 32
REL_MAX_DISTANCE = 128
EPS = 1e-6

V7X_LANES = 128
V7X_SUBLANES = 8
V7X_VMEM_BYTES = 64 * 1024 * 1024
VMEM_LIMIT = V7X_VMEM_BYTES - 8 * 1024 * 1024

MASK_VALUE = -1e30
KV_SLAB = Q_PER_KV * HEAD_DIM
LANE_CHUNKS = SSM_WIDTH // V7X_LANES
SLOTS = 2
SLOT_COLS = SLOTS * 2 * V7X_LANES

ATTN_BLOCKS = 4
IN_TM = 512
FFN_TM = 512
FFN_TF = 512
SSM_T = 64
SSM_ROWS = SSM_T * BATCH
SSM_ROWS2 = 2 * SSM_ROWS

f32 = jnp.float32
bf16 = jnp.bfloat16


def _dot(a, b):
    return jnp.dot(a, b, preferred_element_type=f32)


def _rms_scale(x):
    return lax.rsqrt(jnp.mean(x * x, axis=-1, keepdims=True) + EPS)


def _in_proj_kernel(x_ref, g_ref, w_ref, seg_ref, rep_ref, gqk_ref, q_ref, k_ref, v_ref, u_ref, proj_scr):
    xf = x_ref[...]
    h = (xf * _rms_scale(xf) * g_ref[...]).astype(bf16)
    proj_scr[...] = _dot(h, w_ref[...])

    n_slabs = (ATTN_WIDTH + KV_WIDTH) // KV_SLAB
    slab = lambda c: proj_scr[:, c * KV_SLAB:(c + 1) * KV_SLAB]
    sq = jnp.concatenate([slab(c) * slab(c) for c in range(n_slabs)], axis=0)
    hi = sq.astype(bf16)
    lo = (sq - hi.astype(f32)).astype(bf16)
    ss = _dot(jnp.concatenate([hi, lo], axis=0), seg_ref[...])
    ss = ss[:n_slabs * IN_TM] + ss[n_slabs * IN_TM:]
    normed = []
    for c in range(n_slabs):
        cols = slice(c * KV_SLAB, (c + 1) * KV_SLAB)
        scale = lax.rsqrt(ss[c * IN_TM:(c + 1) * IN_TM] * (1.0 / HEAD_DIM) + EPS)
        normed.append((slab(c) * scale * gqk_ref[:, cols]).astype(bf16))
    for c in range(n_slabs - 1):
        q_ref[:, c * KV_SLAB:(c + 1) * KV_SLAB] = normed[c]
    v_cols = slice(ATTN_WIDTH + KV_WIDTH, ATTN_WIDTH + 2 * KV_WIDTH)
    kv = _dot(jnp.concatenate([normed[-1], proj_scr[:, v_cols].astype(bf16)], axis=0), rep_ref[...])
    k_ref[...] = kv[:IN_TM].astype(bf16)
    v_ref[...] = kv[IN_TM:].astype(bf16)
    u_ref[...] = proj_scr[:, ATTN_WIDTH + 2 * KV_WIDTH:].astype(bf16)


def _const_spec(shape):
    nd = len(shape)
    return pl.BlockSpec(shape, lambda *_: (0,) * nd)


def _in_proj(x, ln1_g, w, seg, rep, gqk):
    nt = SEQ // IN_TM
    row_spec = lambda w: pl.BlockSpec((None, IN_TM, w), lambda b, t: (b, t, 0))
    return pl.pallas_call(
        _in_proj_kernel,
        grid=(BATCH, nt),
        in_specs=[
            row_spec(D_MODEL),
            _const_spec((1, D_MODEL)),
            _const_spec((D_MODEL, IN_WIDTH)),
            _const_spec((KV_SLAB, KV_SLAB)),
            _const_spec((KV_WIDTH, N_KV_HEADS * KV_SLAB)),
            _const_spec((1, ATTN_WIDTH + KV_WIDTH)),
        ],
        out_specs=[
            row_spec(ATTN_WIDTH),
            row_spec(N_KV_HEADS * KV_SLAB),
            row_spec(N_KV_HEADS * KV_SLAB),
            row_spec(SSM_WIDTH),
        ],
        out_shape=[
            jax.ShapeDtypeStruct((BATCH, SEQ, ATTN_WIDTH), bf16),
            jax.ShapeDtypeStruct((BATCH, SEQ, N_KV_HEADS * KV_SLAB), bf16),
            jax.ShapeDtypeStruct((BATCH, SEQ, N_KV_HEADS * KV_SLAB), bf16),
            jax.ShapeDtypeStruct((BATCH, SEQ, SSM_WIDTH), bf16),
        ],
        scratch_shapes=[pltpu.VMEM((IN_TM, IN_WIDTH), f32)],
        compiler_params=pltpu.CompilerParams(
            dimension_semantics=("arbitrary", "arbitrary"), vmem_limit_bytes=VMEM_LIMIT),
        name="in_proj",
    )(x, ln1_g, w, seg, rep, gqk)


def _attn_kernel(sink_ref, q_ref, kc_ref, kp_ref, vc_ref, vp_ref, bias_ref, qmask_ref, tri_ref, g_ref,
                 o_ref, y_scr):
    qi = lax.broadcasted_iota(jnp.int32, (BLOCK, BLOCK), 0)
    kj = lax.broadcasted_iota(jnp.int32, (BLOCK, BLOCK), 1)
    from_prev = kj > qi
    lane_head = lax.broadcasted_iota(jnp.int32, (BLOCK, KV_SLAB), 1) // HEAD_DIM
    first_table = jnp.minimum(pl.program_id(1), 1)
    for j in range(ATTN_BLOCKS):
        rows = slice(j * BLOCK, (j + 1) * BLOCK)
        prev_rows = slice((j - 1) * BLOCK, j * BLOCK)
        for kh in range(N_KV_HEADS):
            cols = slice(kh * KV_SLAB, (kh + 1) * KV_SLAB)
            qs = q_ref[rows, cols]
            q4 = jnp.concatenate([qs * qmask_ref[g] for g in range(Q_PER_KV)], axis=0)
            k_prev = kp_ref[:, cols] if j == 0 else kc_ref[prev_rows, cols]
            v_prev = vp_ref[:, cols] if j == 0 else vc_ref[prev_rows, cols]
            keys = jnp.concatenate([k_prev, kc_ref[rows, cols]], axis=0)
            vals = jnp.concatenate([v_prev, vc_ref[rows, cols]], axis=0)
            s4 = lax.dot_general(q4, keys, (((1,), (1,)), ((), ())), preferred_element_type=f32)
            acc = jnp.zeros((BLOCK, KV_SLAB), f32)
            for g in range(Q_PER_KV):
                head = kh * Q_PER_KV + g
                bias = bias_ref[first_table, head] if j == 0 else bias_ref[1, head]
                sg = s4[g * BLOCK:(g + 1) * BLOCK]
                s = jnp.where(from_prev, sg[:, :BLOCK], sg[:, BLOCK:]) + bias
                sink = sink_ref[head]
                m = jnp.max(s, axis=-1, keepdims=True)
                p = jnp.exp(s - m)
                den = jnp.sum(p, axis=-1, keepdims=True) + jnp.exp(sink - m)
                w = (p * (1.0 / den)).astype(bf16)
                o = _dot(jnp.concatenate([w * tri_ref[0], w * tri_ref[1]], axis=1), vals)
                acc = jnp.where(lane_head == g, o, acc)
            y_scr[rows, cols] = acc
    y = y_scr[...]
    o_ref[...] = (y * _rms_scale(y) * g_ref[...]).astype(bf16)


def _attention(sinks, q, k_rep, v_rep, bias2, qmask, tri, attn_out_g):
    rows = ATTN_BLOCKS * BLOCK
    w = N_KV_HEADS * KV_SLAB
    cur = pl.BlockSpec((None, rows, w), lambda b, n: (b, n, 0))
    prev = pl.BlockSpec((None, BLOCK, w), lambda b, n: (b, jnp.maximum(n * ATTN_BLOCKS - 1, 0), 0))
    return pl.pallas_call(
        _attn_kernel,
        grid=(BATCH, SEQ // rows),
        in_specs=[
            pl.BlockSpec(memory_space=pltpu.SMEM),
            cur, cur, prev, cur, prev,
            _const_spec((2, N_Q_HEADS, BLOCK, BLOCK)),
            _const_spec((Q_PER_KV, BLOCK, KV_SLAB)),
            _const_spec((2, BLOCK, BLOCK)),
            _const_spec((1, ATTN_WIDTH)),
        ],
        out_specs=cur,
        out_shape=jax.ShapeDtypeStruct((BATCH, SEQ, ATTN_WIDTH), bf16),
        scratch_shapes=[pltpu.VMEM((rows, ATTN_WIDTH), f32)],
        compiler_params=pltpu.CompilerParams(
            dimension_semantics=("arbitrary", "arbitrary"), vmem_limit_bytes=VMEM_LIMIT),
        name="attn",
    )(sinks, q, k_rep, k_rep, v_rep, v_rep, bias2, qmask, tri, attn_out_g)


def _gelu_tanh(x):
    return 0.5 * x * (1.0 + jnp.tanh(math.sqrt(2.0 / math.pi) * (x + 0.044715 * (x * x * x))))


def _sigmoid(x):
    return 1.0 / (1.0 + jnp.exp(-x))


def _ssm_kernel(u_ref, spread_ref, gather_ref, wb_ref, lam_ref, wc_ref, d_ref, wglu_ref, g_ref,
                o_ref, u2_scr, bu_scr, h_scr, y2_scr, y_scr, st_scr):
    @pl.when(pl.program_id(0) == 0)
    def _():
        st_scr[...] = jnp.zeros_like(st_scr)

    u_bt = u_ref[...].reshape(SSM_ROWS, SSM_WIDTH)
    u2 = _dot(spread_ref[...], u_bt)
    row_par = lax.broadcasted_iota(jnp.int32, (SSM_ROWS2, V7X_LANES), 0) % 2
    pair_par = (lax.broadcasted_iota(jnp.int32, (SSM_ROWS2, V7X_LANES), 1) // (2 * SSM_GROUP)) % 2
    own_pair = row_par == pair_par

    for c in range(LANE_CHUNKS):
        uc = u2[:, c * V7X_LANES:(c + 1) * V7X_LANES]
        u2_scr[c] = uc
        bu_scr[c] = _dot(jnp.where(own_pair, uc, 0.0).astype(bf16), wb_ref[c])

    half = LANE_CHUNKS // 2
    for c0 in (0, half):
        chains = [(c, s) for c in range(c0, c0 + half) for s in range(SLOTS)]

        def cols(s):
            base = s * 2 * V7X_LANES
            return slice(base, base + V7X_LANES), slice(base + V7X_LANES, base + 2 * V7X_LANES)

        lams = [(lam_ref[c, :, cols(s)[0]], lam_ref[c, :, cols(s)[1]]) for c, s in chains]
        init = tuple((st_scr[c, :, cols(s)[0]], st_scr[c, :, cols(s)[1]]) for c, s in chains)

        def body(t, carry, chains=chains, lams=lams, cols=cols):
            rows = pl.ds(t * V7X_SUBLANES, V7X_SUBLANES)
            out = []
            for (c, s), (l_re, l_im), (s_re, s_im) in zip(chains, lams, carry):
                cr, ci = cols(s)
                n_re = (l_re * s_re - l_im * s_im) + bu_scr[c, rows, cr]
                n_im = (l_re * s_im + l_im * s_re) + bu_scr[c, rows, ci]
                h_scr[c, rows, cr] = n_re
                h_scr[c, rows, ci] = n_im
                out.append((n_re, n_im))
            return tuple(out)

        fin = init
        for t in range(SSM_T):
            fin = body(t, fin)
        for (c, s), (s_re, s_im) in zip(chains, fin):
            st_scr[c, :, cols(s)[0]] = s_re
            st_scr[c, :, cols(s)[1]] = s_im

    even = pl.ds(0, SSM_ROWS, stride=2)
    odd = pl.ds(1, SSM_ROWS, stride=2)
    for c in range(LANE_CHUNKS):
        cols_c = slice(c * V7X_LANES, (c + 1) * V7X_LANES)
        y2 = _dot(h_scr[c].astype(bf16), wc_ref[c])
        y2_scr[2 * c] = y2[:, :V7X_LANES]
        y2_scr[2 * c + 1] = y2[:, V7X_LANES:]
        yc = y2_scr.at[2 * c][even, :] + y2_scr.at[2 * c + 1][odd, :]
        yc = yc + d_ref[:, cols_c] * u2_scr.at[c][even, :]
        y_scr[:, cols_c] = _gelu_tanh(yc)
    y = y_scr[...]
    out = y * _sigmoid(_dot(y.astype(bf16), wglu_ref[...]))
    out_tb = (out * _rms_scale(out) * g_ref[...]).astype(bf16)
    o_ref[...] = _dot(gather_ref[...], out_tb).astype(bf16).reshape(BATCH, SSM_T, SSM_WIDTH)


def _ssm(u, spread, gather, wb, lam, wc, d, wglu, ssm_out_g):
    return pl.pallas_call(
        _ssm_kernel,
        grid=(SEQ // SSM_T,),
        in_specs=[
            pl.BlockSpec((BATCH, SSM_T, SSM_WIDTH), lambda i: (0, i, 0)),
            _const_spec((SSM_ROWS2, SSM_ROWS)),
            _const_spec((SSM_ROWS, SSM_ROWS)),
            _const_spec((LANE_CHUNKS, V7X_LANES, SLOT_COLS)),
            _const_spec((LANE_CHUNKS, V7X_SUBLANES, SLOT_COLS)),
            _const_spec((LANE_CHUNKS, SLOT_COLS, 2 * V7X_LANES)),
            _const_spec((1, SSM_WIDTH)),
            _const_spec((SSM_WIDTH, SSM_WIDTH)),
            _const_spec((1, SSM_WIDTH)),
        ],
        out_specs=pl.BlockSpec((BATCH, SSM_T, SSM_WIDTH), lambda i: (0, i, 0)),
        out_shape=jax.ShapeDtypeStruct((BATCH, SEQ, SSM_WIDTH), bf16),
        scratch_shapes=[
            pltpu.VMEM((LANE_CHUNKS, SSM_ROWS2, V7X_LANES), f32),
            pltpu.VMEM((LANE_CHUNKS, SSM_ROWS2, SLOT_COLS), f32),
            pltpu.VMEM((LANE_CHUNKS, SSM_ROWS2, SLOT_COLS), f32),
            pltpu.VMEM((2 * LANE_CHUNKS, SSM_ROWS2, V7X_LANES), f32),
            pltpu.VMEM((SSM_ROWS, SSM_WIDTH), f32),
            pltpu.VMEM((LANE_CHUNKS, V7X_SUBLANES, SLOT_COLS), f32),
        ],
        compiler_params=pltpu.CompilerParams(
            dimension_semantics=("arbitrary",), vmem_limit_bytes=VMEM_LIMIT),
        name="ssm",
    )(u, spread, gather, wb, lam, wc, d, wglu, ssm_out_g)


def _out_proj_kernel(a_ref, s_ref, x_ref, wa_ref, ws_ref, o_ref):
    o_ref[...] = x_ref[...] + _dot(a_ref[...], wa_ref[...]) + _dot(s_ref[...], ws_ref[...])


def _out_proj(a, s, x, wa, ws):
    nt = SEQ // IN_TM
    row_spec = lambda w: pl.BlockSpec((None, IN_TM, w), lambda b, t: (b, t, 0))
    return pl.pallas_call(
        _out_proj_kernel,
        grid=(BATCH, nt),
        in_specs=[
            row_spec(ATTN_WIDTH),
            row_spec(SSM_WIDTH),
            row_spec(D_MODEL),
            _const_spec((ATTN_WIDTH, D_MODEL)),
            _const_spec((SSM_WIDTH, D_MODEL)),
        ],
        out_specs=row_spec(D_MODEL),
        out_shape=jax.ShapeDtypeStruct((BATCH, SEQ, D_MODEL), f32),
        compiler_params=pltpu.CompilerParams(
            dimension_semantics=("arbitrary", "arbitrary"), vmem_limit_bytes=VMEM_LIMIT),
        name="out_proj",
    )(a, s, x, wa, ws)


def _ffn_kernel(x_ref, g_ref, wg_ref, wu_ref, wd_ref, o_ref, h_scr):
    @pl.when(pl.program_id(1) == 0)
    def _():
        xf = x_ref[...]
        h_scr[...] = (xf * _rms_scale(xf) * g_ref[...]).astype(bf16)
        o_ref[...] = xf

    h = h_scr[...]
    gate = _dot(h, wg_ref[...])
    up = _dot(h, wu_ref[...])
    act = (gate * _sigmoid(gate) * up).astype(bf16)
    o_ref[...] += _dot(act, wd_ref[...])


def _ffn(x, ln2_g, wg, wu, wd):
    n_rows = BATCH * SEQ
    return pl.pallas_call(
        _ffn_kernel,
        grid=(n_rows // FFN_TM, FF_HIDDEN // FFN_TF),
        in_specs=[
            pl.BlockSpec((FFN_TM, D_MODEL), lambda i, f: (i, 0)),
            _const_spec((1, D_MODEL)),
            pl.BlockSpec((D_MODEL, FFN_TF), lambda i, f: (0, f)),
            pl.BlockSpec((D_MODEL, FFN_TF), lambda i, f: (0, f)),
            pl.BlockSpec((FFN_TF, D_MODEL), lambda i, f: (f, 0)),
        ],
        out_specs=pl.BlockSpec((FFN_TM, D_MODEL), lambda i, f: (i, 0)),
        out_shape=jax.ShapeDtypeStruct((n_rows, D_MODEL), f32),
        scratch_shapes=[pltpu.VMEM((FFN_TM, D_MODEL), bf16)],
        compiler_params=pltpu.CompilerParams(
            dimension_semantics=("arbitrary", "arbitrary"), vmem_limit_bytes=VMEM_LIMIT),
        name="ffn",
    )(x, ln2_g, wg, wu, wd)


def _t5_bucket(dist):
    n = np.maximum(dist, 0)
    max_exact = REL_BUCKETS // 2
    nf = np.maximum(n, 1).astype(np.float32)
    large = max_exact + (np.log(nf / max_exact) / math.log(REL_MAX_DISTANCE / max_exact)
                         * (REL_BUCKETS - max_exact)).astype(np.int32)
    large = np.minimum(large, REL_BUCKETS - 1)
    return np.where(n < max_exact, n, large).astype(np.int32)


def _bias_kernel(rb_ref, bucket_ref, valid_ref, o_ref):
    head = pl.program_id(1)
    bucket = bucket_ref[...]
    acc = jnp.zeros((BLOCK, BLOCK), f32)
    for k in range(REL_BUCKETS):
        acc = jnp.where(bucket == k, rb_ref[k, head], acc)
    o_ref[0, 0] = jnp.where(valid_ref[0] != 0, acc, MASK_VALUE)


def _bias_tables(rel_bias):
    qi = np.arange(BLOCK)[:, None]
    kj = np.arange(BLOCK)[None, :]
    from_prev = kj > qi
    dist = np.where(from_prev, qi + BLOCK - kj, qi - kj)
    assert ((dist >= 0) & (dist < WINDOW)).all()
    valid = np.stack([~from_prev, np.ones_like(from_prev)]).astype(np.int32)
    return pl.pallas_call(
        _bias_kernel,
        grid=(2, N_Q_HEADS),
        in_specs=[
            pl.BlockSpec(memory_space=pltpu.SMEM),
            _const_spec((BLOCK, BLOCK)),
            pl.BlockSpec((1, BLOCK, BLOCK), lambda t, h: (t, 0, 0)),
        ],
        out_specs=pl.BlockSpec((1, 1, BLOCK, BLOCK), lambda t, h: (t, h, 0, 0)),
        out_shape=jax.ShapeDtypeStruct((2, N_Q_HEADS, BLOCK, BLOCK), f32),
        name="bias_table",
    )(rel_bias.astype(f32), jnp.asarray(_t5_bucket(dist)), jnp.asarray(valid))


def _ssm_params(a_re, a_im, log_dt, b_re, b_im, c_re, c_im):
    dt = jnp.exp(log_dt)[:, None]
    mag = jnp.exp(a_re * dt)
    ang = a_im * dt
    lb_re, lb_im = mag * jnp.cos(ang), mag * jnp.sin(ang)
    nr, ni = lb_re - 1.0, lb_im
    den = a_re * a_re + a_im * a_im
    f_re = (nr * a_re + ni * a_im) / den
    f_im = (ni * a_re - nr * a_im) / den
    bb_re = f_re[..., None] * b_re - f_im[..., None] * b_im
    bb_im = f_re[..., None] * b_im + f_im[..., None] * b_re

    eye = np.eye(2, dtype=np.float32)
    split = lambda t: t.reshape((2, LANE_CHUNKS, SLOTS, 2, 2) + t.shape[2:])
    bbs = split(jnp.stack([bb_re, bb_im]))
    wb = jnp.einsum('us,gh,acsjgnp->cujgpsahn', eye, eye, bbs).reshape(LANE_CHUNKS, V7X_LANES, SLOT_COLS)
    cs = split(jnp.stack([c_re, -c_im]))
    wc = jnp.einsum('vj,us,gh,acsjgpn->csagnvujhp', eye, eye, eye, cs).reshape(
        LANE_CHUNKS, SLOT_COLS, 2 * V7X_LANES)
    lam = jnp.transpose(split(jnp.stack([lb_re, lb_im])), (1, 3, 2, 0, 4, 5)).reshape(LANE_CHUNKS, 1, 2, SLOT_COLS)
    lam = jnp.broadcast_to(lam, (LANE_CHUNKS, BATCH, 2, SLOT_COLS)).reshape(LANE_CHUNKS, V7X_SUBLANES, SLOT_COLS)
    return wb.astype(bf16), wc.astype(bf16), lam


def _layer(x, rel_bias, ln1_g, w_in, q_norm_g, k_norm_g, attn_sinks, ssm_a_re, ssm_a_im,
           ssm_log_dt, ssm_b_re, ssm_b_im, ssm_c_re, ssm_c_im, ssm_d, w_glu,
           attn_out_g, ssm_out_g, w_out, ln2_g, w_ff_gate, w_ff_up, w_ff_down):
    row = lambda v: v.reshape(1, -1).astype(f32)

    lanes = np.arange(KV_SLAB)
    seg = jnp.asarray(lanes[:, None] // HEAD_DIM == lanes[None, :] // HEAD_DIM, bf16)
    src = np.arange(KV_WIDTH)
    dst = np.arange(N_KV_HEADS * KV_SLAB)
    rep = jnp.asarray((src[:, None] // HEAD_DIM == dst[None, :] // KV_SLAB)
                      & (src[:, None] % HEAD_DIM == dst[None, :] % HEAD_DIM), bf16)
    gqk = row(jnp.concatenate([jnp.tile(q_norm_g, N_Q_HEADS) * (HEAD_DIM ** -0.5),
                               jnp.tile(k_norm_g, N_KV_HEADS)]))
    qmask = jnp.asarray(np.broadcast_to(
        (lanes[None, None, :] // HEAD_DIM) == np.arange(Q_PER_KV)[:, None, None],
        (Q_PER_KV, BLOCK, KV_SLAB)), bf16)

    q, k_rep, v_rep, u = _in_proj(x, row(ln1_g), w_in.astype(bf16), seg, rep, gqk)

    pos = np.arange(BLOCK)
    from_prev = pos[None, :] > pos[:, None]
    tri = jnp.asarray(np.stack([from_prev, ~from_prev]), bf16)
    y_attn = _attention(attn_sinks.astype(f32), q, k_rep, v_rep, _bias_tables(rel_bias), qmask, tri,
                        row(attn_out_g))

    wb, wc, lam = _ssm_params(ssm_a_re.astype(f32), ssm_a_im.astype(f32), ssm_log_dt.astype(f32),
                                     ssm_b_re.astype(f32), ssm_b_im.astype(f32),
                                     ssm_c_re.astype(f32), ssm_c_im.astype(f32))
    r_bt = np.arange(SSM_ROWS)
    r_tb = (r_bt % SSM_T) * BATCH + r_bt // SSM_T
    gather = r_tb[:, None] == np.arange(SSM_ROWS)[None, :]
    spread = np.arange(SSM_ROWS2)[:, None] // 2 == r_tb[None, :]
    y_ssm = _ssm(u, jnp.asarray(spread, bf16), jnp.asarray(gather, bf16), wb, lam, wc, row(ssm_d),
                 w_glu.astype(bf16), row(ssm_out_g))

    x1 = _out_proj(y_attn, y_ssm, x,
                   w_out[:ATTN_WIDTH].astype(bf16), w_out[ATTN_WIDTH:].astype(bf16))

    out = _ffn(x1.reshape(BATCH * SEQ, D_MODEL), row(ln2_g), w_ff_gate.astype(bf16),
               w_ff_up.astype(bf16), w_ff_down.astype(bf16))
    return out.reshape(BATCH, SEQ, D_MODEL)


def kernel(x, rel_bias, ln1_g, w_in, q_norm_g, k_norm_g, attn_sinks, ssm_a_re, ssm_a_im, ssm_log_dt, ssm_b_re, ssm_b_im, ssm_c_re, ssm_c_im, ssm_d, w_glu, attn_out_g, ssm_out_g, w_out, ln2_g, w_ff_gate, w_ff_up, w_ff_down):
    for l in range(ln1_g.shape[0]):
        x = _layer(x, rel_bias, ln1_g[l], w_in[l], q_norm_g[l], k_norm_g[l], attn_sinks[l],
                   ssm_a_re[l], ssm_a_im[l], ssm_log_dt[l], ssm_b_re[l], ssm_b_im[l],
                   ssm_c_re[l], ssm_c_im[l], ssm_d[l], w_glu[l], attn_out_g[l], ssm_out_g[l],
                   w_out[l], ln2_g[l], w_ff_gate[l], w_ff_up[l], w_ff_down[l])
    return x
```

```python
import functools
import math

import jax
import jax.numpy as jnp
import numpy as np
from jax import lax
from jax.experimental import pallas as pl
from jax.experimental.pallas import tpu as pltpu

D_MODEL = 2048
BATCH = 4
SEQ = 4096
HEAD_DIM = 64
N_Q_HEADS = 16
N_KV_HEADS = 4
Q_PER_KV = N_Q_HEADS // N_KV_HEADS
ATTN_WIDTH = N_Q_HEADS * HEAD_DIM
KV_WIDTH = N_KV_HEADS * HEAD_DIM
WINDOW = 128
BLOCK = 128
SSM_WIDTH = D_MODEL - ATTN_WIDTH
IN_WIDTH = ATTN_WIDTH + 2 * KV_WIDTH + SSM_WIDTH
SSM_GROUP = 16
SSM_GROUPS = SSM_WIDTH // SSM_GROUP
SSM_STATE = 64
FF_HIDDEN = 5632
REL_BUCKETS = 32
REL_MAX_DISTANCE = 128
EPS = 1e-6

V7X_LANES = 128
V7X_SUBLANES = 8
V7X_VMEM_BYTES = 64 * 1024 * 1024
VMEM_LIMIT = V7X_VMEM_BYTES - 8 * 1024 * 1024

MASK_VALUE = -1e30
KV_SLAB = Q_PER_KV * HEAD_DIM
LANE_CHUNKS = SSM_WIDTH // V7X_LANES
SLOTS = 2
SLOT_COLS = SLOTS * 2 * V7X_LANES

ATTN_BLOCKS = 4
IN_TM = 512
FFN_TM = 1024
FFN_TF = 512
SSM_T = 64
SSM_ROWS = SSM_T * BATCH
SSM_ROWS2 = 2 * SSM_ROWS

f32 = jnp.float32
bf16 = jnp.bfloat16


def _dot(a, b):
    return jnp.dot(a, b, preferred_element_type=f32)


def _rms_scale(x):
    return lax.rsqrt(jnp.mean(x * x, axis=-1, keepdims=True) + EPS)


def _in_proj_kernel(x_ref, g_ref, w_ref, seg_ref, rep_ref, gqk_ref, q_ref, k_ref, v_ref, u_ref, proj_scr):
    xf = x_ref[...]
    h = (xf * _rms_scale(xf) * g_ref[...]).astype(bf16)
    proj_scr[...] = _dot(h, w_ref[...])

    n_slabs = (ATTN_WIDTH + KV_WIDTH) // KV_SLAB
    slab = lambda c: proj_scr[:, c * KV_SLAB:(c + 1) * KV_SLAB]
    sq = jnp.concatenate([slab(c) * slab(c) for c in range(n_slabs)], axis=0)
    hi = sq.astype(bf16)
    lo = (sq - hi.astype(f32)).astype(bf16)
    ss = _dot(jnp.concatenate([hi, lo], axis=0), seg_ref[...])
    ss = ss[:n_slabs * IN_TM] + ss[n_slabs * IN_TM:]
    normed = []
    for c in range(n_slabs):
        cols = slice(c * KV_SLAB, (c + 1) * KV_SLAB)
        scale = lax.rsqrt(ss[c * IN_TM:(c + 1) * IN_TM] * (1.0 / HEAD_DIM) + EPS)
        normed.append((slab(c) * scale * gqk_ref[:, cols]).astype(bf16))
    for c in range(n_slabs - 1):
        q_ref[:, c * KV_SLAB:(c + 1) * KV_SLAB] = normed[c]
    v_cols = slice(ATTN_WIDTH + KV_WIDTH, ATTN_WIDTH + 2 * KV_WIDTH)
    kv = _dot(jnp.concatenate([normed[-1], proj_scr[:, v_cols].astype(bf16)], axis=0), rep_ref[...])
    k_ref[...] = kv[:IN_TM].astype(bf16)
    v_ref[...] = kv[IN_TM:].astype(bf16)
    u_ref[...] = proj_scr[:, ATTN_WIDTH + 2 * KV_WIDTH:].astype(bf16)


def _const_spec(shape):
    nd = len(shape)
    return pl.BlockSpec(shape, lambda *_: (0,) * nd)


def _in_proj(x, ln1_g, w, seg, rep, gqk):
    nt = SEQ // IN_TM
    row_spec = lambda w: pl.BlockSpec((None, IN_TM, w), lambda b, t: (b, t, 0))
    return pl.pallas_call(
        _in_proj_kernel,
        grid=(BATCH, nt),
        in_specs=[
            row_spec(D_MODEL),
            _const_spec((1, D_MODEL)),
            _const_spec((D_MODEL, IN_WIDTH)),
            _const_spec((KV_SLAB, KV_SLAB)),
            _const_spec((KV_WIDTH, N_KV_HEADS * KV_SLAB)),
            _const_spec((1, ATTN_WIDTH + KV_WIDTH)),
        ],
        out_specs=[
            row_spec(ATTN_WIDTH),
            row_spec(N_KV_HEADS * KV_SLAB),
            row_spec(N_KV_HEADS * KV_SLAB),
            row_spec(SSM_WIDTH),
        ],
        out_shape=[
            jax.ShapeDtypeStruct((BATCH, SEQ, ATTN_WIDTH), bf16),
            jax.ShapeDtypeStruct((BATCH, SEQ, N_KV_HEADS * KV_SLAB), bf16),
            jax.ShapeDtypeStruct((BATCH, SEQ, N_KV_HEADS * KV_SLAB), bf16),
            jax.ShapeDtypeStruct((BATCH, SEQ, SSM_WIDTH), bf16),
        ],
        scratch_shapes=[pltpu.VMEM((IN_TM, IN_WIDTH), f32)],
        compiler_params=pltpu.CompilerParams(
            dimension_semantics=("arbitrary", "arbitrary"), vmem_limit_bytes=VMEM_LIMIT),
        name="in_proj",
    )(x, ln1_g, w, seg, rep, gqk)


def _attn_kernel(sink_ref, q_ref, kc_ref, kp_ref, vc_ref, vp_ref, bias_ref, qmask_ref, tri_ref, g_ref,
                 o_ref, y_scr):
    qi = lax.broadcasted_iota(jnp.int32, (BLOCK, BLOCK), 0)
    kj = lax.broadcasted_iota(jnp.int32, (BLOCK, BLOCK), 1)
    from_prev = kj > qi
    lane_head = lax.broadcasted_iota(jnp.int32, (BLOCK, KV_SLAB), 1) // HEAD_DIM
    first_table = jnp.minimum(pl.program_id(1), 1)
    for j in range(ATTN_BLOCKS):
        rows = slice(j * BLOCK, (j + 1) * BLOCK)
        prev_rows = slice((j - 1) * BLOCK, j * BLOCK)
        for kh in range(N_KV_HEADS):
            cols = slice(kh * KV_SLAB, (kh + 1) * KV_SLAB)
            qs = q_ref[rows, cols]
            q4 = jnp.concatenate([qs * qmask_ref[g] for g in range(Q_PER_KV)], axis=0)
            k_prev = kp_ref[:, cols] if j == 0 else kc_ref[prev_rows, cols]
            v_prev = vp_ref[:, cols] if j == 0 else vc_ref[prev_rows, cols]
            keys = jnp.concatenate([k_prev, kc_ref[rows, cols]], axis=0)
            vals = jnp.concatenate([v_prev, vc_ref[rows, cols]], axis=0)
            s4 = lax.dot_general(q4, keys, (((1,), (1,)), ((), ())), preferred_element_type=f32)
            acc = jnp.zeros((BLOCK, KV_SLAB), f32)
            for g in range(Q_PER_KV):
                head = kh * Q_PER_KV + g
                bias = bias_ref[first_table, head] if j == 0 else bias_ref[1, head]
                sg = s4[g * BLOCK:(g + 1) * BLOCK]
                s = jnp.where(from_prev, sg[:, :BLOCK], sg[:, BLOCK:]) + bias
                sink = sink_ref[head]
                m = jnp.max(s, axis=-1, keepdims=True)
                p = jnp.exp(s - m)
                den = jnp.sum(p, axis=-1, keepdims=True) + jnp.exp(sink - m)
                w = (p * (1.0 / den)).astype(bf16)
                o = _dot(jnp.concatenate([w * tri_ref[0], w * tri_ref[1]], axis=1), vals)
                acc = jnp.where(lane_head == g, o, acc)
            y_scr[rows, cols] = acc
    y = y_scr[...]
    o_ref[...] = (y * _rms_scale(y) * g_ref[...]).astype(bf16)


def _attention(sinks, q, k_rep, v_rep, bias2, qmask, tri, attn_out_g):
    rows = ATTN_BLOCKS * BLOCK
    w = N_KV_HEADS * KV_SLAB
    cur = pl.BlockSpec((None, rows, w), lambda b, n: (b, n, 0))
    prev = pl.BlockSpec((None, BLOCK, w), lambda b, n: (b, jnp.maximum(n * ATTN_BLOCKS - 1, 0), 0))
    return pl.pallas_call(
        _attn_kernel,
        grid=(BATCH, SEQ // rows),
        in_specs=[
            pl.BlockSpec(memory_space=pltpu.SMEM),
            cur, cur, prev, cur, prev,
            _const_spec((2, N_Q_HEADS, BLOCK, BLOCK)),
            _const_spec((Q_PER_KV, BLOCK, KV_SLAB)),
            _const_spec((2, BLOCK, BLOCK)),
            _const_spec((1, ATTN_WIDTH)),
        ],
        out_specs=cur,
        out_shape=jax.ShapeDtypeStruct((BATCH, SEQ, ATTN_WIDTH), bf16),
        scratch_shapes=[pltpu.VMEM((rows, ATTN_WIDTH), f32)],
        compiler_params=pltpu.CompilerParams(
            dimension_semantics=("arbitrary", "arbitrary"), vmem_limit_bytes=VMEM_LIMIT),
        name="attn",
    )(sinks, q, k_rep, k_rep, v_rep, v_rep, bias2, qmask, tri, attn_out_g)


def _gelu_tanh(x):
    return 0.5 * x * (1.0 + jnp.tanh(math.sqrt(2.0 / math.pi) * (x + 0.044715 * (x * x * x))))


def _sigmoid(x):
    return 1.0 / (1.0 + jnp.exp(-x))


def _ssm_kernel(u_ref, spread_ref, gather_ref, wb_ref, lam_ref, wc_ref, d_ref, wglu_ref, g_ref,
                o_ref, u2_scr, bu_scr, h_scr, y2_scr, y_scr, st_scr):
    @pl.when(pl.program_id(0) == 0)
    def _():
        st_scr[...] = jnp.zeros_like(st_scr)

    u_bt = u_ref[...].reshape(SSM_ROWS, SSM_WIDTH)
    u2 = _dot(spread_ref[...], u_bt)
    row_par = lax.broadcasted_iota(jnp.int32, (SSM_ROWS2, V7X_LANES), 0) % 2
    pair_par = (lax.broadcasted_iota(jnp.int32, (SSM_ROWS2, V7X_LANES), 1) // (2 * SSM_GROUP)) % 2
    own_pair = row_par == pair_par

    for c in range(LANE_CHUNKS):
        uc = u2[:, c * V7X_LANES:(c + 1) * V7X_LANES]
        u2_scr[c] = uc
        bu_scr[c] = _dot(jnp.where(own_pair, uc, 0.0).astype(bf16), wb_ref[c])

    half = LANE_CHUNKS // 2
    for c0 in (0, half):
        chains = [(c, s) for c in range(c0, c0 + half) for s in range(SLOTS)]

        def cols(s):
            base = s * 2 * V7X_LANES
            return slice(base, base + V7X_LANES), slice(base + V7X_LANES, base + 2 * V7X_LANES)

        lams = [(lam_ref[c, :, cols(s)[0]], lam_ref[c, :, cols(s)[1]]) for c, s in chains]
        init = tuple((st_scr[c, :, cols(s)[0]], st_scr[c, :, cols(s)[1]]) for c, s in chains)

        def body(t, carry, chains=chains, lams=lams, cols=cols):
            rows = pl.ds(t * V7X_SUBLANES, V7X_SUBLANES)
            out = []
            for (c, s), (l_re, l_im), (s_re, s_im) in zip(chains, lams, carry):
                cr, ci = cols(s)
                n_re = (l_re * s_re - l_im * s_im) + bu_scr[c, rows, cr]
                n_im = (l_re * s_im + l_im * s_re) + bu_scr[c, rows, ci]
                h_scr[c, rows, cr] = n_re
                h_scr[c, rows, ci] = n_im
                out.append((n_re, n_im))
            return tuple(out)

        fin = init
        for t in range(SSM_T):
            fin = body(t, fin)
        for (c, s), (s_re, s_im) in zip(chains, fin):
            st_scr[c, :, cols(s)[0]] = s_re
            st_scr[c, :, cols(s)[1]] = s_im

    even = pl.ds(0, SSM_ROWS, stride=2)
    odd = pl.ds(1, SSM_ROWS, stride=2)
    for c in range(LANE_CHUNKS):
        cols_c = slice(c * V7X_LANES, (c + 1) * V7X_LANES)
        y2 = _dot(h_scr[c].astype(bf16), wc_ref[c])
        y2_scr[2 * c] = y2[:, :V7X_LANES]
        y2_scr[2 * c + 1] = y2[:, V7X_LANES:]
        yc = y2_scr.at[2 * c][even, :] + y2_scr.at[2 * c + 1][odd, :]
        yc = yc + d_ref[:, cols_c] * u2_scr.at[c][even, :]
        y_scr[:, cols_c] = _gelu_tanh(yc)
    y = y_scr[...]
    out = y * _sigmoid(_dot(y.astype(bf16), wglu_ref[...]))
    out_tb = (out * _rms_scale(out) * g_ref[...]).astype(bf16)
    o_ref[...] = _dot(gather_ref[...], out_tb).astype(bf16).reshape(BATCH, SSM_T, SSM_WIDTH)


def _ssm(u, spread, gather, wb, lam, wc, d, wglu, ssm_out_g):
    return pl.pallas_call(
        _ssm_kernel,
        grid=(SEQ // SSM_T,),
        in_specs=[
            pl.BlockSpec((BATCH, SSM_T, SSM_WIDTH), lambda i: (0, i, 0)),
            _const_spec((SSM_ROWS2, SSM_ROWS)),
            _const_spec((SSM_ROWS, SSM_ROWS)),
            _const_spec((LANE_CHUNKS, V7X_LANES, SLOT_COLS)),
            _const_spec((LANE_CHUNKS, V7X_SUBLANES, SLOT_COLS)),
            _const_spec((LANE_CHUNKS, SLOT_COLS, 2 * V7X_LANES)),
            _const_spec((1, SSM_WIDTH)),
            _const_spec((SSM_WIDTH, SSM_WIDTH)),
            _const_spec((1, SSM_WIDTH)),
        ],
        out_specs=pl.BlockSpec((BATCH, SSM_T, SSM_WIDTH), lambda i: (0, i, 0)),
        out_shape=jax.ShapeDtypeStruct((BATCH, SEQ, SSM_WIDTH), bf16),
        scratch_shapes=[
            pltpu.VMEM((LANE_CHUNKS, SSM_ROWS2, V7X_LANES), f32),
            pltpu.VMEM((LANE_CHUNKS, SSM_ROWS2, SLOT_COLS), f32),
            pltpu.VMEM((LANE_CHUNKS, SSM_ROWS2, SLOT_COLS), f32),
            pltpu.VMEM((2 * LANE_CHUNKS, SSM_ROWS2, V7X_LANES), f32),
            pltpu.VMEM((SSM_ROWS, SSM_WIDTH), f32),
            pltpu.VMEM((LANE_CHUNKS, V7X_SUBLANES, SLOT_COLS), f32),
        ],
        compiler_params=pltpu.CompilerParams(
            dimension_semantics=("arbitrary",), vmem_limit_bytes=VMEM_LIMIT),
        name="ssm",
    )(u, spread, gather, wb, lam, wc, d, wglu, ssm_out_g)


def _out_proj_kernel(a_ref, s_ref, x_ref, wa_ref, ws_ref, g_ref, o_ref, h_ref):
    x1 = x_ref[...] + _dot(a_ref[...], wa_ref[...]) + _dot(s_ref[...], ws_ref[...])
    o_ref[...] = x1
    h_ref[...] = (x1 * _rms_scale(x1) * g_ref[...]).astype(bf16)


def _out_proj(a, s, x, wa, ws, ln2_g):
    nt = SEQ // IN_TM
    row_spec = lambda w: pl.BlockSpec((None, IN_TM, w), lambda b, t: (b, t, 0))
    return pl.pallas_call(
        _out_proj_kernel,
        grid=(BATCH, nt),
        in_specs=[
            row_spec(ATTN_WIDTH),
            row_spec(SSM_WIDTH),
            row_spec(D_MODEL),
            _const_spec((ATTN_WIDTH, D_MODEL)),
            _const_spec((SSM_WIDTH, D_MODEL)),
            _const_spec((1, D_MODEL)),
        ],
        out_specs=[row_spec(D_MODEL), row_spec(D_MODEL)],
        out_shape=[jax.ShapeDtypeStruct((BATCH, SEQ, D_MODEL), f32),
                   jax.ShapeDtypeStruct((BATCH, SEQ, D_MODEL), bf16)],
        compiler_params=pltpu.CompilerParams(
            dimension_semantics=("arbitrary", "arbitrary"), vmem_limit_bytes=VMEM_LIMIT),
        name="out_proj",
    )(a, s, x, wa, ws, ln2_g)


def _ffn_kernel(x_hbm, h_ref, wg_ref, wu_ref, wd_ref, o_ref, sem):
    first = pl.program_id(1) == 0
    rows = pl.ds(pl.multiple_of(pl.program_id(0) * FFN_TM, FFN_TM), FFN_TM)
    residual_copy = pltpu.make_async_copy(x_hbm.at[rows, :], o_ref, sem)

    @pl.when(first)
    def _():
        residual_copy.start()

    h = h_ref[...]
    gate = _dot(h, wg_ref[...].astype(bf16))
    up = _dot(h, wu_ref[...].astype(bf16))
    act = (gate * _sigmoid(gate) * up).astype(bf16)

    @pl.when(first)
    def _():
        residual_copy.wait()

    o_ref[...] += _dot(act, wd_ref[...].astype(bf16))


def _ffn(x, h, wg, wu, wd):
    n_rows = BATCH * SEQ
    return pl.pallas_call(
        _ffn_kernel,
        grid=(n_rows // FFN_TM, FF_HIDDEN // FFN_TF),
        in_specs=[
            pl.BlockSpec(memory_space=pl.ANY),
            pl.BlockSpec((FFN_TM, D_MODEL), lambda i, f: (i, 0)),
            pl.BlockSpec((D_MODEL, FFN_TF), lambda i, f: (0, f)),
            pl.BlockSpec((D_MODEL, FFN_TF), lambda i, f: (0, f)),
            pl.BlockSpec((FFN_TF, D_MODEL), lambda i, f: (f, 0)),
        ],
        out_specs=pl.BlockSpec((FFN_TM, D_MODEL), lambda i, f: (i, 0)),
        out_shape=jax.ShapeDtypeStruct((n_rows, D_MODEL), f32),
        scratch_shapes=[pltpu.SemaphoreType.DMA(())],
        compiler_params=pltpu.CompilerParams(
            dimension_semantics=("arbitrary", "arbitrary"), vmem_limit_bytes=VMEM_LIMIT),
        name="ffn",
    )(x, h, wg, wu, wd)


def _t5_bucket(dist):
    n = np.maximum(dist, 0)
    max_exact = REL_BUCKETS // 2
    nf = np.maximum(n, 1).astype(np.float32)
    large = max_exact + (np.log(nf / max_exact) / math.log(REL_MAX_DISTANCE / max_exact)
                         * (REL_BUCKETS - max_exact)).astype(np.int32)
    large = np.minimum(large, REL_BUCKETS - 1)
    return np.where(n < max_exact, n, large).astype(np.int32)


def _bias_kernel(rb_ref, bucket_ref, valid_ref, o_ref):
    head = pl.program_id(1)
    bucket = bucket_ref[...]
    acc = jnp.zeros((BLOCK, BLOCK), f32)
    for k in range(REL_BUCKETS):
        acc = jnp.where(bucket == k, rb_ref[k, head], acc)
    o_ref[0, 0] = jnp.where(valid_ref[0] != 0, acc, MASK_VALUE)


def _bias_tables(rel_bias):
    qi = np.arange(BLOCK)[:, None]
    kj = np.arange(BLOCK)[None, :]
    from_prev = kj > qi
    dist = np.where(from_prev, qi + BLOCK - kj, qi - kj)
    assert ((dist >= 0) & (dist < WINDOW)).all()
    valid = np.stack([~from_prev, np.ones_like(from_prev)]).astype(np.int32)
    return pl.pallas_call(
        _bias_kernel,
        grid=(2, N_Q_HEADS),
        in_specs=[
            pl.BlockSpec(memory_space=pltpu.SMEM),
            _const_spec((BLOCK, BLOCK)),
            pl.BlockSpec((1, BLOCK, BLOCK), lambda t, h: (t, 0, 0)),
        ],
        out_specs=pl.BlockSpec((1, 1, BLOCK, BLOCK), lambda t, h: (t, h, 0, 0)),
        out_shape=jax.ShapeDtypeStruct((2, N_Q_HEADS, BLOCK, BLOCK), f32),
        name="bias_table",
    )(rel_bias.astype(f32), jnp.asarray(_t5_bucket(dist)), jnp.asarray(valid))


def _ssm_params(a_re, a_im, log_dt, b_re, b_im, c_re, c_im):
    dt = jnp.exp(log_dt)[:, None]
    mag = jnp.exp(a_re * dt)
    ang = a_im * dt
    lb_re, lb_im = mag * jnp.cos(ang), mag * jnp.sin(ang)
    nr, ni = lb_re - 1.0, lb_im
    den = a_re * a_re + a_im * a_im
    f_re = (nr * a_re + ni * a_im) / den
    f_im = (ni * a_re - nr * a_im) / den
    bb_re = f_re[..., None] * b_re - f_im[..., None] * b_im
    bb_im = f_re[..., None] * b_im + f_im[..., None] * b_re

    eye = np.eye(2, dtype=np.float32)
    split = lambda t: t.reshape((2, LANE_CHUNKS, SLOTS, 2, 2) + t.shape[2:])
    bbs = split(jnp.stack([bb_re, bb_im]))
    wb = jnp.einsum('us,gh,acsjgnp->cujgpsahn', eye, eye, bbs).reshape(LANE_CHUNKS, V7X_LANES, SLOT_COLS)
    cs = split(jnp.stack([c_re, -c_im]))
    wc = jnp.einsum('vj,us,gh,acsjgpn->csagnvujhp', eye, eye, eye, cs).reshape(
        LANE_CHUNKS, SLOT_COLS, 2 * V7X_LANES)
    lam = jnp.transpose(split(jnp.stack([lb_re, lb_im])), (1, 3, 2, 0, 4, 5)).reshape(LANE_CHUNKS, 1, 2, SLOT_COLS)
    lam = jnp.broadcast_to(lam, (LANE_CHUNKS, BATCH, 2, SLOT_COLS)).reshape(LANE_CHUNKS, V7X_SUBLANES, SLOT_COLS)
    return wb.astype(bf16), wc.astype(bf16), lam


def _layer(x, rel_bias, ln1_g, w_in, q_norm_g, k_norm_g, attn_sinks, ssm_a_re, ssm_a_im,
           ssm_log_dt, ssm_b_re, ssm_b_im, ssm_c_re, ssm_c_im, ssm_d, w_glu,
           attn_out_g, ssm_out_g, w_out, ln2_g, w_ff_gate, w_ff_up, w_ff_down):
    row = lambda v: v.reshape(1, -1).astype(f32)

    lanes = np.arange(KV_SLAB)
    seg = jnp.asarray(lanes[:, None] // HEAD_DIM == lanes[None, :] // HEAD_DIM, bf16)
    src = np.arange(KV_WIDTH)
    dst = np.arange(N_KV_HEADS * KV_SLAB)
    rep = jnp.asarray((src[:, None] // HEAD_DIM == dst[None, :] // KV_SLAB)
                      & (src[:, None] % HEAD_DIM == dst[None, :] % HEAD_DIM), bf16)
    gqk = row(jnp.concatenate([jnp.tile(q_norm_g, N_Q_HEADS) * (HEAD_DIM ** -0.5),
                               jnp.tile(k_norm_g, N_KV_HEADS)]))
    qmask = jnp.asarray(np.broadcast_to(
        (lanes[None, None, :] // HEAD_DIM) == np.arange(Q_PER_KV)[:, None, None],
        (Q_PER_KV, BLOCK, KV_SLAB)), bf16)

    q, k_rep, v_rep, u = _in_proj(x, row(ln1_g), w_in.astype(bf16), seg, rep, gqk)

    pos = np.arange(BLOCK)
    from_prev = pos[None, :] > pos[:, None]
    tri = jnp.asarray(np.stack([from_prev, ~from_prev]), bf16)
    y_attn = _attention(attn_sinks.astype(f32), q, k_rep, v_rep, _bias_tables(rel_bias), qmask, tri,
                        row(attn_out_g))

    wb, wc, lam = _ssm_params(ssm_a_re.astype(f32), ssm_a_im.astype(f32), ssm_log_dt.astype(f32),
                                     ssm_b_re.astype(f32), ssm_b_im.astype(f32),
                                     ssm_c_re.astype(f32), ssm_c_im.astype(f32))
    r_bt = np.arange(SSM_ROWS)
    r_tb = (r_bt % SSM_T) * BATCH + r_bt // SSM_T
    gather = r_tb[:, None] == np.arange(SSM_ROWS)[None, :]
    spread = np.arange(SSM_ROWS2)[:, None] // 2 == r_tb[None, :]
    y_ssm = _ssm(u, jnp.asarray(spread, bf16), jnp.asarray(gather, bf16), wb, lam, wc, row(ssm_d),
                 w_glu.astype(bf16), row(ssm_out_g))

    x1, h2 = _out_proj(y_attn, y_ssm, x, w_out[:ATTN_WIDTH].astype(bf16),
                       w_out[ATTN_WIDTH:].astype(bf16), row(ln2_g))

    out = _ffn(x1.reshape(BATCH * SEQ, D_MODEL), h2.reshape(BATCH * SEQ, D_MODEL),
               w_ff_gate.astype(f32), w_ff_up.astype(f32), w_ff_down.astype(f32))
    return out.reshape(BATCH, SEQ, D_MODEL)


def kernel(x, rel_bias, ln1_g, w_in, q_norm_g, k_norm_g, attn_sinks, ssm_a_re, ssm_a_im, ssm_log_dt, ssm_b_re, ssm_b_im, ssm_c_re, ssm_c_im, ssm_d, w_glu, attn_out_g, ssm_out_g, w_out, ln2_g, w_ff_gate, w_ff_up, w_ff_down):
    for l in range(ln1_g.shape[0]):
        x = _layer(x, rel_bias, ln1_g[l], w_in[l], q_norm_g[l], k_norm_g[l], attn_sinks[l],
                   ssm_a_re[l], ssm_a_im[l], ssm_log_dt[l], ssm_b_re[l], ssm_b_im[l],
                   ssm_c_re[l], ssm_c_im[l], ssm_d[l], w_glu[l], attn_out_g[l], ssm_out_g[l],
                   w_out[l], ln2_g[l], w_ff_gate[l], w_ff_up[l], w_ff_down[l])
    return x
```

```python
import functools
import math

import jax
import jax.numpy as jnp
import numpy as np
from jax import lax
from jax.experimental import pallas as pl
from jax.experimental.pallas import tpu as pltpu

D_MODEL = 2048
BATCH = 4
SEQ = 4096
HEAD_DIM = 64
N_Q_HEADS = 16
N_KV_HEADS = 4
Q_PER_KV = N_Q_HEADS // N_KV_HEADS
ATTN_WIDTH = N_Q_HEADS * HEAD_DIM
KV_WIDTH = N_KV_HEADS * HEAD_DIM
WINDOW = 128
BLOCK = 128
SSM_WIDTH = D_MODEL - ATTN_WIDTH
IN_WIDTH = ATTN_WIDTH + 2 * KV_WIDTH + SSM_WIDTH
SSM_GROUP = 16
SSM_GROUPS = SSM_WIDTH // SSM_GROUP
SSM_STATE = 64
FF_HIDDEN = 5632
REL_BUCKETS = 32
REL_MAX_DISTANCE = 128
EPS = 1e-6

V7X_LANES = 128
V7X_SUBLANES = 8
V7X_VMEM_BYTES = 64 * 1024 * 1024
VMEM_LIMIT = V7X_VMEM_BYTES - 8 * 1024 * 1024

MASK_VALUE = -1e30
KV_SLAB = Q_PER_KV * HEAD_DIM
LANE_CHUNKS = SSM_WIDTH // V7X_LANES
SLOTS = 2
SLOT_COLS = SLOTS * 2 * V7X_LANES

ATTN_BLOCKS = 4
IN_TM = 512
FFN_TM = 1024
FFN_TF = 512
SSM_T = 64
SSM_ROWS = SSM_T * BATCH
SSM_ROWS2 = 2 * SSM_ROWS

f32 = jnp.float32
bf16 = jnp.bfloat16


def _dot(a, b):
    return jnp.dot(a, b, preferred_element_type=f32)


def _rms_scale(x):
    return lax.rsqrt(jnp.mean(x * x, axis=-1, keepdims=True) + EPS)


def _in_proj_kernel(x_ref, g_ref, w_ref, seg_ref, gqk_ref, q_ref, k_ref, v_ref, u_ref, proj_scr):
    xf = x_ref[...]
    h = (xf * _rms_scale(xf) * g_ref[...]).astype(bf16)
    proj_scr[...] = _dot(h, w_ref[...])

    n_slabs = (ATTN_WIDTH + KV_WIDTH) // KV_SLAB
    slab = lambda c: proj_scr[:, c * KV_SLAB:(c + 1) * KV_SLAB]
    sq = jnp.concatenate([slab(c) * slab(c) for c in range(n_slabs)], axis=0)
    hi = sq.astype(bf16)
    lo = (sq - hi.astype(f32)).astype(bf16)
    ss = _dot(jnp.concatenate([hi, lo], axis=0), seg_ref[...])
    ss = ss[:n_slabs * IN_TM] + ss[n_slabs * IN_TM:]
    normed = []
    for c in range(n_slabs):
        cols = slice(c * KV_SLAB, (c + 1) * KV_SLAB)
        scale = lax.rsqrt(ss[c * IN_TM:(c + 1) * IN_TM] * (1.0 / HEAD_DIM) + EPS)
        normed.append((slab(c) * scale * gqk_ref[:, cols]).astype(bf16))
    for c in range(n_slabs - 1):
        q_ref[:, c * KV_SLAB:(c + 1) * KV_SLAB] = normed[c]
    k_ref[...] = normed[-1]
    v_ref[...] = proj_scr[:, ATTN_WIDTH + KV_WIDTH:ATTN_WIDTH + 2 * KV_WIDTH].astype(bf16)
    u_ref[...] = proj_scr[:, ATTN_WIDTH + 2 * KV_WIDTH:].astype(bf16)


def _const_spec(shape):
    nd = len(shape)
    return pl.BlockSpec(shape, lambda *_: (0,) * nd)


def _in_proj(x, ln1_g, w, seg, gqk):
    nt = SEQ // IN_TM
    row_spec = lambda w: pl.BlockSpec((None, IN_TM, w), lambda b, t: (b, t, 0))
    return pl.pallas_call(
        _in_proj_kernel,
        grid=(BATCH, nt),
        in_specs=[
            row_spec(D_MODEL),
            _const_spec((1, D_MODEL)),
            _const_spec((D_MODEL, IN_WIDTH)),
            _const_spec((KV_SLAB, KV_SLAB)),
            _const_spec((1, ATTN_WIDTH + KV_WIDTH)),
        ],
        out_specs=[row_spec(ATTN_WIDTH), row_spec(KV_WIDTH), row_spec(KV_WIDTH), row_spec(SSM_WIDTH)],
        out_shape=[
            jax.ShapeDtypeStruct((BATCH, SEQ, ATTN_WIDTH), bf16),
            jax.ShapeDtypeStruct((BATCH, SEQ, KV_WIDTH), bf16),
            jax.ShapeDtypeStruct((BATCH, SEQ, KV_WIDTH), bf16),
            jax.ShapeDtypeStruct((BATCH, SEQ, SSM_WIDTH), bf16),
        ],
        scratch_shapes=[pltpu.VMEM((IN_TM, IN_WIDTH), f32)],
        compiler_params=pltpu.CompilerParams(
            dimension_semantics=("arbitrary", "arbitrary"), vmem_limit_bytes=VMEM_LIMIT),
        name="in_proj",
    )(x, ln1_g, w, seg, gqk)


def _attn_kernel(sink_ref, q_ref, kc_ref, kp_ref, vc_ref, vp_ref, bias_ref, qmask_ref, tri_ref, g_ref,
                 o_ref, y_scr):
    qi = lax.broadcasted_iota(jnp.int32, (BLOCK, BLOCK), 0)
    kj = lax.broadcasted_iota(jnp.int32, (BLOCK, BLOCK), 1)
    from_prev = kj > qi
    first_table = jnp.minimum(pl.program_id(1), 1)
    for j in range(ATTN_BLOCKS):
        rows = slice(j * BLOCK, (j + 1) * BLOCK)
        prev_rows = slice((j - 1) * BLOCK, j * BLOCK)
        k_prev = kp_ref[...] if j == 0 else kc_ref[prev_rows, :]
        v_prev = vp_ref[...] if j == 0 else vc_ref[prev_rows, :]
        keys = jnp.concatenate([k_prev, kc_ref[rows, :]], axis=0)
        vals = jnp.concatenate([v_prev, vc_ref[rows, :]], axis=0)
        for kh in range(N_KV_HEADS):
            q4 = jnp.concatenate(
                [q_ref[rows, g * KV_WIDTH:(g + 1) * KV_WIDTH] * qmask_ref[kh] for g in range(Q_PER_KV)], axis=0)
            s4 = lax.dot_general(q4, keys, (((1,), (1,)), ((), ())), preferred_element_type=f32)
            lanes = slice(kh * HEAD_DIM, (kh + 1) * HEAD_DIM)
            for g in range(Q_PER_KV):
                head = kh * Q_PER_KV + g
                bias = bias_ref[first_table, head] if j == 0 else bias_ref[1, head]
                sg = s4[g * BLOCK:(g + 1) * BLOCK]
                s = jnp.where(from_prev, sg[:, :BLOCK], sg[:, BLOCK:]) + bias
                sink = sink_ref[head]
                m = jnp.max(s, axis=-1, keepdims=True)
                p = jnp.exp(s - m)
                den = jnp.sum(p, axis=-1, keepdims=True) + jnp.exp(sink - m)
                w = (p * (1.0 / den)).astype(bf16)
                o = _dot(jnp.concatenate([w * tri_ref[0], w * tri_ref[1]], axis=1), vals)
                y_scr[rows, g * KV_WIDTH + kh * HEAD_DIM:g * KV_WIDTH + (kh + 1) * HEAD_DIM] = o[:, lanes]
    y = y_scr[...]
    o_ref[...] = (y * _rms_scale(y) * g_ref[...]).astype(bf16)


def _attention(sinks, q, k, v, bias2, qmask, tri, attn_out_g):
    rows = ATTN_BLOCKS * BLOCK
    cur = pl.BlockSpec((None, rows, ATTN_WIDTH), lambda b, n: (b, n, 0))
    kv_cur = pl.BlockSpec((None, rows, KV_WIDTH), lambda b, n: (b, n, 0))
    kv_prev = pl.BlockSpec((None, BLOCK, KV_WIDTH), lambda b, n: (b, jnp.maximum(n * ATTN_BLOCKS - 1, 0), 0))
    return pl.pallas_call(
        _attn_kernel,
        grid=(BATCH, SEQ // rows),
        in_specs=[
            pl.BlockSpec(memory_space=pltpu.SMEM),
            cur, kv_cur, kv_prev, kv_cur, kv_prev,
            _const_spec((2, N_Q_HEADS, BLOCK, BLOCK)),
            _const_spec((Q_PER_KV, BLOCK, KV_SLAB)),
            _const_spec((2, BLOCK, BLOCK)),
            _const_spec((1, ATTN_WIDTH)),
        ],
        out_specs=cur,
        out_shape=jax.ShapeDtypeStruct((BATCH, SEQ, ATTN_WIDTH), bf16),
        scratch_shapes=[pltpu.VMEM((rows, ATTN_WIDTH), f32)],
        compiler_params=pltpu.CompilerParams(
            dimension_semantics=("arbitrary", "arbitrary"), vmem_limit_bytes=VMEM_LIMIT),
        name="attn",
    )(sinks, q, k, k, v, v, bias2, qmask, tri, attn_out_g)


def _gelu_tanh(x):
    return 0.5 * x * (1.0 + jnp.tanh(math.sqrt(2.0 / math.pi) * (x + 0.044715 * (x * x * x))))


def _sigmoid(x):
    return 1.0 / (1.0 + jnp.exp(-x))


def _ssm_kernel(u_ref, spread_ref, gather_ref, wb_ref, lam_ref, wc_ref, d_ref, wglu_ref, g_ref,
                o_ref, u2_scr, bu_scr, h_scr, y2_scr, y_scr, st_scr):
    @pl.when(pl.program_id(0) == 0)
    def _():
        st_scr[...] = jnp.zeros_like(st_scr)

    u_bt = u_ref[...].reshape(SSM_ROWS, SSM_WIDTH)
    u2 = _dot(spread_ref[...], u_bt)
    row_par = lax.broadcasted_iota(jnp.int32, (SSM_ROWS2, V7X_LANES), 0) % 2
    pair_par = (lax.broadcasted_iota(jnp.int32, (SSM_ROWS2, V7X_LANES), 1) // (2 * SSM_GROUP)) % 2
    own_pair = row_par == pair_par

    for c in range(LANE_CHUNKS):
        uc = u2[:, c * V7X_LANES:(c + 1) * V7X_LANES]
        u2_scr[c] = uc
        bu_scr[c] = _dot(jnp.where(own_pair, uc, 0.0).astype(bf16), wb_ref[c])

    half = LANE_CHUNKS // 2
    for c0 in (0, half):
        chains = [(c, s) for c in range(c0, c0 + half) for s in range(SLOTS)]

        def cols(s):
            base = s * 2 * V7X_LANES
            return slice(base, base + V7X_LANES), slice(base + V7X_LANES, base + 2 * V7X_LANES)

        lams = [(lam_ref[c, :, cols(s)[0]], lam_ref[c, :, cols(s)[1]]) for c, s in chains]
        init = tuple((st_scr[c, :, cols(s)[0]], st_scr[c, :, cols(s)[1]]) for c, s in chains)

        def body(t, carry, chains=chains, lams=lams, cols=cols):
            rows = pl.ds(t * V7X_SUBLANES, V7X_SUBLANES)
            out = []
            for (c, s), (l_re, l_im), (s_re, s_im) in zip(chains, lams, carry):
                cr, ci = cols(s)
                n_re = (l_re * s_re - l_im * s_im) + bu_scr[c, rows, cr]
                n_im = (l_re * s_im + l_im * s_re) + bu_scr[c, rows, ci]
                h_scr[c, rows, cr] = n_re
                h_scr[c, rows, ci] = n_im
                out.append((n_re, n_im))
            return tuple(out)

        fin = init
        for t in range(SSM_T):
            fin = body(t, fin)
        for (c, s), (s_re, s_im) in zip(chains, fin):
            st_scr[c, :, cols(s)[0]] = s_re
            st_scr[c, :, cols(s)[1]] = s_im

    even = pl.ds(0, SSM_ROWS, stride=2)
    odd = pl.ds(1, SSM_ROWS, stride=2)
    for c in range(LANE_CHUNKS):
        cols_c = slice(c * V7X_LANES, (c + 1) * V7X_LANES)
        y2 = _dot(h_scr[c].astype(bf16), wc_ref[c])
        y2_scr[2 * c] = y2[:, :V7X_LANES]
        y2_scr[2 * c + 1] = y2[:, V7X_LANES:]
        yc = y2_scr.at[2 * c][even, :] + y2_scr.at[2 * c + 1][odd, :]
        yc = yc + d_ref[:, cols_c] * u2_scr.at[c][even, :]
        y_scr[:, cols_c] = _gelu_tanh(yc)
    y = y_scr[...]
    out = y * _sigmoid(_dot(y.astype(bf16), wglu_ref[...]))
    out_tb = (out * _rms_scale(out) * g_ref[...]).astype(bf16)
    o_ref[...] = _dot(gather_ref[...], out_tb).astype(bf16).reshape(BATCH, SSM_T, SSM_WIDTH)


def _ssm(u, spread, gather, wb, lam, wc, d, wglu, ssm_out_g):
    return pl.pallas_call(
        _ssm_kernel,
        grid=(SEQ // SSM_T,),
        in_specs=[
            pl.BlockSpec((BATCH, SSM_T, SSM_WIDTH), lambda i: (0, i, 0)),
            _const_spec((SSM_ROWS2, SSM_ROWS)),
            _const_spec((SSM_ROWS, SSM_ROWS)),
            _const_spec((LANE_CHUNKS, V7X_LANES, SLOT_COLS)),
            _const_spec((LANE_CHUNKS, V7X_SUBLANES, SLOT_COLS)),
            _const_spec((LANE_CHUNKS, SLOT_COLS, 2 * V7X_LANES)),
            _const_spec((1, SSM_WIDTH)),
            _const_spec((SSM_WIDTH, SSM_WIDTH)),
            _const_spec((1, SSM_WIDTH)),
        ],
        out_specs=pl.BlockSpec((BATCH, SSM_T, SSM_WIDTH), lambda i: (0, i, 0)),
        out_shape=jax.ShapeDtypeStruct((BATCH, SEQ, SSM_WIDTH), bf16),
        scratch_shapes=[
            pltpu.VMEM((LANE_CHUNKS, SSM_ROWS2, V7X_LANES), f32),
            pltpu.VMEM((LANE_CHUNKS, SSM_ROWS2, SLOT_COLS), f32),
            pltpu.VMEM((LANE_CHUNKS, SSM_ROWS2, SLOT_COLS), f32),
            pltpu.VMEM((2 * LANE_CHUNKS, SSM_ROWS2, V7X_LANES), f32),
            pltpu.VMEM((SSM_ROWS, SSM_WIDTH), f32),
            pltpu.VMEM((LANE_CHUNKS, V7X_SUBLANES, SLOT_COLS), f32),
        ],
        compiler_params=pltpu.CompilerParams(
            dimension_semantics=("arbitrary",), vmem_limit_bytes=VMEM_LIMIT),
        name="ssm",
    )(u, spread, gather, wb, lam, wc, d, wglu, ssm_out_g)


def _out_proj_kernel(a_ref, s_ref, x_ref, wa_ref, ws_ref, g_ref, o_ref, h_ref):
    x1 = x_ref[...] + _dot(a_ref[...], wa_ref[...]) + _dot(s_ref[...], ws_ref[...])
    o_ref[...] = x1
    h_ref[...] = (x1 * _rms_scale(x1) * g_ref[...]).astype(bf16)


def _out_proj(a, s, x, wa, ws, ln2_g):
    nt = SEQ // IN_TM
    row_spec = lambda w: pl.BlockSpec((None, IN_TM, w), lambda b, t: (b, t, 0))
    return pl.pallas_call(
        _out_proj_kernel,
        grid=(BATCH, nt),
        in_specs=[
            row_spec(ATTN_WIDTH),
            row_spec(SSM_WIDTH),
            row_spec(D_MODEL),
            _const_spec((ATTN_WIDTH, D_MODEL)),
            _const_spec((SSM_WIDTH, D_MODEL)),
            _const_spec((1, D_MODEL)),
        ],
        out_specs=[row_spec(D_MODEL), row_spec(D_MODEL)],
        out_shape=[jax.ShapeDtypeStruct((BATCH, SEQ, D_MODEL), f32),
                   jax.ShapeDtypeStruct((BATCH, SEQ, D_MODEL), bf16)],
        compiler_params=pltpu.CompilerParams(
            dimension_semantics=("arbitrary", "arbitrary"), vmem_limit_bytes=VMEM_LIMIT),
        name="out_proj",
    )(a, s, x, wa, ws, ln2_g)


def _ffn_kernel(x_hbm, h_ref, wg_ref, wu_ref, wd_ref, o_ref, sem):
    first = pl.program_id(1) == 0
    rows = pl.ds(pl.multiple_of(pl.program_id(0) * FFN_TM, FFN_TM), FFN_TM)
    residual_copy = pltpu.make_async_copy(x_hbm.at[rows, :], o_ref, sem)

    @pl.when(first)
    def _():
        residual_copy.start()

    h = h_ref[...]
    gate = _dot(h, wg_ref[...].astype(bf16))
    up = _dot(h, wu_ref[...].astype(bf16))
    act = (gate * _sigmoid(gate) * up).astype(bf16)

    @pl.when(first)
    def _():
        residual_copy.wait()

    o_ref[...] += _dot(act, wd_ref[...].astype(bf16))


def _ffn(x, h, wg, wu, wd):
    n_rows = BATCH * SEQ
    return pl.pallas_call(
        _ffn_kernel,
        grid=(n_rows // FFN_TM, FF_HIDDEN // FFN_TF),
        in_specs=[
            pl.BlockSpec(memory_space=pl.ANY),
            pl.BlockSpec((FFN_TM, D_MODEL), lambda i, f: (i, 0)),
            pl.BlockSpec((D_MODEL, FFN_TF), lambda i, f: (0, f)),
            pl.BlockSpec((D_MODEL, FFN_TF), lambda i, f: (0, f)),
            pl.BlockSpec((FFN_TF, D_MODEL), lambda i, f: (f, 0)),
        ],
        out_specs=pl.BlockSpec((FFN_TM, D_MODEL), lambda i, f: (i, 0)),
        out_shape=jax.ShapeDtypeStruct((n_rows, D_MODEL), f32),
        scratch_shapes=[pltpu.SemaphoreType.DMA(())],
        compiler_params=pltpu.CompilerParams(
            dimension_semantics=("arbitrary", "arbitrary"), vmem_limit_bytes=VMEM_LIMIT),
        name="ffn",
    )(x, h, wg, wu, wd)


def _t5_bucket(dist):
    n = np.maximum(dist, 0)
    max_exact = REL_BUCKETS // 2
    nf = np.maximum(n, 1).astype(np.float32)
    large = max_exact + (np.log(nf / max_exact) / math.log(REL_MAX_DISTANCE / max_exact)
                         * (REL_BUCKETS - max_exact)).astype(np.int32)
    large = np.minimum(large, REL_BUCKETS - 1)
    return np.where(n < max_exact, n, large).astype(np.int32)


def _bias_kernel(rb_ref, bucket_ref, valid_ref, o_ref):
    head = pl.program_id(1)
    bucket = bucket_ref[...]
    acc = jnp.zeros((BLOCK, BLOCK), f32)
    for k in range(REL_BUCKETS):
        acc = jnp.where(bucket == k, rb_ref[k, head], acc)
    o_ref[0, 0] = jnp.where(valid_ref[0] != 0, acc, MASK_VALUE)


def _bias_tables(rel_bias):
    qi = np.arange(BLOCK)[:, None]
    kj = np.arange(BLOCK)[None, :]
    from_prev = kj > qi
    dist = np.where(from_prev, qi + BLOCK - kj, qi - kj)
    assert ((dist >= 0) & (dist < WINDOW)).all()
    valid = np.stack([~from_prev, np.ones_like(from_prev)]).astype(np.int32)
    return pl.pallas_call(
        _bias_kernel,
        grid=(2, N_Q_HEADS),
        in_specs=[
            pl.BlockSpec(memory_space=pltpu.SMEM),
            _const_spec((BLOCK, BLOCK)),
            pl.BlockSpec((1, BLOCK, BLOCK), lambda t, h: (t, 0, 0)),
        ],
        out_specs=pl.BlockSpec((1, 1, BLOCK, BLOCK), lambda t, h: (t, h, 0, 0)),
        out_shape=jax.ShapeDtypeStruct((2, N_Q_HEADS, BLOCK, BLOCK), f32),
        name="bias_table",
    )(rel_bias.astype(f32), jnp.asarray(_t5_bucket(dist)), jnp.asarray(valid))


def _ssm_params(a_re, a_im, log_dt, b_re, b_im, c_re, c_im):
    dt = jnp.exp(log_dt)[:, None]
    mag = jnp.exp(a_re * dt)
    ang = a_im * dt
    lb_re, lb_im = mag * jnp.cos(ang), mag * jnp.sin(ang)
    nr, ni = lb_re - 1.0, lb_im
    den = a_re * a_re + a_im * a_im
    f_re = (nr * a_re + ni * a_im) / den
    f_im = (ni * a_re - nr * a_im) / den
    bb_re = f_re[..., None] * b_re - f_im[..., None] * b_im
    bb_im = f_re[..., None] * b_im + f_im[..., None] * b_re

    eye = np.eye(2, dtype=np.float32)
    split = lambda t: t.reshape((2, LANE_CHUNKS, SLOTS, 2, 2) + t.shape[2:])
    bbs = split(jnp.stack([bb_re, bb_im]))
    wb = jnp.einsum('us,gh,acsjgnp->cujgpsahn', eye, eye, bbs).reshape(LANE_CHUNKS, V7X_LANES, SLOT_COLS)
    cs = split(jnp.stack([c_re, -c_im]))
    wc = jnp.einsum('vj,us,gh,acsjgpn->csagnvujhp', eye, eye, eye, cs).reshape(
        LANE_CHUNKS, SLOT_COLS, 2 * V7X_LANES)
    lam = jnp.transpose(split(jnp.stack([lb_re, lb_im])), (1, 3, 2, 0, 4, 5)).reshape(LANE_CHUNKS, 1, 2, SLOT_COLS)
    lam = jnp.broadcast_to(lam, (LANE_CHUNKS, BATCH, 2, SLOT_COLS)).reshape(LANE_CHUNKS, V7X_SUBLANES, SLOT_COLS)
    return wb.astype(bf16), wc.astype(bf16), lam


def _layer(x, rel_bias, ln1_g, w_in, q_norm_g, k_norm_g, attn_sinks, ssm_a_re, ssm_a_im,
           ssm_log_dt, ssm_b_re, ssm_b_im, ssm_c_re, ssm_c_im, ssm_d, w_glu,
           attn_out_g, ssm_out_g, w_out, ln2_g, w_ff_gate, w_ff_up, w_ff_down):
    row = lambda v: v.reshape(1, -1).astype(f32)

    lanes = np.arange(KV_SLAB)
    seg = jnp.asarray(lanes[:, None] // HEAD_DIM == lanes[None, :] // HEAD_DIM, bf16)
    def regroup(t, axis):
        shape = t.shape[:axis] + (N_KV_HEADS, Q_PER_KV, HEAD_DIM) + t.shape[axis + 1:]
        return jnp.swapaxes(t.reshape(shape), axis, axis + 1).reshape(t.shape)

    w_in_r = jnp.concatenate([regroup(w_in[:, :ATTN_WIDTH], 1), w_in[:, ATTN_WIDTH:]], axis=1).astype(bf16)
    gqk = row(jnp.concatenate([jnp.tile(q_norm_g, N_Q_HEADS) * (HEAD_DIM ** -0.5),
                               jnp.tile(k_norm_g, N_KV_HEADS)]))
    kv_mask = jnp.asarray(np.broadcast_to(
        (np.arange(KV_WIDTH)[None, None, :] // HEAD_DIM) == np.arange(N_KV_HEADS)[:, None, None],
        (N_KV_HEADS, BLOCK, KV_WIDTH)), bf16)

    q, k, v, u = _in_proj(x, row(ln1_g), w_in_r, seg, gqk)

    pos = np.arange(BLOCK)
    from_prev = pos[None, :] > pos[:, None]
    tri = jnp.asarray(np.stack([from_prev, ~from_prev]), bf16)
    y_attn = _attention(attn_sinks.astype(f32), q, k, v, _bias_tables(rel_bias), kv_mask, tri,
                        row(regroup(attn_out_g, 0)))

    wb, wc, lam = _ssm_params(ssm_a_re.astype(f32), ssm_a_im.astype(f32), ssm_log_dt.astype(f32),
                                     ssm_b_re.astype(f32), ssm_b_im.astype(f32),
                                     ssm_c_re.astype(f32), ssm_c_im.astype(f32))
    r_bt = np.arange(SSM_ROWS)
    r_tb = (r_bt % SSM_T) * BATCH + r_bt // SSM_T
    gather = r_tb[:, None] == np.arange(SSM_ROWS)[None, :]
    spread = np.arange(SSM_ROWS2)[:, None] // 2 == r_tb[None, :]
    y_ssm = _ssm(u, jnp.asarray(spread, bf16), jnp.asarray(gather, bf16), wb, lam, wc, row(ssm_d),
                 w_glu.astype(bf16), row(ssm_out_g))

    x1, h2 = _out_proj(y_attn, y_ssm, x, regroup(w_out[:ATTN_WIDTH], 0).astype(bf16),
                       w_out[ATTN_WIDTH:].astype(bf16), row(ln2_g))

    out = _ffn(x1.reshape(BATCH * SEQ, D_MODEL), h2.reshape(BATCH * SEQ, D_MODEL),
               w_ff_gate.astype(f32), w_ff_up.astype(f32), w_ff_down.astype(f32))
    return out.reshape(BATCH, SEQ, D_MODEL)


def kernel(x, rel_bias, ln1_g, w_in, q_norm_g, k_norm_g, attn_sinks, ssm_a_re, ssm_a_im, ssm_log_dt, ssm_b_re, ssm_b_im, ssm_c_re, ssm_c_im, ssm_d, w_glu, attn_out_g, ssm_out_g, w_out, ln2_g, w_ff_gate, w_ff_up, w_ff_down):
    for l in range(ln1_g.shape[0]):
        x = _layer(x, rel_bias, ln1_g[l], w_in[l], q_norm_g[l], k_norm_g[l], attn_sinks[l],
                   ssm_a_re[l], ssm_a_im[l], ssm_log_dt[l], ssm_b_re[l], ssm_b_im[l],
                   ssm_c_re[l], ssm_c_im[l], ssm_d[l], w_glu[l], attn_out_g[l], ssm_out_g[l],
                   w_out[l], ln2_g[l], w_ff_gate[l], w_ff_up[l], w_ff_down[l])
    return x
```

```python
import functools
import math

import jax
import jax.numpy as jnp
import numpy as np
from jax import lax
from jax.experimental import pallas as pl
from jax.experimental.pallas import tpu as pltpu

D_MODEL = 2048
BATCH = 4
SEQ = 4096
HEAD_DIM = 64
N_Q_HEADS = 16
N_KV_HEADS = 4
Q_PER_KV = N_Q_HEADS // N_KV_HEADS
ATTN_WIDTH = N_Q_HEADS * HEAD_DIM
KV_WIDTH = N_KV_HEADS * HEAD_DIM
WINDOW = 128
BLOCK = 128
SSM_WIDTH = D_MODEL - ATTN_WIDTH
IN_WIDTH = ATTN_WIDTH + 2 * KV_WIDTH + SSM_WIDTH
SSM_GROUP = 16
SSM_GROUPS = SSM_WIDTH // SSM_GROUP
SSM_STATE = 64
FF_HIDDEN = 5632
REL_BUCKETS = 32
REL_MAX_DISTANCE = 128
EPS = 1e-6

V7X_LANES = 128
V7X_SUBLANES = 8
V7X_VMEM_BYTES = 64 * 1024 * 1024
VMEM_LIMIT = V7X_VMEM_BYTES - 8 * 1024 * 1024

MASK_VALUE = -1e30
KV_SLAB = Q_PER_KV * HEAD_DIM
LANE_CHUNKS = SSM_WIDTH // V7X_LANES
SLOTS = 2
SLOT_COLS = SLOTS * 2 * V7X_LANES

ATTN_BLOCKS = 4
IN_TM = 512
FFN_TM = 1024
FFN_TF = 512
SSM_T = 64
SSM_ROWS = SSM_T * BATCH
SSM_ROWS2 = 2 * SSM_ROWS

f32 = jnp.float32
bf16 = jnp.bfloat16


def _dot(a, b):
    return jnp.dot(a, b, preferred_element_type=f32)


def _rms_scale(x):
    return lax.rsqrt(jnp.mean(x * x, axis=-1, keepdims=True) + EPS)


def _in_proj_kernel(x_ref, g_ref, w_ref, seg_ref, gqk_ref, q_ref, k_ref, v_ref, u_ref, proj_scr):
    xf = x_ref[...]
    h = (xf * _rms_scale(xf) * g_ref[...]).astype(bf16)
    proj_scr[...] = _dot(h, w_ref[...])

    n_slabs = (ATTN_WIDTH + KV_WIDTH) // KV_SLAB
    slab = lambda c: proj_scr[:, c * KV_SLAB:(c + 1) * KV_SLAB]
    sq = jnp.concatenate([slab(c) * slab(c) for c in range(n_slabs)], axis=0)
    hi = sq.astype(bf16)
    lo = (sq - hi.astype(f32)).astype(bf16)
    ss = _dot(jnp.concatenate([hi, lo], axis=0), seg_ref[...])
    ss = ss[:n_slabs * IN_TM] + ss[n_slabs * IN_TM:]
    normed = []
    for c in range(n_slabs):
        cols = slice(c * KV_SLAB, (c + 1) * KV_SLAB)
        scale = lax.rsqrt(ss[c * IN_TM:(c + 1) * IN_TM] * (1.0 / HEAD_DIM) + EPS)
        normed.append((slab(c) * scale * gqk_ref[:, cols]).astype(bf16))
    for c in range(n_slabs - 1):
        q_ref[:, c * KV_SLAB:(c + 1) * KV_SLAB] = normed[c]
    k_ref[...] = normed[-1]
    v_ref[...] = proj_scr[:, ATTN_WIDTH + KV_WIDTH:ATTN_WIDTH + 2 * KV_WIDTH].astype(bf16)
    u_ref[...] = proj_scr[:, ATTN_WIDTH + 2 * KV_WIDTH:]


def _const_spec(shape):
    nd = len(shape)
    return pl.BlockSpec(shape, lambda *_: (0,) * nd)


def _in_proj(x, ln1_g, w, seg, gqk):
    nt = SEQ // IN_TM
    row_spec = lambda w: pl.BlockSpec((None, IN_TM, w), lambda b, t: (b, t, 0))
    return pl.pallas_call(
        _in_proj_kernel,
        grid=(BATCH, nt),
        in_specs=[
            row_spec(D_MODEL),
            _const_spec((1, D_MODEL)),
            _const_spec((D_MODEL, IN_WIDTH)),
            _const_spec((KV_SLAB, KV_SLAB)),
            _const_spec((1, ATTN_WIDTH + KV_WIDTH)),
        ],
        out_specs=[row_spec(ATTN_WIDTH), row_spec(KV_WIDTH), row_spec(KV_WIDTH), row_spec(SSM_WIDTH)],
        out_shape=[
            jax.ShapeDtypeStruct((BATCH, SEQ, ATTN_WIDTH), bf16),
            jax.ShapeDtypeStruct((BATCH, SEQ, KV_WIDTH), bf16),
            jax.ShapeDtypeStruct((BATCH, SEQ, KV_WIDTH), bf16),
            jax.ShapeDtypeStruct((BATCH, SEQ, SSM_WIDTH), f32),
        ],
        scratch_shapes=[pltpu.VMEM((IN_TM, IN_WIDTH), f32)],
        compiler_params=pltpu.CompilerParams(
            dimension_semantics=("arbitrary", "arbitrary"), vmem_limit_bytes=VMEM_LIMIT),
        name="in_proj",
    )(x, ln1_g, w, seg, gqk)


def _attn_kernel(sink_ref, q_ref, kc_ref, kp_ref, vc_ref, vp_ref, bias_ref, qmask_ref, tri_ref, g_ref,
                 o_ref, y_scr):
    qi = lax.broadcasted_iota(jnp.int32, (BLOCK, BLOCK), 0)
    kj = lax.broadcasted_iota(jnp.int32, (BLOCK, BLOCK), 1)
    from_prev = kj > qi
    first_table = jnp.minimum(pl.program_id(1), 1)
    for j in range(ATTN_BLOCKS):
        rows = slice(j * BLOCK, (j + 1) * BLOCK)
        prev_rows = slice((j - 1) * BLOCK, j * BLOCK)
        k_prev = kp_ref[...] if j == 0 else kc_ref[prev_rows, :]
        v_prev = vp_ref[...] if j == 0 else vc_ref[prev_rows, :]
        keys = jnp.concatenate([k_prev, kc_ref[rows, :]], axis=0)
        vals = jnp.concatenate([v_prev, vc_ref[rows, :]], axis=0)
        for kh in range(N_KV_HEADS):
            q4 = jnp.concatenate(
                [q_ref[rows, g * KV_WIDTH:(g + 1) * KV_WIDTH] * qmask_ref[kh] for g in range(Q_PER_KV)], axis=0)
            s4 = lax.dot_general(q4, keys, (((1,), (1,)), ((), ())), preferred_element_type=f32)
            lanes = slice(kh * HEAD_DIM, (kh + 1) * HEAD_DIM)
            for g in range(Q_PER_KV):
                head = kh * Q_PER_KV + g
                bias = bias_ref[first_table, head] if j == 0 else bias_ref[1, head]
                sg = s4[g * BLOCK:(g + 1) * BLOCK]
                s = jnp.where(from_prev, sg[:, :BLOCK], sg[:, BLOCK:]) + bias
                sink = sink_ref[head]
                m = jnp.max(s, axis=-1, keepdims=True)
                p = jnp.exp(s - m)
                den = jnp.sum(p, axis=-1, keepdims=True) + jnp.exp(sink - m)
                w = (p * (1.0 / den)).astype(bf16)
                o = _dot(jnp.concatenate([w * tri_ref[0], w * tri_ref[1]], axis=1), vals)
                y_scr[rows, g * KV_WIDTH + kh * HEAD_DIM:g * KV_WIDTH + (kh + 1) * HEAD_DIM] = o[:, lanes]
    y = y_scr[...]
    o_ref[...] = (y * _rms_scale(y) * g_ref[...]).astype(bf16)


def _attention(sinks, q, k, v, bias2, qmask, tri, attn_out_g):
    rows = ATTN_BLOCKS * BLOCK
    cur = pl.BlockSpec((None, rows, ATTN_WIDTH), lambda b, n: (b, n, 0))
    kv_cur = pl.BlockSpec((None, rows, KV_WIDTH), lambda b, n: (b, n, 0))
    kv_prev = pl.BlockSpec((None, BLOCK, KV_WIDTH), lambda b, n: (b, jnp.maximum(n * ATTN_BLOCKS - 1, 0), 0))
    return pl.pallas_call(
        _attn_kernel,
        grid=(BATCH, SEQ // rows),
        in_specs=[
            pl.BlockSpec(memory_space=pltpu.SMEM),
            cur, kv_cur, kv_prev, kv_cur, kv_prev,
            _const_spec((2, N_Q_HEADS, BLOCK, BLOCK)),
            _const_spec((Q_PER_KV, BLOCK, KV_SLAB)),
            _const_spec((2, BLOCK, BLOCK)),
            _const_spec((1, ATTN_WIDTH)),
        ],
        out_specs=cur,
        out_shape=jax.ShapeDtypeStruct((BATCH, SEQ, ATTN_WIDTH), bf16),
        scratch_shapes=[pltpu.VMEM((rows, ATTN_WIDTH), f32)],
        compiler_params=pltpu.CompilerParams(
            dimension_semantics=("arbitrary", "arbitrary"), vmem_limit_bytes=VMEM_LIMIT),
        name="attn",
    )(sinks, q, k, k, v, v, bias2, qmask, tri, attn_out_g)


def _gelu_tanh(x):
    return 0.5 * x * (1.0 + jnp.tanh(math.sqrt(2.0 / math.pi) * (x + 0.044715 * (x * x * x))))


def _sigmoid(x):
    return 1.0 / (1.0 + jnp.exp(-x))


def _ssm_kernel(u_ref, uprev_ref, wb_ref, lam_ref, wc_ref, d_ref, wglu_ref, g_ref,
                o_ref, u2_scr, bu_scr, h_scr, y2_scr, y_scr, st_scr):
    step = pl.program_id(0)

    @pl.when(step == 0)
    def _():
        st_scr[...] = jnp.zeros_like(st_scr)
        h_scr[1] = jnp.zeros(h_scr.shape[1:], f32)

    for parity in range(2):
        pl.when(step % 2 == parity)(functools.partial(
            _ssm_step, u_ref, uprev_ref, wb_ref, lam_ref, wc_ref, d_ref, wglu_ref, g_ref, o_ref,
            u2_scr, bu_scr, h_scr.at[parity], h_scr.at[1 - parity], y2_scr, y_scr, st_scr))


def _ssm_step(u_ref, uprev_ref, wb_ref, lam_ref, wc_ref, d_ref, wglu_ref, g_ref, o_ref,
              u2_scr, bu_scr, h_new, h_old, y2_scr, y_scr, st_scr):
    for c in range(LANE_CHUNKS):
        for b in range(BATCH):
            for par in range(2):
                u2_scr.at[c][pl.ds(2 * b + par, SSM_T, stride=V7X_SUBLANES), :] = (
                    u_ref[b, :, c * V7X_LANES:(c + 1) * V7X_LANES])
    row_par = lax.broadcasted_iota(jnp.int32, (SSM_ROWS2, V7X_LANES), 0) % 2
    pair_par = (lax.broadcasted_iota(jnp.int32, (SSM_ROWS2, V7X_LANES), 1) // (2 * SSM_GROUP)) % 2
    own_pair = row_par == pair_par

    for c in range(LANE_CHUNKS):
        bu_scr[c] = _dot(jnp.where(own_pair, u2_scr[c], 0.0).astype(bf16), wb_ref[c])

    half = LANE_CHUNKS // 2
    for c0 in (0, half):
        chains = [(c, s) for c in range(c0, c0 + half) for s in range(SLOTS)]

        def cols(s):
            base = s * 2 * V7X_LANES
            return slice(base, base + V7X_LANES), slice(base + V7X_LANES, base + 2 * V7X_LANES)

        lams = [(lam_ref[c, :, cols(s)[0]], lam_ref[c, :, cols(s)[1]]) for c, s in chains]
        init = tuple((st_scr[c, :, cols(s)[0]], st_scr[c, :, cols(s)[1]]) for c, s in chains)

        def body(t, carry, chains=chains, lams=lams, cols=cols, h_new=h_new):
            rows = pl.ds(t * V7X_SUBLANES, V7X_SUBLANES)
            out = []
            for (c, s), (l_re, l_im), (s_re, s_im) in zip(chains, lams, carry):
                cr, ci = cols(s)
                n_re = (l_re * s_re - l_im * s_im) + bu_scr[c, rows, cr]
                n_im = (l_re * s_im + l_im * s_re) + bu_scr[c, rows, ci]
                h_new[c, rows, cr] = n_re
                h_new[c, rows, ci] = n_im
                out.append((n_re, n_im))
            return tuple(out)

        fin = init
        for t in range(SSM_T):
            fin = body(t, fin)
        for (c, s), (s_re, s_im) in zip(chains, fin):
            st_scr[c, :, cols(s)[0]] = s_re
            st_scr[c, :, cols(s)[1]] = s_im

    for c in range(LANE_CHUNKS):
        cols_c = slice(c * V7X_LANES, (c + 1) * V7X_LANES)
        y2 = _dot(h_old[c].astype(bf16), wc_ref[c])
        y2_scr[2 * c] = y2[:, :V7X_LANES]
        y2_scr[2 * c + 1] = y2[:, V7X_LANES:]
        for b in range(BATCH):
            yc = (y2_scr.at[2 * c][pl.ds(2 * b, SSM_T, stride=V7X_SUBLANES), :]
                  + y2_scr.at[2 * c + 1][pl.ds(2 * b + 1, SSM_T, stride=V7X_SUBLANES), :])
            yc = yc + d_ref[:, cols_c] * uprev_ref[b, :, cols_c]
            y_scr[b * SSM_T:(b + 1) * SSM_T, cols_c] = _gelu_tanh(yc)
    y = y_scr[...]
    out = y * _sigmoid(_dot(y.astype(bf16), wglu_ref[...]))
    out = (out * _rms_scale(out) * g_ref[...]).astype(bf16)
    o_ref[...] = out.reshape(BATCH, SSM_T, SSM_WIDTH)


def _ssm(u, wb, lam, wc, d, wglu, ssm_out_g):
    n_chunks = SEQ // SSM_T
    return pl.pallas_call(
        _ssm_kernel,
        grid=(n_chunks + 1,),
        in_specs=[
            pl.BlockSpec((BATCH, SSM_T, SSM_WIDTH), lambda i: (0, jnp.minimum(i, n_chunks - 1), 0)),
            pl.BlockSpec((BATCH, SSM_T, SSM_WIDTH), lambda i: (0, jnp.maximum(i - 1, 0), 0)),
            _const_spec((LANE_CHUNKS, V7X_LANES, SLOT_COLS)),
            _const_spec((LANE_CHUNKS, V7X_SUBLANES, SLOT_COLS)),
            _const_spec((LANE_CHUNKS, SLOT_COLS, 2 * V7X_LANES)),
            _const_spec((1, SSM_WIDTH)),
            _const_spec((SSM_WIDTH, SSM_WIDTH)),
            _const_spec((1, SSM_WIDTH)),
        ],
        out_specs=pl.BlockSpec((BATCH, SSM_T, SSM_WIDTH), lambda i: (0, jnp.maximum(i - 1, 0), 0)),
        out_shape=jax.ShapeDtypeStruct((BATCH, SEQ, SSM_WIDTH), bf16),
        scratch_shapes=[
            pltpu.VMEM((LANE_CHUNKS, SSM_ROWS2, V7X_LANES), f32),
            pltpu.VMEM((LANE_CHUNKS, SSM_ROWS2, SLOT_COLS), f32),
            pltpu.VMEM((2, LANE_CHUNKS, SSM_ROWS2, SLOT_COLS), f32),
            pltpu.VMEM((2 * LANE_CHUNKS, SSM_ROWS2, V7X_LANES), f32),
            pltpu.VMEM((SSM_ROWS, SSM_WIDTH), f32),
            pltpu.VMEM((LANE_CHUNKS, V7X_SUBLANES, SLOT_COLS), f32),
        ],
        compiler_params=pltpu.CompilerParams(
            dimension_semantics=("arbitrary",), vmem_limit_bytes=VMEM_LIMIT),
        name="ssm",
    )(u, u, wb, lam, wc, d, wglu, ssm_out_g)


def _out_proj_kernel(a_ref, s_ref, x_ref, wa_ref, ws_ref, g_ref, o_ref, h_ref):
    x1 = x_ref[...] + _dot(a_ref[...], wa_ref[...]) + _dot(s_ref[...], ws_ref[...])
    o_ref[...] = x1
    h_ref[...] = (x1 * _rms_scale(x1) * g_ref[...]).astype(bf16)


def _out_proj(a, s, x, wa, ws, ln2_g):
    nt = SEQ // IN_TM
    row_spec = lambda w: pl.BlockSpec((None, IN_TM, w), lambda b, t: (b, t, 0))
    return pl.pallas_call(
        _out_proj_kernel,
        grid=(BATCH, nt),
        in_specs=[
            row_spec(ATTN_WIDTH),
            row_spec(SSM_WIDTH),
            row_spec(D_MODEL),
            _const_spec((ATTN_WIDTH, D_MODEL)),
            _const_spec((SSM_WIDTH, D_MODEL)),
            _const_spec((1, D_MODEL)),
        ],
        out_specs=[row_spec(D_MODEL), row_spec(D_MODEL)],
        out_shape=[jax.ShapeDtypeStruct((BATCH, SEQ, D_MODEL), f32),
                   jax.ShapeDtypeStruct((BATCH, SEQ, D_MODEL), bf16)],
        compiler_params=pltpu.CompilerParams(
            dimension_semantics=("arbitrary", "arbitrary"), vmem_limit_bytes=VMEM_LIMIT),
        name="out_proj",
    )(a, s, x, wa, ws, ln2_g)


def _ffn_kernel(x_hbm, h_ref, wg_ref, wu_ref, wd_ref, o_ref, sem):
    first = pl.program_id(1) == 0
    rows = pl.ds(pl.multiple_of(pl.program_id(0) * FFN_TM, FFN_TM), FFN_TM)
    residual_copy = pltpu.make_async_copy(x_hbm.at[rows, :], o_ref, sem)

    @pl.when(first)
    def _():
        residual_copy.start()

    h = h_ref[...]
    gate = _dot(h, wg_ref[...].astype(bf16))
    up = _dot(h, wu_ref[...].astype(bf16))
    act = (gate * _sigmoid(gate) * up).astype(bf16)

    @pl.when(first)
    def _():
        residual_copy.wait()

    o_ref[...] += _dot(act, wd_ref[...].astype(bf16))


def _ffn(x, h, wg, wu, wd):
    n_rows = BATCH * SEQ
    return pl.pallas_call(
        _ffn_kernel,
        grid=(n_rows // FFN_TM, FF_HIDDEN // FFN_TF),
        in_specs=[
            pl.BlockSpec(memory_space=pl.ANY),
            pl.BlockSpec((FFN_TM, D_MODEL), lambda i, f: (i, 0)),
            pl.BlockSpec((D_MODEL, FFN_TF), lambda i, f: (0, f)),
            pl.BlockSpec((D_MODEL, FFN_TF), lambda i, f: (0, f)),
            pl.BlockSpec((FFN_TF, D_MODEL), lambda i, f: (f, 0)),
        ],
        out_specs=pl.BlockSpec((FFN_TM, D_MODEL), lambda i, f: (i, 0)),
        out_shape=jax.ShapeDtypeStruct((n_rows, D_MODEL), f32),
        scratch_shapes=[pltpu.SemaphoreType.DMA(())],
        compiler_params=pltpu.CompilerParams(
            dimension_semantics=("arbitrary", "arbitrary"), vmem_limit_bytes=VMEM_LIMIT),
        name="ffn",
    )(x, h, wg, wu, wd)


def _t5_bucket(dist):
    n = np.maximum(dist, 0)
    max_exact = REL_BUCKETS // 2
    nf = np.maximum(n, 1).astype(np.float32)
    large = max_exact + (np.log(nf / max_exact) / math.log(REL_MAX_DISTANCE / max_exact)
                         * (REL_BUCKETS - max_exact)).astype(np.int32)
    large = np.minimum(large, REL_BUCKETS - 1)
    return np.where(n < max_exact, n, large).astype(np.int32)


def _bias_kernel(rb_ref, bucket_ref, valid_ref, o_ref):
    head = pl.program_id(1)
    bucket = bucket_ref[...]
    acc = jnp.zeros((BLOCK, BLOCK), f32)
    for k in range(REL_BUCKETS):
        acc = jnp.where(bucket == k, rb_ref[k, head], acc)
    o_ref[0, 0] = jnp.where(valid_ref[0] != 0, acc, MASK_VALUE)


def _bias_tables(rel_bias):
    qi = np.arange(BLOCK)[:, None]
    kj = np.arange(BLOCK)[None, :]
    from_prev = kj > qi
    dist = np.where(from_prev, qi + BLOCK - kj, qi - kj)
    assert ((dist >= 0) & (dist < WINDOW)).all()
    valid = np.stack([~from_prev, np.ones_like(from_prev)]).astype(np.int32)
    return pl.pallas_call(
        _bias_kernel,
        grid=(2, N_Q_HEADS),
        in_specs=[
            pl.BlockSpec(memory_space=pltpu.SMEM),
            _const_spec((BLOCK, BLOCK)),
            pl.BlockSpec((1, BLOCK, BLOCK), lambda t, h: (t, 0, 0)),
        ],
        out_specs=pl.BlockSpec((1, 1, BLOCK, BLOCK), lambda t, h: (t, h, 0, 0)),
        out_shape=jax.ShapeDtypeStruct((2, N_Q_HEADS, BLOCK, BLOCK), f32),
        name="bias_table",
    )(rel_bias.astype(f32), jnp.asarray(_t5_bucket(dist)), jnp.asarray(valid))


def _ssm_params(a_re, a_im, log_dt, b_re, b_im, c_re, c_im):
    dt = jnp.exp(log_dt)[:, None]
    mag = jnp.exp(a_re * dt)
    ang = a_im * dt
    lb_re, lb_im = mag * jnp.cos(ang), mag * jnp.sin(ang)
    nr, ni = lb_re - 1.0, lb_im
    den = a_re * a_re + a_im * a_im
    f_re = (nr * a_re + ni * a_im) / den
    f_im = (ni * a_re - nr * a_im) / den
    bb_re = f_re[..., None] * b_re - f_im[..., None] * b_im
    bb_im = f_re[..., None] * b_im + f_im[..., None] * b_re

    eye = np.eye(2, dtype=np.float32)
    split = lambda t: t.reshape((2, LANE_CHUNKS, SLOTS, 2, 2) + t.shape[2:])
    bbs = split(jnp.stack([bb_re, bb_im]))
    wb = jnp.einsum('us,gh,acsjgnp->cujgpsahn', eye, eye, bbs).reshape(LANE_CHUNKS, V7X_LANES, SLOT_COLS)
    cs = split(jnp.stack([c_re, -c_im]))
    wc = jnp.einsum('vj,us,gh,acsjgpn->csagnvujhp', eye, eye, eye, cs).reshape(
        LANE_CHUNKS, SLOT_COLS, 2 * V7X_LANES)
    lam = jnp.transpose(split(jnp.stack([lb_re, lb_im])), (1, 3, 2, 0, 4, 5)).reshape(LANE_CHUNKS, 1, 2, SLOT_COLS)
    lam = jnp.broadcast_to(lam, (LANE_CHUNKS, BATCH, 2, SLOT_COLS)).reshape(LANE_CHUNKS, V7X_SUBLANES, SLOT_COLS)
    return wb.astype(bf16), wc.astype(bf16), lam


def _layer(x, rel_bias, ln1_g, w_in, q_norm_g, k_norm_g, attn_sinks, ssm_a_re, ssm_a_im,
           ssm_log_dt, ssm_b_re, ssm_b_im, ssm_c_re, ssm_c_im, ssm_d, w_glu,
           attn_out_g, ssm_out_g, w_out, ln2_g, w_ff_gate, w_ff_up, w_ff_down):
    row = lambda v: v.reshape(1, -1).astype(f32)

    lanes = np.arange(KV_SLAB)
    seg = jnp.asarray(lanes[:, None] // HEAD_DIM == lanes[None, :] // HEAD_DIM, bf16)
    def regroup(t, axis):
        shape = t.shape[:axis] + (N_KV_HEADS, Q_PER_KV, HEAD_DIM) + t.shape[axis + 1:]
        return jnp.swapaxes(t.reshape(shape), axis, axis + 1).reshape(t.shape)

    w_in_r = jnp.concatenate([regroup(w_in[:, :ATTN_WIDTH], 1), w_in[:, ATTN_WIDTH:]], axis=1).astype(bf16)
    gqk = row(jnp.concatenate([jnp.tile(q_norm_g, N_Q_HEADS) * (HEAD_DIM ** -0.5),
                               jnp.tile(k_norm_g, N_KV_HEADS)]))
    kv_mask = jnp.asarray(np.broadcast_to(
        (np.arange(KV_WIDTH)[None, None, :] // HEAD_DIM) == np.arange(N_KV_HEADS)[:, None, None],
        (N_KV_HEADS, BLOCK, KV_WIDTH)), bf16)

    q, k, v, u = _in_proj(x, row(ln1_g), w_in_r, seg, gqk)

    pos = np.arange(BLOCK)
    from_prev = pos[None, :] > pos[:, None]
    tri = jnp.asarray(np.stack([from_prev, ~from_prev]), bf16)
    y_attn = _attention(attn_sinks.astype(f32), q, k, v, _bias_tables(rel_bias), kv_mask, tri,
                        row(regroup(attn_out_g, 0)))

    wb, wc, lam = _ssm_params(ssm_a_re.astype(f32), ssm_a_im.astype(f32), ssm_log_dt.astype(f32),
                                     ssm_b_re.astype(f32), ssm_b_im.astype(f32),
                                     ssm_c_re.astype(f32), ssm_c_im.astype(f32))
    y_ssm = _ssm(u, wb, lam, wc, row(ssm_d), w_glu.astype(bf16), row(ssm_out_g))

    x1, h2 = _out_proj(y_attn, y_ssm, x, regroup(w_out[:ATTN_WIDTH], 0).astype(bf16),
                       w_out[ATTN_WIDTH:].astype(bf16), row(ln2_g))

    out = _ffn(x1.reshape(BATCH * SEQ, D_MODEL), h2.reshape(BATCH * SEQ, D_MODEL),
               w_ff_gate.astype(f32), w_ff_up.astype(f32), w_ff_down.astype(f32))
    return out.reshape(BATCH, SEQ, D_MODEL)


def kernel(x, rel_bias, ln1_g, w_in, q_norm_g, k_norm_g, attn_sinks, ssm_a_re, ssm_a_im, ssm_log_dt, ssm_b_re, ssm_b_im, ssm_c_re, ssm_c_im, ssm_d, w_glu, attn_out_g, ssm_out_g, w_out, ln2_g, w_ff_gate, w_ff_up, w_ff_down):
    for l in range(ln1_g.shape[0]):
        x = _layer(x, rel_bias, ln1_g[l], w_in[l], q_norm_g[l], k_norm_g[l], attn_sinks[l],
                   ssm_a_re[l], ssm_a_im[l], ssm_log_dt[l], ssm_b_re[l], ssm_b_im[l],
                   ssm_c_re[l], ssm_c_im[l], ssm_d[l], w_glu[l], attn_out_g[l], ssm_out_g[l],
                   w_out[l], ln2_g[l], w_ff_gate[l], w_ff_up[l], w_ff_down[l])
    return x
```

```python
import functools
import math

import jax
import jax.numpy as jnp
import numpy as np
from jax import lax
from jax.experimental import pallas as pl
from jax.experimental.pallas import tpu as pltpu

D_MODEL = 2048
BATCH = 4
SEQ = 4096
HEAD_DIM = 64
N_Q_HEADS = 16
N_KV_HEADS = 4
Q_PER_KV = N_Q_HEADS // N_KV_HEADS
ATTN_WIDTH = N_Q_HEADS * HEAD_DIM
KV_WIDTH = N_KV_HEADS * HEAD_DIM
WINDOW = 128
BLOCK = 128
SSM_WIDTH = D_MODEL - ATTN_WIDTH
IN_WIDTH = ATTN_WIDTH + 2 * KV_WIDTH + SSM_WIDTH
SSM_GROUP = 16
SSM_GROUPS = SSM_WIDTH // SSM_GROUP
SSM_STATE = 64
FF_HIDDEN = 5632
REL_BUCKETS = 32
REL_MAX_DISTANCE = 128
EPS = 1e-6

V7X_LANES = 128
V7X_SUBLANES = 8
V7X_VMEM_BYTES = 64 * 1024 * 1024
VMEM_LIMIT = V7X_VMEM_BYTES - 8 * 1024 * 1024

MASK_VALUE = -1e30
KV_SLAB = Q_PER_KV * HEAD_DIM
LANE_CHUNKS = SSM_WIDTH // V7X_LANES
SLOTS = 2
SLOT_COLS = SLOTS * 2 * V7X_LANES

ATTN_BLOCKS = 4
IN_TM = 512
FFN_TM = 1024
FFN_TF = 512
SSM_T = 64
SSM_ROWS = SSM_T * BATCH
SSM_ROWS2 = 2 * SSM_ROWS

f32 = jnp.float32
bf16 = jnp.bfloat16


def _dot(a, b):
    return jnp.dot(a, b, preferred_element_type=f32)


def _rms_scale(x):
    return lax.rsqrt(jnp.mean(x * x, axis=-1, keepdims=True) + EPS)


def _in_proj_kernel(x_ref, g_ref, w_ref, seg_ref, gqk_ref, q_ref, k_ref, v_ref, u_ref, proj_scr):
    xf = x_ref[...]
    h = (xf * _rms_scale(xf) * g_ref[...]).astype(bf16)
    proj_scr[...] = _dot(h, w_ref[...])

    n_slabs = (ATTN_WIDTH + KV_WIDTH) // KV_SLAB
    slab = lambda c: proj_scr[:, c * KV_SLAB:(c + 1) * KV_SLAB]
    sq = jnp.concatenate([slab(c) * slab(c) for c in range(n_slabs)], axis=0)
    hi = sq.astype(bf16)
    lo = (sq - hi.astype(f32)).astype(bf16)
    ss = _dot(jnp.concatenate([hi, lo], axis=0), seg_ref[...])
    ss = ss[:n_slabs * IN_TM] + ss[n_slabs * IN_TM:]
    normed = []
    for c in range(n_slabs):
        cols = slice(c * KV_SLAB, (c + 1) * KV_SLAB)
        scale = lax.rsqrt(ss[c * IN_TM:(c + 1) * IN_TM] * (1.0 / HEAD_DIM) + EPS)
        normed.append((slab(c) * scale * gqk_ref[:, cols]).astype(bf16))
    for c in range(n_slabs - 1):
        q_ref[:, c * KV_SLAB:(c + 1) * KV_SLAB] = normed[c]
    k_ref[...] = normed[-1]
    v_ref[...] = proj_scr[:, ATTN_WIDTH + KV_WIDTH:ATTN_WIDTH + 2 * KV_WIDTH].astype(bf16)
    u_ref[...] = proj_scr[:, ATTN_WIDTH + 2 * KV_WIDTH:]


def _const_spec(shape):
    nd = len(shape)
    return pl.BlockSpec(shape, lambda *_: (0,) * nd)


def _in_proj(x, ln1_g, w, seg, gqk):
    nt = SEQ // IN_TM
    row_spec = lambda w: pl.BlockSpec((None, IN_TM, w), lambda b, t: (b, t, 0))
    return pl.pallas_call(
        _in_proj_kernel,
        grid=(BATCH, nt),
        in_specs=[
            row_spec(D_MODEL),
            _const_spec((1, D_MODEL)),
            _const_spec((D_MODEL, IN_WIDTH)),
            _const_spec((KV_SLAB, KV_SLAB)),
            _const_spec((1, ATTN_WIDTH + KV_WIDTH)),
        ],
        out_specs=[row_spec(ATTN_WIDTH), row_spec(KV_WIDTH), row_spec(KV_WIDTH), row_spec(SSM_WIDTH)],
        out_shape=[
            jax.ShapeDtypeStruct((BATCH, SEQ, ATTN_WIDTH), bf16),
            jax.ShapeDtypeStruct((BATCH, SEQ, KV_WIDTH), bf16),
            jax.ShapeDtypeStruct((BATCH, SEQ, KV_WIDTH), bf16),
            jax.ShapeDtypeStruct((BATCH, SEQ, SSM_WIDTH), f32),
        ],
        scratch_shapes=[pltpu.VMEM((IN_TM, IN_WIDTH), f32)],
        compiler_params=pltpu.CompilerParams(
            dimension_semantics=("arbitrary", "arbitrary"), vmem_limit_bytes=VMEM_LIMIT),
        name="in_proj",
    )(x, ln1_g, w, seg, gqk)


def _attn_kernel(sink_ref, q_ref, kc_ref, kp_ref, vc_ref, vp_ref, bias_ref, qmask_ref, tri_ref, g_ref,
                 w0_ref, w1_ref, w2_ref, w3_ref, o_ref, c0_ref, c1_ref, c2_ref, c3_ref, y_scr):
    for w_ref, c_ref in ((w0_ref, c0_ref), (w1_ref, c1_ref), (w2_ref, c2_ref), (w3_ref, c3_ref)):
        c_ref[...] = w_ref[...].astype(bf16)

    qi = lax.broadcasted_iota(jnp.int32, (BLOCK, BLOCK), 0)
    kj = lax.broadcasted_iota(jnp.int32, (BLOCK, BLOCK), 1)
    from_prev = kj > qi
    first_table = jnp.minimum(pl.program_id(1), 1)
    for j in range(ATTN_BLOCKS):
        rows = slice(j * BLOCK, (j + 1) * BLOCK)
        prev_rows = slice((j - 1) * BLOCK, j * BLOCK)
        k_prev = kp_ref[...] if j == 0 else kc_ref[prev_rows, :]
        v_prev = vp_ref[...] if j == 0 else vc_ref[prev_rows, :]
        keys = jnp.concatenate([k_prev, kc_ref[rows, :]], axis=0)
        vals = jnp.concatenate([v_prev, vc_ref[rows, :]], axis=0)
        for kh in range(N_KV_HEADS):
            q4 = jnp.concatenate(
                [q_ref[rows, g * KV_WIDTH:(g + 1) * KV_WIDTH] * qmask_ref[kh] for g in range(Q_PER_KV)], axis=0)
            s4 = lax.dot_general(q4, keys, (((1,), (1,)), ((), ())), preferred_element_type=f32)
            lanes = slice(kh * HEAD_DIM, (kh + 1) * HEAD_DIM)
            for g in range(Q_PER_KV):
                head = kh * Q_PER_KV + g
                bias = bias_ref[first_table, head] if j == 0 else bias_ref[1, head]
                sg = s4[g * BLOCK:(g + 1) * BLOCK]
                s = jnp.where(from_prev, sg[:, :BLOCK], sg[:, BLOCK:]) + bias
                sink = sink_ref[head]
                m = jnp.max(s, axis=-1, keepdims=True)
                p = jnp.exp(s - m)
                den = jnp.sum(p, axis=-1, keepdims=True) + jnp.exp(sink - m)
                w = (p * (1.0 / den)).astype(bf16)
                o = _dot(jnp.concatenate([w * tri_ref[0], w * tri_ref[1]], axis=1), vals)
                y_scr[rows, g * KV_WIDTH + kh * HEAD_DIM:g * KV_WIDTH + (kh + 1) * HEAD_DIM] = o[:, lanes]
    y = y_scr[...]
    o_ref[...] = (y * _rms_scale(y) * g_ref[...]).astype(bf16)


def _attention(sinks, q, k, v, bias2, qmask, tri, attn_out_g, w_gate, w_up, w_down, w_out):
    rows = ATTN_BLOCKS * BLOCK
    n_blk = SEQ // rows
    steps = BATCH * n_blk
    step = lambda b, n: b * n_blk + n

    def slab(w):
        return pl.BlockSpec((w.shape[0] // steps, w.shape[1]), lambda b, n: (step(b, n), 0))

    assert w_out.shape[0] // steps == HEAD_DIM and N_Q_HEADS * 2 == steps

    def w_out_src(b, n):
        s = step(b, n)
        return (jnp.where(s < N_Q_HEADS, (s % N_KV_HEADS) * Q_PER_KV + s // N_KV_HEADS, s), 0)

    cast_in = [slab(w_gate), slab(w_up), slab(w_down), pl.BlockSpec((HEAD_DIM, D_MODEL), w_out_src)]
    cast_out = [slab(w_gate), slab(w_up), slab(w_down), slab(w_out)]
    cur = pl.BlockSpec((None, rows, ATTN_WIDTH), lambda b, n: (b, n, 0))
    kv_cur = pl.BlockSpec((None, rows, KV_WIDTH), lambda b, n: (b, n, 0))
    kv_prev = pl.BlockSpec((None, BLOCK, KV_WIDTH), lambda b, n: (b, jnp.maximum(n * ATTN_BLOCKS - 1, 0), 0))
    return pl.pallas_call(
        _attn_kernel,
        grid=(BATCH, SEQ // rows),
        in_specs=[
            pl.BlockSpec(memory_space=pltpu.SMEM),
            cur, kv_cur, kv_prev, kv_cur, kv_prev,
            _const_spec((2, N_Q_HEADS, BLOCK, BLOCK)),
            _const_spec((Q_PER_KV, BLOCK, KV_SLAB)),
            _const_spec((2, BLOCK, BLOCK)),
            _const_spec((1, ATTN_WIDTH)),
        ] + cast_in,
        out_specs=[cur] + cast_out,
        out_shape=[jax.ShapeDtypeStruct((BATCH, SEQ, ATTN_WIDTH), bf16)]
        + [jax.ShapeDtypeStruct(w.shape, bf16) for w in (w_gate, w_up, w_down, w_out)],
        scratch_shapes=[pltpu.VMEM((rows, ATTN_WIDTH), f32)],
        compiler_params=pltpu.CompilerParams(
            dimension_semantics=("arbitrary", "arbitrary"), vmem_limit_bytes=VMEM_LIMIT),
        name="attn",
    )(sinks, q, k, k, v, v, bias2, qmask, tri, attn_out_g, w_gate, w_up, w_down, w_out)


def _gelu_tanh(x):
    return 0.5 * x * (1.0 + jnp.tanh(math.sqrt(2.0 / math.pi) * (x + 0.044715 * (x * x * x))))


def _sigmoid(x):
    return 1.0 / (1.0 + jnp.exp(-x))


def _ssm_kernel(u_ref, uprev_ref, wb_ref, lam_ref, wc_ref, d_ref, wglu_ref, g_ref,
                o_ref, u2_scr, bu_scr, h_scr, y2_scr, y_scr, st_scr):
    step = pl.program_id(0)

    @pl.when(step == 0)
    def _():
        st_scr[...] = jnp.zeros_like(st_scr)
        h_scr[1] = jnp.zeros(h_scr.shape[1:], f32)

    for parity in range(2):
        pl.when(step % 2 == parity)(functools.partial(
            _ssm_step, u_ref, uprev_ref, wb_ref, lam_ref, wc_ref, d_ref, wglu_ref, g_ref, o_ref,
            u2_scr, bu_scr, h_scr.at[parity], h_scr.at[1 - parity], y2_scr, y_scr, st_scr))


def _ssm_step(u_ref, uprev_ref, wb_ref, lam_ref, wc_ref, d_ref, wglu_ref, g_ref, o_ref,
              u2_scr, bu_scr, h_new, h_old, y2_scr, y_scr, st_scr):
    for c in range(LANE_CHUNKS):
        for b in range(BATCH):
            for par in range(2):
                u2_scr.at[c][pl.ds(2 * b + par, SSM_T, stride=V7X_SUBLANES), :] = (
                    u_ref[b, :, c * V7X_LANES:(c + 1) * V7X_LANES])
    row_par = lax.broadcasted_iota(jnp.int32, (SSM_ROWS2, V7X_LANES), 0) % 2
    pair_par = (lax.broadcasted_iota(jnp.int32, (SSM_ROWS2, V7X_LANES), 1) // (2 * SSM_GROUP)) % 2
    own_pair = row_par == pair_par

    for c in range(LANE_CHUNKS):
        bu_scr[c] = _dot(jnp.where(own_pair, u2_scr[c], 0.0).astype(bf16), wb_ref[c])

    half = LANE_CHUNKS // 2
    for c0 in (0, half):
        chains = [(c, s) for c in range(c0, c0 + half) for s in range(SLOTS)]

        def cols(s):
            base = s * 2 * V7X_LANES
            return slice(base, base + V7X_LANES), slice(base + V7X_LANES, base + 2 * V7X_LANES)

        lams = [(lam_ref[c, :, cols(s)[0]], lam_ref[c, :, cols(s)[1]]) for c, s in chains]
        init = tuple((st_scr[c, :, cols(s)[0]], st_scr[c, :, cols(s)[1]]) for c, s in chains)

        def body(t, carry, chains=chains, lams=lams, cols=cols, h_new=h_new):
            rows = pl.ds(t * V7X_SUBLANES, V7X_SUBLANES)
            out = []
            for (c, s), (l_re, l_im), (s_re, s_im) in zip(chains, lams, carry):
                cr, ci = cols(s)
                n_re = (l_re * s_re - l_im * s_im) + bu_scr[c, rows, cr]
                n_im = (l_re * s_im + l_im * s_re) + bu_scr[c, rows, ci]
                h_new[c, rows, cr] = n_re
                h_new[c, rows, ci] = n_im
                out.append((n_re, n_im))
            return tuple(out)

        fin = init
        for t in range(SSM_T):
            fin = body(t, fin)
        for (c, s), (s_re, s_im) in zip(chains, fin):
            st_scr[c, :, cols(s)[0]] = s_re
            st_scr[c, :, cols(s)[1]] = s_im

    for c in range(LANE_CHUNKS):
        cols_c = slice(c * V7X_LANES, (c + 1) * V7X_LANES)
        y2 = _dot(h_old[c].astype(bf16), wc_ref[c])
        y2_scr[2 * c] = y2[:, :V7X_LANES]
        y2_scr[2 * c + 1] = y2[:, V7X_LANES:]
        for b in range(BATCH):
            yc = (y2_scr.at[2 * c][pl.ds(2 * b, SSM_T, stride=V7X_SUBLANES), :]
                  + y2_scr.at[2 * c + 1][pl.ds(2 * b + 1, SSM_T, stride=V7X_SUBLANES), :])
            yc = yc + d_ref[:, cols_c] * uprev_ref[b, :, cols_c]
            y_scr[b * SSM_T:(b + 1) * SSM_T, cols_c] = _gelu_tanh(yc)
    y = y_scr[...]
    out = y * _sigmoid(_dot(y.astype(bf16), wglu_ref[...]))
    out = (out * _rms_scale(out) * g_ref[...]).astype(bf16)
    o_ref[...] = out.reshape(BATCH, SSM_T, SSM_WIDTH)


def _ssm(u, wb, lam, wc, d, wglu, ssm_out_g):
    n_chunks = SEQ // SSM_T
    return pl.pallas_call(
        _ssm_kernel,
        grid=(n_chunks + 1,),
        in_specs=[
            pl.BlockSpec((BATCH, SSM_T, SSM_WIDTH), lambda i: (0, jnp.minimum(i, n_chunks - 1), 0)),
            pl.BlockSpec((BATCH, SSM_T, SSM_WIDTH), lambda i: (0, jnp.maximum(i - 1, 0), 0)),
            _const_spec((LANE_CHUNKS, V7X_LANES, SLOT_COLS)),
            _const_spec((LANE_CHUNKS, V7X_SUBLANES, SLOT_COLS)),
            _const_spec((LANE_CHUNKS, SLOT_COLS, 2 * V7X_LANES)),
            _const_spec((1, SSM_WIDTH)),
            _const_spec((SSM_WIDTH, SSM_WIDTH)),
            _const_spec((1, SSM_WIDTH)),
        ],
        out_specs=pl.BlockSpec((BATCH, SSM_T, SSM_WIDTH), lambda i: (0, jnp.maximum(i - 1, 0), 0)),
        out_shape=jax.ShapeDtypeStruct((BATCH, SEQ, SSM_WIDTH), bf16),
        scratch_shapes=[
            pltpu.VMEM((LANE_CHUNKS, SSM_ROWS2, V7X_LANES), f32),
            pltpu.VMEM((LANE_CHUNKS, SSM_ROWS2, SLOT_COLS), f32),
            pltpu.VMEM((2, LANE_CHUNKS, SSM_ROWS2, SLOT_COLS), f32),
            pltpu.VMEM((2 * LANE_CHUNKS, SSM_ROWS2, V7X_LANES), f32),
            pltpu.VMEM((SSM_ROWS, SSM_WIDTH), f32),
            pltpu.VMEM((LANE_CHUNKS, V7X_SUBLANES, SLOT_COLS), f32),
        ],
        compiler_params=pltpu.CompilerParams(
            dimension_semantics=("arbitrary",), vmem_limit_bytes=VMEM_LIMIT),
        name="ssm",
    )(u, u, wb, lam, wc, d, wglu, ssm_out_g)


def _out_proj_kernel(a_ref, s_ref, x_ref, wa_ref, ws_ref, g_ref, o_ref, h_ref):
    x1 = x_ref[...] + _dot(a_ref[...], wa_ref[...]) + _dot(s_ref[...], ws_ref[...])
    o_ref[...] = x1
    h_ref[...] = (x1 * _rms_scale(x1) * g_ref[...]).astype(bf16)


def _out_proj(a, s, x, w, ln2_g):
    nt = SEQ // IN_TM
    row_spec = lambda w: pl.BlockSpec((None, IN_TM, w), lambda b, t: (b, t, 0))
    return pl.pallas_call(
        _out_proj_kernel,
        grid=(BATCH, nt),
        in_specs=[
            row_spec(ATTN_WIDTH),
            row_spec(SSM_WIDTH),
            row_spec(D_MODEL),
            pl.BlockSpec((ATTN_WIDTH, D_MODEL), lambda b, t: (0, 0)),
            pl.BlockSpec((SSM_WIDTH, D_MODEL), lambda b, t: (1, 0)),
            _const_spec((1, D_MODEL)),
        ],
        out_specs=[row_spec(D_MODEL), row_spec(D_MODEL)],
        out_shape=[jax.ShapeDtypeStruct((BATCH, SEQ, D_MODEL), f32),
                   jax.ShapeDtypeStruct((BATCH, SEQ, D_MODEL), bf16)],
        compiler_params=pltpu.CompilerParams(
            dimension_semantics=("arbitrary", "arbitrary"), vmem_limit_bytes=VMEM_LIMIT),
        name="out_proj",
    )(a, s, x, w, w, ln2_g)


def _ffn_kernel(x_hbm, h_ref, wg_ref, wu_ref, wd_ref, o_ref, sem):
    first = pl.program_id(1) == 0
    rows = pl.ds(pl.multiple_of(pl.program_id(0) * FFN_TM, FFN_TM), FFN_TM)
    residual_copy = pltpu.make_async_copy(x_hbm.at[rows, :], o_ref, sem)

    @pl.when(first)
    def _():
        residual_copy.start()

    h = h_ref[...]
    gate = _dot(h, wg_ref[...])
    up = _dot(h, wu_ref[...])
    act = (gate * _sigmoid(gate) * up).astype(bf16)

    @pl.when(first)
    def _():
        residual_copy.wait()

    o_ref[...] += _dot(act, wd_ref[...])


def _ffn(x, h, wg, wu, wd):
    n_rows = BATCH * SEQ
    return pl.pallas_call(
        _ffn_kernel,
        grid=(n_rows // FFN_TM, FF_HIDDEN // FFN_TF),
        in_specs=[
            pl.BlockSpec(memory_space=pl.ANY),
            pl.BlockSpec((FFN_TM, D_MODEL), lambda i, f: (i, 0)),
            pl.BlockSpec((D_MODEL, FFN_TF), lambda i, f: (0, f)),
            pl.BlockSpec((D_MODEL, FFN_TF), lambda i, f: (0, f)),
            pl.BlockSpec((FFN_TF, D_MODEL), lambda i, f: (f, 0)),
        ],
        out_specs=pl.BlockSpec((FFN_TM, D_MODEL), lambda i, f: (i, 0)),
        out_shape=jax.ShapeDtypeStruct((n_rows, D_MODEL), f32),
        scratch_shapes=[pltpu.SemaphoreType.DMA(())],
        compiler_params=pltpu.CompilerParams(
            dimension_semantics=("arbitrary", "arbitrary"), vmem_limit_bytes=VMEM_LIMIT),
        name="ffn",
    )(x, h, wg, wu, wd)


def _t5_bucket(dist):
    n = np.maximum(dist, 0)
    max_exact = REL_BUCKETS // 2
    nf = np.maximum(n, 1).astype(np.float32)
    large = max_exact + (np.log(nf / max_exact) / math.log(REL_MAX_DISTANCE / max_exact)
                         * (REL_BUCKETS - max_exact)).astype(np.int32)
    large = np.minimum(large, REL_BUCKETS - 1)
    return np.where(n < max_exact, n, large).astype(np.int32)


def _bias_kernel(rb_ref, bucket_ref, o_ref):
    bucket = bucket_ref[...]
    from_prev = (lax.broadcasted_iota(jnp.int32, (BLOCK, BLOCK), 1)
                 > lax.broadcasted_iota(jnp.int32, (BLOCK, BLOCK), 0))
    for head in range(N_Q_HEADS):
        acc = jnp.zeros((BLOCK, BLOCK), f32)
        for k in range(REL_BUCKETS):
            acc = jnp.where(bucket == k, rb_ref[k, head], acc)
        o_ref[1, head] = acc
        o_ref[0, head] = jnp.where(from_prev, MASK_VALUE, acc)


def _bias_tables(rel_bias):
    qi = np.arange(BLOCK)[:, None]
    kj = np.arange(BLOCK)[None, :]
    from_prev = kj > qi
    dist = np.where(from_prev, qi + BLOCK - kj, qi - kj)
    assert ((dist >= 0) & (dist < WINDOW)).all()
    return pl.pallas_call(
        _bias_kernel,
        in_specs=[
            pl.BlockSpec(memory_space=pltpu.SMEM),
            pl.BlockSpec(memory_space=pltpu.VMEM),
        ],
        out_specs=pl.BlockSpec(memory_space=pltpu.VMEM),
        out_shape=jax.ShapeDtypeStruct((2, N_Q_HEADS, BLOCK, BLOCK), f32),
        name="bias_table",
    )(rel_bias.astype(f32), jnp.asarray(_t5_bucket(dist)))


def _ssm_params(a_re, a_im, log_dt, b_re, b_im, c_re, c_im):
    dt = jnp.exp(log_dt)[:, None]
    mag = jnp.exp(a_re * dt)
    ang = a_im * dt
    lb_re, lb_im = mag * jnp.cos(ang), mag * jnp.sin(ang)
    nr, ni = lb_re - 1.0, lb_im
    den = a_re * a_re + a_im * a_im
    f_re = (nr * a_re + ni * a_im) / den
    f_im = (ni * a_re - nr * a_im) / den
    bb_re = f_re[..., None] * b_re - f_im[..., None] * b_im
    bb_im = f_re[..., None] * b_im + f_im[..., None] * b_re

    eye = np.eye(2, dtype=np.float32)
    split = lambda t: t.reshape((2, LANE_CHUNKS, SLOTS, 2, 2) + t.shape[2:])
    bbs = split(jnp.stack([bb_re, bb_im]))
    wb = jnp.einsum('us,gh,acsjgnp->cujgpsahn', eye, eye, bbs).reshape(LANE_CHUNKS, V7X_LANES, SLOT_COLS)
    cs = split(jnp.stack([c_re, -c_im]))
    wc = jnp.einsum('vj,us,gh,acsjgpn->csagnvujhp', eye, eye, eye, cs).reshape(
        LANE_CHUNKS, SLOT_COLS, 2 * V7X_LANES)
    lam = jnp.transpose(split(jnp.stack([lb_re, lb_im])), (1, 3, 2, 0, 4, 5)).reshape(LANE_CHUNKS, 1, 2, SLOT_COLS)
    lam = jnp.broadcast_to(lam, (LANE_CHUNKS, BATCH, 2, SLOT_COLS)).reshape(LANE_CHUNKS, V7X_SUBLANES, SLOT_COLS)
    return wb.astype(bf16), wc.astype(bf16), lam


def _layer(x, rel_bias, ln1_g, w_in, q_norm_g, k_norm_g, attn_sinks, ssm_a_re, ssm_a_im,
           ssm_log_dt, ssm_b_re, ssm_b_im, ssm_c_re, ssm_c_im, ssm_d, w_glu,
           attn_out_g, ssm_out_g, w_out, ln2_g, w_ff_gate, w_ff_up, w_ff_down):
    row = lambda v: v.reshape(1, -1).astype(f32)

    lanes = np.arange(KV_SLAB)
    seg = jnp.asarray(lanes[:, None] // HEAD_DIM == lanes[None, :] // HEAD_DIM, bf16)
    def regroup(t, axis):
        shape = t.shape[:axis] + (N_KV_HEADS, Q_PER_KV, HEAD_DIM) + t.shape[axis + 1:]
        return jnp.swapaxes(t.reshape(shape), axis, axis + 1).reshape(t.shape)

    w_in_r = jnp.concatenate([regroup(w_in[:, :ATTN_WIDTH], 1), w_in[:, ATTN_WIDTH:]], axis=1).astype(bf16)
    gqk = row(jnp.concatenate([jnp.tile(q_norm_g, N_Q_HEADS) * (HEAD_DIM ** -0.5),
                               jnp.tile(k_norm_g, N_KV_HEADS)]))
    kv_mask = jnp.asarray(np.broadcast_to(
        (np.arange(KV_WIDTH)[None, None, :] // HEAD_DIM) == np.arange(N_KV_HEADS)[:, None, None],
        (N_KV_HEADS, BLOCK, KV_WIDTH)), bf16)

    q, k, v, u = _in_proj(x, row(ln1_g), w_in_r, seg, gqk)

    pos = np.arange(BLOCK)
    from_prev = pos[None, :] > pos[:, None]
    tri = jnp.asarray(np.stack([from_prev, ~from_prev]), bf16)
    y_attn, wg, wu, wd, wo = _attention(
        attn_sinks.astype(f32), q, k, v, _bias_tables(rel_bias), kv_mask, tri, row(regroup(attn_out_g, 0)),
        w_ff_gate.astype(f32), w_ff_up.astype(f32), w_ff_down.astype(f32), w_out.astype(f32))

    wb, wc, lam = _ssm_params(ssm_a_re.astype(f32), ssm_a_im.astype(f32), ssm_log_dt.astype(f32),
                                     ssm_b_re.astype(f32), ssm_b_im.astype(f32),
                                     ssm_c_re.astype(f32), ssm_c_im.astype(f32))
    y_ssm = _ssm(u, wb, lam, wc, row(ssm_d), w_glu.astype(bf16), row(ssm_out_g))

    x1, h2 = _out_proj(y_attn, y_ssm, x, wo, row(ln2_g))

    out = _ffn(x1.reshape(BATCH * SEQ, D_MODEL), h2.reshape(BATCH * SEQ, D_MODEL), wg, wu, wd)
    return out.reshape(BATCH, SEQ, D_MODEL)


def kernel(x, rel_bias, ln1_g, w_in, q_norm_g, k_norm_g, attn_sinks, ssm_a_re, ssm_a_im, ssm_log_dt, ssm_b_re, ssm_b_im, ssm_c_re, ssm_c_im, ssm_d, w_glu, attn_out_g, ssm_out_g, w_out, ln2_g, w_ff_gate, w_ff_up, w_ff_down):
    for l in range(ln1_g.shape[0]):
        x = _layer(x, rel_bias, ln1_g[l], w_in[l], q_norm_g[l], k_norm_g[l], attn_sinks[l],
                   ssm_a_re[l], ssm_a_im[l], ssm_log_dt[l], ssm_b_re[l], ssm_b_im[l],
                   ssm_c_re[l], ssm_c_im[l], ssm_d[l], w_glu[l], attn_out_g[l], ssm_out_g[l],
                   w_out[l], ln2_g[l], w_ff_gate[l], w_ff_up[l], w_ff_down[l])
    return x
```

```python
import functools
import math

import jax
import jax.numpy as jnp
import numpy as np
from jax import lax
from jax.experimental import pallas as pl
from jax.experimental.pallas import tpu as pltpu

D_MODEL = 2048
BATCH = 4
SEQ = 4096
HEAD_DIM = 64
N_Q_HEADS = 16
N_KV_HEADS = 4
Q_PER_KV = N_Q_HEADS // N_KV_HEADS
ATTN_WIDTH = N_Q_HEADS * HEAD_DIM
KV_WIDTH = N_KV_HEADS * HEAD_DIM
WINDOW = 128
BLOCK = 128
SSM_WIDTH = D_MODEL - ATTN_WIDTH
IN_WIDTH = ATTN_WIDTH + 2 * KV_WIDTH + SSM_WIDTH
SSM_GROUP = 16
SSM_GROUPS = SSM_WIDTH // SSM_GROUP
SSM_STATE = 64
FF_HIDDEN = 5632
REL_BUCKETS = 32
REL_MAX_DISTANCE = 128
EPS = 1e-6

V7X_LANES = 128
V7X_SUBLANES = 8
V7X_VMEM_BYTES = 64 * 1024 * 1024
VMEM_LIMIT = V7X_VMEM_BYTES - 8 * 1024 * 1024

MASK_VALUE = -1e30
KV_SLAB = Q_PER_KV * HEAD_DIM
LANE_CHUNKS = SSM_WIDTH // V7X_LANES
SLOTS = 2
SLOT_COLS = SLOTS * 2 * V7X_LANES

ATTN_BLOCKS = 4
IN_TM = 512
FFN_TM = 1024
FFN_TF = 512
SSM_T = 64
SSM_ROWS = SSM_T * BATCH
SSM_ROWS2 = 2 * SSM_ROWS

f32 = jnp.float32
bf16 = jnp.bfloat16


def _dot(a, b):
    return jnp.dot(a, b, preferred_element_type=f32)


def _rms_scale(x):
    return lax.rsqrt(jnp.mean(x * x, axis=-1, keepdims=True) + EPS)


def _in_proj_kernel(x_ref, g_ref, wq_ref, wr_ref, seg_ref, gqk_ref, q_ref, k_ref, v_ref, u_ref, proj_scr):
    xf = x_ref[...]
    h = (xf * _rms_scale(xf) * g_ref[...]).astype(bf16)
    proj_scr[:, :ATTN_WIDTH] = _dot(h, wq_ref[...])
    proj_scr[:, ATTN_WIDTH:] = _dot(h, wr_ref[...])

    n_slabs = (ATTN_WIDTH + KV_WIDTH) // KV_SLAB
    slab = lambda c: proj_scr[:, c * KV_SLAB:(c + 1) * KV_SLAB]
    sq = jnp.concatenate([slab(c) * slab(c) for c in range(n_slabs)], axis=0)
    hi = sq.astype(bf16)
    lo = (sq - hi.astype(f32)).astype(bf16)
    ss = _dot(jnp.concatenate([hi, lo], axis=0), seg_ref[...])
    ss = ss[:n_slabs * IN_TM] + ss[n_slabs * IN_TM:]
    normed = []
    for c in range(n_slabs):
        cols = slice(c * KV_SLAB, (c + 1) * KV_SLAB)
        scale = lax.rsqrt(ss[c * IN_TM:(c + 1) * IN_TM] * (1.0 / HEAD_DIM) + EPS)
        normed.append((slab(c) * scale * gqk_ref[:, cols]).astype(bf16))
    for c in range(n_slabs - 1):
        q_ref[:, c * KV_SLAB:(c + 1) * KV_SLAB] = normed[c]
    k_ref[...] = normed[-1]
    v_ref[...] = proj_scr[:, ATTN_WIDTH + KV_WIDTH:ATTN_WIDTH + 2 * KV_WIDTH].astype(bf16)
    u_ref[...] = proj_scr[:, ATTN_WIDTH + 2 * KV_WIDTH:]


def _const_spec(shape):
    nd = len(shape)
    return pl.BlockSpec(shape, lambda *_: (0,) * nd)


def _in_proj(x, ln1_g, wq, wr, seg, gqk):
    nt = SEQ // IN_TM
    row_spec = lambda w: pl.BlockSpec((None, IN_TM, w), lambda b, t: (b, t, 0))
    return pl.pallas_call(
        _in_proj_kernel,
        grid=(BATCH, nt),
        in_specs=[
            row_spec(D_MODEL),
            _const_spec((1, D_MODEL)),
            _const_spec((D_MODEL, ATTN_WIDTH)),
            _const_spec((D_MODEL, IN_WIDTH - ATTN_WIDTH)),
            _const_spec((KV_SLAB, KV_SLAB)),
            _const_spec((1, ATTN_WIDTH + KV_WIDTH)),
        ],
        out_specs=[row_spec(ATTN_WIDTH), row_spec(KV_WIDTH), row_spec(KV_WIDTH), row_spec(SSM_WIDTH)],
        out_shape=[
            jax.ShapeDtypeStruct((BATCH, SEQ, ATTN_WIDTH), bf16),
            jax.ShapeDtypeStruct((BATCH, SEQ, KV_WIDTH), bf16),
            jax.ShapeDtypeStruct((BATCH, SEQ, KV_WIDTH), bf16),
            jax.ShapeDtypeStruct((BATCH, SEQ, SSM_WIDTH), f32),
        ],
        scratch_shapes=[pltpu.VMEM((IN_TM, IN_WIDTH), f32)],
        compiler_params=pltpu.CompilerParams(
            dimension_semantics=("arbitrary", "arbitrary"), vmem_limit_bytes=VMEM_LIMIT),
        name="in_proj",
    )(x, ln1_g, wq, wr, seg, gqk)


def _attn_kernel(sink_ref, q_ref, kc_ref, kp_ref, vc_ref, vp_ref, bias_ref, qmask_ref, tri_ref, g_ref,
                 w0_ref, w1_ref, w2_ref, w3_ref, o_ref, c0_ref, c1_ref, c2_ref, c3_ref, y_scr):
    for w_ref, c_ref in ((w0_ref, c0_ref), (w1_ref, c1_ref), (w2_ref, c2_ref), (w3_ref, c3_ref)):
        c_ref[...] = w_ref[...].astype(bf16)

    qi = lax.broadcasted_iota(jnp.int32, (BLOCK, BLOCK), 0)
    kj = lax.broadcasted_iota(jnp.int32, (BLOCK, BLOCK), 1)
    from_prev = kj > qi
    first_table = jnp.minimum(pl.program_id(1), 1)
    for j in range(ATTN_BLOCKS):
        rows = slice(j * BLOCK, (j + 1) * BLOCK)
        prev_rows = slice((j - 1) * BLOCK, j * BLOCK)
        k_prev = kp_ref[...] if j == 0 else kc_ref[prev_rows, :]
        v_prev = vp_ref[...] if j == 0 else vc_ref[prev_rows, :]
        keys = jnp.concatenate([k_prev, kc_ref[rows, :]], axis=0)
        vals = jnp.concatenate([v_prev, vc_ref[rows, :]], axis=0)
        for kh in range(N_KV_HEADS):
            q4 = jnp.concatenate(
                [q_ref[rows, g * KV_WIDTH:(g + 1) * KV_WIDTH] * qmask_ref[kh] for g in range(Q_PER_KV)], axis=0)
            s4 = lax.dot_general(q4, keys, (((1,), (1,)), ((), ())), preferred_element_type=f32)
            lanes = slice(kh * HEAD_DIM, (kh + 1) * HEAD_DIM)
            for g in range(Q_PER_KV):
                head = kh * Q_PER_KV + g
                bias = bias_ref[first_table, head] if j == 0 else bias_ref[1, head]
                sg = s4[g * BLOCK:(g + 1) * BLOCK]
                s = jnp.where(from_prev, sg[:, :BLOCK], sg[:, BLOCK:]) + bias
                sink = sink_ref[head]
                m = jnp.max(s, axis=-1, keepdims=True)
                p = jnp.exp(s - m)
                den = jnp.sum(p, axis=-1, keepdims=True) + jnp.exp(sink - m)
                w = (p * (1.0 / den)).astype(bf16)
                o = _dot(jnp.concatenate([w * tri_ref[0], w * tri_ref[1]], axis=1), vals)
                y_scr[rows, g * KV_WIDTH + kh * HEAD_DIM:g * KV_WIDTH + (kh + 1) * HEAD_DIM] = o[:, lanes]
    y = y_scr[...]
    o_ref[...] = (y * _rms_scale(y) * g_ref[...]).astype(bf16)


def _attention(sinks, q, k, v, bias2, qmask, tri, attn_out_g, w_gate, w_up, w_down, w_out):
    rows = ATTN_BLOCKS * BLOCK
    n_blk = SEQ // rows
    steps = BATCH * n_blk
    step = lambda b, n: b * n_blk + n

    def slab(w):
        return pl.BlockSpec((w.shape[0] // steps, w.shape[1]), lambda b, n: (step(b, n), 0))

    assert w_out.shape[0] // steps == HEAD_DIM and N_Q_HEADS * 2 == steps

    def w_out_src(b, n):
        s = step(b, n)
        return (jnp.where(s < N_Q_HEADS, (s % N_KV_HEADS) * Q_PER_KV + s // N_KV_HEADS, s), 0)

    cast_in = [slab(w_gate), slab(w_up), slab(w_down), pl.BlockSpec((HEAD_DIM, D_MODEL), w_out_src)]
    cast_out = [slab(w_gate), slab(w_up), slab(w_down), slab(w_out)]
    cur = pl.BlockSpec((None, rows, ATTN_WIDTH), lambda b, n: (b, n, 0))
    kv_cur = pl.BlockSpec((None, rows, KV_WIDTH), lambda b, n: (b, n, 0))
    kv_prev = pl.BlockSpec((None, BLOCK, KV_WIDTH), lambda b, n: (b, jnp.maximum(n * ATTN_BLOCKS - 1, 0), 0))
    return pl.pallas_call(
        _attn_kernel,
        grid=(BATCH, SEQ // rows),
        in_specs=[
            pl.BlockSpec(memory_space=pltpu.SMEM),
            cur, kv_cur, kv_prev, kv_cur, kv_prev,
            _const_spec((2, N_Q_HEADS, BLOCK, BLOCK)),
            _const_spec((Q_PER_KV, BLOCK, KV_SLAB)),
            _const_spec((2, BLOCK, BLOCK)),
            _const_spec((1, ATTN_WIDTH)),
        ] + cast_in,
        out_specs=[cur] + cast_out,
        out_shape=[jax.ShapeDtypeStruct((BATCH, SEQ, ATTN_WIDTH), bf16)]
        + [jax.ShapeDtypeStruct(w.shape, bf16) for w in (w_gate, w_up, w_down, w_out)],
        scratch_shapes=[pltpu.VMEM((rows, ATTN_WIDTH), f32)],
        compiler_params=pltpu.CompilerParams(
            dimension_semantics=("arbitrary", "arbitrary"), vmem_limit_bytes=VMEM_LIMIT),
        name="attn",
    )(sinks, q, k, k, v, v, bias2, qmask, tri, attn_out_g, w_gate, w_up, w_down, w_out)


def _gelu_tanh(x):
    return 0.5 * x * (1.0 + jnp.tanh(math.sqrt(2.0 / math.pi) * (x + 0.044715 * (x * x * x))))


def _sigmoid(x):
    return 1.0 / (1.0 + jnp.exp(-x))


def _ssm_kernel(u_ref, uprev_ref, wb_ref, lam_ref, wc_ref, d_ref, wglu_ref, g_ref,
                o_ref, u2_scr, bu_scr, h_scr, y2_scr, y_scr, st_scr):
    step = pl.program_id(0)

    @pl.when(step == 0)
    def _():
        st_scr[...] = jnp.zeros_like(st_scr)
        h_scr[1] = jnp.zeros(h_scr.shape[1:], f32)

    for parity in range(2):
        pl.when(step % 2 == parity)(functools.partial(
            _ssm_step, u_ref, uprev_ref, wb_ref, lam_ref, wc_ref, d_ref, wglu_ref, g_ref, o_ref,
            u2_scr, bu_scr, h_scr.at[parity], h_scr.at[1 - parity], y2_scr, y_scr, st_scr))


def _ssm_step(u_ref, uprev_ref, wb_ref, lam_ref, wc_ref, d_ref, wglu_ref, g_ref, o_ref,
              u2_scr, bu_scr, h_new, h_old, y2_scr, y_scr, st_scr):
    for c in range(LANE_CHUNKS):
        for b in range(BATCH):
            for par in range(2):
                u2_scr.at[c][pl.ds(2 * b + par, SSM_T, stride=V7X_SUBLANES), :] = (
                    u_ref[b, :, c * V7X_LANES:(c + 1) * V7X_LANES])
    row_par = lax.broadcasted_iota(jnp.int32, (SSM_ROWS2, V7X_LANES), 0) % 2
    pair_par = (lax.broadcasted_iota(jnp.int32, (SSM_ROWS2, V7X_LANES), 1) // (2 * SSM_GROUP)) % 2
    own_pair = row_par == pair_par

    for c in range(LANE_CHUNKS):
        bu_scr[c] = _dot(jnp.where(own_pair, u2_scr[c], 0.0).astype(bf16), wb_ref[c])

    half = LANE_CHUNKS // 2
    for c0 in (0, half):
        chains = [(c, s) for c in range(c0, c0 + half) for s in range(SLOTS)]

        def cols(s):
            base = s * 2 * V7X_LANES
            return slice(base, base + V7X_LANES), slice(base + V7X_LANES, base + 2 * V7X_LANES)

        lams = [(lam_ref[c, :, cols(s)[0]], lam_ref[c, :, cols(s)[1]]) for c, s in chains]
        init = tuple((st_scr[c, :, cols(s)[0]], st_scr[c, :, cols(s)[1]]) for c, s in chains)

        def body(t, carry, chains=chains, lams=lams, cols=cols, h_new=h_new):
            rows = pl.ds(t * V7X_SUBLANES, V7X_SUBLANES)
            out = []
            for (c, s), (l_re, l_im), (s_re, s_im) in zip(chains, lams, carry):
                cr, ci = cols(s)
                n_re = (l_re * s_re - l_im * s_im) + bu_scr[c, rows, cr]
                n_im = (l_re * s_im + l_im * s_re) + bu_scr[c, rows, ci]
                h_new[c, rows, cr] = n_re
                h_new[c, rows, ci] = n_im
                out.append((n_re, n_im))
            return tuple(out)

        fin = init
        for t in range(SSM_T):
            fin = body(t, fin)
        for (c, s), (s_re, s_im) in zip(chains, fin):
            st_scr[c, :, cols(s)[0]] = s_re
            st_scr[c, :, cols(s)[1]] = s_im

    for c in range(LANE_CHUNKS):
        cols_c = slice(c * V7X_LANES, (c + 1) * V7X_LANES)
        y2 = _dot(h_old[c].astype(bf16), wc_ref[c])
        y2_scr[2 * c] = y2[:, :V7X_LANES]
        y2_scr[2 * c + 1] = y2[:, V7X_LANES:]
        for b in range(BATCH):
            yc = (y2_scr.at[2 * c][pl.ds(2 * b, SSM_T, stride=V7X_SUBLANES), :]
                  + y2_scr.at[2 * c + 1][pl.ds(2 * b + 1, SSM_T, stride=V7X_SUBLANES), :])
            yc = yc + d_ref[:, cols_c] * uprev_ref[b, :, cols_c]
            y_scr[b * SSM_T:(b + 1) * SSM_T, cols_c] = _gelu_tanh(yc)
    y = y_scr[...]
    out = y * _sigmoid(_dot(y.astype(bf16), wglu_ref[...]))
    out = (out * _rms_scale(out) * g_ref[...]).astype(bf16)
    o_ref[...] = out.reshape(BATCH, SSM_T, SSM_WIDTH)


def _ssm(u, wb, lam, wc, d, wglu, ssm_out_g):
    n_chunks = SEQ // SSM_T
    return pl.pallas_call(
        _ssm_kernel,
        grid=(n_chunks + 1,),
        in_specs=[
            pl.BlockSpec((BATCH, SSM_T, SSM_WIDTH), lambda i: (0, jnp.minimum(i, n_chunks - 1), 0)),
            pl.BlockSpec((BATCH, SSM_T, SSM_WIDTH), lambda i: (0, jnp.maximum(i - 1, 0), 0)),
            _const_spec((LANE_CHUNKS, V7X_LANES, SLOT_COLS)),
            _const_spec((LANE_CHUNKS, V7X_SUBLANES, SLOT_COLS)),
            _const_spec((LANE_CHUNKS, SLOT_COLS, 2 * V7X_LANES)),
            _const_spec((1, SSM_WIDTH)),
            _const_spec((SSM_WIDTH, SSM_WIDTH)),
            _const_spec((1, SSM_WIDTH)),
        ],
        out_specs=pl.BlockSpec((BATCH, SSM_T, SSM_WIDTH), lambda i: (0, jnp.maximum(i - 1, 0), 0)),
        out_shape=jax.ShapeDtypeStruct((BATCH, SEQ, SSM_WIDTH), bf16),
        scratch_shapes=[
            pltpu.VMEM((LANE_CHUNKS, SSM_ROWS2, V7X_LANES), f32),
            pltpu.VMEM((LANE_CHUNKS, SSM_ROWS2, SLOT_COLS), f32),
            pltpu.VMEM((2, LANE_CHUNKS, SSM_ROWS2, SLOT_COLS), f32),
            pltpu.VMEM((2 * LANE_CHUNKS, SSM_ROWS2, V7X_LANES), f32),
            pltpu.VMEM((SSM_ROWS, SSM_WIDTH), f32),
            pltpu.VMEM((LANE_CHUNKS, V7X_SUBLANES, SLOT_COLS), f32),
        ],
        compiler_params=pltpu.CompilerParams(
            dimension_semantics=("arbitrary",), vmem_limit_bytes=VMEM_LIMIT),
        name="ssm",
    )(u, u, wb, lam, wc, d, wglu, ssm_out_g)


def _out_proj_kernel(a_ref, s_ref, x_ref, wa_ref, ws_ref, g_ref, o_ref, h_ref):
    x1 = x_ref[...] + _dot(a_ref[...], wa_ref[...]) + _dot(s_ref[...], ws_ref[...])
    o_ref[...] = x1
    h_ref[...] = (x1 * _rms_scale(x1) * g_ref[...]).astype(bf16)


def _out_proj(a, s, x, w, ln2_g):
    nt = SEQ // IN_TM
    row_spec = lambda w: pl.BlockSpec((None, IN_TM, w), lambda b, t: (b, t, 0))
    return pl.pallas_call(
        _out_proj_kernel,
        grid=(BATCH, nt),
        in_specs=[
            row_spec(ATTN_WIDTH),
            row_spec(SSM_WIDTH),
            row_spec(D_MODEL),
            pl.BlockSpec((ATTN_WIDTH, D_MODEL), lambda b, t: (0, 0)),
            pl.BlockSpec((SSM_WIDTH, D_MODEL), lambda b, t: (1, 0)),
            _const_spec((1, D_MODEL)),
        ],
        out_specs=[row_spec(D_MODEL), row_spec(D_MODEL)],
        out_shape=[jax.ShapeDtypeStruct((BATCH, SEQ, D_MODEL), f32),
                   jax.ShapeDtypeStruct((BATCH, SEQ, D_MODEL), bf16)],
        compiler_params=pltpu.CompilerParams(
            dimension_semantics=("arbitrary", "arbitrary"), vmem_limit_bytes=VMEM_LIMIT),
        name="out_proj",
    )(a, s, x, w, w, ln2_g)


def _ffn_kernel(x_hbm, h_ref, wg_ref, wu_ref, wd_ref, o_ref, sem):
    first = pl.program_id(1) == 0
    rows = pl.ds(pl.multiple_of(pl.program_id(0) * FFN_TM, FFN_TM), FFN_TM)
    residual_copy = pltpu.make_async_copy(x_hbm.at[rows, :], o_ref, sem)

    @pl.when(first)
    def _():
        residual_copy.start()

    h = h_ref[...]
    gate = _dot(h, wg_ref[...])
    up = _dot(h, wu_ref[...])
    act = (gate * _sigmoid(gate) * up).astype(bf16)

    @pl.when(first)
    def _():
        residual_copy.wait()

    o_ref[...] += _dot(act, wd_ref[...])


def _ffn(x, h, wg, wu, wd):
    n_rows = BATCH * SEQ
    return pl.pallas_call(
        _ffn_kernel,
        grid=(n_rows // FFN_TM, FF_HIDDEN // FFN_TF),
        in_specs=[
            pl.BlockSpec(memory_space=pl.ANY),
            pl.BlockSpec((FFN_TM, D_MODEL), lambda i, f: (i, 0)),
            pl.BlockSpec((D_MODEL, FFN_TF), lambda i, f: (0, f)),
            pl.BlockSpec((D_MODEL, FFN_TF), lambda i, f: (0, f)),
            pl.BlockSpec((FFN_TF, D_MODEL), lambda i, f: (f, 0)),
        ],
        out_specs=pl.BlockSpec((FFN_TM, D_MODEL), lambda i, f: (i, 0)),
        out_shape=jax.ShapeDtypeStruct((n_rows, D_MODEL), f32),
        scratch_shapes=[pltpu.SemaphoreType.DMA(())],
        compiler_params=pltpu.CompilerParams(
            dimension_semantics=("arbitrary", "arbitrary"), vmem_limit_bytes=VMEM_LIMIT),
        name="ffn",
    )(x, h, wg, wu, wd)


def _t5_bucket(dist):
    n = np.maximum(dist, 0)
    max_exact = REL_BUCKETS // 2
    nf = np.maximum(n, 1).astype(np.float32)
    large = max_exact + (np.log(nf / max_exact) / math.log(REL_MAX_DISTANCE / max_exact)
                         * (REL_BUCKETS - max_exact)).astype(np.int32)
    large = np.minimum(large, REL_BUCKETS - 1)
    return np.where(n < max_exact, n, large).astype(np.int32)


def _bias_kernel(rb_ref, bucket_ref, o_ref):
    bucket = bucket_ref[...]
    from_prev = (lax.broadcasted_iota(jnp.int32, (BLOCK, BLOCK), 1)
                 > lax.broadcasted_iota(jnp.int32, (BLOCK, BLOCK), 0))
    for head in range(N_Q_HEADS):
        acc = jnp.zeros((BLOCK, BLOCK), f32)
        for k in range(REL_BUCKETS):
            acc = jnp.where(bucket == k, rb_ref[k, head], acc)
        o_ref[1, head] = acc
        o_ref[0, head] = jnp.where(from_prev, MASK_VALUE, acc)


def _bias_tables(rel_bias):
    qi = np.arange(BLOCK)[:, None]
    kj = np.arange(BLOCK)[None, :]
    from_prev = kj > qi
    dist = np.where(from_prev, qi + BLOCK - kj, qi - kj)
    assert ((dist >= 0) & (dist < WINDOW)).all()
    return pl.pallas_call(
        _bias_kernel,
        in_specs=[
            pl.BlockSpec(memory_space=pltpu.SMEM),
            pl.BlockSpec(memory_space=pltpu.VMEM),
        ],
        out_specs=pl.BlockSpec(memory_space=pltpu.VMEM),
        out_shape=jax.ShapeDtypeStruct((2, N_Q_HEADS, BLOCK, BLOCK), f32),
        name="bias_table",
    )(rel_bias.astype(f32), jnp.asarray(_t5_bucket(dist)))


def _ssm_params(a_re, a_im, log_dt, b_re, b_im, c_re, c_im):
    dt = jnp.exp(log_dt)[:, None]
    mag = jnp.exp(a_re * dt)
    ang = a_im * dt
    lb_re, lb_im = mag * jnp.cos(ang), mag * jnp.sin(ang)
    nr, ni = lb_re - 1.0, lb_im
    den = a_re * a_re + a_im * a_im
    f_re = (nr * a_re + ni * a_im) / den
    f_im = (ni * a_re - nr * a_im) / den
    bb_re = f_re[..., None] * b_re - f_im[..., None] * b_im
    bb_im = f_re[..., None] * b_im + f_im[..., None] * b_re

    split = lambda t: t.reshape((2, LANE_CHUNKS, SLOTS, 2, 2) + t.shape[2:])
    two = np.arange(2)
    bbs = split(jnp.stack([bb_re, bb_im]))
    k = np.arange(V7X_LANES)
    k_u, k_g = k // (V7X_LANES // SLOTS), (k // SSM_GROUP) % 2
    wb_mask = ((k_u[:, None, None] == two[None, :, None]) & (k_g[:, None, None] == two[None, None, :]))
    bt = jnp.transpose(bbs, (1, 2, 3, 4, 6, 0, 5)).reshape(LANE_CHUNKS, V7X_LANES, 1, 2, 1, SSM_STATE)
    wb = (bt * wb_mask.astype(np.float32)[:, :, None, :, None]).reshape(LANE_CHUNKS, V7X_LANES, SLOT_COLS)
    cs = split(jnp.stack([c_re, -c_im]))
    r = np.arange(SLOT_COLS)
    r_s, r_g = r // (SLOT_COLS // SLOTS), (r // SSM_STATE) % 2
    wc_mask = ((two[:, None, None, None] == two[None, None, :, None])[None]
               & (r_s[:, None, None, None, None] == two[None, None, :, None, None])
               & (r_g[:, None, None, None, None] == two[None, None, None, None, :]))
    ct = jnp.transpose(cs, (1, 2, 0, 4, 6, 3, 5)).reshape(LANE_CHUNKS, SLOT_COLS, 1, 1, 2, 1, SSM_GROUP)
    wc = (ct * wc_mask.astype(np.float32)[..., None]).reshape(LANE_CHUNKS, SLOT_COLS, 2 * V7X_LANES)
    lam = jnp.transpose(split(jnp.stack([lb_re, lb_im])), (1, 3, 2, 0, 4, 5)).reshape(LANE_CHUNKS, 1, 2, SLOT_COLS)
    lam = jnp.broadcast_to(lam, (LANE_CHUNKS, BATCH, 2, SLOT_COLS)).reshape(LANE_CHUNKS, V7X_SUBLANES, SLOT_COLS)
    return wb.astype(bf16), wc.astype(bf16), lam


def _layer(x, rel_bias, ln1_g, w_in, q_norm_g, k_norm_g, attn_sinks, ssm_a_re, ssm_a_im,
           ssm_log_dt, ssm_b_re, ssm_b_im, ssm_c_re, ssm_c_im, ssm_d, w_glu,
           attn_out_g, ssm_out_g, w_out, ln2_g, w_ff_gate, w_ff_up, w_ff_down):
    row = lambda v: v.reshape(1, -1).astype(f32)

    lanes = np.arange(KV_SLAB)
    seg = jnp.asarray(lanes[:, None] // HEAD_DIM == lanes[None, :] // HEAD_DIM, bf16)
    def regroup(t, axis):
        shape = t.shape[:axis] + (N_KV_HEADS, Q_PER_KV, HEAD_DIM) + t.shape[axis + 1:]
        return jnp.swapaxes(t.reshape(shape), axis, axis + 1).reshape(t.shape)

    wq = regroup(w_in[:, :ATTN_WIDTH].astype(bf16), 1)
    wr = w_in[:, ATTN_WIDTH:].astype(bf16)
    gqk = row(jnp.concatenate([jnp.tile(q_norm_g, N_Q_HEADS) * (HEAD_DIM ** -0.5),
                               jnp.tile(k_norm_g, N_KV_HEADS)]))
    kv_mask = jnp.asarray(np.broadcast_to(
        (np.arange(KV_WIDTH)[None, None, :] // HEAD_DIM) == np.arange(N_KV_HEADS)[:, None, None],
        (N_KV_HEADS, BLOCK, KV_WIDTH)), bf16)

    q, k, v, u = _in_proj(x, row(ln1_g), wq, wr, seg, gqk)

    pos = np.arange(BLOCK)
    from_prev = pos[None, :] > pos[:, None]
    tri = jnp.asarray(np.stack([from_prev, ~from_prev]), bf16)
    y_attn, wg, wu, wd, wo = _attention(
        attn_sinks.astype(f32), q, k, v, _bias_tables(rel_bias), kv_mask, tri, row(regroup(attn_out_g, 0)),
        w_ff_gate.astype(f32), w_ff_up.astype(f32), w_ff_down.astype(f32), w_out.astype(f32))

    wb, wc, lam = _ssm_params(ssm_a_re.astype(f32), ssm_a_im.astype(f32), ssm_log_dt.astype(f32),
                                     ssm_b_re.astype(f32), ssm_b_im.astype(f32),
                                     ssm_c_re.astype(f32), ssm_c_im.astype(f32))
    y_ssm = _ssm(u, wb, lam, wc, row(ssm_d), w_glu.astype(bf16), row(ssm_out_g))

    x1, h2 = _out_proj(y_attn, y_ssm, x, wo, row(ln2_g))

    out = _ffn(x1.reshape(BATCH * SEQ, D_MODEL), h2.reshape(BATCH * SEQ, D_MODEL), wg, wu, wd)
    return out.reshape(BATCH, SEQ, D_MODEL)


def kernel(x, rel_bias, ln1_g, w_in, q_norm_g, k_norm_g, attn_sinks, ssm_a_re, ssm_a_im, ssm_log_dt, ssm_b_re, ssm_b_im, ssm_c_re, ssm_c_im, ssm_d, w_glu, attn_out_g, ssm_out_g, w_out, ln2_g, w_ff_gate, w_ff_up, w_ff_down):
    for l in range(ln1_g.shape[0]):
        x = _layer(x, rel_bias, ln1_g[l], w_in[l], q_norm_g[l], k_norm_g[l], attn_sinks[l],
                   ssm_a_re[l], ssm_a_im[l], ssm_log_dt[l], ssm_b_re[l], ssm_b_im[l],
                   ssm_c_re[l], ssm_c_im[l], ssm_d[l], w_glu[l], attn_out_g[l], ssm_out_g[l],
                   w_out[l], ln2_g[l], w_ff_gate[l], w_ff_up[l], w_ff_down[l])
    return x
```

```python
import functools
import math

import jax
import jax.numpy as jnp
import numpy as np
from jax import lax
from jax.experimental import pallas as pl
from jax.experimental.pallas import tpu as pltpu

D_MODEL = 2048
BATCH = 4
SEQ = 4096
HEAD_DIM = 64
N_Q_HEADS = 16
N_KV_HEADS = 4
Q_PER_KV = N_Q_HEADS // N_KV_HEADS
ATTN_WIDTH = N_Q_HEADS * HEAD_DIM
KV_WIDTH = N_KV_HEADS * HEAD_DIM
WINDOW = 128
BLOCK = 128
SSM_WIDTH = D_MODEL - ATTN_WIDTH
IN_WIDTH = ATTN_WIDTH + 2 * KV_WIDTH + SSM_WIDTH
SSM_GROUP = 16
SSM_GROUPS = SSM_WIDTH // SSM_GROUP
SSM_STATE = 64
FF_HIDDEN = 5632
REL_BUCKETS = 32
REL_MAX_DISTANCE = 128
EPS = 1e-6

V7X_LANES = 128
V7X_SUBLANES = 8
V7X_VMEM_BYTES = 64 * 1024 * 1024
VMEM_LIMIT = V7X_VMEM_BYTES - 8 * 1024 * 1024

MASK_VALUE = -1e30
KV_SLAB = Q_PER_KV * HEAD_DIM
LANE_CHUNKS = SSM_WIDTH // V7X_LANES
SLOTS = 2
SLOT_COLS = SLOTS * 2 * V7X_LANES

ATTN_BLOCKS = 4
IN_TM = 512
FFN_TM = 1024
FFN_TF = 512
SSM_T = 64
SSM_ROWS = SSM_T * BATCH
SSM_ROWS2 = 2 * SSM_ROWS

f32 = jnp.float32
bf16 = jnp.bfloat16


def _dot(a, b):
    return jnp.dot(a, b, preferred_element_type=f32)


def _rms_scale(x):
    return lax.rsqrt(jnp.mean(x * x, axis=-1, keepdims=True) + EPS)


def _in_proj_kernel(x_ref, g_ref, wq_ref, wr_ref, seg_ref, gqk_ref, q_ref, k_ref, v_ref, u_ref, proj_scr):
    xf = x_ref[...]
    h = (xf * _rms_scale(xf) * g_ref[...]).astype(bf16)
    proj_scr[:, :ATTN_WIDTH] = _dot(h, wq_ref[...])
    proj_scr[:, ATTN_WIDTH:] = _dot(h, wr_ref[...])

    n_slabs = (ATTN_WIDTH + KV_WIDTH) // KV_SLAB
    slab = lambda c: proj_scr[:, c * KV_SLAB:(c + 1) * KV_SLAB]
    sq = jnp.concatenate([slab(c) * slab(c) for c in range(n_slabs)], axis=0)
    hi = sq.astype(bf16)
    lo = (sq - hi.astype(f32)).astype(bf16)
    ss = _dot(jnp.concatenate([hi, lo], axis=0), seg_ref[...])
    ss = ss[:n_slabs * IN_TM] + ss[n_slabs * IN_TM:]
    normed = []
    for c in range(n_slabs):
        cols = slice(c * KV_SLAB, (c + 1) * KV_SLAB)
        scale = lax.rsqrt(ss[c * IN_TM:(c + 1) * IN_TM] * (1.0 / HEAD_DIM) + EPS)
        normed.append((slab(c) * scale * gqk_ref[:, cols]).astype(bf16))
    for c in range(n_slabs - 1):
        q_ref[:, c * KV_SLAB:(c + 1) * KV_SLAB] = normed[c]
    k_ref[...] = normed[-1]
    v_ref[...] = proj_scr[:, ATTN_WIDTH + KV_WIDTH:ATTN_WIDTH + 2 * KV_WIDTH].astype(bf16)
    u_ref[...] = proj_scr[:, ATTN_WIDTH + 2 * KV_WIDTH:]


def _const_spec(shape):
    nd = len(shape)
    return pl.BlockSpec(shape, lambda *_: (0,) * nd)


def _in_proj(x, ln1_g, wq, wr, seg, gqk):
    nt = SEQ // IN_TM
    row_spec = lambda w: pl.BlockSpec((None, IN_TM, w), lambda b, t: (b, t, 0))
    return pl.pallas_call(
        _in_proj_kernel,
        grid=(BATCH, nt),
        in_specs=[
            row_spec(D_MODEL),
            _const_spec((1, D_MODEL)),
            _const_spec((D_MODEL, ATTN_WIDTH)),
            _const_spec((D_MODEL, IN_WIDTH - ATTN_WIDTH)),
            _const_spec((KV_SLAB, KV_SLAB)),
            _const_spec((1, ATTN_WIDTH + KV_WIDTH)),
        ],
        out_specs=[row_spec(ATTN_WIDTH), row_spec(KV_WIDTH), row_spec(KV_WIDTH), row_spec(SSM_WIDTH)],
        out_shape=[
            jax.ShapeDtypeStruct((BATCH, SEQ, ATTN_WIDTH), bf16),
            jax.ShapeDtypeStruct((BATCH, SEQ, KV_WIDTH), bf16),
            jax.ShapeDtypeStruct((BATCH, SEQ, KV_WIDTH), bf16),
            jax.ShapeDtypeStruct((BATCH, SEQ, SSM_WIDTH), f32),
        ],
        scratch_shapes=[pltpu.VMEM((IN_TM, IN_WIDTH), f32)],
        compiler_params=pltpu.CompilerParams(
            dimension_semantics=("arbitrary", "arbitrary"), vmem_limit_bytes=VMEM_LIMIT),
        name="in_proj",
    )(x, ln1_g, wq, wr, seg, gqk)


def _attn_kernel(sink_ref, q_ref, kc_ref, kp_ref, vc_ref, vp_ref, bias_ref, qmask_ref, tri_ref, g_ref,
                 w0_ref, w1_ref, w2_ref, w3_ref, o_ref, c0_ref, c1_ref, c2_ref, c3_ref,
                 y_scr, s_scr, p_scr):
    for w_ref, c_ref in ((w0_ref, c0_ref), (w1_ref, c1_ref), (w2_ref, c2_ref), (w3_ref, c3_ref)):
        c_ref[...] = w_ref[...].astype(bf16)

    qi = lax.broadcasted_iota(jnp.int32, (BLOCK, BLOCK), 0)
    kj = lax.broadcasted_iota(jnp.int32, (BLOCK, BLOCK), 1)
    from_prev = kj > qi
    first_table = jnp.minimum(pl.program_id(1), 1)
    for j in range(ATTN_BLOCKS):
        rows = slice(j * BLOCK, (j + 1) * BLOCK)
        prev_rows = slice((j - 1) * BLOCK, j * BLOCK)
        k_prev = kp_ref[...] if j == 0 else kc_ref[prev_rows, :]
        v_prev = vp_ref[...] if j == 0 else vc_ref[prev_rows, :]
        keys = jnp.concatenate([k_prev, kc_ref[rows, :]], axis=0)
        vals = jnp.concatenate([v_prev, vc_ref[rows, :]], axis=0)
        q16 = jnp.concatenate(
            [q_ref[rows, g * KV_WIDTH:(g + 1) * KV_WIDTH] * qmask_ref[kh]
             for kh in range(N_KV_HEADS) for g in range(Q_PER_KV)], axis=0)
        s_scr[j] = lax.dot_general(q16, keys, (((1,), (1,)), ((), ())), preferred_element_type=f32)
        for head in range(N_Q_HEADS):
            head_rows = slice(head * BLOCK, (head + 1) * BLOCK)
            bias = bias_ref[first_table, head] if j == 0 else bias_ref[1, head]
            s = jnp.where(from_prev, s_scr[j, head_rows, :BLOCK], s_scr[j, head_rows, BLOCK:]) + bias
            sink = sink_ref[head]
            m = jnp.max(s, axis=-1, keepdims=True)
            p = jnp.exp(s - m)
            den = jnp.sum(p, axis=-1, keepdims=True) + jnp.exp(sink - m)
            w = (p * (1.0 / den)).astype(bf16)
            p_scr[j, head_rows, :BLOCK] = w * tri_ref[0]
            p_scr[j, head_rows, BLOCK:] = w * tri_ref[1]
        o16 = _dot(p_scr[j], vals)
        for head in range(N_Q_HEADS):
            kh, g = divmod(head, Q_PER_KV)
            y_scr[rows, g * KV_WIDTH + kh * HEAD_DIM:g * KV_WIDTH + (kh + 1) * HEAD_DIM] = (
                o16[head * BLOCK:(head + 1) * BLOCK, kh * HEAD_DIM:(kh + 1) * HEAD_DIM])
    y = y_scr[...]
    o_ref[...] = (y * _rms_scale(y) * g_ref[...]).astype(bf16)


def _attention(sinks, q, k, v, bias2, qmask, tri, attn_out_g, w_gate, w_up, w_down, w_out):
    rows = ATTN_BLOCKS * BLOCK
    n_blk = SEQ // rows
    steps = BATCH * n_blk
    step = lambda b, n: b * n_blk + n

    def slab(w):
        return pl.BlockSpec((w.shape[0] // steps, w.shape[1]), lambda b, n: (step(b, n), 0))

    assert w_out.shape[0] // steps == HEAD_DIM and N_Q_HEADS * 2 == steps

    def w_out_src(b, n):
        s = step(b, n)
        return (jnp.where(s < N_Q_HEADS, (s % N_KV_HEADS) * Q_PER_KV + s // N_KV_HEADS, s), 0)

    cast_in = [slab(w_gate), slab(w_up), slab(w_down), pl.BlockSpec((HEAD_DIM, D_MODEL), w_out_src)]
    cast_out = [slab(w_gate), slab(w_up), slab(w_down), slab(w_out)]
    cur = pl.BlockSpec((None, rows, ATTN_WIDTH), lambda b, n: (b, n, 0))
    kv_cur = pl.BlockSpec((None, rows, KV_WIDTH), lambda b, n: (b, n, 0))
    kv_prev = pl.BlockSpec((None, BLOCK, KV_WIDTH), lambda b, n: (b, jnp.maximum(n * ATTN_BLOCKS - 1, 0), 0))
    return pl.pallas_call(
        _attn_kernel,
        grid=(BATCH, SEQ // rows),
        in_specs=[
            pl.BlockSpec(memory_space=pltpu.SMEM),
            cur, kv_cur, kv_prev, kv_cur, kv_prev,
            _const_spec((2, N_Q_HEADS, BLOCK, BLOCK)),
            _const_spec((Q_PER_KV, BLOCK, KV_SLAB)),
            _const_spec((2, BLOCK, BLOCK)),
            _const_spec((1, ATTN_WIDTH)),
        ] + cast_in,
        out_specs=[cur] + cast_out,
        out_shape=[jax.ShapeDtypeStruct((BATCH, SEQ, ATTN_WIDTH), bf16)]
        + [jax.ShapeDtypeStruct(w.shape, bf16) for w in (w_gate, w_up, w_down, w_out)],
        scratch_shapes=[
            pltpu.VMEM((rows, ATTN_WIDTH), f32),
            pltpu.VMEM((ATTN_BLOCKS, N_Q_HEADS * BLOCK, 2 * BLOCK), f32),
            pltpu.VMEM((ATTN_BLOCKS, N_Q_HEADS * BLOCK, 2 * BLOCK), bf16),
        ],
        compiler_params=pltpu.CompilerParams(
            dimension_semantics=("arbitrary", "arbitrary"), vmem_limit_bytes=VMEM_LIMIT),
        name="attn",
    )(sinks, q, k, k, v, v, bias2, qmask, tri, attn_out_g, w_gate, w_up, w_down, w_out)


def _gelu_tanh(x):
    return 0.5 * x * (1.0 + jnp.tanh(math.sqrt(2.0 / math.pi) * (x + 0.044715 * (x * x * x))))


def _sigmoid(x):
    return 1.0 / (1.0 + jnp.exp(-x))


def _ssm_kernel(u_ref, uprev_ref, wb_ref, lam_ref, wc_ref, d_ref, wglu_ref, g_ref,
                o_ref, u2_scr, bu_scr, h_scr, y2_scr, y_scr, st_scr):
    step = pl.program_id(0)

    @pl.when(step == 0)
    def _():
        st_scr[...] = jnp.zeros_like(st_scr)
        h_scr[1] = jnp.zeros(h_scr.shape[1:], f32)

    for parity in range(2):
        pl.when(step % 2 == parity)(functools.partial(
            _ssm_step, u_ref, uprev_ref, wb_ref, lam_ref, wc_ref, d_ref, wglu_ref, g_ref, o_ref,
            u2_scr, bu_scr, h_scr.at[parity], h_scr.at[1 - parity], y2_scr, y_scr, st_scr))


def _ssm_step(u_ref, uprev_ref, wb_ref, lam_ref, wc_ref, d_ref, wglu_ref, g_ref, o_ref,
              u2_scr, bu_scr, h_new, h_old, y2_scr, y_scr, st_scr):
    for c in range(LANE_CHUNKS):
        for b in range(BATCH):
            for par in range(2):
                u2_scr.at[c][pl.ds(2 * b + par, SSM_T, stride=V7X_SUBLANES), :] = (
                    u_ref[b, :, c * V7X_LANES:(c + 1) * V7X_LANES])
    row_par = lax.broadcasted_iota(jnp.int32, (SSM_ROWS2, V7X_LANES), 0) % 2
    pair_par = (lax.broadcasted_iota(jnp.int32, (SSM_ROWS2, V7X_LANES), 1) // (2 * SSM_GROUP)) % 2
    own_pair = row_par == pair_par

    for c in range(LANE_CHUNKS):
        bu_scr[c] = _dot(jnp.where(own_pair, u2_scr[c], 0.0).astype(bf16), wb_ref[c])

    half = LANE_CHUNKS // 2
    for c0 in (0, half):
        chains = [(c, s) for c in range(c0, c0 + half) for s in range(SLOTS)]

        def cols(s):
            base = s * 2 * V7X_LANES
            return slice(base, base + V7X_LANES), slice(base + V7X_LANES, base + 2 * V7X_LANES)

        lams = [(lam_ref[c, :, cols(s)[0]], lam_ref[c, :, cols(s)[1]]) for c, s in chains]
        init = tuple((st_scr[c, :, cols(s)[0]], st_scr[c, :, cols(s)[1]]) for c, s in chains)

        def body(t, carry, chains=chains, lams=lams, cols=cols, h_new=h_new):
            rows = pl.ds(t * V7X_SUBLANES, V7X_SUBLANES)
            out = []
            for (c, s), (l_re, l_im), (s_re, s_im) in zip(chains, lams, carry):
                cr, ci = cols(s)
                n_re = (l_re * s_re - l_im * s_im) + bu_scr[c, rows, cr]
                n_im = (l_re * s_im + l_im * s_re) + bu_scr[c, rows, ci]
                h_new[c, rows, cr] = n_re
                h_new[c, rows, ci] = n_im
                out.append((n_re, n_im))
            return tuple(out)

        fin = init
        for t in range(SSM_T):
            fin = body(t, fin)
        for (c, s), (s_re, s_im) in zip(chains, fin):
            st_scr[c, :, cols(s)[0]] = s_re
            st_scr[c, :, cols(s)[1]] = s_im

    for c in range(LANE_CHUNKS):
        cols_c = slice(c * V7X_LANES, (c + 1) * V7X_LANES)
        y2 = _dot(h_old[c].astype(bf16), wc_ref[c])
        y2_scr[2 * c] = y2[:, :V7X_LANES]
        y2_scr[2 * c + 1] = y2[:, V7X_LANES:]
        for b in range(BATCH):
            yc = (y2_scr.at[2 * c][pl.ds(2 * b, SSM_T, stride=V7X_SUBLANES), :]
                  + y2_scr.at[2 * c + 1][pl.ds(2 * b + 1, SSM_T, stride=V7X_SUBLANES), :])
            yc = yc + d_ref[:, cols_c] * uprev_ref[b, :, cols_c]
            y_scr[b * SSM_T:(b + 1) * SSM_T, cols_c] = _gelu_tanh(yc)
    y = y_scr[...]
    out = y * _sigmoid(_dot(y.astype(bf16), wglu_ref[...]))
    out = (out * _rms_scale(out) * g_ref[...]).astype(bf16)
    o_ref[...] = out.reshape(BATCH, SSM_T, SSM_WIDTH)


def _ssm(u, wb, lam, wc, d, wglu, ssm_out_g):
    n_chunks = SEQ // SSM_T
    return pl.pallas_call(
        _ssm_kernel,
        grid=(n_chunks + 1,),
        in_specs=[
            pl.BlockSpec((BATCH, SSM_T, SSM_WIDTH), lambda i: (0, jnp.minimum(i, n_chunks - 1), 0)),
            pl.BlockSpec((BATCH, SSM_T, SSM_WIDTH), lambda i: (0, jnp.maximum(i - 1, 0), 0)),
            _const_spec((LANE_CHUNKS, V7X_LANES, SLOT_COLS)),
            _const_spec((LANE_CHUNKS, V7X_SUBLANES, SLOT_COLS)),
            _const_spec((LANE_CHUNKS, SLOT_COLS, 2 * V7X_LANES)),
            _const_spec((1, SSM_WIDTH)),
            _const_spec((SSM_WIDTH, SSM_WIDTH)),
            _const_spec((1, SSM_WIDTH)),
        ],
        out_specs=pl.BlockSpec((BATCH, SSM_T, SSM_WIDTH), lambda i: (0, jnp.maximum(i - 1, 0), 0)),
        out_shape=jax.ShapeDtypeStruct((BATCH, SEQ, SSM_WIDTH), bf16),
        scratch_shapes=[
            pltpu.VMEM((LANE_CHUNKS, SSM_ROWS2, V7X_LANES), f32),
            pltpu.VMEM((LANE_CHUNKS, SSM_ROWS2, SLOT_COLS), f32),
            pltpu.VMEM((2, LANE_CHUNKS, SSM_ROWS2, SLOT_COLS), f32),
            pltpu.VMEM((2 * LANE_CHUNKS, SSM_ROWS2, V7X_LANES), f32),
            pltpu.VMEM((SSM_ROWS, SSM_WIDTH), f32),
            pltpu.VMEM((LANE_CHUNKS, V7X_SUBLANES, SLOT_COLS), f32),
        ],
        compiler_params=pltpu.CompilerParams(
            dimension_semantics=("arbitrary",), vmem_limit_bytes=VMEM_LIMIT),
        name="ssm",
    )(u, u, wb, lam, wc, d, wglu, ssm_out_g)


def _out_proj_kernel(a_ref, s_ref, x_ref, wa_ref, ws_ref, g_ref, o_ref, h_ref):
    x1 = x_ref[...] + _dot(a_ref[...], wa_ref[...]) + _dot(s_ref[...], ws_ref[...])
    o_ref[...] = x1
    h_ref[...] = (x1 * _rms_scale(x1) * g_ref[...]).astype(bf16)


def _out_proj(a, s, x, w, ln2_g):
    nt = SEQ // IN_TM
    row_spec = lambda w: pl.BlockSpec((None, IN_TM, w), lambda b, t: (b, t, 0))
    return pl.pallas_call(
        _out_proj_kernel,
        grid=(BATCH, nt),
        in_specs=[
            row_spec(ATTN_WIDTH),
            row_spec(SSM_WIDTH),
            row_spec(D_MODEL),
            pl.BlockSpec((ATTN_WIDTH, D_MODEL), lambda b, t: (0, 0)),
            pl.BlockSpec((SSM_WIDTH, D_MODEL), lambda b, t: (1, 0)),
            _const_spec((1, D_MODEL)),
        ],
        out_specs=[row_spec(D_MODEL), row_spec(D_MODEL)],
        out_shape=[jax.ShapeDtypeStruct((BATCH, SEQ, D_MODEL), f32),
                   jax.ShapeDtypeStruct((BATCH, SEQ, D_MODEL), bf16)],
        compiler_params=pltpu.CompilerParams(
            dimension_semantics=("arbitrary", "arbitrary"), vmem_limit_bytes=VMEM_LIMIT),
        name="out_proj",
    )(a, s, x, w, w, ln2_g)


def _ffn_kernel(x_hbm, h_ref, wg_ref, wu_ref, wd_ref, o_ref, sem):
    first = pl.program_id(1) == 0
    rows = pl.ds(pl.multiple_of(pl.program_id(0) * FFN_TM, FFN_TM), FFN_TM)
    residual_copy = pltpu.make_async_copy(x_hbm.at[rows, :], o_ref, sem)

    @pl.when(first)
    def _():
        residual_copy.start()

    h = h_ref[...]
    gate = _dot(h, wg_ref[...])
    up = _dot(h, wu_ref[...])
    act = (gate * _sigmoid(gate) * up).astype(bf16)

    @pl.when(first)
    def _():
        residual_copy.wait()

    o_ref[...] += _dot(act, wd_ref[...])


def _ffn(x, h, wg, wu, wd):
    n_rows = BATCH * SEQ
    return pl.pallas_call(
        _ffn_kernel,
        grid=(n_rows // FFN_TM, FF_HIDDEN // FFN_TF),
        in_specs=[
            pl.BlockSpec(memory_space=pl.ANY),
            pl.BlockSpec((FFN_TM, D_MODEL), lambda i, f: (i, 0)),
            pl.BlockSpec((D_MODEL, FFN_TF), lambda i, f: (0, f)),
            pl.BlockSpec((D_MODEL, FFN_TF), lambda i, f: (0, f)),
            pl.BlockSpec((FFN_TF, D_MODEL), lambda i, f: (f, 0)),
        ],
        out_specs=pl.BlockSpec((FFN_TM, D_MODEL), lambda i, f: (i, 0)),
        out_shape=jax.ShapeDtypeStruct((n_rows, D_MODEL), f32),
        scratch_shapes=[pltpu.SemaphoreType.DMA(())],
        compiler_params=pltpu.CompilerParams(
            dimension_semantics=("arbitrary", "arbitrary"), vmem_limit_bytes=VMEM_LIMIT),
        name="ffn",
    )(x, h, wg, wu, wd)


def _t5_bucket(dist):
    n = np.maximum(dist, 0)
    max_exact = REL_BUCKETS // 2
    nf = np.maximum(n, 1).astype(np.float32)
    large = max_exact + (np.log(nf / max_exact) / math.log(REL_MAX_DISTANCE / max_exact)
                         * (REL_BUCKETS - max_exact)).astype(np.int32)
    large = np.minimum(large, REL_BUCKETS - 1)
    return np.where(n < max_exact, n, large).astype(np.int32)


def _bias_kernel(rb_ref, bucket_ref, o_ref):
    bucket = bucket_ref[...]
    from_prev = (lax.broadcasted_iota(jnp.int32, (BLOCK, BLOCK), 1)
                 > lax.broadcasted_iota(jnp.int32, (BLOCK, BLOCK), 0))
    for head in range(N_Q_HEADS):
        acc = jnp.zeros((BLOCK, BLOCK), f32)
        for k in range(REL_BUCKETS):
            acc = jnp.where(bucket == k, rb_ref[k, head], acc)
        o_ref[1, head] = acc
        o_ref[0, head] = jnp.where(from_prev, MASK_VALUE, acc)


def _bias_tables(rel_bias):
    qi = np.arange(BLOCK)[:, None]
    kj = np.arange(BLOCK)[None, :]
    from_prev = kj > qi
    dist = np.where(from_prev, qi + BLOCK - kj, qi - kj)
    assert ((dist >= 0) & (dist < WINDOW)).all()
    return pl.pallas_call(
        _bias_kernel,
        in_specs=[
            pl.BlockSpec(memory_space=pltpu.SMEM),
            pl.BlockSpec(memory_space=pltpu.VMEM),
        ],
        out_specs=pl.BlockSpec(memory_space=pltpu.VMEM),
        out_shape=jax.ShapeDtypeStruct((2, N_Q_HEADS, BLOCK, BLOCK), f32),
        name="bias_table",
    )(rel_bias.astype(f32), jnp.asarray(_t5_bucket(dist)))


def _ssm_params(a_re, a_im, log_dt, b_re, b_im, c_re, c_im):
    dt = jnp.exp(log_dt)[:, None]
    mag = jnp.exp(a_re * dt)
    ang = a_im * dt
    lb_re, lb_im = mag * jnp.cos(ang), mag * jnp.sin(ang)
    nr, ni = lb_re - 1.0, lb_im
    den = a_re * a_re + a_im * a_im
    f_re = (nr * a_re + ni * a_im) / den
    f_im = (ni * a_re - nr * a_im) / den
    bb_re = f_re[..., None] * b_re - f_im[..., None] * b_im
    bb_im = f_re[..., None] * b_im + f_im[..., None] * b_re

    split = lambda t: t.reshape((2, LANE_CHUNKS, SLOTS, 2, 2) + t.shape[2:])
    two = np.arange(2)
    bbs = split(jnp.stack([bb_re, bb_im]))
    k = np.arange(V7X_LANES)
    k_u, k_g = k // (V7X_LANES // SLOTS), (k // SSM_GROUP) % 2
    wb_mask = ((k_u[:, None, None] == two[None, :, None]) & (k_g[:, None, None] == two[None, None, :]))
    bt = jnp.transpose(bbs, (1, 2, 3, 4, 6, 0, 5)).reshape(LANE_CHUNKS, V7X_LANES, 1, 2, 1, SSM_STATE)
    wb = (bt * wb_mask.astype(np.float32)[:, :, None, :, None]).reshape(LANE_CHUNKS, V7X_LANES, SLOT_COLS)
    cs = split(jnp.stack([c_re, -c_im]))
    r = np.arange(SLOT_COLS)
    r_s, r_g = r // (SLOT_COLS // SLOTS), (r // SSM_STATE) % 2
    wc_mask = ((two[:, None, None, None] == two[None, None, :, None])[None]
               & (r_s[:, None, None, None, None] == two[None, None, :, None, None])
               & (r_g[:, None, None, None, None] == two[None, None, None, None, :]))
    ct = jnp.transpose(cs, (1, 2, 0, 4, 6, 3, 5)).reshape(LANE_CHUNKS, SLOT_COLS, 1, 1, 2, 1, SSM_GROUP)
    wc = (ct * wc_mask.astype(np.float32)[..., None]).reshape(LANE_CHUNKS, SLOT_COLS, 2 * V7X_LANES)
    lam = jnp.transpose(split(jnp.stack([lb_re, lb_im])), (1, 3, 2, 0, 4, 5)).reshape(LANE_CHUNKS, 1, 2, SLOT_COLS)
    lam = jnp.broadcast_to(lam, (LANE_CHUNKS, BATCH, 2, SLOT_COLS)).reshape(LANE_CHUNKS, V7X_SUBLANES, SLOT_COLS)
    return wb.astype(bf16), wc.astype(bf16), lam


def _layer(x, rel_bias, ln1_g, w_in, q_norm_g, k_norm_g, attn_sinks, ssm_a_re, ssm_a_im,
           ssm_log_dt, ssm_b_re, ssm_b_im, ssm_c_re, ssm_c_im, ssm_d, w_glu,
           attn_out_g, ssm_out_g, w_out, ln2_g, w_ff_gate, w_ff_up, w_ff_down):
    row = lambda v: v.reshape(1, -1).astype(f32)

    lanes = np.arange(KV_SLAB)
    seg = jnp.asarray(lanes[:, None] // HEAD_DIM == lanes[None, :] // HEAD_DIM, bf16)
    def regroup(t, axis):
        shape = t.shape[:axis] + (N_KV_HEADS, Q_PER_KV, HEAD_DIM) + t.shape[axis + 1:]
        return jnp.swapaxes(t.reshape(shape), axis, axis + 1).reshape(t.shape)

    wq = regroup(w_in[:, :ATTN_WIDTH].astype(bf16), 1)
    wr = w_in[:, ATTN_WIDTH:].astype(bf16)
    gqk = row(jnp.concatenate([jnp.tile(q_norm_g, N_Q_HEADS) * (HEAD_DIM ** -0.5),
                               jnp.tile(k_norm_g, N_KV_HEADS)]))
    kv_mask = jnp.asarray(np.broadcast_to(
        (np.arange(KV_WIDTH)[None, None, :] // HEAD_DIM) == np.arange(N_KV_HEADS)[:, None, None],
        (N_KV_HEADS, BLOCK, KV_WIDTH)), bf16)

    q, k, v, u = _in_proj(x, row(ln1_g), wq, wr, seg, gqk)

    pos = np.arange(BLOCK)
    from_prev = pos[None, :] > pos[:, None]
    tri = jnp.asarray(np.stack([from_prev, ~from_prev]), bf16)
    y_attn, wg, wu, wd, wo = _attention(
        attn_sinks.astype(f32), q, k, v, _bias_tables(rel_bias), kv_mask, tri, row(regroup(attn_out_g, 0)),
        w_ff_gate.astype(f32), w_ff_up.astype(f32), w_ff_down.astype(f32), w_out.astype(f32))

    wb, wc, lam = _ssm_params(ssm_a_re.astype(f32), ssm_a_im.astype(f32), ssm_log_dt.astype(f32),
                                     ssm_b_re.astype(f32), ssm_b_im.astype(f32),
                                     ssm_c_re.astype(f32), ssm_c_im.astype(f32))
    y_ssm = _ssm(u, wb, lam, wc, row(ssm_d), w_glu.astype(bf16), row(ssm_out_g))

    x1, h2 = _out_proj(y_attn, y_ssm, x, wo, row(ln2_g))

    out = _ffn(x1.reshape(BATCH * SEQ, D_MODEL), h2.reshape(BATCH * SEQ, D_MODEL), wg, wu, wd)
    return out.reshape(BATCH, SEQ, D_MODEL)


def kernel(x, rel_bias, ln1_g, w_in, q_norm_g, k_norm_g, attn_sinks, ssm_a_re, ssm_a_im, ssm_log_dt, ssm_b_re, ssm_b_im, ssm_c_re, ssm_c_im, ssm_d, w_glu, attn_out_g, ssm_out_g, w_out, ln2_g, w_ff_gate, w_ff_up, w_ff_down):
    for l in range(ln1_g.shape[0]):
        x = _layer(x, rel_bias, ln1_g[l], w_in[l], q_norm_g[l], k_norm_g[l], attn_sinks[l],
                   ssm_a_re[l], ssm_a_im[l], ssm_log_dt[l], ssm_b_re[l], ssm_b_im[l],
                   ssm_c_re[l], ssm_c_im[l], ssm_d[l], w_glu[l], attn_out_g[l], ssm_out_g[l],
                   w_out[l], ln2_g[l], w_ff_gate[l], w_ff_up[l], w_ff_down[l])
    return x
```

```python
import functools
import math

import jax
import jax.numpy as jnp
import numpy as np
from jax import lax
from jax.experimental import pallas as pl
from jax.experimental.pallas import tpu as pltpu

D_MODEL = 2048
BATCH = 4
SEQ = 4096
HEAD_DIM = 64
N_Q_HEADS = 16
N_KV_HEADS = 4
Q_PER_KV = N_Q_HEADS // N_KV_HEADS
ATTN_WIDTH = N_Q_HEADS * HEAD_DIM
KV_WIDTH = N_KV_HEADS * HEAD_DIM
WINDOW = 128
BLOCK = 128
SSM_WIDTH = D_MODEL - ATTN_WIDTH
IN_WIDTH = ATTN_WIDTH + 2 * KV_WIDTH + SSM_WIDTH
SSM_GROUP = 16
SSM_GROUPS = SSM_WIDTH // SSM_GROUP
SSM_STATE = 64
FF_HIDDEN = 5632
REL_BUCKETS = 32
REL_MAX_DISTANCE = 128
EPS = 1e-6

V7X_LANES = 128
V7X_SUBLANES = 8
V7X_VMEM_BYTES = 64 * 1024 * 1024
VMEM_LIMIT = V7X_VMEM_BYTES - 8 * 1024 * 1024

MASK_VALUE = -1e30
KV_SLAB = Q_PER_KV * HEAD_DIM
LANE_CHUNKS = SSM_WIDTH // V7X_LANES
SLOTS = 2
SLOT_COLS = SLOTS * 2 * V7X_LANES

ATTN_BLOCKS = 4
IN_TM = 512
FFN_TM = 1024
FFN_TF = 512
SSM_T = 64
SSM_ROWS = SSM_T * BATCH
SSM_ROWS2 = 2 * SSM_ROWS

f32 = jnp.float32
bf16 = jnp.bfloat16


def _dot(a, b):
    return jnp.dot(a, b, preferred_element_type=f32)


def _rms_scale(x):
    return lax.rsqrt(jnp.mean(x * x, axis=-1, keepdims=True) + EPS)


def _in_proj_kernel(x_ref, g_ref, w_ref, seg_ref, gqk_ref, q_ref, k_ref, v_ref, u_ref, proj_scr):
    xf = x_ref[...]
    h = (xf * _rms_scale(xf) * g_ref[...]).astype(bf16)
    proj_scr[...] = _dot(h, w_ref[...])

    n_slabs = (ATTN_WIDTH + KV_WIDTH) // KV_SLAB
    slab = lambda c: proj_scr[:, c * KV_SLAB:(c + 1) * KV_SLAB]
    sq = jnp.concatenate([slab(c) * slab(c) for c in range(n_slabs)], axis=0)
    hi = sq.astype(bf16)
    lo = (sq - hi.astype(f32)).astype(bf16)
    ss = _dot(jnp.concatenate([hi, lo], axis=0), seg_ref[...])
    ss = ss[:n_slabs * IN_TM] + ss[n_slabs * IN_TM:]
    normed = []
    for c in range(n_slabs):
        cols = slice(c * KV_SLAB, (c + 1) * KV_SLAB)
        scale = lax.rsqrt(ss[c * IN_TM:(c + 1) * IN_TM] * (1.0 / HEAD_DIM) + EPS)
        normed.append(slab(c) * scale * gqk_ref[:, cols])
    for g in range(Q_PER_KV):
        q_ref[:, g * KV_WIDTH:(g + 1) * KV_WIDTH] = jnp.concatenate(
            [normed[kh][:, g * HEAD_DIM:(g + 1) * HEAD_DIM] for kh in range(N_KV_HEADS)], axis=1).astype(bf16)
    k_ref[...] = normed[-1].astype(bf16)
    v_ref[...] = proj_scr[:, ATTN_WIDTH + KV_WIDTH:ATTN_WIDTH + 2 * KV_WIDTH].astype(bf16)
    u_ref[...] = proj_scr[:, ATTN_WIDTH + 2 * KV_WIDTH:]


def _const_spec(shape):
    nd = len(shape)
    return pl.BlockSpec(shape, lambda *_: (0,) * nd)


def _in_proj(x, ln1_g, w, seg, gqk):
    nt = SEQ // IN_TM
    row_spec = lambda w: pl.BlockSpec((None, IN_TM, w), lambda b, t: (b, t, 0))
    return pl.pallas_call(
        _in_proj_kernel,
        grid=(BATCH, nt),
        in_specs=[
            row_spec(D_MODEL),
            _const_spec((1, D_MODEL)),
            _const_spec((D_MODEL, IN_WIDTH)),
            _const_spec((KV_SLAB, KV_SLAB)),
            _const_spec((1, ATTN_WIDTH + KV_WIDTH)),
        ],
        out_specs=[row_spec(ATTN_WIDTH), row_spec(KV_WIDTH), row_spec(KV_WIDTH), row_spec(SSM_WIDTH)],
        out_shape=[
            jax.ShapeDtypeStruct((BATCH, SEQ, ATTN_WIDTH), bf16),
            jax.ShapeDtypeStruct((BATCH, SEQ, KV_WIDTH), bf16),
            jax.ShapeDtypeStruct((BATCH, SEQ, KV_WIDTH), bf16),
            jax.ShapeDtypeStruct((BATCH, SEQ, SSM_WIDTH), f32),
        ],
        scratch_shapes=[pltpu.VMEM((IN_TM, IN_WIDTH), f32)],
        compiler_params=pltpu.CompilerParams(
            dimension_semantics=("arbitrary", "arbitrary"), vmem_limit_bytes=VMEM_LIMIT),
        name="in_proj",
    )(x, ln1_g, w, seg, gqk)


def _attn_kernel(sink_ref, q_ref, kc_ref, kp_ref, vc_ref, vp_ref, bias_ref, kv_mask_ref, tri_ref, g_ref,
                 w0_ref, w1_ref, w2_ref, w3_ref, o_ref, c0_ref, c1_ref, c2_ref, c3_ref,
                 y_scr, s_scr, p_scr):
    for w_ref, c_ref in ((w0_ref, c0_ref), (w1_ref, c1_ref), (w2_ref, c2_ref), (w3_ref, c3_ref)):
        c_ref[...] = w_ref[...].astype(bf16)

    qi = lax.broadcasted_iota(jnp.int32, (BLOCK, BLOCK), 0)
    kj = lax.broadcasted_iota(jnp.int32, (BLOCK, BLOCK), 1)
    from_prev = kj > qi
    first_table = jnp.minimum(pl.program_id(1), 1)
    for j in range(ATTN_BLOCKS):
        rows = slice(j * BLOCK, (j + 1) * BLOCK)
        prev_rows = slice((j - 1) * BLOCK, j * BLOCK)
        k_prev = kp_ref[...] if j == 0 else kc_ref[prev_rows, :]
        v_prev = vp_ref[...] if j == 0 else vc_ref[prev_rows, :]
        keys = jnp.concatenate([k_prev, kc_ref[rows, :]], axis=0)
        vals = jnp.concatenate([v_prev, vc_ref[rows, :]], axis=0)
        q16 = jnp.concatenate(
            [q_ref[rows, g * KV_WIDTH:(g + 1) * KV_WIDTH] * kv_mask_ref[kh]
             for kh in range(N_KV_HEADS) for g in range(Q_PER_KV)], axis=0)
        s_scr[j] = lax.dot_general(q16, keys, (((1,), (1,)), ((), ())), preferred_element_type=f32)
        for head in range(N_Q_HEADS):
            head_rows = slice(head * BLOCK, (head + 1) * BLOCK)
            bias = bias_ref[first_table, head] if j == 0 else bias_ref[1, head]
            s = jnp.where(from_prev, s_scr[j, head_rows, :BLOCK], s_scr[j, head_rows, BLOCK:]) + bias
            sink = sink_ref[head]
            m = jnp.max(s, axis=-1, keepdims=True)
            p = jnp.exp(s - m)
            den = jnp.sum(p, axis=-1, keepdims=True) + jnp.exp(sink - m)
            w = (p * (1.0 / den)).astype(bf16)
            p_scr[j, head_rows, :BLOCK] = w * tri_ref[0]
            p_scr[j, head_rows, BLOCK:] = w * tri_ref[1]
        o16 = _dot(p_scr[j], vals)
        for head in range(N_Q_HEADS):
            kh, g = divmod(head, Q_PER_KV)
            y_scr[rows, g * KV_WIDTH + kh * HEAD_DIM:g * KV_WIDTH + (kh + 1) * HEAD_DIM] = (
                o16[head * BLOCK:(head + 1) * BLOCK, kh * HEAD_DIM:(kh + 1) * HEAD_DIM])
    y = y_scr[...]
    o_ref[...] = (y * _rms_scale(y) * g_ref[...]).astype(bf16)


def _attention(sinks, q, k, v, bias2, kv_mask, tri, attn_out_g, w_gate, w_up, w_down, w_out):
    rows = ATTN_BLOCKS * BLOCK
    n_blk = SEQ // rows
    steps = BATCH * n_blk
    step = lambda b, n: b * n_blk + n

    def slab(w):
        return pl.BlockSpec((w.shape[0] // steps, w.shape[1]), lambda b, n: (step(b, n), 0))

    assert w_out.shape[0] // steps == HEAD_DIM and N_Q_HEADS * 2 == steps

    def w_out_src(b, n):
        s = step(b, n)
        return (jnp.where(s < N_Q_HEADS, (s % N_KV_HEADS) * Q_PER_KV + s // N_KV_HEADS, s), 0)

    cast_in = [slab(w_gate), slab(w_up), slab(w_down), pl.BlockSpec((HEAD_DIM, D_MODEL), w_out_src)]
    cast_out = [slab(w_gate), slab(w_up), slab(w_down), slab(w_out)]
    cur = pl.BlockSpec((None, rows, ATTN_WIDTH), lambda b, n: (b, n, 0))
    kv_cur = pl.BlockSpec((None, rows, KV_WIDTH), lambda b, n: (b, n, 0))
    kv_prev = pl.BlockSpec((None, BLOCK, KV_WIDTH), lambda b, n: (b, jnp.maximum(n * ATTN_BLOCKS - 1, 0), 0))
    return pl.pallas_call(
        _attn_kernel,
        grid=(BATCH, SEQ // rows),
        in_specs=[
            pl.BlockSpec(memory_space=pltpu.SMEM),
            cur, kv_cur, kv_prev, kv_cur, kv_prev,
            _const_spec((2, N_Q_HEADS, BLOCK, BLOCK)),
            _const_spec((N_KV_HEADS, BLOCK, KV_WIDTH)),
            _const_spec((2, BLOCK, BLOCK)),
            _const_spec((1, ATTN_WIDTH)),
        ] + cast_in,
        out_specs=[cur] + cast_out,
        out_shape=[jax.ShapeDtypeStruct((BATCH, SEQ, ATTN_WIDTH), bf16)]
        + [jax.ShapeDtypeStruct(w.shape, bf16) for w in (w_gate, w_up, w_down, w_out)],
        scratch_shapes=[
            pltpu.VMEM((rows, ATTN_WIDTH), f32),
            pltpu.VMEM((ATTN_BLOCKS, N_Q_HEADS * BLOCK, 2 * BLOCK), f32),
            pltpu.VMEM((ATTN_BLOCKS, N_Q_HEADS * BLOCK, 2 * BLOCK), bf16),
        ],
        compiler_params=pltpu.CompilerParams(
            dimension_semantics=("arbitrary", "arbitrary"), vmem_limit_bytes=VMEM_LIMIT),
        name="attn",
    )(sinks, q, k, k, v, v, bias2, kv_mask, tri, attn_out_g, w_gate, w_up, w_down, w_out)


def _gelu_tanh(x):
    return 0.5 * x * (1.0 + jnp.tanh(math.sqrt(2.0 / math.pi) * (x + 0.044715 * (x * x * x))))


def _sigmoid(x):
    return 1.0 / (1.0 + jnp.exp(-x))


def _ssm_kernel(u_ref, uprev_ref, wb_ref, lam_ref, wc_ref, d_ref, wglu_ref, g_ref,
                o_ref, u2_scr, bu_scr, h_scr, y2_scr, y_scr, st_scr):
    step = pl.program_id(0)

    @pl.when(step == 0)
    def _():
        st_scr[...] = jnp.zeros_like(st_scr)
        h_scr[1] = jnp.zeros(h_scr.shape[1:], f32)

    for parity in range(2):
        pl.when(step % 2 == parity)(functools.partial(
            _ssm_step, u_ref, uprev_ref, wb_ref, lam_ref, wc_ref, d_ref, wglu_ref, g_ref, o_ref,
            u2_scr, bu_scr, h_scr.at[parity], h_scr.at[1 - parity], y2_scr, y_scr, st_scr))


def _ssm_step(u_ref, uprev_ref, wb_ref, lam_ref, wc_ref, d_ref, wglu_ref, g_ref, o_ref,
              u2_scr, bu_scr, h_new, h_old, y2_scr, y_scr, st_scr):
    for c in range(LANE_CHUNKS):
        for b in range(BATCH):
            for par in range(2):
                u2_scr.at[c][pl.ds(2 * b + par, SSM_T, stride=V7X_SUBLANES), :] = (
                    u_ref[b, :, c * V7X_LANES:(c + 1) * V7X_LANES])
    row_par = lax.broadcasted_iota(jnp.int32, (SSM_ROWS2, V7X_LANES), 0) % 2
    pair_par = (lax.broadcasted_iota(jnp.int32, (SSM_ROWS2, V7X_LANES), 1) // (2 * SSM_GROUP)) % 2
    own_pair = row_par == pair_par

    for c in range(LANE_CHUNKS):
        bu_scr[c] = _dot(jnp.where(own_pair, u2_scr[c], 0.0).astype(bf16), wb_ref[c])

    half = LANE_CHUNKS // 2
    for c0 in (0, half):
        chains = [(c, s) for c in range(c0, c0 + half) for s in range(SLOTS)]

        def cols(s):
            base = s * 2 * V7X_LANES
            return slice(base, base + V7X_LANES), slice(base + V7X_LANES, base + 2 * V7X_LANES)

        lams = [(lam_ref[c, :, cols(s)[0]], lam_ref[c, :, cols(s)[1]]) for c, s in chains]
        init = tuple((st_scr[c, :, cols(s)[0]], st_scr[c, :, cols(s)[1]]) for c, s in chains)

        def body(t, carry, chains=chains, lams=lams, cols=cols, h_new=h_new):
            rows = pl.ds(t * V7X_SUBLANES, V7X_SUBLANES)
            out = []
            for (c, s), (l_re, l_im), (s_re, s_im) in zip(chains, lams, carry):
                cr, ci = cols(s)
                n_re = (l_re * s_re - l_im * s_im) + bu_scr[c, rows, cr]
                n_im = (l_re * s_im + l_im * s_re) + bu_scr[c, rows, ci]
                h_new[c, rows, cr] = n_re
                h_new[c, rows, ci] = n_im
                out.append((n_re, n_im))
            return tuple(out)

        fin = init
        for t in range(SSM_T):
            fin = body(t, fin)
        for (c, s), (s_re, s_im) in zip(chains, fin):
            st_scr[c, :, cols(s)[0]] = s_re
            st_scr[c, :, cols(s)[1]] = s_im

    for c in range(LANE_CHUNKS):
        cols_c = slice(c * V7X_LANES, (c + 1) * V7X_LANES)
        y2 = _dot(h_old[c].astype(bf16), wc_ref[c])
        y2_scr[2 * c] = y2[:, :V7X_LANES]
        y2_scr[2 * c + 1] = y2[:, V7X_LANES:]
        for b in range(BATCH):
            yc = (y2_scr.at[2 * c][pl.ds(2 * b, SSM_T, stride=V7X_SUBLANES), :]
                  + y2_scr.at[2 * c + 1][pl.ds(2 * b + 1, SSM_T, stride=V7X_SUBLANES), :])
            yc = yc + d_ref[:, cols_c] * uprev_ref[b, :, cols_c]
            y_scr[b * SSM_T:(b + 1) * SSM_T, cols_c] = _gelu_tanh(yc)
    y = y_scr[...]
    out = y * _sigmoid(_dot(y.astype(bf16), wglu_ref[...]))
    out = (out * _rms_scale(out) * g_ref[...]).astype(bf16)
    o_ref[...] = out.reshape(BATCH, SSM_T, SSM_WIDTH)


def _ssm(u, wb, lam, wc, d, wglu, ssm_out_g):
    n_chunks = SEQ // SSM_T
    return pl.pallas_call(
        _ssm_kernel,
        grid=(n_chunks + 1,),
        in_specs=[
            pl.BlockSpec((BATCH, SSM_T, SSM_WIDTH), lambda i: (0, jnp.minimum(i, n_chunks - 1), 0)),
            pl.BlockSpec((BATCH, SSM_T, SSM_WIDTH), lambda i: (0, jnp.maximum(i - 1, 0), 0)),
            _const_spec((LANE_CHUNKS, V7X_LANES, SLOT_COLS)),
            _const_spec((LANE_CHUNKS, V7X_SUBLANES, SLOT_COLS)),
            _const_spec((LANE_CHUNKS, SLOT_COLS, 2 * V7X_LANES)),
            _const_spec((1, SSM_WIDTH)),
            _const_spec((SSM_WIDTH, SSM_WIDTH)),
            _const_spec((1, SSM_WIDTH)),
        ],
        out_specs=pl.BlockSpec((BATCH, SSM_T, SSM_WIDTH), lambda i: (0, jnp.maximum(i - 1, 0), 0)),
        out_shape=jax.ShapeDtypeStruct((BATCH, SEQ, SSM_WIDTH), bf16),
        scratch_shapes=[
            pltpu.VMEM((LANE_CHUNKS, SSM_ROWS2, V7X_LANES), f32),
            pltpu.VMEM((LANE_CHUNKS, SSM_ROWS2, SLOT_COLS), f32),
            pltpu.VMEM((2, LANE_CHUNKS, SSM_ROWS2, SLOT_COLS), f32),
            pltpu.VMEM((2 * LANE_CHUNKS, SSM_ROWS2, V7X_LANES), f32),
            pltpu.VMEM((SSM_ROWS, SSM_WIDTH), f32),
            pltpu.VMEM((LANE_CHUNKS, V7X_SUBLANES, SLOT_COLS), f32),
        ],
        compiler_params=pltpu.CompilerParams(
            dimension_semantics=("arbitrary",), vmem_limit_bytes=VMEM_LIMIT),
        name="ssm",
    )(u, u, wb, lam, wc, d, wglu, ssm_out_g)


def _out_proj_kernel(a_ref, s_ref, x_ref, wa_ref, ws_ref, g_ref, o_ref, h_ref):
    x1 = x_ref[...] + _dot(a_ref[...], wa_ref[...]) + _dot(s_ref[...], ws_ref[...])
    o_ref[...] = x1
    h_ref[...] = (x1 * _rms_scale(x1) * g_ref[...]).astype(bf16)


def _out_proj(a, s, x, w, ln2_g):
    nt = SEQ // IN_TM
    row_spec = lambda w: pl.BlockSpec((None, IN_TM, w), lambda b, t: (b, t, 0))
    return pl.pallas_call(
        _out_proj_kernel,
        grid=(BATCH, nt),
        in_specs=[
            row_spec(ATTN_WIDTH),
            row_spec(SSM_WIDTH),
            row_spec(D_MODEL),
            pl.BlockSpec((ATTN_WIDTH, D_MODEL), lambda b, t: (0, 0)),
            pl.BlockSpec((SSM_WIDTH, D_MODEL), lambda b, t: (1, 0)),
            _const_spec((1, D_MODEL)),
        ],
        out_specs=[row_spec(D_MODEL), row_spec(D_MODEL)],
        out_shape=[jax.ShapeDtypeStruct((BATCH, SEQ, D_MODEL), f32),
                   jax.ShapeDtypeStruct((BATCH, SEQ, D_MODEL), bf16)],
        compiler_params=pltpu.CompilerParams(
            dimension_semantics=("arbitrary", "arbitrary"), vmem_limit_bytes=VMEM_LIMIT),
        name="out_proj",
    )(a, s, x, w, w, ln2_g)


def _ffn_kernel(x_hbm, h_ref, wg_ref, wu_ref, wd_ref, o_ref, sem):
    first = pl.program_id(1) == 0
    rows = pl.ds(pl.multiple_of(pl.program_id(0) * FFN_TM, FFN_TM), FFN_TM)
    residual_copy = pltpu.make_async_copy(x_hbm.at[rows, :], o_ref, sem)

    @pl.when(first)
    def _():
        residual_copy.start()

    h = h_ref[...]
    gate = _dot(h, wg_ref[...])
    up = _dot(h, wu_ref[...])
    act = (gate * _sigmoid(gate) * up).astype(bf16)

    @pl.when(first)
    def _():
        residual_copy.wait()

    o_ref[...] += _dot(act, wd_ref[...])


def _ffn(x, h, wg, wu, wd):
    n_rows = BATCH * SEQ
    return pl.pallas_call(
        _ffn_kernel,
        grid=(n_rows // FFN_TM, FF_HIDDEN // FFN_TF),
        in_specs=[
            pl.BlockSpec(memory_space=pl.ANY),
            pl.BlockSpec((FFN_TM, D_MODEL), lambda i, f: (i, 0)),
            pl.BlockSpec((D_MODEL, FFN_TF), lambda i, f: (0, f)),
            pl.BlockSpec((D_MODEL, FFN_TF), lambda i, f: (0, f)),
            pl.BlockSpec((FFN_TF, D_MODEL), lambda i, f: (f, 0)),
        ],
        out_specs=pl.BlockSpec((FFN_TM, D_MODEL), lambda i, f: (i, 0)),
        out_shape=jax.ShapeDtypeStruct((n_rows, D_MODEL), f32),
        scratch_shapes=[pltpu.SemaphoreType.DMA(())],
        compiler_params=pltpu.CompilerParams(
            dimension_semantics=("arbitrary", "arbitrary"), vmem_limit_bytes=VMEM_LIMIT),
        name="ffn",
    )(x, h, wg, wu, wd)


def _t5_bucket(dist):
    n = np.maximum(dist, 0)
    max_exact = REL_BUCKETS // 2
    nf = np.maximum(n, 1).astype(np.float32)
    large = max_exact + (np.log(nf / max_exact) / math.log(REL_MAX_DISTANCE / max_exact)
                         * (REL_BUCKETS - max_exact)).astype(np.int32)
    large = np.minimum(large, REL_BUCKETS - 1)
    return np.where(n < max_exact, n, large).astype(np.int32)


def _bias_kernel(rb_ref, bucket_ref, o_ref):
    bucket = bucket_ref[...]
    from_prev = (lax.broadcasted_iota(jnp.int32, (BLOCK, BLOCK), 1)
                 > lax.broadcasted_iota(jnp.int32, (BLOCK, BLOCK), 0))
    for head in range(N_Q_HEADS):
        acc = jnp.zeros((BLOCK, BLOCK), f32)
        for k in range(REL_BUCKETS):
            acc = jnp.where(bucket == k, rb_ref[k, head], acc)
        o_ref[1, head] = acc
        o_ref[0, head] = jnp.where(from_prev, MASK_VALUE, acc)


def _bias_tables(rel_bias):
    qi = np.arange(BLOCK)[:, None]
    kj = np.arange(BLOCK)[None, :]
    from_prev = kj > qi
    dist = np.where(from_prev, qi + BLOCK - kj, qi - kj)
    assert ((dist >= 0) & (dist < WINDOW)).all()
    return pl.pallas_call(
        _bias_kernel,
        in_specs=[
            pl.BlockSpec(memory_space=pltpu.SMEM),
            pl.BlockSpec(memory_space=pltpu.VMEM),
        ],
        out_specs=pl.BlockSpec(memory_space=pltpu.VMEM),
        out_shape=jax.ShapeDtypeStruct((2, N_Q_HEADS, BLOCK, BLOCK), f32),
        name="bias_table",
    )(rel_bias.astype(f32), jnp.asarray(_t5_bucket(dist)))


def _ssm_params(a_re, a_im, log_dt, b_re, b_im, c_re, c_im):
    dt = jnp.exp(log_dt)[:, None]
    mag = jnp.exp(a_re * dt)
    ang = a_im * dt
    lb_re, lb_im = mag * jnp.cos(ang), mag * jnp.sin(ang)
    nr, ni = lb_re - 1.0, lb_im
    den = a_re * a_re + a_im * a_im
    f_re = (nr * a_re + ni * a_im) / den
    f_im = (ni * a_re - nr * a_im) / den
    bb_re = f_re[..., None] * b_re - f_im[..., None] * b_im
    bb_im = f_re[..., None] * b_im + f_im[..., None] * b_re

    split = lambda t: t.reshape((2, LANE_CHUNKS, SLOTS, 2, 2) + t.shape[2:])
    two = np.arange(2)
    bbs = split(jnp.stack([bb_re, bb_im]))
    k = np.arange(V7X_LANES)
    k_u, k_g = k // (V7X_LANES // SLOTS), (k // SSM_GROUP) % 2
    wb_mask = ((k_u[:, None, None] == two[None, :, None]) & (k_g[:, None, None] == two[None, None, :]))
    bt = jnp.transpose(bbs, (1, 2, 3, 4, 6, 0, 5)).reshape(LANE_CHUNKS, V7X_LANES, 1, 2, 1, SSM_STATE)
    wb = (bt * wb_mask.astype(np.float32)[:, :, None, :, None]).reshape(LANE_CHUNKS, V7X_LANES, SLOT_COLS)
    cs = split(jnp.stack([c_re, -c_im]))
    r = np.arange(SLOT_COLS)
    r_s, r_g = r // (SLOT_COLS // SLOTS), (r // SSM_STATE) % 2
    wc_mask = ((two[:, None, None, None] == two[None, None, :, None])[None]
               & (r_s[:, None, None, None, None] == two[None, None, :, None, None])
               & (r_g[:, None, None, None, None] == two[None, None, None, None, :]))
    ct = jnp.transpose(cs, (1, 2, 0, 4, 6, 3, 5)).reshape(LANE_CHUNKS, SLOT_COLS, 1, 1, 2, 1, SSM_GROUP)
    wc = (ct * wc_mask.astype(np.float32)[..., None]).reshape(LANE_CHUNKS, SLOT_COLS, 2 * V7X_LANES)
    lam = jnp.transpose(split(jnp.stack([lb_re, lb_im])), (1, 3, 2, 0, 4, 5)).reshape(LANE_CHUNKS, 1, 2, SLOT_COLS)
    lam = jnp.broadcast_to(lam, (LANE_CHUNKS, BATCH, 2, SLOT_COLS)).reshape(LANE_CHUNKS, V7X_SUBLANES, SLOT_COLS)
    return wb.astype(bf16), wc.astype(bf16), lam


def _layer(x, rel_bias, ln1_g, w_in, q_norm_g, k_norm_g, attn_sinks, ssm_a_re, ssm_a_im,
           ssm_log_dt, ssm_b_re, ssm_b_im, ssm_c_re, ssm_c_im, ssm_d, w_glu,
           attn_out_g, ssm_out_g, w_out, ln2_g, w_ff_gate, w_ff_up, w_ff_down):
    row = lambda v: v.reshape(1, -1).astype(f32)

    lanes = np.arange(KV_SLAB)
    seg = jnp.asarray(lanes[:, None] // HEAD_DIM == lanes[None, :] // HEAD_DIM, bf16)
    def regroup(t, axis):
        shape = t.shape[:axis] + (N_KV_HEADS, Q_PER_KV, HEAD_DIM) + t.shape[axis + 1:]
        return jnp.swapaxes(t.reshape(shape), axis, axis + 1).reshape(t.shape)

    gqk = row(jnp.concatenate([jnp.tile(q_norm_g, N_Q_HEADS) * (HEAD_DIM ** -0.5),
                               jnp.tile(k_norm_g, N_KV_HEADS)]))
    kv_mask = jnp.asarray(np.broadcast_to(
        (np.arange(KV_WIDTH)[None, None, :] // HEAD_DIM) == np.arange(N_KV_HEADS)[:, None, None],
        (N_KV_HEADS, BLOCK, KV_WIDTH)), bf16)

    q, k, v, u = _in_proj(x, row(ln1_g), w_in.astype(bf16), seg, gqk)

    pos = np.arange(BLOCK)
    from_prev = pos[None, :] > pos[:, None]
    tri = jnp.asarray(np.stack([from_prev, ~from_prev]), bf16)
    y_attn, wg, wu, wd, wo = _attention(
        attn_sinks.astype(f32), q, k, v, _bias_tables(rel_bias), kv_mask, tri, row(regroup(attn_out_g, 0)),
        w_ff_gate.astype(f32), w_ff_up.astype(f32), w_ff_down.astype(f32), w_out.astype(f32))

    wb, wc, lam = _ssm_params(ssm_a_re.astype(f32), ssm_a_im.astype(f32), ssm_log_dt.astype(f32),
                                     ssm_b_re.astype(f32), ssm_b_im.astype(f32),
                                     ssm_c_re.astype(f32), ssm_c_im.astype(f32))
    y_ssm = _ssm(u, wb, lam, wc, row(ssm_d), w_glu.astype(bf16), row(ssm_out_g))

    x1, h2 = _out_proj(y_attn, y_ssm, x, wo, row(ln2_g))

    out = _ffn(x1.reshape(BATCH * SEQ, D_MODEL), h2.reshape(BATCH * SEQ, D_MODEL), wg, wu, wd)
    return out.reshape(BATCH, SEQ, D_MODEL)


def kernel(x, rel_bias, ln1_g, w_in, q_norm_g, k_norm_g, attn_sinks, ssm_a_re, ssm_a_im, ssm_log_dt, ssm_b_re, ssm_b_im, ssm_c_re, ssm_c_im, ssm_d, w_glu, attn_out_g, ssm_out_g, w_out, ln2_g, w_ff_gate, w_ff_up, w_ff_down):
    for l in range(ln1_g.shape[0]):
        x = _layer(x, rel_bias, ln1_g[l], w_in[l], q_norm_g[l], k_norm_g[l], attn_sinks[l],
                   ssm_a_re[l], ssm_a_im[l], ssm_log_dt[l], ssm_b_re[l], ssm_b_im[l],
                   ssm_c_re[l], ssm_c_im[l], ssm_d[l], w_glu[l], attn_out_g[l], ssm_out_g[l],
                   w_out[l], ln2_g[l], w_ff_gate[l], w_ff_up[l], w_ff_down[l])
    return x
```

```python
import functools
import math

import jax
import jax.numpy as jnp
import numpy as np
from jax import lax
from jax.experimental import pallas as pl
from jax.experimental.pallas import tpu as pltpu

D_MODEL = 2048
BATCH = 4
SEQ = 4096
HEAD_DIM = 64
N_Q_HEADS = 16
N_KV_HEADS = 4
Q_PER_KV = N_Q_HEADS // N_KV_HEADS
ATTN_WIDTH = N_Q_HEADS * HEAD_DIM
KV_WIDTH = N_KV_HEADS * HEAD_DIM
WINDOW = 128
BLOCK = 128
SSM_WIDTH = D_MODEL - ATTN_WIDTH
IN_WIDTH = ATTN_WIDTH + 2 * KV_WIDTH + SSM_WIDTH
SSM_GROUP = 16
SSM_GROUPS = SSM_WIDTH // SSM_GROUP
SSM_STATE = 64
FF_HIDDEN = 5632
REL_BUCKETS = 32
REL_MAX_DISTANCE = 128
EPS = 1e-6

V7X_LANES = 128
V7X_SUBLANES = 8
V7X_VMEM_BYTES = 64 * 1024 * 1024
VMEM_LIMIT = V7X_VMEM_BYTES - 8 * 1024 * 1024

MASK_VALUE = -1e30
KV_SLAB = Q_PER_KV * HEAD_DIM
LANE_CHUNKS = SSM_WIDTH // V7X_LANES
SLOTS = 2
SLOT_COLS = SLOTS * 2 * V7X_LANES

ATTN_BLOCKS = 4
IN_TM = 512
FFN_TM = 1024
FFN_TF = 512
SSM_T = 64
SCAN_CHUNKS = 4
SSM_ROWS = SSM_T * BATCH
SSM_ROWS2 = 2 * SSM_ROWS

f32 = jnp.float32
bf16 = jnp.bfloat16


def _dot(a, b):
    return jnp.dot(a, b, preferred_element_type=f32)


def _rms_scale(x):
    return lax.rsqrt(jnp.mean(x * x, axis=-1, keepdims=True) + EPS)


def _in_proj_kernel(x_ref, g_ref, w_ref, seg_ref, gqk_ref, q_ref, k_ref, v_ref, u_ref, proj_scr):
    xf = x_ref[...]
    h = (xf * _rms_scale(xf) * g_ref[...]).astype(bf16)
    proj_scr[...] = _dot(h, w_ref[...])

    n_slabs = (ATTN_WIDTH + KV_WIDTH) // KV_SLAB
    slab = lambda c: proj_scr[:, c * KV_SLAB:(c + 1) * KV_SLAB]
    sq = jnp.concatenate([slab(c) * slab(c) for c in range(n_slabs)], axis=0)
    hi = sq.astype(bf16)
    lo = (sq - hi.astype(f32)).astype(bf16)
    ss = _dot(jnp.concatenate([hi, lo], axis=0), seg_ref[...])
    ss = ss[:n_slabs * IN_TM] + ss[n_slabs * IN_TM:]
    normed = []
    for c in range(n_slabs):
        cols = slice(c * KV_SLAB, (c + 1) * KV_SLAB)
        scale = lax.rsqrt(ss[c * IN_TM:(c + 1) * IN_TM] * (1.0 / HEAD_DIM) + EPS)
        normed.append(slab(c) * scale * gqk_ref[:, cols])
    for g in range(Q_PER_KV):
        q_ref[:, g * KV_WIDTH:(g + 1) * KV_WIDTH] = jnp.concatenate(
            [normed[kh][:, g * HEAD_DIM:(g + 1) * HEAD_DIM] for kh in range(N_KV_HEADS)], axis=1).astype(bf16)
    k_ref[...] = normed[-1].astype(bf16)
    v_ref[...] = proj_scr[:, ATTN_WIDTH + KV_WIDTH:ATTN_WIDTH + 2 * KV_WIDTH].astype(bf16)
    u_ref[...] = proj_scr[:, ATTN_WIDTH + 2 * KV_WIDTH:]


def _const_spec(shape):
    nd = len(shape)
    return pl.BlockSpec(shape, lambda *_: (0,) * nd)


def _in_proj(x, ln1_g, w, seg, gqk):
    nt = SEQ // IN_TM
    row_spec = lambda w: pl.BlockSpec((None, IN_TM, w), lambda b, t: (b, t, 0))
    return pl.pallas_call(
        _in_proj_kernel,
        grid=(BATCH, nt),
        in_specs=[
            row_spec(D_MODEL),
            _const_spec((1, D_MODEL)),
            _const_spec((D_MODEL, IN_WIDTH)),
            _const_spec((KV_SLAB, KV_SLAB)),
            _const_spec((1, ATTN_WIDTH + KV_WIDTH)),
        ],
        out_specs=[row_spec(ATTN_WIDTH), row_spec(KV_WIDTH), row_spec(KV_WIDTH), row_spec(SSM_WIDTH)],
        out_shape=[
            jax.ShapeDtypeStruct((BATCH, SEQ, ATTN_WIDTH), bf16),
            jax.ShapeDtypeStruct((BATCH, SEQ, KV_WIDTH), bf16),
            jax.ShapeDtypeStruct((BATCH, SEQ, KV_WIDTH), bf16),
            jax.ShapeDtypeStruct((BATCH, SEQ, SSM_WIDTH), f32),
        ],
        scratch_shapes=[pltpu.VMEM((IN_TM, IN_WIDTH), f32)],
        compiler_params=pltpu.CompilerParams(
            dimension_semantics=("arbitrary", "arbitrary"), vmem_limit_bytes=VMEM_LIMIT),
        name="in_proj",
    )(x, ln1_g, w, seg, gqk)


def _attn_kernel(sink_ref, q_ref, kc_ref, kp_ref, vc_ref, vp_ref, bias_ref, kv_mask_ref, tri_ref, g_ref,
                 w0_ref, w1_ref, w2_ref, w3_ref, o_ref, c0_ref, c1_ref, c2_ref, c3_ref,
                 y_scr, s_scr, p_scr):
    for w_ref, c_ref in ((w0_ref, c0_ref), (w1_ref, c1_ref), (w2_ref, c2_ref), (w3_ref, c3_ref)):
        c_ref[...] = w_ref[...].astype(bf16)

    qi = lax.broadcasted_iota(jnp.int32, (BLOCK, BLOCK), 0)
    kj = lax.broadcasted_iota(jnp.int32, (BLOCK, BLOCK), 1)
    from_prev = kj > qi
    first_table = jnp.minimum(pl.program_id(1), 1)
    for j in range(ATTN_BLOCKS):
        rows = slice(j * BLOCK, (j + 1) * BLOCK)
        prev_rows = slice((j - 1) * BLOCK, j * BLOCK)
        k_prev = kp_ref[...] if j == 0 else kc_ref[prev_rows, :]
        v_prev = vp_ref[...] if j == 0 else vc_ref[prev_rows, :]
        keys = jnp.concatenate([k_prev, kc_ref[rows, :]], axis=0)
        vals = jnp.concatenate([v_prev, vc_ref[rows, :]], axis=0)
        q16 = jnp.concatenate(
            [q_ref[rows, g * KV_WIDTH:(g + 1) * KV_WIDTH] * kv_mask_ref[kh]
             for kh in range(N_KV_HEADS) for g in range(Q_PER_KV)], axis=0)
        s_scr[j] = lax.dot_general(q16, keys, (((1,), (1,)), ((), ())), preferred_element_type=f32)
        for head in range(N_Q_HEADS):
            head_rows = slice(head * BLOCK, (head + 1) * BLOCK)
            bias = bias_ref[first_table, head] if j == 0 else bias_ref[1, head]
            s = jnp.where(from_prev, s_scr[j, head_rows, :BLOCK], s_scr[j, head_rows, BLOCK:]) + bias
            sink = sink_ref[head]
            m = jnp.max(s, axis=-1, keepdims=True)
            p = jnp.exp(s - m)
            den = jnp.sum(p, axis=-1, keepdims=True) + jnp.exp(sink - m)
            w = (p * (1.0 / den)).astype(bf16)
            p_scr[j, head_rows, :BLOCK] = w * tri_ref[0]
            p_scr[j, head_rows, BLOCK:] = w * tri_ref[1]
        o16 = _dot(p_scr[j], vals)
        for head in range(N_Q_HEADS):
            kh, g = divmod(head, Q_PER_KV)
            y_scr[rows, g * KV_WIDTH + kh * HEAD_DIM:g * KV_WIDTH + (kh + 1) * HEAD_DIM] = (
                o16[head * BLOCK:(head + 1) * BLOCK, kh * HEAD_DIM:(kh + 1) * HEAD_DIM])
    y = y_scr[...]
    o_ref[...] = (y * _rms_scale(y) * g_ref[...]).astype(bf16)


def _attention(sinks, q, k, v, bias2, kv_mask, tri, attn_out_g, w_gate, w_up, w_down, w_out):
    rows = ATTN_BLOCKS * BLOCK
    n_blk = SEQ // rows
    steps = BATCH * n_blk
    step = lambda b, n: b * n_blk + n

    def slab(w):
        return pl.BlockSpec((w.shape[0] // steps, w.shape[1]), lambda b, n: (step(b, n), 0))

    assert w_out.shape[0] // steps == HEAD_DIM and N_Q_HEADS * 2 == steps

    def w_out_src(b, n):
        s = step(b, n)
        return (jnp.where(s < N_Q_HEADS, (s % N_KV_HEADS) * Q_PER_KV + s // N_KV_HEADS, s), 0)

    cast_in = [slab(w_gate), slab(w_up), slab(w_down), pl.BlockSpec((HEAD_DIM, D_MODEL), w_out_src)]
    cast_out = [slab(w_gate), slab(w_up), slab(w_down), slab(w_out)]
    cur = pl.BlockSpec((None, rows, ATTN_WIDTH), lambda b, n: (b, n, 0))
    kv_cur = pl.BlockSpec((None, rows, KV_WIDTH), lambda b, n: (b, n, 0))
    kv_prev = pl.BlockSpec((None, BLOCK, KV_WIDTH), lambda b, n: (b, jnp.maximum(n * ATTN_BLOCKS - 1, 0), 0))
    return pl.pallas_call(
        _attn_kernel,
        grid=(BATCH, SEQ // rows),
        in_specs=[
            pl.BlockSpec(memory_space=pltpu.SMEM),
            cur, kv_cur, kv_prev, kv_cur, kv_prev,
            _const_spec((2, N_Q_HEADS, BLOCK, BLOCK)),
            _const_spec((N_KV_HEADS, BLOCK, KV_WIDTH)),
            _const_spec((2, BLOCK, BLOCK)),
            _const_spec((1, ATTN_WIDTH)),
        ] + cast_in,
        out_specs=[cur] + cast_out,
        out_shape=[jax.ShapeDtypeStruct((BATCH, SEQ, ATTN_WIDTH), bf16)]
        + [jax.ShapeDtypeStruct(w.shape, bf16) for w in (w_gate, w_up, w_down, w_out)],
        scratch_shapes=[
            pltpu.VMEM((rows, ATTN_WIDTH), f32),
            pltpu.VMEM((ATTN_BLOCKS, N_Q_HEADS * BLOCK, 2 * BLOCK), f32),
            pltpu.VMEM((ATTN_BLOCKS, N_Q_HEADS * BLOCK, 2 * BLOCK), bf16),
        ],
        compiler_params=pltpu.CompilerParams(
            dimension_semantics=("arbitrary", "arbitrary"), vmem_limit_bytes=VMEM_LIMIT),
        name="attn",
    )(sinks, q, k, k, v, v, bias2, kv_mask, tri, attn_out_g, w_gate, w_up, w_down, w_out)


def _gelu_tanh(x):
    return 0.5 * x * (1.0 + jnp.tanh(math.sqrt(2.0 / math.pi) * (x + 0.044715 * (x * x * x))))


def _sigmoid(x):
    return 1.0 / (1.0 + jnp.exp(-x))


def _ssm_kernel(u_ref, uprev_ref, wb_ref, lam_ref, wc_ref, d_ref, wglu_ref, g_ref,
                o_ref, u2_scr, bu_scr, h_scr, y2_scr, y_scr, st_scr):
    step = pl.program_id(0)

    @pl.when(step == 0)
    def _():
        st_scr[...] = jnp.zeros_like(st_scr)
        h_scr[1] = jnp.zeros(h_scr.shape[1:], f32)

    for parity in range(2):
        pl.when(step % 2 == parity)(functools.partial(
            _ssm_step, u_ref, uprev_ref, wb_ref, lam_ref, wc_ref, d_ref, wglu_ref, g_ref, o_ref,
            u2_scr, bu_scr, h_scr.at[parity], h_scr.at[1 - parity], y2_scr, y_scr, st_scr))


def _ssm_step(u_ref, uprev_ref, wb_ref, lam_ref, wc_ref, d_ref, wglu_ref, g_ref, o_ref,
              u2_scr, bu_scr, h_new, h_old, y2_scr, y_scr, st_scr):
    for c in range(LANE_CHUNKS):
        for b in range(BATCH):
            for par in range(2):
                u2_scr.at[c][pl.ds(2 * b + par, SSM_T, stride=V7X_SUBLANES), :] = (
                    u_ref[b, :, c * V7X_LANES:(c + 1) * V7X_LANES])
    row_par = lax.broadcasted_iota(jnp.int32, (SSM_ROWS2, V7X_LANES), 0) % 2
    pair_par = (lax.broadcasted_iota(jnp.int32, (SSM_ROWS2, V7X_LANES), 1) // (2 * SSM_GROUP)) % 2
    own_pair = row_par == pair_par

    for c in range(LANE_CHUNKS):
        bu_scr[c] = _dot(jnp.where(own_pair, u2_scr[c], 0.0).astype(bf16), wb_ref[c])

    half = SCAN_CHUNKS
    for c0 in range(0, LANE_CHUNKS, half):
        chains = [(c, s) for c in range(c0, c0 + half) for s in range(SLOTS)]

        def cols(s):
            base = s * 2 * V7X_LANES
            return slice(base, base + V7X_LANES), slice(base + V7X_LANES, base + 2 * V7X_LANES)

        lams = [(lam_ref[c, :, cols(s)[0]], lam_ref[c, :, cols(s)[1]]) for c, s in chains]
        init = tuple((st_scr[c, :, cols(s)[0]], st_scr[c, :, cols(s)[1]]) for c, s in chains)

        def body(t, carry, chains=chains, lams=lams, cols=cols, h_new=h_new):
            rows = pl.ds(t * V7X_SUBLANES, V7X_SUBLANES)
            out = []
            for (c, s), (l_re, l_im), (s_re, s_im) in zip(chains, lams, carry):
                cr, ci = cols(s)
                n_re = (l_re * s_re - l_im * s_im) + bu_scr[c, rows, cr]
                n_im = (l_re * s_im + l_im * s_re) + bu_scr[c, rows, ci]
                h_new[c, rows, cr] = n_re
                h_new[c, rows, ci] = n_im
                out.append((n_re, n_im))
            return tuple(out)

        fin = init
        for t in range(SSM_T):
            fin = body(t, fin)
        for (c, s), (s_re, s_im) in zip(chains, fin):
            st_scr[c, :, cols(s)[0]] = s_re
            st_scr[c, :, cols(s)[1]] = s_im

    for c in range(LANE_CHUNKS):
        cols_c = slice(c * V7X_LANES, (c + 1) * V7X_LANES)
        y2 = _dot(h_old[c].astype(bf16), wc_ref[c])
        y2_scr[2 * c] = y2[:, :V7X_LANES]
        y2_scr[2 * c + 1] = y2[:, V7X_LANES:]
        for b in range(BATCH):
            yc = (y2_scr.at[2 * c][pl.ds(2 * b, SSM_T, stride=V7X_SUBLANES), :]
                  + y2_scr.at[2 * c + 1][pl.ds(2 * b + 1, SSM_T, stride=V7X_SUBLANES), :])
            yc = yc + d_ref[:, cols_c] * uprev_ref[b, :, cols_c]
            y_scr[b * SSM_T:(b + 1) * SSM_T, cols_c] = _gelu_tanh(yc)
    y = y_scr[...]
    out = y * _sigmoid(_dot(y.astype(bf16), wglu_ref[...]))
    out = (out * _rms_scale(out) * g_ref[...]).astype(bf16)
    o_ref[...] = out.reshape(BATCH, SSM_T, SSM_WIDTH)


def _ssm(u, wb, lam, wc, d, wglu, ssm_out_g):
    n_chunks = SEQ // SSM_T
    return pl.pallas_call(
        _ssm_kernel,
        grid=(n_chunks + 1,),
        in_specs=[
            pl.BlockSpec((BATCH, SSM_T, SSM_WIDTH), lambda i: (0, jnp.minimum(i, n_chunks - 1), 0)),
            pl.BlockSpec((BATCH, SSM_T, SSM_WIDTH), lambda i: (0, jnp.maximum(i - 1, 0), 0)),
            _const_spec((LANE_CHUNKS, V7X_LANES, SLOT_COLS)),
            _const_spec((LANE_CHUNKS, V7X_SUBLANES, SLOT_COLS)),
            _const_spec((LANE_CHUNKS, SLOT_COLS, 2 * V7X_LANES)),
            _const_spec((1, SSM_WIDTH)),
            _const_spec((SSM_WIDTH, SSM_WIDTH)),
            _const_spec((1, SSM_WIDTH)),
        ],
        out_specs=pl.BlockSpec((BATCH, SSM_T, SSM_WIDTH), lambda i: (0, jnp.maximum(i - 1, 0), 0)),
        out_shape=jax.ShapeDtypeStruct((BATCH, SEQ, SSM_WIDTH), bf16),
        scratch_shapes=[
            pltpu.VMEM((LANE_CHUNKS, SSM_ROWS2, V7X_LANES), f32),
            pltpu.VMEM((LANE_CHUNKS, SSM_ROWS2, SLOT_COLS), f32),
            pltpu.VMEM((2, LANE_CHUNKS, SSM_ROWS2, SLOT_COLS), f32),
            pltpu.VMEM((2 * LANE_CHUNKS, SSM_ROWS2, V7X_LANES), f32),
            pltpu.VMEM((SSM_ROWS, SSM_WIDTH), f32),
            pltpu.VMEM((LANE_CHUNKS, V7X_SUBLANES, SLOT_COLS), f32),
        ],
        compiler_params=pltpu.CompilerParams(
            dimension_semantics=("arbitrary",), vmem_limit_bytes=VMEM_LIMIT),
        name="ssm",
    )(u, u, wb, lam, wc, d, wglu, ssm_out_g)


def _out_proj_kernel(a_ref, s_ref, x_ref, wa_ref, ws_ref, g_ref, o_ref, h_ref):
    x1 = x_ref[...] + _dot(a_ref[...], wa_ref[...]) + _dot(s_ref[...], ws_ref[...])
    o_ref[...] = x1
    h_ref[...] = (x1 * _rms_scale(x1) * g_ref[...]).astype(bf16)


def _out_proj(a, s, x, w, ln2_g):
    nt = SEQ // IN_TM
    row_spec = lambda w: pl.BlockSpec((None, IN_TM, w), lambda b, t: (b, t, 0))
    return pl.pallas_call(
        _out_proj_kernel,
        grid=(BATCH, nt),
        in_specs=[
            row_spec(ATTN_WIDTH),
            row_spec(SSM_WIDTH),
            row_spec(D_MODEL),
            pl.BlockSpec((ATTN_WIDTH, D_MODEL), lambda b, t: (0, 0)),
            pl.BlockSpec((SSM_WIDTH, D_MODEL), lambda b, t: (1, 0)),
            _const_spec((1, D_MODEL)),
        ],
        out_specs=[row_spec(D_MODEL), row_spec(D_MODEL)],
        out_shape=[jax.ShapeDtypeStruct((BATCH, SEQ, D_MODEL), f32),
                   jax.ShapeDtypeStruct((BATCH, SEQ, D_MODEL), bf16)],
        compiler_params=pltpu.CompilerParams(
            dimension_semantics=("arbitrary", "arbitrary"), vmem_limit_bytes=VMEM_LIMIT),
        name="out_proj",
    )(a, s, x, w, w, ln2_g)


def _ffn_kernel(x_hbm, h_ref, wg_ref, wu_ref, wd_ref, o_ref, sem):
    first = pl.program_id(1) == 0
    rows = pl.ds(pl.multiple_of(pl.program_id(0) * FFN_TM, FFN_TM), FFN_TM)
    residual_copy = pltpu.make_async_copy(x_hbm.at[rows, :], o_ref, sem)

    @pl.when(first)
    def _():
        residual_copy.start(priority=1)

    h = h_ref[...]
    gate = _dot(h, wg_ref[...])
    up = _dot(h, wu_ref[...])
    act = (gate * _sigmoid(gate) * up).astype(bf16)

    @pl.when(first)
    def _():
        residual_copy.wait()

    o_ref[...] += _dot(act, wd_ref[...])


def _ffn(x, h, wg, wu, wd):
    n_rows = BATCH * SEQ
    return pl.pallas_call(
        _ffn_kernel,
        grid=(n_rows // FFN_TM, FF_HIDDEN // FFN_TF),
        in_specs=[
            pl.BlockSpec(memory_space=pl.ANY),
            pl.BlockSpec((FFN_TM, D_MODEL), lambda i, f: (i, 0)),
            pl.BlockSpec((D_MODEL, FFN_TF), lambda i, f: (0, f)),
            pl.BlockSpec((D_MODEL, FFN_TF), lambda i, f: (0, f)),
            pl.BlockSpec((FFN_TF, D_MODEL), lambda i, f: (f, 0)),
        ],
        out_specs=pl.BlockSpec((FFN_TM, D_MODEL), lambda i, f: (i, 0)),
        out_shape=jax.ShapeDtypeStruct((n_rows, D_MODEL), f32),
        scratch_shapes=[pltpu.SemaphoreType.DMA(())],
        compiler_params=pltpu.CompilerParams(
            dimension_semantics=("arbitrary", "arbitrary"), vmem_limit_bytes=VMEM_LIMIT),
        name="ffn",
    )(x, h, wg, wu, wd)


def _t5_bucket(dist):
    n = np.maximum(dist, 0)
    max_exact = REL_BUCKETS // 2
    nf = np.maximum(n, 1).astype(np.float32)
    large = max_exact + (np.log(nf / max_exact) / math.log(REL_MAX_DISTANCE / max_exact)
                         * (REL_BUCKETS - max_exact)).astype(np.int32)
    large = np.minimum(large, REL_BUCKETS - 1)
    return np.where(n < max_exact, n, large).astype(np.int32)


def _bias_kernel(rb_ref, bucket_ref, o_ref):
    bucket = bucket_ref[...]
    from_prev = (lax.broadcasted_iota(jnp.int32, (BLOCK, BLOCK), 1)
                 > lax.broadcasted_iota(jnp.int32, (BLOCK, BLOCK), 0))
    for head in range(N_Q_HEADS):
        acc = jnp.zeros((BLOCK, BLOCK), f32)
        for k in range(REL_BUCKETS):
            acc = jnp.where(bucket == k, rb_ref[k, head], acc)
        o_ref[1, head] = acc
        o_ref[0, head] = jnp.where(from_prev, MASK_VALUE, acc)


def _bias_tables(rel_bias):
    qi = np.arange(BLOCK)[:, None]
    kj = np.arange(BLOCK)[None, :]
    from_prev = kj > qi
    dist = np.where(from_prev, qi + BLOCK - kj, qi - kj)
    assert ((dist >= 0) & (dist < WINDOW)).all()
    return pl.pallas_call(
        _bias_kernel,
        in_specs=[
            pl.BlockSpec(memory_space=pltpu.SMEM),
            pl.BlockSpec(memory_space=pltpu.VMEM),
        ],
        out_specs=pl.BlockSpec(memory_space=pltpu.VMEM),
        out_shape=jax.ShapeDtypeStruct((2, N_Q_HEADS, BLOCK, BLOCK), f32),
        name="bias_table",
    )(rel_bias.astype(f32), jnp.asarray(_t5_bucket(dist)))


def _ssm_params(a_re, a_im, log_dt, b_re, b_im, c_re, c_im):
    dt = jnp.exp(log_dt)[:, None]
    mag = jnp.exp(a_re * dt)
    ang = a_im * dt
    lb_re, lb_im = mag * jnp.cos(ang), mag * jnp.sin(ang)
    nr, ni = lb_re - 1.0, lb_im
    den = a_re * a_re + a_im * a_im
    f_re = (nr * a_re + ni * a_im) / den
    f_im = (ni * a_re - nr * a_im) / den
    bb_re = f_re[..., None] * b_re - f_im[..., None] * b_im
    bb_im = f_re[..., None] * b_im + f_im[..., None] * b_re

    split = lambda t: t.reshape((2, LANE_CHUNKS, SLOTS, 2, 2) + t.shape[2:])
    two = np.arange(2)
    bbs = split(jnp.stack([bb_re, bb_im]))
    k = np.arange(V7X_LANES)
    k_u, k_g = k // (V7X_LANES // SLOTS), (k // SSM_GROUP) % 2
    wb_mask = ((k_u[:, None, None] == two[None, :, None]) & (k_g[:, None, None] == two[None, None, :]))
    bt = jnp.transpose(bbs, (1, 2, 3, 4, 6, 0, 5)).reshape(LANE_CHUNKS, V7X_LANES, 1, 2, 1, SSM_STATE)
    wb = (bt * wb_mask.astype(np.float32)[:, :, None, :, None]).reshape(LANE_CHUNKS, V7X_LANES, SLOT_COLS)
    cs = split(jnp.stack([c_re, -c_im]))
    r = np.arange(SLOT_COLS)
    r_s, r_g = r // (SLOT_COLS // SLOTS), (r // SSM_STATE) % 2
    wc_mask = ((two[:, None, None, None] == two[None, None, :, None])[None]
               & (r_s[:, None, None, None, None] == two[None, None, :, None, None])
               & (r_g[:, None, None, None, None] == two[None, None, None, None, :]))
    ct = jnp.transpose(cs, (1, 2, 0, 4, 6, 3, 5)).reshape(LANE_CHUNKS, SLOT_COLS, 1, 1, 2, 1, SSM_GROUP)
    wc = (ct * wc_mask.astype(np.float32)[..., None]).reshape(LANE_CHUNKS, SLOT_COLS, 2 * V7X_LANES)
    lam = jnp.transpose(split(jnp.stack([lb_re, lb_im])), (1, 3, 2, 0, 4, 5)).reshape(LANE_CHUNKS, 1, 2, SLOT_COLS)
    lam = jnp.broadcast_to(lam, (LANE_CHUNKS, BATCH, 2, SLOT_COLS)).reshape(LANE_CHUNKS, V7X_SUBLANES, SLOT_COLS)
    return wb.astype(bf16), wc.astype(bf16), lam


def _layer(x, rel_bias, ln1_g, w_in, q_norm_g, k_norm_g, attn_sinks, ssm_a_re, ssm_a_im,
           ssm_log_dt, ssm_b_re, ssm_b_im, ssm_c_re, ssm_c_im, ssm_d, w_glu,
           attn_out_g, ssm_out_g, w_out, ln2_g, w_ff_gate, w_ff_up, w_ff_down):
    row = lambda v: v.reshape(1, -1).astype(f32)

    lanes = np.arange(KV_SLAB)
    seg = jnp.asarray(lanes[:, None] // HEAD_DIM == lanes[None, :] // HEAD_DIM, bf16)
    def regroup(t, axis):
        shape = t.shape[:axis] + (N_KV_HEADS, Q_PER_KV, HEAD_DIM) + t.shape[axis + 1:]
        return jnp.swapaxes(t.reshape(shape), axis, axis + 1).reshape(t.shape)

    gqk = row(jnp.concatenate([jnp.tile(q_norm_g, N_Q_HEADS) * (HEAD_DIM ** -0.5),
                               jnp.tile(k_norm_g, N_KV_HEADS)]))
    kv_mask = jnp.asarray(np.broadcast_to(
        (np.arange(KV_WIDTH)[None, None, :] // HEAD_DIM) == np.arange(N_KV_HEADS)[:, None, None],
        (N_KV_HEADS, BLOCK, KV_WIDTH)), bf16)

    q, k, v, u = _in_proj(x, row(ln1_g), w_in.astype(bf16), seg, gqk)

    pos = np.arange(BLOCK)
    from_prev = pos[None, :] > pos[:, None]
    tri = jnp.asarray(np.stack([from_prev, ~from_prev]), bf16)
    y_attn, wg, wu, wd, wo = _attention(
        attn_sinks.astype(f32), q, k, v, _bias_tables(rel_bias), kv_mask, tri, row(regroup(attn_out_g, 0)),
        w_ff_gate.astype(f32), w_ff_up.astype(f32), w_ff_down.astype(f32), w_out.astype(f32))

    wb, wc, lam = _ssm_params(ssm_a_re.astype(f32), ssm_a_im.astype(f32), ssm_log_dt.astype(f32),
                                     ssm_b_re.astype(f32), ssm_b_im.astype(f32),
                                     ssm_c_re.astype(f32), ssm_c_im.astype(f32))
    y_ssm = _ssm(u, wb, lam, wc, row(ssm_d), w_glu.astype(bf16), row(ssm_out_g))

    x1, h2 = _out_proj(y_attn, y_ssm, x, wo, row(ln2_g))

    out = _ffn(x1.reshape(BATCH * SEQ, D_MODEL), h2.reshape(BATCH * SEQ, D_MODEL), wg, wu, wd)
    return out.reshape(BATCH, SEQ, D_MODEL)


def kernel(x, rel_bias, ln1_g, w_in, q_norm_g, k_norm_g, attn_sinks, ssm_a_re, ssm_a_im, ssm_log_dt, ssm_b_re, ssm_b_im, ssm_c_re, ssm_c_im, ssm_d, w_glu, attn_out_g, ssm_out_g, w_out, ln2_g, w_ff_gate, w_ff_up, w_ff_down):
    for l in range(ln1_g.shape[0]):
        x = _layer(x, rel_bias, ln1_g[l], w_in[l], q_norm_g[l], k_norm_g[l], attn_sinks[l],
                   ssm_a_re[l], ssm_a_im[l], ssm_log_dt[l], ssm_b_re[l], ssm_b_im[l],
                   ssm_c_re[l], ssm_c_im[l], ssm_d[l], w_glu[l], attn_out_g[l], ssm_out_g[l],
                   w_out[l], ln2_g[l], w_ff_gate[l], w_ff_up[l], w_ff_down[l])
    return x
```

```python
import functools
import math

import jax
import jax.numpy as jnp
import numpy as np
from jax import lax
from jax.experimental import pallas as pl
from jax.experimental.pallas import tpu as pltpu

D_MODEL = 2048
BATCH = 4
SEQ = 4096
HEAD_DIM = 64
N_Q_HEADS = 16
N_KV_HEADS = 4
Q_PER_KV = N_Q_HEADS // N_KV_HEADS
ATTN_WIDTH = N_Q_HEADS * HEAD_DIM
KV_WIDTH = N_KV_HEADS * HEAD_DIM
WINDOW = 128
BLOCK = 128
SSM_WIDTH = D_MODEL - ATTN_WIDTH
IN_WIDTH = ATTN_WIDTH + 2 * KV_WIDTH + SSM_WIDTH
SSM_GROUP = 16
SSM_GROUPS = SSM_WIDTH // SSM_GROUP
SSM_STATE = 64
FF_HIDDEN = 5632
REL_BUCKETS = 32
REL_MAX_DISTANCE = 128
EPS = 1e-6

V7X_LANES = 128
V7X_SUBLANES = 8
V7X_VMEM_BYTES = 64 * 1024 * 1024
VMEM_LIMIT = V7X_VMEM_BYTES - 8 * 1024 * 1024

MASK_VALUE = -1e30
KV_SLAB = Q_PER_KV * HEAD_DIM
LANE_CHUNKS = SSM_WIDTH // V7X_LANES
SLOTS = 2
SLOT_COLS = SLOTS * 2 * V7X_LANES

ATTN_BLOCKS = 4
IN_TM = 512
FFN_TM = 1024
FFN_TF = 512
SSM_T = 64
SSM_ROWS = SSM_T * BATCH
SSM_ROWS2 = 2 * SSM_ROWS

f32 = jnp.float32
bf16 = jnp.bfloat16


def _dot(a, b):
    return jnp.dot(a, b, preferred_element_type=f32)


def _rms_scale(x):
    return lax.rsqrt(jnp.mean(x * x, axis=-1, keepdims=True) + EPS)


def _in_proj_kernel(x_ref, g_ref, w_ref, gqk_ref, q_ref, k_ref, v_ref, u_ref, proj_scr):
    xf = x_ref[...]
    h = (xf * _rms_scale(xf) * g_ref[...]).astype(bf16)
    proj_scr[...] = _dot(h, w_ref[...])

    low = lax.broadcasted_iota(jnp.int32, (IN_TM, V7X_LANES), 1) < HEAD_DIM

    def head_norm(b):
        cols = slice(b * V7X_LANES, (b + 1) * V7X_LANES)
        xb = proj_scr[:, cols]
        sq = xb * xb
        r_lo = lax.rsqrt(jnp.sum(jnp.where(low, sq, 0.0), axis=-1, keepdims=True) * (1.0 / HEAD_DIM) + EPS)
        r_hi = lax.rsqrt(jnp.sum(jnp.where(low, 0.0, sq), axis=-1, keepdims=True) * (1.0 / HEAD_DIM) + EPS)
        return xb * jnp.where(low, r_lo, r_hi) * gqk_ref[:, cols]

    blocks_per_kv = KV_SLAB // V7X_LANES
    qn = [head_norm(b) for b in range(ATTN_WIDTH // V7X_LANES)]
    for g in range(Q_PER_KV):
        half = slice((g % 2) * HEAD_DIM, (g % 2 + 1) * HEAD_DIM)
        q_ref[:, g * KV_WIDTH:(g + 1) * KV_WIDTH] = jnp.concatenate(
            [qn[kh * blocks_per_kv + g // 2][:, half] for kh in range(N_KV_HEADS)], axis=1).astype(bf16)
    for b in range(KV_WIDTH // V7X_LANES):
        k_ref[:, b * V7X_LANES:(b + 1) * V7X_LANES] = head_norm(ATTN_WIDTH // V7X_LANES + b).astype(bf16)
    v_ref[...] = proj_scr[:, ATTN_WIDTH + KV_WIDTH:ATTN_WIDTH + 2 * KV_WIDTH].astype(bf16)
    u_ref[...] = proj_scr[:, ATTN_WIDTH + 2 * KV_WIDTH:]


def _const_spec(shape):
    nd = len(shape)
    return pl.BlockSpec(shape, lambda *_: (0,) * nd)


def _in_proj(x, ln1_g, w, gqk):
    nt = SEQ // IN_TM
    row_spec = lambda w: pl.BlockSpec((None, IN_TM, w), lambda b, t: (b, t, 0))
    return pl.pallas_call(
        _in_proj_kernel,
        grid=(BATCH, nt),
        in_specs=[
            row_spec(D_MODEL),
            _const_spec((1, D_MODEL)),
            _const_spec((D_MODEL, IN_WIDTH)),
            _const_spec((1, ATTN_WIDTH + KV_WIDTH)),
        ],
        out_specs=[row_spec(ATTN_WIDTH), row_spec(KV_WIDTH), row_spec(KV_WIDTH), row_spec(SSM_WIDTH)],
        out_shape=[
            jax.ShapeDtypeStruct((BATCH, SEQ, ATTN_WIDTH), bf16),
            jax.ShapeDtypeStruct((BATCH, SEQ, KV_WIDTH), bf16),
            jax.ShapeDtypeStruct((BATCH, SEQ, KV_WIDTH), bf16),
            jax.ShapeDtypeStruct((BATCH, SEQ, SSM_WIDTH), f32),
        ],
        scratch_shapes=[pltpu.VMEM((IN_TM, IN_WIDTH), f32)],
        compiler_params=pltpu.CompilerParams(
            dimension_semantics=("arbitrary", "arbitrary"), vmem_limit_bytes=VMEM_LIMIT),
        name="in_proj",
    )(x, ln1_g, w, gqk)


def _attn_kernel(sink_ref, q_ref, kc_ref, kp_ref, vc_ref, vp_ref, bias_ref, kv_mask_ref, tri_ref, g_ref,
                 w0_ref, w1_ref, w2_ref, w3_ref, o_ref, c0_ref, c1_ref, c2_ref, c3_ref,
                 y_scr, s_scr, p_scr):
    for w_ref, c_ref in ((w0_ref, c0_ref), (w1_ref, c1_ref), (w2_ref, c2_ref), (w3_ref, c3_ref)):
        c_ref[...] = w_ref[...].astype(bf16)

    qi = lax.broadcasted_iota(jnp.int32, (BLOCK, BLOCK), 0)
    kj = lax.broadcasted_iota(jnp.int32, (BLOCK, BLOCK), 1)
    from_prev = kj > qi
    first_table = jnp.minimum(pl.program_id(1), 1)
    for j in range(ATTN_BLOCKS):
        rows = slice(j * BLOCK, (j + 1) * BLOCK)
        prev_rows = slice((j - 1) * BLOCK, j * BLOCK)
        k_prev = kp_ref[...] if j == 0 else kc_ref[prev_rows, :]
        v_prev = vp_ref[...] if j == 0 else vc_ref[prev_rows, :]
        keys = jnp.concatenate([k_prev, kc_ref[rows, :]], axis=0)
        vals = jnp.concatenate([v_prev, vc_ref[rows, :]], axis=0)
        q16 = jnp.concatenate(
            [q_ref[rows, g * KV_WIDTH:(g + 1) * KV_WIDTH] * kv_mask_ref[kh]
             for kh in range(N_KV_HEADS) for g in range(Q_PER_KV)], axis=0)
        s_scr[j] = lax.dot_general(q16, keys, (((1,), (1,)), ((), ())), preferred_element_type=f32)
        for head in range(N_Q_HEADS):
            head_rows = slice(head * BLOCK, (head + 1) * BLOCK)
            bias = bias_ref[first_table, head] if j == 0 else bias_ref[1, head]
            s = jnp.where(from_prev, s_scr[j, head_rows, :BLOCK], s_scr[j, head_rows, BLOCK:]) + bias
            sink = sink_ref[head]
            m = jnp.max(s, axis=-1, keepdims=True)
            p = jnp.exp(s - m)
            den = jnp.sum(p, axis=-1, keepdims=True) + jnp.exp(sink - m)
            w = (p * (1.0 / den)).astype(bf16)
            p_scr[j, head_rows, :BLOCK] = w * tri_ref[0]
            p_scr[j, head_rows, BLOCK:] = w * tri_ref[1]
        o16 = _dot(p_scr[j], vals)
        for head in range(N_Q_HEADS):
            kh, g = divmod(head, Q_PER_KV)
            y_scr[rows, g * KV_WIDTH + kh * HEAD_DIM:g * KV_WIDTH + (kh + 1) * HEAD_DIM] = (
                o16[head * BLOCK:(head + 1) * BLOCK, kh * HEAD_DIM:(kh + 1) * HEAD_DIM])
    y = y_scr[...]
    o_ref[...] = (y * _rms_scale(y) * g_ref[...]).astype(bf16)


def _attention(sinks, q, k, v, bias2, kv_mask, tri, attn_out_g, w_gate, w_up, w_down, w_out):
    rows = ATTN_BLOCKS * BLOCK
    n_blk = SEQ // rows
    steps = BATCH * n_blk
    step = lambda b, n: b * n_blk + n

    def slab(w):
        return pl.BlockSpec((w.shape[0] // steps, w.shape[1]), lambda b, n: (step(b, n), 0))

    assert w_out.shape[0] // steps == HEAD_DIM and N_Q_HEADS * 2 == steps

    def w_out_src(b, n):
        s = step(b, n)
        return (jnp.where(s < N_Q_HEADS, (s % N_KV_HEADS) * Q_PER_KV + s // N_KV_HEADS, s), 0)

    cast_in = [slab(w_gate), slab(w_up), slab(w_down), pl.BlockSpec((HEAD_DIM, D_MODEL), w_out_src)]
    cast_out = [slab(w_gate), slab(w_up), slab(w_down), slab(w_out)]
    cur = pl.BlockSpec((None, rows, ATTN_WIDTH), lambda b, n: (b, n, 0))
    kv_cur = pl.BlockSpec((None, rows, KV_WIDTH), lambda b, n: (b, n, 0))
    kv_prev = pl.BlockSpec((None, BLOCK, KV_WIDTH), lambda b, n: (b, jnp.maximum(n * ATTN_BLOCKS - 1, 0), 0))
    return pl.pallas_call(
        _attn_kernel,
        grid=(BATCH, SEQ // rows),
        in_specs=[
            pl.BlockSpec(memory_space=pltpu.SMEM),
            cur, kv_cur, kv_prev, kv_cur, kv_prev,
            _const_spec((2, N_Q_HEADS, BLOCK, BLOCK)),
            _const_spec((N_KV_HEADS, BLOCK, KV_WIDTH)),
            _const_spec((2, BLOCK, BLOCK)),
            _const_spec((1, ATTN_WIDTH)),
        ] + cast_in,
        out_specs=[cur] + cast_out,
        out_shape=[jax.ShapeDtypeStruct((BATCH, SEQ, ATTN_WIDTH), bf16)]
        + [jax.ShapeDtypeStruct(w.shape, bf16) for w in (w_gate, w_up, w_down, w_out)],
        scratch_shapes=[
            pltpu.VMEM((rows, ATTN_WIDTH), f32),
            pltpu.VMEM((ATTN_BLOCKS, N_Q_HEADS * BLOCK, 2 * BLOCK), f32),
            pltpu.VMEM((ATTN_BLOCKS, N_Q_HEADS * BLOCK, 2 * BLOCK), bf16),
        ],
        compiler_params=pltpu.CompilerParams(
            dimension_semantics=("arbitrary", "arbitrary"), vmem_limit_bytes=VMEM_LIMIT),
        name="attn",
    )(sinks, q, k, k, v, v, bias2, kv_mask, tri, attn_out_g, w_gate, w_up, w_down, w_out)


def _gelu_tanh(x):
    return 0.5 * x * (1.0 + jnp.tanh(math.sqrt(2.0 / math.pi) * (x + 0.044715 * (x * x * x))))


def _sigmoid(x):
    return 1.0 / (1.0 + jnp.exp(-x))


def _ssm_kernel(u_ref, uprev_ref, wb_ref, lam_ref, wc_ref, d_ref, wglu_ref, g_ref,
                o_ref, u2_scr, bu_scr, h_scr, y2_scr, y_scr, st_scr):
    step = pl.program_id(0)

    @pl.when(step == 0)
    def _():
        st_scr[...] = jnp.zeros_like(st_scr)
        h_scr[1] = jnp.zeros(h_scr.shape[1:], f32)

    for parity in range(2):
        pl.when(step % 2 == parity)(functools.partial(
            _ssm_step, u_ref, uprev_ref, wb_ref, lam_ref, wc_ref, d_ref, wglu_ref, g_ref, o_ref,
            u2_scr, bu_scr, h_scr.at[parity], h_scr.at[1 - parity], y2_scr, y_scr, st_scr))


def _ssm_step(u_ref, uprev_ref, wb_ref, lam_ref, wc_ref, d_ref, wglu_ref, g_ref, o_ref,
              u2_scr, bu_scr, h_new, h_old, y2_scr, y_scr, st_scr):
    for c in range(LANE_CHUNKS):
        for b in range(BATCH):
            for par in range(2):
                u2_scr.at[c][pl.ds(2 * b + par, SSM_T, stride=V7X_SUBLANES), :] = (
                    u_ref[b, :, c * V7X_LANES:(c + 1) * V7X_LANES])
    row_par = lax.broadcasted_iota(jnp.int32, (SSM_ROWS2, V7X_LANES), 0) % 2
    pair_par = (lax.broadcasted_iota(jnp.int32, (SSM_ROWS2, V7X_LANES), 1) // (2 * SSM_GROUP)) % 2
    own_pair = row_par == pair_par

    for c in range(LANE_CHUNKS):
        bu_scr[c] = _dot(jnp.where(own_pair, u2_scr[c], 0.0).astype(bf16), wb_ref[c])

    half = LANE_CHUNKS // 2
    for c0 in (0, half):
        chains = [(c, s) for c in range(c0, c0 + half) for s in range(SLOTS)]

        def cols(s):
            base = s * 2 * V7X_LANES
            return slice(base, base + V7X_LANES), slice(base + V7X_LANES, base + 2 * V7X_LANES)

        lams = [(lam_ref[c, :, cols(s)[0]], lam_ref[c, :, cols(s)[1]]) for c, s in chains]
        init = tuple((st_scr[c, :, cols(s)[0]], st_scr[c, :, cols(s)[1]]) for c, s in chains)

        def body(t, carry, chains=chains, lams=lams, cols=cols, h_new=h_new):
            rows = pl.ds(t * V7X_SUBLANES, V7X_SUBLANES)
            out = []
            for (c, s), (l_re, l_im), (s_re, s_im) in zip(chains, lams, carry):
                cr, ci = cols(s)
                n_re = (l_re * s_re - l_im * s_im) + bu_scr[c, rows, cr]
                n_im = (l_re * s_im + l_im * s_re) + bu_scr[c, rows, ci]
                h_new[c, rows, cr] = n_re
                h_new[c, rows, ci] = n_im
                out.append((n_re, n_im))
            return tuple(out)

        fin = init
        for t in range(SSM_T):
            fin = body(t, fin)
        for (c, s), (s_re, s_im) in zip(chains, fin):
            st_scr[c, :, cols(s)[0]] = s_re
            st_scr[c, :, cols(s)[1]] = s_im

    for c in range(LANE_CHUNKS):
        cols_c = slice(c * V7X_LANES, (c + 1) * V7X_LANES)
        y2 = _dot(h_old[c].astype(bf16), wc_ref[c])
        y2_scr[2 * c] = y2[:, :V7X_LANES]
        y2_scr[2 * c + 1] = y2[:, V7X_LANES:]
        for b in range(BATCH):
            yc = (y2_scr.at[2 * c][pl.ds(2 * b, SSM_T, stride=V7X_SUBLANES), :]
                  + y2_scr.at[2 * c + 1][pl.ds(2 * b + 1, SSM_T, stride=V7X_SUBLANES), :])
            yc = yc + d_ref[:, cols_c] * uprev_ref[b, :, cols_c]
            y_scr[b * SSM_T:(b + 1) * SSM_T, cols_c] = _gelu_tanh(yc)
    y = y_scr[...]
    out = y * _sigmoid(_dot(y.astype(bf16), wglu_ref[...]))
    out = (out * _rms_scale(out) * g_ref[...]).astype(bf16)
    o_ref[...] = out.reshape(BATCH, SSM_T, SSM_WIDTH)


def _ssm(u, wb, lam, wc, d, wglu, ssm_out_g):
    n_chunks = SEQ // SSM_T
    return pl.pallas_call(
        _ssm_kernel,
        grid=(n_chunks + 1,),
        in_specs=[
            pl.BlockSpec((BATCH, SSM_T, SSM_WIDTH), lambda i: (0, jnp.minimum(i, n_chunks - 1), 0)),
            pl.BlockSpec((BATCH, SSM_T, SSM_WIDTH), lambda i: (0, jnp.maximum(i - 1, 0), 0)),
            _const_spec((LANE_CHUNKS, V7X_LANES, SLOT_COLS)),
            _const_spec((LANE_CHUNKS, V7X_SUBLANES, SLOT_COLS)),
            _const_spec((LANE_CHUNKS, SLOT_COLS, 2 * V7X_LANES)),
            _const_spec((1, SSM_WIDTH)),
            _const_spec((SSM_WIDTH, SSM_WIDTH)),
            _const_spec((1, SSM_WIDTH)),
        ],
        out_specs=pl.BlockSpec((BATCH, SSM_T, SSM_WIDTH), lambda i: (0, jnp.maximum(i - 1, 0), 0)),
        out_shape=jax.ShapeDtypeStruct((BATCH, SEQ, SSM_WIDTH), bf16),
        scratch_shapes=[
            pltpu.VMEM((LANE_CHUNKS, SSM_ROWS2, V7X_LANES), f32),
            pltpu.VMEM((LANE_CHUNKS, SSM_ROWS2, SLOT_COLS), f32),
            pltpu.VMEM((2, LANE_CHUNKS, SSM_ROWS2, SLOT_COLS), f32),
            pltpu.VMEM((2 * LANE_CHUNKS, SSM_ROWS2, V7X_LANES), f32),
            pltpu.VMEM((SSM_ROWS, SSM_WIDTH), f32),
            pltpu.VMEM((LANE_CHUNKS, V7X_SUBLANES, SLOT_COLS), f32),
        ],
        compiler_params=pltpu.CompilerParams(
            dimension_semantics=("arbitrary",), vmem_limit_bytes=VMEM_LIMIT),
        name="ssm",
    )(u, u, wb, lam, wc, d, wglu, ssm_out_g)


def _out_proj_kernel(a_ref, s_ref, x_ref, wa_ref, ws_ref, g_ref, o_ref, h_ref):
    x1 = x_ref[...] + _dot(a_ref[...], wa_ref[...]) + _dot(s_ref[...], ws_ref[...])
    o_ref[...] = x1
    h_ref[...] = (x1 * _rms_scale(x1) * g_ref[...]).astype(bf16)


def _out_proj(a, s, x, w, ln2_g):
    nt = SEQ // IN_TM
    row_spec = lambda w: pl.BlockSpec((None, IN_TM, w), lambda b, t: (b, t, 0))
    return pl.pallas_call(
        _out_proj_kernel,
        grid=(BATCH, nt),
        in_specs=[
            row_spec(ATTN_WIDTH),
            row_spec(SSM_WIDTH),
            row_spec(D_MODEL),
            pl.BlockSpec((ATTN_WIDTH, D_MODEL), lambda b, t: (0, 0)),
            pl.BlockSpec((SSM_WIDTH, D_MODEL), lambda b, t: (1, 0)),
            _const_spec((1, D_MODEL)),
        ],
        out_specs=[row_spec(D_MODEL), row_spec(D_MODEL)],
        out_shape=[jax.ShapeDtypeStruct((BATCH, SEQ, D_MODEL), f32),
                   jax.ShapeDtypeStruct((BATCH, SEQ, D_MODEL), bf16)],
        compiler_params=pltpu.CompilerParams(
            dimension_semantics=("arbitrary", "arbitrary"), vmem_limit_bytes=VMEM_LIMIT),
        name="out_proj",
    )(a, s, x, w, w, ln2_g)


def _ffn_kernel(x_hbm, h_ref, wg_ref, wu_ref, wd_ref, o_ref, sem):
    first = pl.program_id(1) == 0
    rows = pl.ds(pl.multiple_of(pl.program_id(0) * FFN_TM, FFN_TM), FFN_TM)
    residual_copy = pltpu.make_async_copy(x_hbm.at[rows, :], o_ref, sem)

    @pl.when(first)
    def _():
        residual_copy.start()

    h = h_ref[...]
    gate = _dot(h, wg_ref[...])
    up = _dot(h, wu_ref[...])
    act = (gate * _sigmoid(gate) * up).astype(bf16)

    @pl.when(first)
    def _():
        residual_copy.wait()

    o_ref[...] += _dot(act, wd_ref[...])


def _ffn(x, h, wg, wu, wd):
    n_rows = BATCH * SEQ
    return pl.pallas_call(
        _ffn_kernel,
        grid=(n_rows // FFN_TM, FF_HIDDEN // FFN_TF),
        in_specs=[
            pl.BlockSpec(memory_space=pl.ANY),
            pl.BlockSpec((FFN_TM, D_MODEL), lambda i, f: (i, 0)),
            pl.BlockSpec((D_MODEL, FFN_TF), lambda i, f: (0, f)),
            pl.BlockSpec((D_MODEL, FFN_TF), lambda i, f: (0, f)),
            pl.BlockSpec((FFN_TF, D_MODEL), lambda i, f: (f, 0)),
        ],
        out_specs=pl.BlockSpec((FFN_TM, D_MODEL), lambda i, f: (i, 0)),
        out_shape=jax.ShapeDtypeStruct((n_rows, D_MODEL), f32),
        scratch_shapes=[pltpu.SemaphoreType.DMA(())],
        compiler_params=pltpu.CompilerParams(
            dimension_semantics=("arbitrary", "arbitrary"), vmem_limit_bytes=VMEM_LIMIT),
        name="ffn",
    )(x, h, wg, wu, wd)


def _t5_bucket(dist):
    n = np.maximum(dist, 0)
    max_exact = REL_BUCKETS // 2
    nf = np.maximum(n, 1).astype(np.float32)
    large = max_exact + (np.log(nf / max_exact) / math.log(REL_MAX_DISTANCE / max_exact)
                         * (REL_BUCKETS - max_exact)).astype(np.int32)
    large = np.minimum(large, REL_BUCKETS - 1)
    return np.where(n < max_exact, n, large).astype(np.int32)


def _bias_kernel(rb_ref, bucket_ref, o_ref):
    bucket = bucket_ref[...]
    from_prev = (lax.broadcasted_iota(jnp.int32, (BLOCK, BLOCK), 1)
                 > lax.broadcasted_iota(jnp.int32, (BLOCK, BLOCK), 0))
    for head in range(N_Q_HEADS):
        acc = jnp.zeros((BLOCK, BLOCK), f32)
        for k in range(REL_BUCKETS):
            acc = jnp.where(bucket == k, rb_ref[k, head], acc)
        o_ref[1, head] = acc
        o_ref[0, head] = jnp.where(from_prev, MASK_VALUE, acc)


def _bias_tables(rel_bias):
    qi = np.arange(BLOCK)[:, None]
    kj = np.arange(BLOCK)[None, :]
    from_prev = kj > qi
    dist = np.where(from_prev, qi + BLOCK - kj, qi - kj)
    assert ((dist >= 0) & (dist < WINDOW)).all()
    return pl.pallas_call(
        _bias_kernel,
        in_specs=[
            pl.BlockSpec(memory_space=pltpu.SMEM),
            pl.BlockSpec(memory_space=pltpu.VMEM),
        ],
        out_specs=pl.BlockSpec(memory_space=pltpu.VMEM),
        out_shape=jax.ShapeDtypeStruct((2, N_Q_HEADS, BLOCK, BLOCK), f32),
        name="bias_table",
    )(rel_bias.astype(f32), jnp.asarray(_t5_bucket(dist)))


def _ssm_params(a_re, a_im, log_dt, b_re, b_im, c_re, c_im):
    dt = jnp.exp(log_dt)[:, None]
    mag = jnp.exp(a_re * dt)
    ang = a_im * dt
    lb_re, lb_im = mag * jnp.cos(ang), mag * jnp.sin(ang)
    nr, ni = lb_re - 1.0, lb_im
    den = a_re * a_re + a_im * a_im
    f_re = (nr * a_re + ni * a_im) / den
    f_im = (ni * a_re - nr * a_im) / den
    bb_re = f_re[..., None] * b_re - f_im[..., None] * b_im
    bb_im = f_re[..., None] * b_im + f_im[..., None] * b_re

    split = lambda t: t.reshape((2, LANE_CHUNKS, SLOTS, 2, 2) + t.shape[2:])
    two = np.arange(2)
    bbs = split(jnp.stack([bb_re, bb_im]))
    k = np.arange(V7X_LANES)
    k_u, k_g = k // (V7X_LANES // SLOTS), (k // SSM_GROUP) % 2
    wb_mask = ((k_u[:, None, None] == two[None, :, None]) & (k_g[:, None, None] == two[None, None, :]))
    bt = jnp.transpose(bbs, (1, 2, 3, 4, 6, 0, 5)).reshape(LANE_CHUNKS, V7X_LANES, 1, 2, 1, SSM_STATE)
    wb = (bt * wb_mask.astype(np.float32)[:, :, None, :, None]).reshape(LANE_CHUNKS, V7X_LANES, SLOT_COLS)
    cs = split(jnp.stack([c_re, -c_im]))
    r = np.arange(SLOT_COLS)
    r_s, r_g = r // (SLOT_COLS // SLOTS), (r // SSM_STATE) % 2
    wc_mask = ((two[:, None, None, None] == two[None, None, :, None])[None]
               & (r_s[:, None, None, None, None] == two[None, None, :, None, None])
               & (r_g[:, None, None, None, None] == two[None, None, None, None, :]))
    ct = jnp.transpose(cs, (1, 2, 0, 4, 6, 3, 5)).reshape(LANE_CHUNKS, SLOT_COLS, 1, 1, 2, 1, SSM_GROUP)
    wc = (ct * wc_mask.astype(np.float32)[..., None]).reshape(LANE_CHUNKS, SLOT_COLS, 2 * V7X_LANES)
    lam = jnp.transpose(split(jnp.stack([lb_re, lb_im])), (1, 3, 2, 0, 4, 5)).reshape(LANE_CHUNKS, 1, 2, SLOT_COLS)
    lam = jnp.broadcast_to(lam, (LANE_CHUNKS, BATCH, 2, SLOT_COLS)).reshape(LANE_CHUNKS, V7X_SUBLANES, SLOT_COLS)
    return wb.astype(bf16), wc.astype(bf16), lam


def _layer(x, rel_bias, ln1_g, w_in, q_norm_g, k_norm_g, attn_sinks, ssm_a_re, ssm_a_im,
           ssm_log_dt, ssm_b_re, ssm_b_im, ssm_c_re, ssm_c_im, ssm_d, w_glu,
           attn_out_g, ssm_out_g, w_out, ln2_g, w_ff_gate, w_ff_up, w_ff_down):
    row = lambda v: v.reshape(1, -1).astype(f32)

    def regroup(t, axis):
        shape = t.shape[:axis] + (N_KV_HEADS, Q_PER_KV, HEAD_DIM) + t.shape[axis + 1:]
        return jnp.swapaxes(t.reshape(shape), axis, axis + 1).reshape(t.shape)

    gqk = row(jnp.concatenate([jnp.tile(q_norm_g, N_Q_HEADS) * (HEAD_DIM ** -0.5),
                               jnp.tile(k_norm_g, N_KV_HEADS)]))
    kv_mask = jnp.asarray(np.broadcast_to(
        (np.arange(KV_WIDTH)[None, None, :] // HEAD_DIM) == np.arange(N_KV_HEADS)[:, None, None],
        (N_KV_HEADS, BLOCK, KV_WIDTH)), bf16)

    q, k, v, u = _in_proj(x, row(ln1_g), w_in.astype(bf16), gqk)

    pos = np.arange(BLOCK)
    from_prev = pos[None, :] > pos[:, None]
    tri = jnp.asarray(np.stack([from_prev, ~from_prev]), bf16)
    y_attn, wg, wu, wd, wo = _attention(
        attn_sinks.astype(f32), q, k, v, _bias_tables(rel_bias), kv_mask, tri, row(regroup(attn_out_g, 0)),
        w_ff_gate.astype(f32), w_ff_up.astype(f32), w_ff_down.astype(f32), w_out.astype(f32))

    wb, wc, lam = _ssm_params(ssm_a_re.astype(f32), ssm_a_im.astype(f32), ssm_log_dt.astype(f32),
                                     ssm_b_re.astype(f32), ssm_b_im.astype(f32),
                                     ssm_c_re.astype(f32), ssm_c_im.astype(f32))
    y_ssm = _ssm(u, wb, lam, wc, row(ssm_d), w_glu.astype(bf16), row(ssm_out_g))

    x1, h2 = _out_proj(y_attn, y_ssm, x, wo, row(ln2_g))

    out = _ffn(x1.reshape(BATCH * SEQ, D_MODEL), h2.reshape(BATCH * SEQ, D_MODEL), wg, wu, wd)
    return out.reshape(BATCH, SEQ, D_MODEL)


def kernel(x, rel_bias, ln1_g, w_in, q_norm_g, k_norm_g, attn_sinks, ssm_a_re, ssm_a_im, ssm_log_dt, ssm_b_re, ssm_b_im, ssm_c_re, ssm_c_im, ssm_d, w_glu, attn_out_g, ssm_out_g, w_out, ln2_g, w_ff_gate, w_ff_up, w_ff_down):
    for l in range(ln1_g.shape[0]):
        x = _layer(x, rel_bias, ln1_g[l], w_in[l], q_norm_g[l], k_norm_g[l], attn_sinks[l],
                   ssm_a_re[l], ssm_a_im[l], ssm_log_dt[l], ssm_b_re[l], ssm_b_im[l],
                   ssm_c_re[l], ssm_c_im[l], ssm_d[l], w_glu[l], attn_out_g[l], ssm_out_g[l],
                   w_out[l], ln2_g[l], w_ff_gate[l], w_ff_up[l], w_ff_down[l])
    return x
```

```python
import functools
import math

import jax
import jax.numpy as jnp
import numpy as np
from jax import lax
from jax.experimental import pallas as pl
from jax.experimental.pallas import tpu as pltpu

D_MODEL = 2048
BATCH = 4
SEQ = 4096
HEAD_DIM = 64
N_Q_HEADS = 16
N_KV_HEADS = 4
Q_PER_KV = N_Q_HEADS // N_KV_HEADS
ATTN_WIDTH = N_Q_HEADS * HEAD_DIM
KV_WIDTH = N_KV_HEADS * HEAD_DIM
WINDOW = 128
BLOCK = 128
SSM_WIDTH = D_MODEL - ATTN_WIDTH
IN_WIDTH = ATTN_WIDTH + 2 * KV_WIDTH + SSM_WIDTH
SSM_GROUP = 16
SSM_GROUPS = SSM_WIDTH // SSM_GROUP
SSM_STATE = 64
FF_HIDDEN = 5632
REL_BUCKETS = 32
REL_MAX_DISTANCE = 128
EPS = 1e-6

V7X_LANES = 128
V7X_SUBLANES = 8
V7X_VMEM_BYTES = 64 * 1024 * 1024
VMEM_LIMIT = V7X_VMEM_BYTES - 8 * 1024 * 1024

MASK_VALUE = -1e30
KV_SLAB = Q_PER_KV * HEAD_DIM
LANE_CHUNKS = SSM_WIDTH // V7X_LANES
SLOTS = 2
SLOT_COLS = SLOTS * 2 * V7X_LANES

ATTN_BLOCKS = 8
ATTN_SLOTS = 4
IN_TM = 512
FFN_TM = 1024
FFN_TF = 512
SSM_T = 64
SSM_ROWS = SSM_T * BATCH
SSM_ROWS2 = 2 * SSM_ROWS

f32 = jnp.float32
bf16 = jnp.bfloat16


def _dot(a, b):
    return jnp.dot(a, b, preferred_element_type=f32)


def _rms_scale(x):
    return lax.rsqrt(jnp.mean(x * x, axis=-1, keepdims=True) + EPS)


def _in_proj_kernel(x_ref, g_ref, w_ref, gqk_ref, q_ref, k_ref, v_ref, u_ref, proj_scr):
    xf = x_ref[...]
    h = (xf * _rms_scale(xf) * g_ref[...]).astype(bf16)
    proj_scr[...] = _dot(h, w_ref[...])

    low = lax.broadcasted_iota(jnp.int32, (IN_TM, V7X_LANES), 1) < HEAD_DIM

    def head_norm(b):
        cols = slice(b * V7X_LANES, (b + 1) * V7X_LANES)
        xb = proj_scr[:, cols]
        sq = xb * xb
        r_lo = lax.rsqrt(jnp.sum(jnp.where(low, sq, 0.0), axis=-1, keepdims=True) * (1.0 / HEAD_DIM) + EPS)
        r_hi = lax.rsqrt(jnp.sum(jnp.where(low, 0.0, sq), axis=-1, keepdims=True) * (1.0 / HEAD_DIM) + EPS)
        return xb * jnp.where(low, r_lo, r_hi) * gqk_ref[:, cols]

    blocks_per_kv = KV_SLAB // V7X_LANES
    qn = [head_norm(b) for b in range(ATTN_WIDTH // V7X_LANES)]
    for g in range(Q_PER_KV):
        half = slice((g % 2) * HEAD_DIM, (g % 2 + 1) * HEAD_DIM)
        q_ref[:, g * KV_WIDTH:(g + 1) * KV_WIDTH] = jnp.concatenate(
            [qn[kh * blocks_per_kv + g // 2][:, half] for kh in range(N_KV_HEADS)], axis=1).astype(bf16)
    for b in range(KV_WIDTH // V7X_LANES):
        k_ref[:, b * V7X_LANES:(b + 1) * V7X_LANES] = head_norm(ATTN_WIDTH // V7X_LANES + b).astype(bf16)
    v_ref[...] = proj_scr[:, ATTN_WIDTH + KV_WIDTH:ATTN_WIDTH + 2 * KV_WIDTH].astype(bf16)
    u_ref[...] = proj_scr[:, ATTN_WIDTH + 2 * KV_WIDTH:]


def _const_spec(shape):
    nd = len(shape)
    return pl.BlockSpec(shape, lambda *_: (0,) * nd)


def _in_proj(x, ln1_g, w, gqk):
    nt = SEQ // IN_TM
    row_spec = lambda w: pl.BlockSpec((None, IN_TM, w), lambda b, t: (b, t, 0))
    return pl.pallas_call(
        _in_proj_kernel,
        grid=(BATCH, nt),
        in_specs=[
            row_spec(D_MODEL),
            _const_spec((1, D_MODEL)),
            _const_spec((D_MODEL, IN_WIDTH)),
            _const_spec((1, ATTN_WIDTH + KV_WIDTH)),
        ],
        out_specs=[row_spec(ATTN_WIDTH), row_spec(KV_WIDTH), row_spec(KV_WIDTH), row_spec(SSM_WIDTH)],
        out_shape=[
            jax.ShapeDtypeStruct((BATCH, SEQ, ATTN_WIDTH), bf16),
            jax.ShapeDtypeStruct((BATCH, SEQ, KV_WIDTH), bf16),
            jax.ShapeDtypeStruct((BATCH, SEQ, KV_WIDTH), bf16),
            jax.ShapeDtypeStruct((BATCH, SEQ, SSM_WIDTH), f32),
        ],
        scratch_shapes=[pltpu.VMEM((IN_TM, IN_WIDTH), f32)],
        compiler_params=pltpu.CompilerParams(
            dimension_semantics=("arbitrary", "arbitrary"), vmem_limit_bytes=VMEM_LIMIT),
        name="in_proj",
    )(x, ln1_g, w, gqk)


def _attn_kernel(n_heads, sink_ref, q_ref, kc_ref, kp_ref, vc_ref, vp_ref, bias_ref, kv_mask_ref, tri_ref,
                 g_ref, w0_ref, w1_ref, w2_ref, *refs):
    wo_refs, (o_ref, c0_ref, c1_ref, c2_ref, co_ref, y_scr, s_scr, p_scr) = refs[:n_heads], refs[n_heads:]
    for w_ref, c_ref in ((w0_ref, c0_ref), (w1_ref, c1_ref), (w2_ref, c2_ref)):
        c_ref[...] = w_ref[...].astype(bf16)
    for r, w_ref in enumerate(wo_refs):
        co_ref[r * HEAD_DIM:(r + 1) * HEAD_DIM, :] = w_ref[...].astype(bf16)

    qi = lax.broadcasted_iota(jnp.int32, (BLOCK, BLOCK), 0)
    kj = lax.broadcasted_iota(jnp.int32, (BLOCK, BLOCK), 1)
    from_prev = kj > qi
    first_table = jnp.minimum(pl.program_id(1), 1)
    for j in range(ATTN_BLOCKS):
        rows = slice(j * BLOCK, (j + 1) * BLOCK)
        prev_rows = slice((j - 1) * BLOCK, j * BLOCK)
        k_prev = kp_ref[...] if j == 0 else kc_ref[prev_rows, :]
        v_prev = vp_ref[...] if j == 0 else vc_ref[prev_rows, :]
        keys = jnp.concatenate([k_prev, kc_ref[rows, :]], axis=0)
        vals = jnp.concatenate([v_prev, vc_ref[rows, :]], axis=0)
        q16 = jnp.concatenate(
            [q_ref[rows, g * KV_WIDTH:(g + 1) * KV_WIDTH] * kv_mask_ref[kh]
             for kh in range(N_KV_HEADS) for g in range(Q_PER_KV)], axis=0)
        slot = j % ATTN_SLOTS
        s_scr[slot] = lax.dot_general(q16, keys, (((1,), (1,)), ((), ())), preferred_element_type=f32)
        for head in range(N_Q_HEADS):
            head_rows = slice(head * BLOCK, (head + 1) * BLOCK)
            bias = bias_ref[first_table, head] if j == 0 else bias_ref[1, head]
            s = jnp.where(from_prev, s_scr[slot, head_rows, :BLOCK], s_scr[slot, head_rows, BLOCK:]) + bias
            sink = sink_ref[head]
            m = jnp.max(s, axis=-1, keepdims=True)
            p = jnp.exp(s - m)
            den = jnp.sum(p, axis=-1, keepdims=True) + jnp.exp(sink - m)
            w = (p * (1.0 / den)).astype(bf16)
            p_scr[slot, head_rows, :BLOCK] = w * tri_ref[0]
            p_scr[slot, head_rows, BLOCK:] = w * tri_ref[1]
        o16 = _dot(p_scr[slot], vals)
        for head in range(N_Q_HEADS):
            kh, g = divmod(head, Q_PER_KV)
            y_scr[rows, g * KV_WIDTH + kh * HEAD_DIM:g * KV_WIDTH + (kh + 1) * HEAD_DIM] = (
                o16[head * BLOCK:(head + 1) * BLOCK, kh * HEAD_DIM:(kh + 1) * HEAD_DIM])
    y = y_scr[...]
    o_ref[...] = (y * _rms_scale(y) * g_ref[...]).astype(bf16)


def _attention(sinks, q, k, v, bias2, kv_mask, tri, attn_out_g, w_gate, w_up, w_down, w_out):
    rows = ATTN_BLOCKS * BLOCK
    n_blk = SEQ // rows
    steps = BATCH * n_blk
    step = lambda b, n: b * n_blk + n

    def slab(w):
        return pl.BlockSpec((w.shape[0] // steps, w.shape[1]), lambda b, n: (step(b, n), 0))

    heads_per_step = w_out.shape[0] // HEAD_DIM // steps

    def w_out_piece(r):
        def index(b, n):
            t = step(b, n) * heads_per_step + r
            return (jnp.where(t < N_Q_HEADS, (t % N_KV_HEADS) * Q_PER_KV + t // N_KV_HEADS, t), 0)
        return pl.BlockSpec((HEAD_DIM, D_MODEL), index)

    cast_in = [slab(w_gate), slab(w_up), slab(w_down)] + [w_out_piece(r) for r in range(heads_per_step)]
    cast_out = [slab(w_gate), slab(w_up), slab(w_down), slab(w_out)]
    cast_args = [w_gate, w_up, w_down] + [w_out] * heads_per_step
    cur = pl.BlockSpec((None, rows, ATTN_WIDTH), lambda b, n: (b, n, 0))
    kv_cur = pl.BlockSpec((None, rows, KV_WIDTH), lambda b, n: (b, n, 0))
    kv_prev = pl.BlockSpec((None, BLOCK, KV_WIDTH), lambda b, n: (b, jnp.maximum(n * ATTN_BLOCKS - 1, 0), 0))
    return pl.pallas_call(
        functools.partial(_attn_kernel, heads_per_step),
        grid=(BATCH, SEQ // rows),
        in_specs=[
            pl.BlockSpec(memory_space=pltpu.SMEM),
            cur, kv_cur, kv_prev, kv_cur, kv_prev,
            _const_spec((2, N_Q_HEADS, BLOCK, BLOCK)),
            _const_spec((N_KV_HEADS, BLOCK, KV_WIDTH)),
            _const_spec((2, BLOCK, BLOCK)),
            _const_spec((1, ATTN_WIDTH)),
        ] + cast_in,
        out_specs=[cur] + cast_out,
        out_shape=[jax.ShapeDtypeStruct((BATCH, SEQ, ATTN_WIDTH), bf16)]
        + [jax.ShapeDtypeStruct(w.shape, bf16) for w in (w_gate, w_up, w_down, w_out)],
        scratch_shapes=[
            pltpu.VMEM((rows, ATTN_WIDTH), f32),
            pltpu.VMEM((ATTN_SLOTS, N_Q_HEADS * BLOCK, 2 * BLOCK), f32),
            pltpu.VMEM((ATTN_SLOTS, N_Q_HEADS * BLOCK, 2 * BLOCK), bf16),
        ],
        compiler_params=pltpu.CompilerParams(
            dimension_semantics=("arbitrary", "arbitrary"), vmem_limit_bytes=VMEM_LIMIT),
        name="attn",
    )(sinks, q, k, k, v, v, bias2, kv_mask, tri, attn_out_g, *cast_args)


def _gelu_tanh(x):
    return 0.5 * x * (1.0 + jnp.tanh(math.sqrt(2.0 / math.pi) * (x + 0.044715 * (x * x * x))))


def _sigmoid(x):
    return 1.0 / (1.0 + jnp.exp(-x))


def _ssm_kernel(u_ref, uprev_ref, wb_ref, lam_ref, wc_ref, d_ref, wglu_ref, g_ref,
                o_ref, u2_scr, bu_scr, h_scr, y2_scr, y_scr, st_scr):
    step = pl.program_id(0)

    @pl.when(step == 0)
    def _():
        st_scr[...] = jnp.zeros_like(st_scr)
        h_scr[1] = jnp.zeros(h_scr.shape[1:], f32)

    for parity in range(2):
        pl.when(step % 2 == parity)(functools.partial(
            _ssm_step, u_ref, uprev_ref, wb_ref, lam_ref, wc_ref, d_ref, wglu_ref, g_ref, o_ref,
            u2_scr, bu_scr, h_scr.at[parity], h_scr.at[1 - parity], y2_scr, y_scr, st_scr))


def _ssm_step(u_ref, uprev_ref, wb_ref, lam_ref, wc_ref, d_ref, wglu_ref, g_ref, o_ref,
              u2_scr, bu_scr, h_new, h_old, y2_scr, y_scr, st_scr):
    for c in range(LANE_CHUNKS):
        for b in range(BATCH):
            for par in range(2):
                u2_scr.at[c][pl.ds(2 * b + par, SSM_T, stride=V7X_SUBLANES), :] = (
                    u_ref[b, :, c * V7X_LANES:(c + 1) * V7X_LANES])
    row_par = lax.broadcasted_iota(jnp.int32, (SSM_ROWS2, V7X_LANES), 0) % 2
    pair_par = (lax.broadcasted_iota(jnp.int32, (SSM_ROWS2, V7X_LANES), 1) // (2 * SSM_GROUP)) % 2
    own_pair = row_par == pair_par

    for c in range(LANE_CHUNKS):
        bu_scr[c] = _dot(jnp.where(own_pair, u2_scr[c], 0.0).astype(bf16), wb_ref[c])

    half = LANE_CHUNKS // 2
    for c0 in (0, half):
        chains = [(c, s) for c in range(c0, c0 + half) for s in range(SLOTS)]

        def cols(s):
            base = s * 2 * V7X_LANES
            return slice(base, base + V7X_LANES), slice(base + V7X_LANES, base + 2 * V7X_LANES)

        lams = [(lam_ref[c, :, cols(s)[0]], lam_ref[c, :, cols(s)[1]]) for c, s in chains]
        init = tuple((st_scr[c, :, cols(s)[0]], st_scr[c, :, cols(s)[1]]) for c, s in chains)

        def body(t, carry, chains=chains, lams=lams, cols=cols, h_new=h_new):
            rows = pl.ds(t * V7X_SUBLANES, V7X_SUBLANES)
            out = []
            for (c, s), (l_re, l_im), (s_re, s_im) in zip(chains, lams, carry):
                cr, ci = cols(s)
                n_re = (l_re * s_re - l_im * s_im) + bu_scr[c, rows, cr]
                n_im = (l_re * s_im + l_im * s_re) + bu_scr[c, rows, ci]
                h_new[c, rows, cr] = n_re
                h_new[c, rows, ci] = n_im
                out.append((n_re, n_im))
            return tuple(out)

        fin = init
        for t in range(SSM_T):
            fin = body(t, fin)
        for (c, s), (s_re, s_im) in zip(chains, fin):
            st_scr[c, :, cols(s)[0]] = s_re
            st_scr[c, :, cols(s)[1]] = s_im

    for c in range(LANE_CHUNKS):
        cols_c = slice(c * V7X_LANES, (c + 1) * V7X_LANES)
        y2 = _dot(h_old[c].astype(bf16), wc_ref[c])
        y2_scr[2 * c] = y2[:, :V7X_LANES]
        y2_scr[2 * c + 1] = y2[:, V7X_LANES:]
        for b in range(BATCH):
            yc = (y2_scr.at[2 * c][pl.ds(2 * b, SSM_T, stride=V7X_SUBLANES), :]
                  + y2_scr.at[2 * c + 1][pl.ds(2 * b + 1, SSM_T, stride=V7X_SUBLANES), :])
            yc = yc + d_ref[:, cols_c] * uprev_ref[b, :, cols_c]
            y_scr[b * SSM_T:(b + 1) * SSM_T, cols_c] = _gelu_tanh(yc)
    y = y_scr[...]
    out = y * _sigmoid(_dot(y.astype(bf16), wglu_ref[...]))
    out = (out * _rms_scale(out) * g_ref[...]).astype(bf16)
    o_ref[...] = out.reshape(BATCH, SSM_T, SSM_WIDTH)


def _ssm(u, wb, lam, wc, d, wglu, ssm_out_g):
    n_chunks = SEQ // SSM_T
    return pl.pallas_call(
        _ssm_kernel,
        grid=(n_chunks + 1,),
        in_specs=[
            pl.BlockSpec((BATCH, SSM_T, SSM_WIDTH), lambda i: (0, jnp.minimum(i, n_chunks - 1), 0)),
            pl.BlockSpec((BATCH, SSM_T, SSM_WIDTH), lambda i: (0, jnp.maximum(i - 1, 0), 0)),
            _const_spec((LANE_CHUNKS, V7X_LANES, SLOT_COLS)),
            _const_spec((LANE_CHUNKS, V7X_SUBLANES, SLOT_COLS)),
            _const_spec((LANE_CHUNKS, SLOT_COLS, 2 * V7X_LANES)),
            _const_spec((1, SSM_WIDTH)),
            _const_spec((SSM_WIDTH, SSM_WIDTH)),
            _const_spec((1, SSM_WIDTH)),
        ],
        out_specs=pl.BlockSpec((BATCH, SSM_T, SSM_WIDTH), lambda i: (0, jnp.maximum(i - 1, 0), 0)),
        out_shape=jax.ShapeDtypeStruct((BATCH, SEQ, SSM_WIDTH), bf16),
        scratch_shapes=[
            pltpu.VMEM((LANE_CHUNKS, SSM_ROWS2, V7X_LANES), f32),
            pltpu.VMEM((LANE_CHUNKS, SSM_ROWS2, SLOT_COLS), f32),
            pltpu.VMEM((2, LANE_CHUNKS, SSM_ROWS2, SLOT_COLS), f32),
            pltpu.VMEM((2 * LANE_CHUNKS, SSM_ROWS2, V7X_LANES), f32),
            pltpu.VMEM((SSM_ROWS, SSM_WIDTH), f32),
            pltpu.VMEM((LANE_CHUNKS, V7X_SUBLANES, SLOT_COLS), f32),
        ],
        compiler_params=pltpu.CompilerParams(
            dimension_semantics=("arbitrary",), vmem_limit_bytes=VMEM_LIMIT),
        name="ssm",
    )(u, u, wb, lam, wc, d, wglu, ssm_out_g)


def _out_proj_kernel(a_ref, s_ref, x_ref, wa_ref, ws_ref, g_ref, o_ref, h_ref):
    x1 = x_ref[...] + _dot(a_ref[...], wa_ref[...]) + _dot(s_ref[...], ws_ref[...])
    o_ref[...] = x1
    h_ref[...] = (x1 * _rms_scale(x1) * g_ref[...]).astype(bf16)


def _out_proj(a, s, x, w, ln2_g):
    nt = SEQ // IN_TM
    row_spec = lambda w: pl.BlockSpec((None, IN_TM, w), lambda b, t: (b, t, 0))
    return pl.pallas_call(
        _out_proj_kernel,
        grid=(BATCH, nt),
        in_specs=[
            row_spec(ATTN_WIDTH),
            row_spec(SSM_WIDTH),
            row_spec(D_MODEL),
            pl.BlockSpec((ATTN_WIDTH, D_MODEL), lambda b, t: (0, 0)),
            pl.BlockSpec((SSM_WIDTH, D_MODEL), lambda b, t: (1, 0)),
            _const_spec((1, D_MODEL)),
        ],
        out_specs=[row_spec(D_MODEL), row_spec(D_MODEL)],
        out_shape=[jax.ShapeDtypeStruct((BATCH, SEQ, D_MODEL), f32),
                   jax.ShapeDtypeStruct((BATCH, SEQ, D_MODEL), bf16)],
        compiler_params=pltpu.CompilerParams(
            dimension_semantics=("arbitrary", "arbitrary"), vmem_limit_bytes=VMEM_LIMIT),
        name="out_proj",
    )(a, s, x, w, w, ln2_g)


def _ffn_kernel(x_hbm, h_ref, wg_ref, wu_ref, wd_ref, o_ref, sem):
    first = pl.program_id(1) == 0
    rows = pl.ds(pl.multiple_of(pl.program_id(0) * FFN_TM, FFN_TM), FFN_TM)
    residual_copy = pltpu.make_async_copy(x_hbm.at[rows, :], o_ref, sem)

    @pl.when(first)
    def _():
        residual_copy.start()

    h = h_ref[...]
    gate = _dot(h, wg_ref[...])
    up = _dot(h, wu_ref[...])
    act = (gate * _sigmoid(gate) * up).astype(bf16)

    @pl.when(first)
    def _():
        residual_copy.wait()

    o_ref[...] += _dot(act, wd_ref[...])


def _ffn(x, h, wg, wu, wd):
    n_rows = BATCH * SEQ
    return pl.pallas_call(
        _ffn_kernel,
        grid=(n_rows // FFN_TM, FF_HIDDEN // FFN_TF),
        in_specs=[
            pl.BlockSpec(memory_space=pl.ANY),
            pl.BlockSpec((FFN_TM, D_MODEL), lambda i, f: (i, 0)),
            pl.BlockSpec((D_MODEL, FFN_TF), lambda i, f: (0, f)),
            pl.BlockSpec((D_MODEL, FFN_TF), lambda i, f: (0, f)),
            pl.BlockSpec((FFN_TF, D_MODEL), lambda i, f: (f, 0)),
        ],
        out_specs=pl.BlockSpec((FFN_TM, D_MODEL), lambda i, f: (i, 0)),
        out_shape=jax.ShapeDtypeStruct((n_rows, D_MODEL), f32),
        scratch_shapes=[pltpu.SemaphoreType.DMA(())],
        compiler_params=pltpu.CompilerParams(
            dimension_semantics=("arbitrary", "arbitrary"), vmem_limit_bytes=VMEM_LIMIT),
        name="ffn",
    )(x, h, wg, wu, wd)


def _t5_bucket(dist):
    n = np.maximum(dist, 0)
    max_exact = REL_BUCKETS // 2
    nf = np.maximum(n, 1).astype(np.float32)
    large = max_exact + (np.log(nf / max_exact) / math.log(REL_MAX_DISTANCE / max_exact)
                         * (REL_BUCKETS - max_exact)).astype(np.int32)
    large = np.minimum(large, REL_BUCKETS - 1)
    return np.where(n < max_exact, n, large).astype(np.int32)


def _bias_kernel(rb_ref, bucket_ref, o_ref):
    bucket = bucket_ref[...]
    from_prev = (lax.broadcasted_iota(jnp.int32, (BLOCK, BLOCK), 1)
                 > lax.broadcasted_iota(jnp.int32, (BLOCK, BLOCK), 0))
    for head in range(N_Q_HEADS):
        acc = jnp.zeros((BLOCK, BLOCK), f32)
        for k in range(REL_BUCKETS):
            acc = jnp.where(bucket == k, rb_ref[k, head], acc)
        o_ref[1, head] = acc
        o_ref[0, head] = jnp.where(from_prev, MASK_VALUE, acc)


def _bias_tables(rel_bias):
    qi = np.arange(BLOCK)[:, None]
    kj = np.arange(BLOCK)[None, :]
    from_prev = kj > qi
    dist = np.where(from_prev, qi + BLOCK - kj, qi - kj)
    assert ((dist >= 0) & (dist < WINDOW)).all()
    return pl.pallas_call(
        _bias_kernel,
        in_specs=[
            pl.BlockSpec(memory_space=pltpu.SMEM),
            pl.BlockSpec(memory_space=pltpu.VMEM),
        ],
        out_specs=pl.BlockSpec(memory_space=pltpu.VMEM),
        out_shape=jax.ShapeDtypeStruct((2, N_Q_HEADS, BLOCK, BLOCK), f32),
        name="bias_table",
    )(rel_bias.astype(f32), jnp.asarray(_t5_bucket(dist)))


def _ssm_params(a_re, a_im, log_dt, b_re, b_im, c_re, c_im):
    dt = jnp.exp(log_dt)[:, None]
    mag = jnp.exp(a_re * dt)
    ang = a_im * dt
    lb_re, lb_im = mag * jnp.cos(ang), mag * jnp.sin(ang)
    nr, ni = lb_re - 1.0, lb_im
    den = a_re * a_re + a_im * a_im
    f_re = (nr * a_re + ni * a_im) / den
    f_im = (ni * a_re - nr * a_im) / den
    bb_re = f_re[..., None] * b_re - f_im[..., None] * b_im
    bb_im = f_re[..., None] * b_im + f_im[..., None] * b_re

    split = lambda t: t.reshape((2, LANE_CHUNKS, SLOTS, 2, 2) + t.shape[2:])
    two = np.arange(2)
    bbs = split(jnp.stack([bb_re, bb_im]))
    k = np.arange(V7X_LANES)
    k_u, k_g = k // (V7X_LANES // SLOTS), (k // SSM_GROUP) % 2
    wb_mask = ((k_u[:, None, None] == two[None, :, None]) & (k_g[:, None, None] == two[None, None, :]))
    bt = jnp.transpose(bbs, (1, 2, 3, 4, 6, 0, 5)).reshape(LANE_CHUNKS, V7X_LANES, 1, 2, 1, SSM_STATE)
    wb = (bt * wb_mask.astype(np.float32)[:, :, None, :, None]).reshape(LANE_CHUNKS, V7X_LANES, SLOT_COLS)
    cs = split(jnp.stack([c_re, -c_im]))
    r = np.arange(SLOT_COLS)
    r_s, r_g = r // (SLOT_COLS // SLOTS), (r // SSM_STATE) % 2
    wc_mask = ((two[:, None, None, None] == two[None, None, :, None])[None]
               & (r_s[:, None, None, None, None] == two[None, None, :, None, None])
               & (r_g[:, None, None, None, None] == two[None, None, None, None, :]))
    ct = jnp.transpose(cs, (1, 2, 0, 4, 6, 3, 5)).reshape(LANE_CHUNKS, SLOT_COLS, 1, 1, 2, 1, SSM_GROUP)
    wc = (ct * wc_mask.astype(np.float32)[..., None]).reshape(LANE_CHUNKS, SLOT_COLS, 2 * V7X_LANES)
    lam = jnp.transpose(split(jnp.stack([lb_re, lb_im])), (1, 3, 2, 0, 4, 5)).reshape(LANE_CHUNKS, 1, 2, SLOT_COLS)
    lam = jnp.broadcast_to(lam, (LANE_CHUNKS, BATCH, 2, SLOT_COLS)).reshape(LANE_CHUNKS, V7X_SUBLANES, SLOT_COLS)
    return wb.astype(bf16), wc.astype(bf16), lam


def _layer(x, rel_bias, ln1_g, w_in, q_norm_g, k_norm_g, attn_sinks, ssm_a_re, ssm_a_im,
           ssm_log_dt, ssm_b_re, ssm_b_im, ssm_c_re, ssm_c_im, ssm_d, w_glu,
           attn_out_g, ssm_out_g, w_out, ln2_g, w_ff_gate, w_ff_up, w_ff_down):
    row = lambda v: v.reshape(1, -1).astype(f32)

    def regroup(t, axis):
        shape = t.shape[:axis] + (N_KV_HEADS, Q_PER_KV, HEAD_DIM) + t.shape[axis + 1:]
        return jnp.swapaxes(t.reshape(shape), axis, axis + 1).reshape(t.shape)

    gqk = row(jnp.concatenate([jnp.tile(q_norm_g, N_Q_HEADS) * (HEAD_DIM ** -0.5),
                               jnp.tile(k_norm_g, N_KV_HEADS)]))
    kv_mask = jnp.asarray(np.broadcast_to(
        (np.arange(KV_WIDTH)[None, None, :] // HEAD_DIM) == np.arange(N_KV_HEADS)[:, None, None],
        (N_KV_HEADS, BLOCK, KV_WIDTH)), bf16)

    q, k, v, u = _in_proj(x, row(ln1_g), w_in.astype(bf16), gqk)

    pos = np.arange(BLOCK)
    from_prev = pos[None, :] > pos[:, None]
    tri = jnp.asarray(np.stack([from_prev, ~from_prev]), bf16)
    y_attn, wg, wu, wd, wo = _attention(
        attn_sinks.astype(f32), q, k, v, _bias_tables(rel_bias), kv_mask, tri, row(regroup(attn_out_g, 0)),
        w_ff_gate.astype(f32), w_ff_up.astype(f32), w_ff_down.astype(f32), w_out.astype(f32))

    wb, wc, lam = _ssm_params(ssm_a_re.astype(f32), ssm_a_im.astype(f32), ssm_log_dt.astype(f32),
                                     ssm_b_re.astype(f32), ssm_b_im.astype(f32),
                                     ssm_c_re.astype(f32), ssm_c_im.astype(f32))
    y_ssm = _ssm(u, wb, lam, wc, row(ssm_d), w_glu.astype(bf16), row(ssm_out_g))

    x1, h2 = _out_proj(y_attn, y_ssm, x, wo, row(ln2_g))

    out = _ffn(x1.reshape(BATCH * SEQ, D_MODEL), h2.reshape(BATCH * SEQ, D_MODEL), wg, wu, wd)
    return out.reshape(BATCH, SEQ, D_MODEL)


def kernel(x, rel_bias, ln1_g, w_in, q_norm_g, k_norm_g, attn_sinks, ssm_a_re, ssm_a_im, ssm_log_dt, ssm_b_re, ssm_b_im, ssm_c_re, ssm_c_im, ssm_d, w_glu, attn_out_g, ssm_out_g, w_out, ln2_g, w_ff_gate, w_ff_up, w_ff_down):
    for l in range(ln1_g.shape[0]):
        x = _layer(x, rel_bias, ln1_g[l], w_in[l], q_norm_g[l], k_norm_g[l], attn_sinks[l],
                   ssm_a_re[l], ssm_a_im[l], ssm_log_dt[l], ssm_b_re[l], ssm_b_im[l],
                   ssm_c_re[l], ssm_c_im[l], ssm_d[l], w_glu[l], attn_out_g[l], ssm_out_g[l],
                   w_out[l], ln2_g[l], w_ff_gate[l], w_ff_up[l], w_ff_down[l])
    return x
```

```python
import functools
import math

import jax
import jax.numpy as jnp
import numpy as np
from jax import lax
from jax.experimental import pallas as pl
from jax.experimental.pallas import tpu as pltpu

D_MODEL = 2048
BATCH = 4
SEQ = 4096
HEAD_DIM = 64
N_Q_HEADS = 16
N_KV_HEADS = 4
Q_PER_KV = N_Q_HEADS // N_KV_HEADS
ATTN_WIDTH = N_Q_HEADS * HEAD_DIM
KV_WIDTH = N_KV_HEADS * HEAD_DIM
WINDOW = 128
BLOCK = 128
SSM_WIDTH = D_MODEL - ATTN_WIDTH
IN_WIDTH = ATTN_WIDTH + 2 * KV_WIDTH + SSM_WIDTH
SSM_GROUP = 16
SSM_GROUPS = SSM_WIDTH // SSM_GROUP
SSM_STATE = 64
FF_HIDDEN = 5632
REL_BUCKETS = 32
REL_MAX_DISTANCE = 128
EPS = 1e-6

V7X_LANES = 128
V7X_SUBLANES = 8
V7X_VMEM_BYTES = 64 * 1024 * 1024
VMEM_LIMIT = V7X_VMEM_BYTES - 8 * 1024 * 1024

MASK_VALUE = -1e30
KV_SLAB = Q_PER_KV * HEAD_DIM
LANE_CHUNKS = SSM_WIDTH // V7X_LANES
SLOTS = 2
SLOT_COLS = SLOTS * 2 * V7X_LANES

ATTN_BLOCKS = 8
ATTN_SLOTS = 4
IN_TM = 512
FFN_TM = 1024
FFN_TF = 512
FFN_STEPS = FF_HIDDEN // FFN_TF
SSM_T = 64
SSM_ROWS = SSM_T * BATCH
SSM_ROWS2 = 2 * SSM_ROWS

f32 = jnp.float32
bf16 = jnp.bfloat16


def _dot(a, b):
    return jnp.dot(a, b, preferred_element_type=f32)


def _rms_scale(x):
    return lax.rsqrt(jnp.mean(x * x, axis=-1, keepdims=True) + EPS)


def _in_proj_kernel(x_ref, g_ref, w_ref, gqk_ref, q_ref, k_ref, v_ref, u_ref, proj_scr):
    xf = x_ref[...]
    h = (xf * _rms_scale(xf) * g_ref[...]).astype(bf16)
    proj_scr[...] = _dot(h, w_ref[...])

    low = lax.broadcasted_iota(jnp.int32, (IN_TM, V7X_LANES), 1) < HEAD_DIM

    def head_norm(b):
        cols = slice(b * V7X_LANES, (b + 1) * V7X_LANES)
        xb = proj_scr[:, cols]
        sq = xb * xb
        r_lo = lax.rsqrt(jnp.sum(jnp.where(low, sq, 0.0), axis=-1, keepdims=True) * (1.0 / HEAD_DIM) + EPS)
        r_hi = lax.rsqrt(jnp.sum(jnp.where(low, 0.0, sq), axis=-1, keepdims=True) * (1.0 / HEAD_DIM) + EPS)
        return xb * jnp.where(low, r_lo, r_hi) * gqk_ref[:, cols]

    blocks_per_kv = KV_SLAB // V7X_LANES
    qn = [head_norm(b) for b in range(ATTN_WIDTH // V7X_LANES)]
    for g in range(Q_PER_KV):
        half = slice((g % 2) * HEAD_DIM, (g % 2 + 1) * HEAD_DIM)
        q_ref[:, g * KV_WIDTH:(g + 1) * KV_WIDTH] = jnp.concatenate(
            [qn[kh * blocks_per_kv + g // 2][:, half] for kh in range(N_KV_HEADS)], axis=1).astype(bf16)
    for b in range(KV_WIDTH // V7X_LANES):
        k_ref[:, b * V7X_LANES:(b + 1) * V7X_LANES] = head_norm(ATTN_WIDTH // V7X_LANES + b).astype(bf16)
    v_ref[...] = proj_scr[:, ATTN_WIDTH + KV_WIDTH:ATTN_WIDTH + 2 * KV_WIDTH].astype(bf16)
    u_ref[...] = proj_scr[:, ATTN_WIDTH + 2 * KV_WIDTH:]


def _const_spec(shape):
    nd = len(shape)
    return pl.BlockSpec(shape, lambda *_: (0,) * nd)


def _in_proj(x, ln1_g, w, gqk):
    nt = SEQ // IN_TM
    row_spec = lambda w: pl.BlockSpec((None, IN_TM, w), lambda b, t: (b, t, 0))
    return pl.pallas_call(
        _in_proj_kernel,
        grid=(BATCH, nt),
        in_specs=[
            row_spec(D_MODEL),
            _const_spec((1, D_MODEL)),
            _const_spec((D_MODEL, IN_WIDTH)),
            _const_spec((1, ATTN_WIDTH + KV_WIDTH)),
        ],
        out_specs=[row_spec(ATTN_WIDTH), row_spec(KV_WIDTH), row_spec(KV_WIDTH), row_spec(SSM_WIDTH)],
        out_shape=[
            jax.ShapeDtypeStruct((BATCH, SEQ, ATTN_WIDTH), bf16),
            jax.ShapeDtypeStruct((BATCH, SEQ, KV_WIDTH), bf16),
            jax.ShapeDtypeStruct((BATCH, SEQ, KV_WIDTH), bf16),
            jax.ShapeDtypeStruct((BATCH, SEQ, SSM_WIDTH), f32),
        ],
        scratch_shapes=[pltpu.VMEM((IN_TM, IN_WIDTH), f32)],
        compiler_params=pltpu.CompilerParams(
            dimension_semantics=("arbitrary", "arbitrary"), vmem_limit_bytes=VMEM_LIMIT),
        name="in_proj",
    )(x, ln1_g, w, gqk)


def _attn_kernel(n_heads, sink_ref, q_ref, kc_ref, kp_ref, vc_ref, vp_ref, bias_ref, kv_mask_ref, tri_ref,
                 g_ref, w0_ref, w1_ref, w2_ref, *refs):
    wo_refs, (o_ref, c0_ref, c1_ref, c2_ref, co_ref, y_scr, s_scr, p_scr) = refs[:n_heads], refs[n_heads:]
    for w_ref, c_ref in ((w0_ref, c0_ref), (w1_ref, c1_ref), (w2_ref, c2_ref)):
        c_ref[...] = w_ref[...].astype(bf16)
    for r, w_ref in enumerate(wo_refs):
        co_ref[r * HEAD_DIM:(r + 1) * HEAD_DIM, :] = w_ref[...].astype(bf16)

    qi = lax.broadcasted_iota(jnp.int32, (BLOCK, BLOCK), 0)
    kj = lax.broadcasted_iota(jnp.int32, (BLOCK, BLOCK), 1)
    from_prev = kj > qi
    first_table = jnp.minimum(pl.program_id(1), 1)
    for j in range(ATTN_BLOCKS):
        rows = slice(j * BLOCK, (j + 1) * BLOCK)
        prev_rows = slice((j - 1) * BLOCK, j * BLOCK)
        k_prev = kp_ref[...] if j == 0 else kc_ref[prev_rows, :]
        v_prev = vp_ref[...] if j == 0 else vc_ref[prev_rows, :]
        keys = jnp.concatenate([k_prev, kc_ref[rows, :]], axis=0)
        vals = jnp.concatenate([v_prev, vc_ref[rows, :]], axis=0)
        q16 = jnp.concatenate(
            [q_ref[rows, g * KV_WIDTH:(g + 1) * KV_WIDTH] * kv_mask_ref[kh]
             for kh in range(N_KV_HEADS) for g in range(Q_PER_KV)], axis=0)
        slot = j % ATTN_SLOTS
        s_scr[slot] = lax.dot_general(q16, keys, (((1,), (1,)), ((), ())), preferred_element_type=f32)
        for head in range(N_Q_HEADS):
            head_rows = slice(head * BLOCK, (head + 1) * BLOCK)
            bias = bias_ref[first_table, head] if j == 0 else bias_ref[1, head]
            s = jnp.where(from_prev, s_scr[slot, head_rows, :BLOCK], s_scr[slot, head_rows, BLOCK:]) + bias
            sink = sink_ref[head]
            m = jnp.max(s, axis=-1, keepdims=True)
            p = jnp.exp(s - m)
            den = jnp.sum(p, axis=-1, keepdims=True) + jnp.exp(sink - m)
            w = (p * (1.0 / den)).astype(bf16)
            p_scr[slot, head_rows, :BLOCK] = w * tri_ref[0]
            p_scr[slot, head_rows, BLOCK:] = w * tri_ref[1]
        o16 = _dot(p_scr[slot], vals)
        for head in range(N_Q_HEADS):
            kh, g = divmod(head, Q_PER_KV)
            y_scr[rows, g * KV_WIDTH + kh * HEAD_DIM:g * KV_WIDTH + (kh + 1) * HEAD_DIM] = (
                o16[head * BLOCK:(head + 1) * BLOCK, kh * HEAD_DIM:(kh + 1) * HEAD_DIM])
    y = y_scr[...]
    o_ref[...] = (y * _rms_scale(y) * g_ref[...]).astype(bf16)


def _attention(sinks, q, k, v, bias2, kv_mask, tri, attn_out_g, w_gate, w_up, w_down, w_out):
    rows = ATTN_BLOCKS * BLOCK
    n_blk = SEQ // rows
    steps = BATCH * n_blk
    step = lambda b, n: b * n_blk + n

    def slab(w):
        return pl.BlockSpec((w.shape[0] // steps, w.shape[1]), lambda b, n: (step(b, n), 0))

    heads_per_step = w_out.shape[0] // HEAD_DIM // steps

    def w_out_piece(r):
        def index(b, n):
            t = step(b, n) * heads_per_step + r
            return (jnp.where(t < N_Q_HEADS, (t % N_KV_HEADS) * Q_PER_KV + t // N_KV_HEADS, t), 0)
        return pl.BlockSpec((HEAD_DIM, D_MODEL), index)

    cast_in = [slab(w_gate), slab(w_up), slab(w_down)] + [w_out_piece(r) for r in range(heads_per_step)]
    cast_out = [slab(w_gate), slab(w_up), slab(w_down), slab(w_out)]
    cast_args = [w_gate, w_up, w_down] + [w_out] * heads_per_step
    cur = pl.BlockSpec((None, rows, ATTN_WIDTH), lambda b, n: (b, n, 0))
    kv_cur = pl.BlockSpec((None, rows, KV_WIDTH), lambda b, n: (b, n, 0))
    kv_prev = pl.BlockSpec((None, BLOCK, KV_WIDTH), lambda b, n: (b, jnp.maximum(n * ATTN_BLOCKS - 1, 0), 0))
    return pl.pallas_call(
        functools.partial(_attn_kernel, heads_per_step),
        grid=(BATCH, SEQ // rows),
        in_specs=[
            pl.BlockSpec(memory_space=pltpu.SMEM),
            cur, kv_cur, kv_prev, kv_cur, kv_prev,
            _const_spec((2, N_Q_HEADS, BLOCK, BLOCK)),
            _const_spec((N_KV_HEADS, BLOCK, KV_WIDTH)),
            _const_spec((2, BLOCK, BLOCK)),
            _const_spec((1, ATTN_WIDTH)),
        ] + cast_in,
        out_specs=[cur] + cast_out,
        out_shape=[jax.ShapeDtypeStruct((BATCH, SEQ, ATTN_WIDTH), bf16)]
        + [jax.ShapeDtypeStruct(w.shape, bf16) for w in (w_gate, w_up, w_down, w_out)],
        scratch_shapes=[
            pltpu.VMEM((rows, ATTN_WIDTH), f32),
            pltpu.VMEM((ATTN_SLOTS, N_Q_HEADS * BLOCK, 2 * BLOCK), f32),
            pltpu.VMEM((ATTN_SLOTS, N_Q_HEADS * BLOCK, 2 * BLOCK), bf16),
        ],
        compiler_params=pltpu.CompilerParams(
            dimension_semantics=("arbitrary", "arbitrary"), vmem_limit_bytes=VMEM_LIMIT),
        name="attn",
    )(sinks, q, k, k, v, v, bias2, kv_mask, tri, attn_out_g, *cast_args)


def _gelu_tanh(x):
    return 0.5 * x * (1.0 + jnp.tanh(math.sqrt(2.0 / math.pi) * (x + 0.044715 * (x * x * x))))


def _sigmoid(x):
    return 1.0 / (1.0 + jnp.exp(-x))


def _ssm_kernel(u_ref, uprev_ref, wb_ref, lam_ref, wc_ref, d_ref, wglu_ref, g_ref,
                o_ref, u2_scr, bu_scr, h_scr, y2_scr, y_scr, st_scr):
    step = pl.program_id(0)

    @pl.when(step == 0)
    def _():
        st_scr[...] = jnp.zeros_like(st_scr)
        h_scr[1] = jnp.zeros(h_scr.shape[1:], f32)

    for parity in range(2):
        pl.when(step % 2 == parity)(functools.partial(
            _ssm_step, u_ref, uprev_ref, wb_ref, lam_ref, wc_ref, d_ref, wglu_ref, g_ref, o_ref,
            u2_scr, bu_scr, h_scr.at[parity], h_scr.at[1 - parity], y2_scr, y_scr, st_scr))


def _ssm_step(u_ref, uprev_ref, wb_ref, lam_ref, wc_ref, d_ref, wglu_ref, g_ref, o_ref,
              u2_scr, bu_scr, h_new, h_old, y2_scr, y_scr, st_scr):
    for c in range(LANE_CHUNKS):
        for b in range(BATCH):
            for par in range(2):
                u2_scr.at[c][pl.ds(2 * b + par, SSM_T, stride=V7X_SUBLANES), :] = (
                    u_ref[b, :, c * V7X_LANES:(c + 1) * V7X_LANES])
    row_par = lax.broadcasted_iota(jnp.int32, (SSM_ROWS2, V7X_LANES), 0) % 2
    pair_par = (lax.broadcasted_iota(jnp.int32, (SSM_ROWS2, V7X_LANES), 1) // (2 * SSM_GROUP)) % 2
    own_pair = row_par == pair_par

    for c in range(LANE_CHUNKS):
        bu_scr[c] = _dot(jnp.where(own_pair, u2_scr[c], 0.0).astype(bf16), wb_ref[c])

    half = LANE_CHUNKS // 2
    for c0 in (0, half):
        chains = [(c, s) for c in range(c0, c0 + half) for s in range(SLOTS)]

        def cols(s):
            base = s * 2 * V7X_LANES
            return slice(base, base + V7X_LANES), slice(base + V7X_LANES, base + 2 * V7X_LANES)

        lams = [(lam_ref[c, :, cols(s)[0]], lam_ref[c, :, cols(s)[1]]) for c, s in chains]
        init = tuple((st_scr[c, :, cols(s)[0]], st_scr[c, :, cols(s)[1]]) for c, s in chains)

        def body(t, carry, chains=chains, lams=lams, cols=cols, h_new=h_new):
            rows = pl.ds(t * V7X_SUBLANES, V7X_SUBLANES)
            out = []
            for (c, s), (l_re, l_im), (s_re, s_im) in zip(chains, lams, carry):
                cr, ci = cols(s)
                n_re = (l_re * s_re - l_im * s_im) + bu_scr[c, rows, cr]
                n_im = (l_re * s_im + l_im * s_re) + bu_scr[c, rows, ci]
                h_new[c, rows, cr] = n_re
                h_new[c, rows, ci] = n_im
                out.append((n_re, n_im))
            return tuple(out)

        fin = init
        for t in range(SSM_T):
            fin = body(t, fin)
        for (c, s), (s_re, s_im) in zip(chains, fin):
            st_scr[c, :, cols(s)[0]] = s_re
            st_scr[c, :, cols(s)[1]] = s_im

    for c in range(LANE_CHUNKS):
        cols_c = slice(c * V7X_LANES, (c + 1) * V7X_LANES)
        y2 = _dot(h_old[c].astype(bf16), wc_ref[c])
        y2_scr[2 * c] = y2[:, :V7X_LANES]
        y2_scr[2 * c + 1] = y2[:, V7X_LANES:]
        for b in range(BATCH):
            yc = (y2_scr.at[2 * c][pl.ds(2 * b, SSM_T, stride=V7X_SUBLANES), :]
                  + y2_scr.at[2 * c + 1][pl.ds(2 * b + 1, SSM_T, stride=V7X_SUBLANES), :])
            yc = yc + d_ref[:, cols_c] * uprev_ref[b, :, cols_c]
            y_scr[b * SSM_T:(b + 1) * SSM_T, cols_c] = _gelu_tanh(yc)
    y = y_scr[...]
    out = y * _sigmoid(_dot(y.astype(bf16), wglu_ref[...]))
    out = (out * _rms_scale(out) * g_ref[...]).astype(bf16)
    o_ref[...] = out.reshape(BATCH, SSM_T, SSM_WIDTH)


def _ssm(u, wb, lam, wc, d, wglu, ssm_out_g):
    n_chunks = SEQ // SSM_T
    return pl.pallas_call(
        _ssm_kernel,
        grid=(n_chunks + 1,),
        in_specs=[
            pl.BlockSpec((BATCH, SSM_T, SSM_WIDTH), lambda i: (0, jnp.minimum(i, n_chunks - 1), 0)),
            pl.BlockSpec((BATCH, SSM_T, SSM_WIDTH), lambda i: (0, jnp.maximum(i - 1, 0), 0)),
            _const_spec((LANE_CHUNKS, V7X_LANES, SLOT_COLS)),
            _const_spec((LANE_CHUNKS, V7X_SUBLANES, SLOT_COLS)),
            _const_spec((LANE_CHUNKS, SLOT_COLS, 2 * V7X_LANES)),
            _const_spec((1, SSM_WIDTH)),
            _const_spec((SSM_WIDTH, SSM_WIDTH)),
            _const_spec((1, SSM_WIDTH)),
        ],
        out_specs=pl.BlockSpec((BATCH, SSM_T, SSM_WIDTH), lambda i: (0, jnp.maximum(i - 1, 0), 0)),
        out_shape=jax.ShapeDtypeStruct((BATCH, SEQ, SSM_WIDTH), bf16),
        scratch_shapes=[
            pltpu.VMEM((LANE_CHUNKS, SSM_ROWS2, V7X_LANES), f32),
            pltpu.VMEM((LANE_CHUNKS, SSM_ROWS2, SLOT_COLS), f32),
            pltpu.VMEM((2, LANE_CHUNKS, SSM_ROWS2, SLOT_COLS), f32),
            pltpu.VMEM((2 * LANE_CHUNKS, SSM_ROWS2, V7X_LANES), f32),
            pltpu.VMEM((SSM_ROWS, SSM_WIDTH), f32),
            pltpu.VMEM((LANE_CHUNKS, V7X_SUBLANES, SLOT_COLS), f32),
        ],
        compiler_params=pltpu.CompilerParams(
            dimension_semantics=("arbitrary",), vmem_limit_bytes=VMEM_LIMIT),
        name="ssm",
    )(u, u, wb, lam, wc, d, wglu, ssm_out_g)


def _out_proj_kernel(a_ref, s_ref, x_ref, wa_ref, ws_ref, g_ref, o_ref, h_ref):
    x1 = x_ref[...] + _dot(a_ref[...], wa_ref[...]) + _dot(s_ref[...], ws_ref[...])
    o_ref[...] = x1
    h_ref[...] = (x1 * _rms_scale(x1) * g_ref[...]).astype(bf16)


def _out_proj(a, s, x, w, ln2_g):
    nt = SEQ // IN_TM
    row_spec = lambda w: pl.BlockSpec((None, IN_TM, w), lambda b, t: (b, t, 0))
    return pl.pallas_call(
        _out_proj_kernel,
        grid=(BATCH, nt),
        in_specs=[
            row_spec(ATTN_WIDTH),
            row_spec(SSM_WIDTH),
            row_spec(D_MODEL),
            pl.BlockSpec((ATTN_WIDTH, D_MODEL), lambda b, t: (0, 0)),
            pl.BlockSpec((SSM_WIDTH, D_MODEL), lambda b, t: (1, 0)),
            _const_spec((1, D_MODEL)),
        ],
        out_specs=[row_spec(D_MODEL), row_spec(D_MODEL)],
        out_shape=[jax.ShapeDtypeStruct((BATCH, SEQ, D_MODEL), f32),
                   jax.ShapeDtypeStruct((BATCH, SEQ, D_MODEL), bf16)],
        compiler_params=pltpu.CompilerParams(
            dimension_semantics=("arbitrary", "arbitrary"), vmem_limit_bytes=VMEM_LIMIT),
        name="out_proj",
    )(a, s, x, w, w, ln2_g)


def _ffn_kernel(x_hbm, h_ref, wg_ref, wu_ref, wd_ref, o_ref, act_scr, sem):
    f = pl.program_id(1)
    rows = pl.ds(pl.multiple_of(pl.program_id(0) * FFN_TM, FFN_TM), FFN_TM)
    residual_copy = pltpu.make_async_copy(x_hbm.at[rows, :], o_ref, sem)

    def activations(dst):
        h = h_ref[...]
        gate = _dot(h, wg_ref[...])
        up = _dot(h, wu_ref[...])
        dst[...] = (gate * _sigmoid(gate) * up).astype(bf16)

    def accumulate(src):
        o_ref[...] += _dot(src[...], wd_ref[...])

    @pl.when(f == 0)
    def _():
        residual_copy.start()
        activations(act_scr.at[0])

    @pl.when(f == 1)
    def _():
        residual_copy.wait()

    for parity in range(2):
        @pl.when((f > 0) & (f < FFN_STEPS) & (f % 2 == parity))
        def _(parity=parity):
            activations(act_scr.at[parity])
            accumulate(act_scr.at[1 - parity])

    @pl.when(f == FFN_STEPS)
    def _():
        accumulate(act_scr.at[(FFN_STEPS - 1) % 2])


def _ffn(x, h, wg, wu, wd):
    n_rows = BATCH * SEQ
    last = FFN_STEPS - 1
    return pl.pallas_call(
        _ffn_kernel,
        grid=(n_rows // FFN_TM, FFN_STEPS + 1),
        in_specs=[
            pl.BlockSpec(memory_space=pl.ANY),
            pl.BlockSpec((FFN_TM, D_MODEL), lambda i, f: (i, 0)),
            pl.BlockSpec((D_MODEL, FFN_TF), lambda i, f: (0, jnp.minimum(f, last))),
            pl.BlockSpec((D_MODEL, FFN_TF), lambda i, f: (0, jnp.minimum(f, last))),
            pl.BlockSpec((FFN_TF, D_MODEL), lambda i, f: (jnp.maximum(f - 1, 0), 0)),
        ],
        out_specs=pl.BlockSpec((FFN_TM, D_MODEL), lambda i, f: (i, 0)),
        out_shape=jax.ShapeDtypeStruct((n_rows, D_MODEL), f32),
        scratch_shapes=[pltpu.VMEM((2, FFN_TM, FFN_TF), bf16), pltpu.SemaphoreType.DMA(())],
        compiler_params=pltpu.CompilerParams(
            dimension_semantics=("arbitrary", "arbitrary"), vmem_limit_bytes=VMEM_LIMIT),
        name="ffn",
    )(x, h, wg, wu, wd)


def _t5_bucket(dist):
    n = np.maximum(dist, 0)
    max_exact = REL_BUCKETS // 2
    nf = np.maximum(n, 1).astype(np.float32)
    large = max_exact + (np.log(nf / max_exact) / math.log(REL_MAX_DISTANCE / max_exact)
                         * (REL_BUCKETS - max_exact)).astype(np.int32)
    large = np.minimum(large, REL_BUCKETS - 1)
    return np.where(n < max_exact, n, large).astype(np.int32)


def _bias_kernel(rb_ref, bucket_ref, o_ref):
    bucket = bucket_ref[...]
    from_prev = (lax.broadcasted_iota(jnp.int32, (BLOCK, BLOCK), 1)
                 > lax.broadcasted_iota(jnp.int32, (BLOCK, BLOCK), 0))
    for head in range(N_Q_HEADS):
        acc = jnp.zeros((BLOCK, BLOCK), f32)
        for k in range(REL_BUCKETS):
            acc = jnp.where(bucket == k, rb_ref[k, head], acc)
        o_ref[1, head] = acc
        o_ref[0, head] = jnp.where(from_prev, MASK_VALUE, acc)


def _bias_tables(rel_bias):
    qi = np.arange(BLOCK)[:, None]
    kj = np.arange(BLOCK)[None, :]
    from_prev = kj > qi
    dist = np.where(from_prev, qi + BLOCK - kj, qi - kj)
    assert ((dist >= 0) & (dist < WINDOW)).all()
    return pl.pallas_call(
        _bias_kernel,
        in_specs=[
            pl.BlockSpec(memory_space=pltpu.SMEM),
            pl.BlockSpec(memory_space=pltpu.VMEM),
        ],
        out_specs=pl.BlockSpec(memory_space=pltpu.VMEM),
        out_shape=jax.ShapeDtypeStruct((2, N_Q_HEADS, BLOCK, BLOCK), f32),
        name="bias_table",
    )(rel_bias.astype(f32), jnp.asarray(_t5_bucket(dist)))


def _ssm_params(a_re, a_im, log_dt, b_re, b_im, c_re, c_im):
    dt = jnp.exp(log_dt)[:, None]
    mag = jnp.exp(a_re * dt)
    ang = a_im * dt
    lb_re, lb_im = mag * jnp.cos(ang), mag * jnp.sin(ang)
    nr, ni = lb_re - 1.0, lb_im
    den = a_re * a_re + a_im * a_im
    f_re = (nr * a_re + ni * a_im) / den
    f_im = (ni * a_re - nr * a_im) / den
    bb_re = f_re[..., None] * b_re - f_im[..., None] * b_im
    bb_im = f_re[..., None] * b_im + f_im[..., None] * b_re

    split = lambda t: t.reshape((2, LANE_CHUNKS, SLOTS, 2, 2) + t.shape[2:])
    two = np.arange(2)
    bbs = split(jnp.stack([bb_re, bb_im]))
    k = np.arange(V7X_LANES)
    k_u, k_g = k // (V7X_LANES // SLOTS), (k // SSM_GROUP) % 2
    wb_mask = ((k_u[:, None, None] == two[None, :, None]) & (k_g[:, None, None] == two[None, None, :]))
    bt = jnp.transpose(bbs, (1, 2, 3, 4, 6, 0, 5)).reshape(LANE_CHUNKS, V7X_LANES, 1, 2, 1, SSM_STATE)
    wb = (bt * wb_mask.astype(np.float32)[:, :, None, :, None]).reshape(LANE_CHUNKS, V7X_LANES, SLOT_COLS)
    cs = split(jnp.stack([c_re, -c_im]))
    r = np.arange(SLOT_COLS)
    r_s, r_g = r // (SLOT_COLS // SLOTS), (r // SSM_STATE) % 2
    wc_mask = ((two[:, None, None, None] == two[None, None, :, None])[None]
               & (r_s[:, None, None, None, None] == two[None, None, :, None, None])
               & (r_g[:, None, None, None, None] == two[None, None, None, None, :]))
    ct = jnp.transpose(cs, (1, 2, 0, 4, 6, 3, 5)).reshape(LANE_CHUNKS, SLOT_COLS, 1, 1, 2, 1, SSM_GROUP)
    wc = (ct * wc_mask.astype(np.float32)[..., None]).reshape(LANE_CHUNKS, SLOT_COLS, 2 * V7X_LANES)
    lam = jnp.transpose(split(jnp.stack([lb_re, lb_im])), (1, 3, 2, 0, 4, 5)).reshape(LANE_CHUNKS, 1, 2, SLOT_COLS)
    lam = jnp.broadcast_to(lam, (LANE_CHUNKS, BATCH, 2, SLOT_COLS)).reshape(LANE_CHUNKS, V7X_SUBLANES, SLOT_COLS)
    return wb.astype(bf16), wc.astype(bf16), lam


def _layer(x, rel_bias, ln1_g, w_in, q_norm_g, k_norm_g, attn_sinks, ssm_a_re, ssm_a_im,
           ssm_log_dt, ssm_b_re, ssm_b_im, ssm_c_re, ssm_c_im, ssm_d, w_glu,
           attn_out_g, ssm_out_g, w_out, ln2_g, w_ff_gate, w_ff_up, w_ff_down):
    row = lambda v: v.reshape(1, -1).astype(f32)

    def regroup(t, axis):
        shape = t.shape[:axis] + (N_KV_HEADS, Q_PER_KV, HEAD_DIM) + t.shape[axis + 1:]
        return jnp.swapaxes(t.reshape(shape), axis, axis + 1).reshape(t.shape)

    gqk = row(jnp.concatenate([jnp.tile(q_norm_g, N_Q_HEADS) * (HEAD_DIM ** -0.5),
                               jnp.tile(k_norm_g, N_KV_HEADS)]))
    kv_mask = jnp.asarray(np.broadcast_to(
        (np.arange(KV_WIDTH)[None, None, :] // HEAD_DIM) == np.arange(N_KV_HEADS)[:, None, None],
        (N_KV_HEADS, BLOCK, KV_WIDTH)), bf16)

    q, k, v, u = _in_proj(x, row(ln1_g), w_in.astype(bf16), gqk)

    pos = np.arange(BLOCK)
    from_prev = pos[None, :] > pos[:, None]
    tri = jnp.asarray(np.stack([from_prev, ~from_prev]), bf16)
    y_attn, wg, wu, wd, wo = _attention(
        attn_sinks.astype(f32), q, k, v, _bias_tables(rel_bias), kv_mask, tri, row(regroup(attn_out_g, 0)),
        w_ff_gate.astype(f32), w_ff_up.astype(f32), w_ff_down.astype(f32), w_out.astype(f32))

    wb, wc, lam = _ssm_params(ssm_a_re.astype(f32), ssm_a_im.astype(f32), ssm_log_dt.astype(f32),
                                     ssm_b_re.astype(f32), ssm_b_im.astype(f32),
                                     ssm_c_re.astype(f32), ssm_c_im.astype(f32))
    y_ssm = _ssm(u, wb, lam, wc, row(ssm_d), w_glu.astype(bf16), row(ssm_out_g))

    x1, h2 = _out_proj(y_attn, y_ssm, x, wo, row(ln2_g))

    out = _ffn(x1.reshape(BATCH * SEQ, D_MODEL), h2.reshape(BATCH * SEQ, D_MODEL), wg, wu, wd)
    return out.reshape(BATCH, SEQ, D_MODEL)


def kernel(x, rel_bias, ln1_g, w_in, q_norm_g, k_norm_g, attn_sinks, ssm_a_re, ssm_a_im, ssm_log_dt, ssm_b_re, ssm_b_im, ssm_c_re, ssm_c_im, ssm_d, w_glu, attn_out_g, ssm_out_g, w_out, ln2_g, w_ff_gate, w_ff_up, w_ff_down):
    for l in range(ln1_g.shape[0]):
        x = _layer(x, rel_bias, ln1_g[l], w_in[l], q_norm_g[l], k_norm_g[l], attn_sinks[l],
                   ssm_a_re[l], ssm_a_im[l], ssm_log_dt[l], ssm_b_re[l], ssm_b_im[l],
                   ssm_c_re[l], ssm_c_im[l], ssm_d[l], w_glu[l], attn_out_g[l], ssm_out_g[l],
                   w_out[l], ln2_g[l], w_ff_gate[l], w_ff_up[l], w_ff_down[l])
    return x
```

```python
import functools
import math

import jax
import jax.numpy as jnp
import numpy as np
from jax import lax
from jax.experimental import pallas as pl
from jax.experimental.pallas import tpu as pltpu

D_MODEL = 2048
BATCH = 4
SEQ = 4096
HEAD_DIM = 64
N_Q_HEADS = 16
N_KV_HEADS = 4
Q_PER_KV = N_Q_HEADS // N_KV_HEADS
ATTN_WIDTH = N_Q_HEADS * HEAD_DIM
KV_WIDTH = N_KV_HEADS * HEAD_DIM
WINDOW = 128
BLOCK = 128
SSM_WIDTH = D_MODEL - ATTN_WIDTH
IN_WIDTH = ATTN_WIDTH + 2 * KV_WIDTH + SSM_WIDTH
SSM_GROUP = 16
SSM_GROUPS = SSM_WIDTH // SSM_GROUP
SSM_STATE = 64
FF_HIDDEN = 5632
REL_BUCKETS = 32
REL_MAX_DISTANCE = 128
EPS = 1e-6

V7X_LANES = 128
V7X_SUBLANES = 8
V7X_VMEM_BYTES = 64 * 1024 * 1024
VMEM_LIMIT = V7X_VMEM_BYTES - 8 * 1024 * 1024

MASK_VALUE = -1e30
KV_SLAB = Q_PER_KV * HEAD_DIM
LANE_CHUNKS = SSM_WIDTH // V7X_LANES
SLOTS = 2
SLOT_COLS = SLOTS * 2 * V7X_LANES

ATTN_BLOCKS = 8
ATTN_SLOTS = 4
IN_TM = 512
FFN_TM = 1024
FFN_TF = 512
FFN_STEPS = FF_HIDDEN // FFN_TF
SSM_T = 64
SSM_ROWS = SSM_T * BATCH
SSM_ROWS2 = 2 * SSM_ROWS

f32 = jnp.float32
bf16 = jnp.bfloat16


def _dot(a, b):
    return jnp.dot(a, b, preferred_element_type=f32)


def _rms_scale(x):
    return lax.rsqrt(jnp.mean(x * x, axis=-1, keepdims=True) + EPS)


def _in_proj_kernel(x_ref, g_ref, w_ref, gqk_ref, q_ref, k_ref, v_ref, u_ref, proj_scr):
    xf = x_ref[...]
    h = (xf * _rms_scale(xf) * g_ref[...]).astype(bf16)
    proj_scr[...] = _dot(h, w_ref[...])

    low = lax.broadcasted_iota(jnp.int32, (IN_TM, V7X_LANES), 1) < HEAD_DIM

    def head_norm(b):
        cols = slice(b * V7X_LANES, (b + 1) * V7X_LANES)
        xb = proj_scr[:, cols]
        sq = xb * xb
        r_lo = lax.rsqrt(jnp.sum(jnp.where(low, sq, 0.0), axis=-1, keepdims=True) * (1.0 / HEAD_DIM) + EPS)
        r_hi = lax.rsqrt(jnp.sum(jnp.where(low, 0.0, sq), axis=-1, keepdims=True) * (1.0 / HEAD_DIM) + EPS)
        return xb * jnp.where(low, r_lo, r_hi) * gqk_ref[:, cols]

    blocks_per_kv = KV_SLAB // V7X_LANES
    qn = [head_norm(b) for b in range(ATTN_WIDTH // V7X_LANES)]
    for g in range(Q_PER_KV):
        half = slice((g % 2) * HEAD_DIM, (g % 2 + 1) * HEAD_DIM)
        q_ref[:, g * KV_WIDTH:(g + 1) * KV_WIDTH] = jnp.concatenate(
            [qn[kh * blocks_per_kv + g // 2][:, half] for kh in range(N_KV_HEADS)], axis=1).astype(bf16)
    for b in range(KV_WIDTH // V7X_LANES):
        k_ref[:, b * V7X_LANES:(b + 1) * V7X_LANES] = head_norm(ATTN_WIDTH // V7X_LANES + b).astype(bf16)
    v_ref[...] = proj_scr[:, ATTN_WIDTH + KV_WIDTH:ATTN_WIDTH + 2 * KV_WIDTH].astype(bf16)
    u_ref[...] = proj_scr[:, ATTN_WIDTH + 2 * KV_WIDTH:]


def _const_spec(shape):
    nd = len(shape)
    return pl.BlockSpec(shape, lambda *_: (0,) * nd)


def _in_proj(x, ln1_g, w, gqk):
    nt = SEQ // IN_TM
    row_spec = lambda w: pl.BlockSpec((None, IN_TM, w), lambda b, t: (b, t, 0))
    return pl.pallas_call(
        _in_proj_kernel,
        grid=(BATCH, nt),
        in_specs=[
            row_spec(D_MODEL),
            _const_spec((1, D_MODEL)),
            _const_spec((D_MODEL, IN_WIDTH)),
            _const_spec((1, ATTN_WIDTH + KV_WIDTH)),
        ],
        out_specs=[row_spec(ATTN_WIDTH), row_spec(KV_WIDTH), row_spec(KV_WIDTH), row_spec(SSM_WIDTH)],
        out_shape=[
            jax.ShapeDtypeStruct((BATCH, SEQ, ATTN_WIDTH), bf16),
            jax.ShapeDtypeStruct((BATCH, SEQ, KV_WIDTH), bf16),
            jax.ShapeDtypeStruct((BATCH, SEQ, KV_WIDTH), bf16),
            jax.ShapeDtypeStruct((BATCH, SEQ, SSM_WIDTH), f32),
        ],
        scratch_shapes=[pltpu.VMEM((IN_TM, IN_WIDTH), f32)],
        compiler_params=pltpu.CompilerParams(
            dimension_semantics=("arbitrary", "arbitrary"), vmem_limit_bytes=VMEM_LIMIT),
        name="in_proj",
    )(x, ln1_g, w, gqk)


def _attn_kernel(n_heads, sink_ref, q_ref, kc_ref, kp_ref, vc_ref, vp_ref, bias_ref, kv_mask_ref, tri_ref,
                 g_ref, w0_ref, w1_ref, w2_ref, *refs):
    wo_refs, (o_ref, c0_ref, c1_ref, c2_ref, co_ref, y_scr, s_scr, p_scr) = refs[:n_heads], refs[n_heads:]
    for w_ref, c_ref in ((w0_ref, c0_ref), (w1_ref, c1_ref), (w2_ref, c2_ref)):
        c_ref[...] = w_ref[...].astype(bf16)
    for r, w_ref in enumerate(wo_refs):
        co_ref[r * HEAD_DIM:(r + 1) * HEAD_DIM, :] = w_ref[...].astype(bf16)

    qi = lax.broadcasted_iota(jnp.int32, (BLOCK, BLOCK), 0)
    kj = lax.broadcasted_iota(jnp.int32, (BLOCK, BLOCK), 1)
    from_prev = kj > qi
    first_table = jnp.minimum(pl.program_id(1), 1)
    for j in range(ATTN_BLOCKS):
        rows = slice(j * BLOCK, (j + 1) * BLOCK)
        prev_rows = slice((j - 1) * BLOCK, j * BLOCK)
        k_prev = kp_ref[...] if j == 0 else kc_ref[prev_rows, :]
        v_prev = vp_ref[...] if j == 0 else vc_ref[prev_rows, :]
        keys = jnp.concatenate([k_prev, kc_ref[rows, :]], axis=0)
        vals = jnp.concatenate([v_prev, vc_ref[rows, :]], axis=0)
        q16 = jnp.concatenate(
            [q_ref[rows, g * KV_WIDTH:(g + 1) * KV_WIDTH] * kv_mask_ref[kh]
             for kh in range(N_KV_HEADS) for g in range(Q_PER_KV)], axis=0)
        slot = j % ATTN_SLOTS
        s_scr[slot] = lax.dot_general(q16, keys, (((1,), (1,)), ((), ())), preferred_element_type=f32)
        for head in range(N_Q_HEADS):
            head_rows = slice(head * BLOCK, (head + 1) * BLOCK)
            bias = bias_ref[first_table, head] if j == 0 else bias_ref[1, head]
            s = jnp.where(from_prev, s_scr[slot, head_rows, :BLOCK], s_scr[slot, head_rows, BLOCK:]) + bias
            sink = sink_ref[head]
            m = jnp.max(s, axis=-1, keepdims=True)
            p = jnp.exp(s - m)
            den = jnp.sum(p, axis=-1, keepdims=True) + jnp.exp(sink - m)
            w = (p * (1.0 / den)).astype(bf16)
            p_scr[slot, head_rows, :BLOCK] = w * tri_ref[0]
            p_scr[slot, head_rows, BLOCK:] = w * tri_ref[1]
        o16 = _dot(p_scr[slot], vals)
        for head in range(N_Q_HEADS):
            kh, g = divmod(head, Q_PER_KV)
            y_scr[rows, g * KV_WIDTH + kh * HEAD_DIM:g * KV_WIDTH + (kh + 1) * HEAD_DIM] = (
                o16[head * BLOCK:(head + 1) * BLOCK, kh * HEAD_DIM:(kh + 1) * HEAD_DIM])
    y = y_scr[...]
    o_ref[...] = (y * _rms_scale(y) * g_ref[...]).astype(bf16)


def _attention(sinks, q, k, v, bias2, kv_mask, tri, attn_out_g, w_gate, w_up, w_down, w_out):
    rows = ATTN_BLOCKS * BLOCK
    n_blk = SEQ // rows
    steps = BATCH * n_blk
    step = lambda b, n: b * n_blk + n

    def slab(w):
        return pl.BlockSpec((w.shape[0] // steps, w.shape[1]), lambda b, n: (step(b, n), 0))

    heads_per_step = w_out.shape[0] // HEAD_DIM // steps

    def w_out_piece(r):
        def index(b, n):
            t = step(b, n) * heads_per_step + r
            return (jnp.where(t < N_Q_HEADS, (t % N_KV_HEADS) * Q_PER_KV + t // N_KV_HEADS, t), 0)
        return pl.BlockSpec((HEAD_DIM, D_MODEL), index)

    cast_in = [slab(w_gate), slab(w_up), slab(w_down)] + [w_out_piece(r) for r in range(heads_per_step)]
    cast_out = [slab(w_gate), slab(w_up), slab(w_down), slab(w_out)]
    cast_args = [w_gate, w_up, w_down] + [w_out] * heads_per_step
    cur = pl.BlockSpec((None, rows, ATTN_WIDTH), lambda b, n: (b, n, 0))
    kv_cur = pl.BlockSpec((None, rows, KV_WIDTH), lambda b, n: (b, n, 0))
    kv_prev = pl.BlockSpec((None, BLOCK, KV_WIDTH), lambda b, n: (b, jnp.maximum(n * ATTN_BLOCKS - 1, 0), 0))
    return pl.pallas_call(
        functools.partial(_attn_kernel, heads_per_step),
        grid=(BATCH, SEQ // rows),
        in_specs=[
            pl.BlockSpec(memory_space=pltpu.SMEM),
            cur, kv_cur, kv_prev, kv_cur, kv_prev,
            _const_spec((2, N_Q_HEADS, BLOCK, BLOCK)),
            _const_spec((N_KV_HEADS, BLOCK, KV_WIDTH)),
            _const_spec((2, BLOCK, BLOCK)),
            _const_spec((1, ATTN_WIDTH)),
        ] + cast_in,
        out_specs=[cur] + cast_out,
        out_shape=[jax.ShapeDtypeStruct((BATCH, SEQ, ATTN_WIDTH), bf16)]
        + [jax.ShapeDtypeStruct(w.shape, bf16) for w in (w_gate, w_up, w_down, w_out)],
        scratch_shapes=[
            pltpu.VMEM((rows, ATTN_WIDTH), f32),
            pltpu.VMEM((ATTN_SLOTS, N_Q_HEADS * BLOCK, 2 * BLOCK), f32),
            pltpu.VMEM((ATTN_SLOTS, N_Q_HEADS * BLOCK, 2 * BLOCK), bf16),
        ],
        compiler_params=pltpu.CompilerParams(
            dimension_semantics=("arbitrary", "arbitrary"), vmem_limit_bytes=VMEM_LIMIT),
        name="attn",
    )(sinks, q, k, k, v, v, bias2, kv_mask, tri, attn_out_g, *cast_args)


def _gelu_tanh(x):
    return 0.5 * x * (1.0 + jnp.tanh(math.sqrt(2.0 / math.pi) * (x + 0.044715 * (x * x * x))))


def _sigmoid(x):
    return 1.0 / (1.0 + jnp.exp(-x))


def _ssm_kernel(u_ref, uprev_ref, wb_ref, lam_ref, wc_ref, d_ref, wglu_ref, g_ref,
                o_ref, u2_scr, bu_scr, h_scr, y2_scr, y_scr, st_scr):
    step = pl.program_id(0)

    @pl.when(step == 0)
    def _():
        st_scr[...] = jnp.zeros_like(st_scr)
        h_scr[1] = jnp.zeros(h_scr.shape[1:], f32)

    for parity in range(2):
        pl.when(step % 2 == parity)(functools.partial(
            _ssm_step, u_ref, uprev_ref, wb_ref, lam_ref, wc_ref, d_ref, wglu_ref, g_ref, o_ref,
            u2_scr, bu_scr, h_scr.at[parity], h_scr.at[1 - parity], y2_scr, y_scr, st_scr))


def _ssm_step(u_ref, uprev_ref, wb_ref, lam_ref, wc_ref, d_ref, wglu_ref, g_ref, o_ref,
              u2_scr, bu_scr, h_new, h_old, y2_scr, y_scr, st_scr):
    for c in range(LANE_CHUNKS):
        for b in range(BATCH):
            for par in range(2):
                u2_scr.at[c][pl.ds(2 * b + par, SSM_T, stride=V7X_SUBLANES), :] = (
                    u_ref[b, :, c * V7X_LANES:(c + 1) * V7X_LANES])
    row_par = lax.broadcasted_iota(jnp.int32, (SSM_ROWS2, V7X_LANES), 0) % 2
    pair_par = (lax.broadcasted_iota(jnp.int32, (SSM_ROWS2, V7X_LANES), 1) // (2 * SSM_GROUP)) % 2
    own_pair = row_par == pair_par

    for c in range(LANE_CHUNKS):
        bu_scr[c] = _dot(jnp.where(own_pair, u2_scr[c], 0.0).astype(bf16), wb_ref[c])

    half = LANE_CHUNKS // 2
    for c0 in (0, half):
        chains = [(c, s) for c in range(c0, c0 + half) for s in range(SLOTS)]

        def cols(s):
            base = s * 2 * V7X_LANES
            return slice(base, base + V7X_LANES), slice(base + V7X_LANES, base + 2 * V7X_LANES)

        lams = [(lam_ref[c, :, cols(s)[0]], lam_ref[c, :, cols(s)[1]]) for c, s in chains]
        init = tuple((st_scr[c, :, cols(s)[0]], st_scr[c, :, cols(s)[1]]) for c, s in chains)

        def body(t, carry, chains=chains, lams=lams, cols=cols, h_new=h_new):
            rows = pl.ds(t * V7X_SUBLANES, V7X_SUBLANES)
            out = []
            for (c, s), (l_re, l_im), (s_re, s_im) in zip(chains, lams, carry):
                cr, ci = cols(s)
                n_re = (l_re * s_re - l_im * s_im) + bu_scr[c, rows, cr]
                n_im = (l_re * s_im + l_im * s_re) + bu_scr[c, rows, ci]
                h_new[c, rows, cr] = n_re
                h_new[c, rows, ci] = n_im
                out.append((n_re, n_im))
            return tuple(out)

        fin = init
        for t in range(SSM_T):
            fin = body(t, fin)
        for (c, s), (s_re, s_im) in zip(chains, fin):
            st_scr[c, :, cols(s)[0]] = s_re
            st_scr[c, :, cols(s)[1]] = s_im

    for c in range(LANE_CHUNKS):
        cols_c = slice(c * V7X_LANES, (c + 1) * V7X_LANES)
        y2 = _dot(h_old[c].astype(bf16), wc_ref[c])
        y2_scr[2 * c] = y2[:, :V7X_LANES]
        y2_scr[2 * c + 1] = y2[:, V7X_LANES:]
        for b in range(BATCH):
            yc = (y2_scr.at[2 * c][pl.ds(2 * b, SSM_T, stride=V7X_SUBLANES), :]
                  + y2_scr.at[2 * c + 1][pl.ds(2 * b + 1, SSM_T, stride=V7X_SUBLANES), :])
            yc = yc + d_ref[:, cols_c] * uprev_ref[b, :, cols_c]
            y_scr[b * SSM_T:(b + 1) * SSM_T, cols_c] = _gelu_tanh(yc)
    y = y_scr[...]
    out = y * _sigmoid(_dot(y.astype(bf16), wglu_ref[...]))
    out = (out * _rms_scale(out) * g_ref[...]).astype(bf16)
    o_ref[...] = out.reshape(BATCH, SSM_T, SSM_WIDTH)


def _ssm(u, wb, lam, wc, d, wglu, ssm_out_g):
    n_chunks = SEQ // SSM_T
    return pl.pallas_call(
        _ssm_kernel,
        grid=(n_chunks + 1,),
        in_specs=[
            pl.BlockSpec((BATCH, SSM_T, SSM_WIDTH), lambda i: (0, jnp.minimum(i, n_chunks - 1), 0)),
            pl.BlockSpec((BATCH, SSM_T, SSM_WIDTH), lambda i: (0, jnp.maximum(i - 1, 0), 0)),
            _const_spec((LANE_CHUNKS, V7X_LANES, SLOT_COLS)),
            _const_spec((LANE_CHUNKS, V7X_SUBLANES, SLOT_COLS)),
            _const_spec((LANE_CHUNKS, SLOT_COLS, 2 * V7X_LANES)),
            _const_spec((1, SSM_WIDTH)),
            _const_spec((SSM_WIDTH, SSM_WIDTH)),
            _const_spec((1, SSM_WIDTH)),
        ],
        out_specs=pl.BlockSpec((BATCH, SSM_T, SSM_WIDTH), lambda i: (0, jnp.maximum(i - 1, 0), 0)),
        out_shape=jax.ShapeDtypeStruct((BATCH, SEQ, SSM_WIDTH), bf16),
        scratch_shapes=[
            pltpu.VMEM((LANE_CHUNKS, SSM_ROWS2, V7X_LANES), f32),
            pltpu.VMEM((LANE_CHUNKS, SSM_ROWS2, SLOT_COLS), f32),
            pltpu.VMEM((2, LANE_CHUNKS, SSM_ROWS2, SLOT_COLS), f32),
            pltpu.VMEM((2 * LANE_CHUNKS, SSM_ROWS2, V7X_LANES), f32),
            pltpu.VMEM((SSM_ROWS, SSM_WIDTH), f32),
            pltpu.VMEM((LANE_CHUNKS, V7X_SUBLANES, SLOT_COLS), f32),
        ],
        compiler_params=pltpu.CompilerParams(
            dimension_semantics=("arbitrary",), vmem_limit_bytes=VMEM_LIMIT),
        name="ssm",
    )(u, u, wb, lam, wc, d, wglu, ssm_out_g)


def _out_proj_kernel(a_ref, s_ref, x_ref, wa_ref, ws_ref, g_ref, o_ref, h_ref):
    x1 = x_ref[...] + _dot(a_ref[...], wa_ref[...]) + _dot(s_ref[...], ws_ref[...])
    o_ref[...] = x1
    h_ref[...] = (x1 * _rms_scale(x1) * g_ref[...]).astype(bf16)


def _out_proj(a, s, x, w, ln2_g):
    nt = SEQ // IN_TM
    row_spec = lambda w: pl.BlockSpec((None, IN_TM, w), lambda b, t: (b, t, 0))
    return pl.pallas_call(
        _out_proj_kernel,
        grid=(BATCH, nt),
        in_specs=[
            row_spec(ATTN_WIDTH),
            row_spec(SSM_WIDTH),
            row_spec(D_MODEL),
            pl.BlockSpec((ATTN_WIDTH, D_MODEL), lambda b, t: (0, 0)),
            pl.BlockSpec((SSM_WIDTH, D_MODEL), lambda b, t: (1, 0)),
            _const_spec((1, D_MODEL)),
        ],
        out_specs=[row_spec(D_MODEL), row_spec(D_MODEL)],
        out_shape=[jax.ShapeDtypeStruct((BATCH, SEQ, D_MODEL), f32),
                   jax.ShapeDtypeStruct((BATCH, SEQ, D_MODEL), bf16)],
        compiler_params=pltpu.CompilerParams(
            dimension_semantics=("arbitrary", "arbitrary"), vmem_limit_bytes=VMEM_LIMIT),
        name="out_proj",
    )(a, s, x, w, w, ln2_g)


def _ffn_kernel(x_hbm, h_ref, wg_ref, wu_ref, wd_ref, o_ref, act_scr, res_scr, sem):
    f = pl.program_id(1)
    rows = pl.ds(pl.multiple_of(pl.program_id(0) * FFN_TM, FFN_TM), FFN_TM)
    residual_copy = pltpu.make_async_copy(x_hbm.at[rows, :], res_scr, sem)

    def activations(dst):
        h = h_ref[...]
        gate = _dot(h, wg_ref[...])
        up = _dot(h, wu_ref[...])
        dst[...] = (gate * _sigmoid(gate) * up).astype(bf16)

    def accumulate(src):
        o_ref[...] += _dot(src[...], wd_ref[...])

    @pl.when(f == 0)
    def _():
        o_ref[...] = jnp.zeros_like(o_ref)
        activations(act_scr.at[0])

    @pl.when(f == FFN_STEPS // 2)
    def _():
        residual_copy.start()

    for parity in range(2):
        @pl.when((f > 0) & (f < FFN_STEPS) & (f % 2 == parity))
        def _(parity=parity):
            activations(act_scr.at[parity])
            accumulate(act_scr.at[1 - parity])

    @pl.when(f == FFN_STEPS)
    def _():
        residual_copy.wait()
        o_ref[...] += _dot(act_scr[(FFN_STEPS - 1) % 2], wd_ref[...]) + res_scr[...]


def _ffn(x, h, wg, wu, wd):
    n_rows = BATCH * SEQ
    last = FFN_STEPS - 1
    return pl.pallas_call(
        _ffn_kernel,
        grid=(n_rows // FFN_TM, FFN_STEPS + 1),
        in_specs=[
            pl.BlockSpec(memory_space=pl.ANY),
            pl.BlockSpec((FFN_TM, D_MODEL), lambda i, f: (i, 0)),
            pl.BlockSpec((D_MODEL, FFN_TF), lambda i, f: (0, jnp.minimum(f, last))),
            pl.BlockSpec((D_MODEL, FFN_TF), lambda i, f: (0, jnp.minimum(f, last))),
            pl.BlockSpec((FFN_TF, D_MODEL), lambda i, f: (jnp.maximum(f - 1, 0), 0)),
        ],
        out_specs=pl.BlockSpec((FFN_TM, D_MODEL), lambda i, f: (i, 0)),
        out_shape=jax.ShapeDtypeStruct((n_rows, D_MODEL), f32),
        scratch_shapes=[pltpu.VMEM((2, FFN_TM, FFN_TF), bf16), pltpu.VMEM((FFN_TM, D_MODEL), f32),
                        pltpu.SemaphoreType.DMA(())],
        compiler_params=pltpu.CompilerParams(
            dimension_semantics=("arbitrary", "arbitrary"), vmem_limit_bytes=VMEM_LIMIT),
        name="ffn",
    )(x, h, wg, wu, wd)


def _t5_bucket(dist):
    n = np.maximum(dist, 0)
    max_exact = REL_BUCKETS // 2
    nf = np.maximum(n, 1).astype(np.float32)
    large = max_exact + (np.log(nf / max_exact) / math.log(REL_MAX_DISTANCE / max_exact)
                         * (REL_BUCKETS - max_exact)).astype(np.int32)
    large = np.minimum(large, REL_BUCKETS - 1)
    return np.where(n < max_exact, n, large).astype(np.int32)


def _bias_kernel(rb_ref, bucket_ref, o_ref):
    bucket = bucket_ref[...]
    from_prev = (lax.broadcasted_iota(jnp.int32, (BLOCK, BLOCK), 1)
                 > lax.broadcasted_iota(jnp.int32, (BLOCK, BLOCK), 0))
    for head in range(N_Q_HEADS):
        acc = jnp.zeros((BLOCK, BLOCK), f32)
        for k in range(REL_BUCKETS):
            acc = jnp.where(bucket == k, rb_ref[k, head], acc)
        o_ref[1, head] = acc
        o_ref[0, head] = jnp.where(from_prev, MASK_VALUE, acc)


def _bias_tables(rel_bias):
    qi = np.arange(BLOCK)[:, None]
    kj = np.arange(BLOCK)[None, :]
    from_prev = kj > qi
    dist = np.where(from_prev, qi + BLOCK - kj, qi - kj)
    assert ((dist >= 0) & (dist < WINDOW)).all()
    return pl.pallas_call(
        _bias_kernel,
        in_specs=[
            pl.BlockSpec(memory_space=pltpu.SMEM),
            pl.BlockSpec(memory_space=pltpu.VMEM),
        ],
        out_specs=pl.BlockSpec(memory_space=pltpu.VMEM),
        out_shape=jax.ShapeDtypeStruct((2, N_Q_HEADS, BLOCK, BLOCK), f32),
        name="bias_table",
    )(rel_bias.astype(f32), jnp.asarray(_t5_bucket(dist)))


def _ssm_params(a_re, a_im, log_dt, b_re, b_im, c_re, c_im):
    dt = jnp.exp(log_dt)[:, None]
    mag = jnp.exp(a_re * dt)
    ang = a_im * dt
    lb_re, lb_im = mag * jnp.cos(ang), mag * jnp.sin(ang)
    nr, ni = lb_re - 1.0, lb_im
    den = a_re * a_re + a_im * a_im
    f_re = (nr * a_re + ni * a_im) / den
    f_im = (ni * a_re - nr * a_im) / den
    bb_re = f_re[..., None] * b_re - f_im[..., None] * b_im
    bb_im = f_re[..., None] * b_im + f_im[..., None] * b_re

    split = lambda t: t.reshape((2, LANE_CHUNKS, SLOTS, 2, 2) + t.shape[2:])
    two = np.arange(2)
    bbs = split(jnp.stack([bb_re, bb_im]))
    k = np.arange(V7X_LANES)
    k_u, k_g = k // (V7X_LANES // SLOTS), (k // SSM_GROUP) % 2
    wb_mask = ((k_u[:, None, None] == two[None, :, None]) & (k_g[:, None, None] == two[None, None, :]))
    bt = jnp.transpose(bbs, (1, 2, 3, 4, 6, 0, 5)).reshape(LANE_CHUNKS, V7X_LANES, 1, 2, 1, SSM_STATE)
    wb = (bt * wb_mask.astype(np.float32)[:, :, None, :, None]).reshape(LANE_CHUNKS, V7X_LANES, SLOT_COLS)
    cs = split(jnp.stack([c_re, -c_im]))
    r = np.arange(SLOT_COLS)
    r_s, r_g = r // (SLOT_COLS // SLOTS), (r // SSM_STATE) % 2
    wc_mask = ((two[:, None, None, None] == two[None, None, :, None])[None]
               & (r_s[:, None, None, None, None] == two[None, None, :, None, None])
               & (r_g[:, None, None, None, None] == two[None, None, None, None, :]))
    ct = jnp.transpose(cs, (1, 2, 0, 4, 6, 3, 5)).reshape(LANE_CHUNKS, SLOT_COLS, 1, 1, 2, 1, SSM_GROUP)
    wc = (ct * wc_mask.astype(np.float32)[..., None]).reshape(LANE_CHUNKS, SLOT_COLS, 2 * V7X_LANES)
    lam = jnp.transpose(split(jnp.stack([lb_re, lb_im])), (1, 3, 2, 0, 4, 5)).reshape(LANE_CHUNKS, 1, 2, SLOT_COLS)
    lam = jnp.broadcast_to(lam, (LANE_CHUNKS, BATCH, 2, SLOT_COLS)).reshape(LANE_CHUNKS, V7X_SUBLANES, SLOT_COLS)
    return wb.astype(bf16), wc.astype(bf16), lam


def _layer(x, rel_bias, ln1_g, w_in, q_norm_g, k_norm_g, attn_sinks, ssm_a_re, ssm_a_im,
           ssm_log_dt, ssm_b_re, ssm_b_im, ssm_c_re, ssm_c_im, ssm_d, w_glu,
           attn_out_g, ssm_out_g, w_out, ln2_g, w_ff_gate, w_ff_up, w_ff_down):
    row = lambda v: v.reshape(1, -1).astype(f32)

    def regroup(t, axis):
        shape = t.shape[:axis] + (N_KV_HEADS, Q_PER_KV, HEAD_DIM) + t.shape[axis + 1:]
        return jnp.swapaxes(t.reshape(shape), axis, axis + 1).reshape(t.shape)

    gqk = row(jnp.concatenate([jnp.tile(q_norm_g, N_Q_HEADS) * (HEAD_DIM ** -0.5),
                               jnp.tile(k_norm_g, N_KV_HEADS)]))
    kv_mask = jnp.asarray(np.broadcast_to(
        (np.arange(KV_WIDTH)[None, None, :] // HEAD_DIM) == np.arange(N_KV_HEADS)[:, None, None],
        (N_KV_HEADS, BLOCK, KV_WIDTH)), bf16)

    q, k, v, u = _in_proj(x, row(ln1_g), w_in.astype(bf16), gqk)

    pos = np.arange(BLOCK)
    from_prev = pos[None, :] > pos[:, None]
    tri = jnp.asarray(np.stack([from_prev, ~from_prev]), bf16)
    y_attn, wg, wu, wd, wo = _attention(
        attn_sinks.astype(f32), q, k, v, _bias_tables(rel_bias), kv_mask, tri, row(regroup(attn_out_g, 0)),
        w_ff_gate.astype(f32), w_ff_up.astype(f32), w_ff_down.astype(f32), w_out.astype(f32))

    wb, wc, lam = _ssm_params(ssm_a_re.astype(f32), ssm_a_im.astype(f32), ssm_log_dt.astype(f32),
                                     ssm_b_re.astype(f32), ssm_b_im.astype(f32),
                                     ssm_c_re.astype(f32), ssm_c_im.astype(f32))
    y_ssm = _ssm(u, wb, lam, wc, row(ssm_d), w_glu.astype(bf16), row(ssm_out_g))

    x1, h2 = _out_proj(y_attn, y_ssm, x, wo, row(ln2_g))

    out = _ffn(x1.reshape(BATCH * SEQ, D_MODEL), h2.reshape(BATCH * SEQ, D_MODEL), wg, wu, wd)
    return out.reshape(BATCH, SEQ, D_MODEL)


def kernel(x, rel_bias, ln1_g, w_in, q_norm_g, k_norm_g, attn_sinks, ssm_a_re, ssm_a_im, ssm_log_dt, ssm_b_re, ssm_b_im, ssm_c_re, ssm_c_im, ssm_d, w_glu, attn_out_g, ssm_out_g, w_out, ln2_g, w_ff_gate, w_ff_up, w_ff_down):
    for l in range(ln1_g.shape[0]):
        x = _layer(x, rel_bias, ln1_g[l], w_in[l], q_norm_g[l], k_norm_g[l], attn_sinks[l],
                   ssm_a_re[l], ssm_a_im[l], ssm_log_dt[l], ssm_b_re[l], ssm_b_im[l],
                   ssm_c_re[l], ssm_c_im[l], ssm_d[l], w_glu[l], attn_out_g[l], ssm_out_g[l],
                   w_out[l], ln2_g[l], w_ff_gate[l], w_ff_up[l], w_ff_down[l])
    return x
```

```python
import functools
import math

import jax
import jax.numpy as jnp
import numpy as np
from jax import lax
from jax.experimental import pallas as pl
from jax.experimental.pallas import tpu as pltpu

D_MODEL = 2048
BATCH = 4
SEQ = 4096
HEAD_DIM = 64
N_Q_HEADS = 16
N_KV_HEADS = 4
Q_PER_KV = N_Q_HEADS // N_KV_HEADS
ATTN_WIDTH = N_Q_HEADS * HEAD_DIM
KV_WIDTH = N_KV_HEADS * HEAD_DIM
WINDOW = 128
BLOCK = 128
SSM_WIDTH = D_MODEL - ATTN_WIDTH
IN_WIDTH = ATTN_WIDTH + 2 * KV_WIDTH + SSM_WIDTH
SSM_GROUP = 16
SSM_GROUPS = SSM_WIDTH // SSM_GROUP
SSM_STATE = 64
FF_HIDDEN = 5632
REL_BUCKETS = 32
REL_MAX_DISTANCE = 128
EPS = 1e-6

V7X_LANES = 128
V7X_SUBLANES = 8
V7X_VMEM_BYTES = 64 * 1024 * 1024
VMEM_LIMIT = V7X_VMEM_BYTES - 8 * 1024 * 1024

MASK_VALUE = -1e30
KV_SLAB = Q_PER_KV * HEAD_DIM
LANE_CHUNKS = SSM_WIDTH // V7X_LANES
SLOTS = 2
SLOT_COLS = SLOTS * 2 * V7X_LANES

ATTN_BLOCKS = 8
ATTN_SLOTS = 4
IN_TM = 512
FFN_TM = 1024
FFN_TF = 512
FFN_STEPS = FF_HIDDEN // FFN_TF
SSM_T = 64
SSM_ROWS = SSM_T * BATCH
SSM_ROWS2 = 2 * SSM_ROWS

f32 = jnp.float32
bf16 = jnp.bfloat16


def _dot(a, b):
    return jnp.dot(a, b, preferred_element_type=f32)


def _rms_scale(x):
    return lax.rsqrt(jnp.mean(x * x, axis=-1, keepdims=True) + EPS)


def _in_proj_kernel(x_ref, g_ref, w_ref, gqk_ref, q_ref, k_ref, v_ref, u_ref, proj_scr):
    xf = x_ref[...]
    h = (xf * _rms_scale(xf) * g_ref[...]).astype(bf16)
    proj_scr[...] = _dot(h, w_ref[...])

    low = lax.broadcasted_iota(jnp.int32, (IN_TM, V7X_LANES), 1) < HEAD_DIM

    def head_norm(b):
        cols = slice(b * V7X_LANES, (b + 1) * V7X_LANES)
        xb = proj_scr[:, cols]
        sq = xb * xb
        r_lo = lax.rsqrt(jnp.sum(jnp.where(low, sq, 0.0), axis=-1, keepdims=True) * (1.0 / HEAD_DIM) + EPS)
        r_hi = lax.rsqrt(jnp.sum(jnp.where(low, 0.0, sq), axis=-1, keepdims=True) * (1.0 / HEAD_DIM) + EPS)
        return xb * jnp.where(low, r_lo, r_hi) * gqk_ref[:, cols]

    blocks_per_kv = KV_SLAB // V7X_LANES
    qn = [head_norm(b) for b in range(ATTN_WIDTH // V7X_LANES)]
    for g in range(Q_PER_KV):
        half = slice((g % 2) * HEAD_DIM, (g % 2 + 1) * HEAD_DIM)
        q_ref[:, g * KV_WIDTH:(g + 1) * KV_WIDTH] = jnp.concatenate(
            [qn[kh * blocks_per_kv + g // 2][:, half] for kh in range(N_KV_HEADS)], axis=1).astype(bf16)
    for b in range(KV_WIDTH // V7X_LANES):
        k_ref[:, b * V7X_LANES:(b + 1) * V7X_LANES] = head_norm(ATTN_WIDTH // V7X_LANES + b).astype(bf16)
    v_ref[...] = proj_scr[:, ATTN_WIDTH + KV_WIDTH:ATTN_WIDTH + 2 * KV_WIDTH].astype(bf16)
    u_ref[...] = proj_scr[:, ATTN_WIDTH + 2 * KV_WIDTH:]


def _const_spec(shape):
    nd = len(shape)
    return pl.BlockSpec(shape, lambda *_: (0,) * nd)


def _in_proj(x, ln1_g, w, gqk):
    nt = SEQ // IN_TM
    row_spec = lambda w: pl.BlockSpec((None, IN_TM, w), lambda b, t: (b, t, 0))
    return pl.pallas_call(
        _in_proj_kernel,
        grid=(BATCH, nt),
        in_specs=[
            row_spec(D_MODEL),
            _const_spec((1, D_MODEL)),
            _const_spec((D_MODEL, IN_WIDTH)),
            _const_spec((1, ATTN_WIDTH + KV_WIDTH)),
        ],
        out_specs=[row_spec(ATTN_WIDTH), row_spec(KV_WIDTH), row_spec(KV_WIDTH), row_spec(SSM_WIDTH)],
        out_shape=[
            jax.ShapeDtypeStruct((BATCH, SEQ, ATTN_WIDTH), bf16),
            jax.ShapeDtypeStruct((BATCH, SEQ, KV_WIDTH), bf16),
            jax.ShapeDtypeStruct((BATCH, SEQ, KV_WIDTH), bf16),
            jax.ShapeDtypeStruct((BATCH, SEQ, SSM_WIDTH), f32),
        ],
        scratch_shapes=[pltpu.VMEM((IN_TM, IN_WIDTH), f32)],
        compiler_params=pltpu.CompilerParams(
            dimension_semantics=("arbitrary", "arbitrary"), vmem_limit_bytes=VMEM_LIMIT),
        name="in_proj",
    )(x, ln1_g, w, gqk)


def _attn_kernel(n_heads, sink_ref, q_ref, kc_ref, kp_ref, vc_ref, vp_ref, bias_ref, kv_mask_ref, tri_ref,
                 g_ref, w0_ref, w1_ref, w2_ref, *refs):
    wo_refs, (o_ref, c0_ref, c1_ref, c2_ref, co_ref, y_scr, s_scr, p_scr) = refs[:n_heads], refs[n_heads:]
    for w_ref, c_ref in ((w0_ref, c0_ref), (w1_ref, c1_ref), (w2_ref, c2_ref)):
        c_ref[...] = w_ref[...].astype(bf16)
    for r, w_ref in enumerate(wo_refs):
        co_ref[r * HEAD_DIM:(r + 1) * HEAD_DIM, :] = w_ref[...].astype(bf16)

    qi = lax.broadcasted_iota(jnp.int32, (BLOCK, BLOCK), 0)
    kj = lax.broadcasted_iota(jnp.int32, (BLOCK, BLOCK), 1)
    from_prev = kj > qi
    first_table = jnp.minimum(pl.program_id(1), 1)
    for j in range(ATTN_BLOCKS):
        rows = slice(j * BLOCK, (j + 1) * BLOCK)
        prev_rows = slice((j - 1) * BLOCK, j * BLOCK)
        k_prev = kp_ref[...] if j == 0 else kc_ref[prev_rows, :]
        v_prev = vp_ref[...] if j == 0 else vc_ref[prev_rows, :]
        keys = jnp.concatenate([k_prev, kc_ref[rows, :]], axis=0)
        vals = jnp.concatenate([v_prev, vc_ref[rows, :]], axis=0)
        q16 = jnp.concatenate(
            [q_ref[rows, g * KV_WIDTH:(g + 1) * KV_WIDTH] * kv_mask_ref[kh]
             for kh in range(N_KV_HEADS) for g in range(Q_PER_KV)], axis=0)
        slot = j % ATTN_SLOTS
        s_scr[slot] = lax.dot_general(q16, keys, (((1,), (1,)), ((), ())), preferred_element_type=f32)
        for head in range(N_Q_HEADS):
            head_rows = slice(head * BLOCK, (head + 1) * BLOCK)
            bias = bias_ref[first_table, head] if j == 0 else bias_ref[1, head]
            s = jnp.where(from_prev, s_scr[slot, head_rows, :BLOCK], s_scr[slot, head_rows, BLOCK:]) + bias
            sink = sink_ref[head]
            m = jnp.max(s, axis=-1, keepdims=True)
            p = jnp.exp(s - m)
            den = jnp.sum(p, axis=-1, keepdims=True) + jnp.exp(sink - m)
            w = (p * (1.0 / den)).astype(bf16)
            p_scr[slot, head_rows, :BLOCK] = w * tri_ref[0]
            p_scr[slot, head_rows, BLOCK:] = w * tri_ref[1]
        o16 = _dot(p_scr[slot], vals)
        for head in range(N_Q_HEADS):
            kh, g = divmod(head, Q_PER_KV)
            y_scr[rows, g * KV_WIDTH + kh * HEAD_DIM:g * KV_WIDTH + (kh + 1) * HEAD_DIM] = (
                o16[head * BLOCK:(head + 1) * BLOCK, kh * HEAD_DIM:(kh + 1) * HEAD_DIM])
    y = y_scr[...]
    o_ref[...] = (y * _rms_scale(y) * g_ref[...]).astype(bf16)


def _attention(sinks, q, k, v, bias2, kv_mask, tri, attn_out_g, w_gate, w_up, w_down, w_out):
    rows = ATTN_BLOCKS * BLOCK
    n_blk = SEQ // rows
    steps = BATCH * n_blk
    step = lambda b, n: b * n_blk + n

    def slab(w):
        return pl.BlockSpec((w.shape[0] // steps, w.shape[1]), lambda b, n: (step(b, n), 0))

    heads_per_step = w_out.shape[0] // HEAD_DIM // steps

    def w_out_piece(r):
        def index(b, n):
            t = step(b, n) * heads_per_step + r
            return (jnp.where(t < N_Q_HEADS, (t % N_KV_HEADS) * Q_PER_KV + t // N_KV_HEADS, t), 0)
        return pl.BlockSpec((HEAD_DIM, D_MODEL), index)

    cast_in = [slab(w_gate), slab(w_up), slab(w_down)] + [w_out_piece(r) for r in range(heads_per_step)]
    cast_out = [slab(w_gate), slab(w_up), slab(w_down), slab(w_out)]
    cast_args = [w_gate, w_up, w_down] + [w_out] * heads_per_step
    cur = pl.BlockSpec((None, rows, ATTN_WIDTH), lambda b, n: (b, n, 0))
    kv_cur = pl.BlockSpec((None, rows, KV_WIDTH), lambda b, n: (b, n, 0))
    kv_prev = pl.BlockSpec((None, BLOCK, KV_WIDTH), lambda b, n: (b, jnp.maximum(n * ATTN_BLOCKS - 1, 0), 0))
    return pl.pallas_call(
        functools.partial(_attn_kernel, heads_per_step),
        grid=(BATCH, SEQ // rows),
        in_specs=[
            pl.BlockSpec(memory_space=pltpu.SMEM),
            cur, kv_cur, kv_prev, kv_cur, kv_prev,
            _const_spec((2, N_Q_HEADS, BLOCK, BLOCK)),
            _const_spec((N_KV_HEADS, BLOCK, KV_WIDTH)),
            _const_spec((2, BLOCK, BLOCK)),
            _const_spec((1, ATTN_WIDTH)),
        ] + cast_in,
        out_specs=[cur] + cast_out,
        out_shape=[jax.ShapeDtypeStruct((BATCH, SEQ, ATTN_WIDTH), bf16)]
        + [jax.ShapeDtypeStruct(w.shape, bf16) for w in (w_gate, w_up, w_down, w_out)],
        scratch_shapes=[
            pltpu.VMEM((rows, ATTN_WIDTH), f32),
            pltpu.VMEM((ATTN_SLOTS, N_Q_HEADS * BLOCK, 2 * BLOCK), f32),
            pltpu.VMEM((ATTN_SLOTS, N_Q_HEADS * BLOCK, 2 * BLOCK), bf16),
        ],
        compiler_params=pltpu.CompilerParams(
            dimension_semantics=("arbitrary", "arbitrary"), vmem_limit_bytes=VMEM_LIMIT),
        name="attn",
    )(sinks, q, k, k, v, v, bias2, kv_mask, tri, attn_out_g, *cast_args)


def _gelu_tanh(x):
    return 0.5 * x * (1.0 + jnp.tanh(math.sqrt(2.0 / math.pi) * (x + 0.044715 * (x * x * x))))


def _sigmoid(x):
    return 1.0 / (1.0 + jnp.exp(-x))


def _ssm_kernel(u_ref, uprev_ref, wb_ref, lam_ref, wc_ref, d_ref, wglu_ref, g_ref,
                o_ref, u2_scr, bu_scr, h_scr, y2_scr, y_scr, st_scr):
    step = pl.program_id(0)

    @pl.when(step == 0)
    def _():
        st_scr[...] = jnp.zeros_like(st_scr)
        h_scr[1] = jnp.zeros(h_scr.shape[1:], f32)

    for parity in range(2):
        pl.when(step % 2 == parity)(functools.partial(
            _ssm_step, u_ref, uprev_ref, wb_ref, lam_ref, wc_ref, d_ref, wglu_ref, g_ref, o_ref,
            u2_scr, bu_scr, h_scr.at[parity], h_scr.at[1 - parity], y2_scr, y_scr, st_scr))


def _ssm_step(u_ref, uprev_ref, wb_ref, lam_ref, wc_ref, d_ref, wglu_ref, g_ref, o_ref,
              u2_scr, bu_scr, h_new, h_old, y2_scr, y_scr, st_scr):
    for c in range(LANE_CHUNKS):
        for b in range(BATCH):
            for par in range(2):
                u2_scr.at[c][pl.ds(2 * b + par, SSM_T, stride=V7X_SUBLANES), :] = (
                    u_ref[b, :, c * V7X_LANES:(c + 1) * V7X_LANES])
    row_par = lax.broadcasted_iota(jnp.int32, (SSM_ROWS2, V7X_LANES), 0) % 2
    pair_par = (lax.broadcasted_iota(jnp.int32, (SSM_ROWS2, V7X_LANES), 1) // (2 * SSM_GROUP)) % 2
    own_pair = row_par == pair_par

    for c in range(LANE_CHUNKS):
        bu_scr[c] = _dot(jnp.where(own_pair, u2_scr[c], 0.0).astype(bf16), wb_ref[c])

    half = LANE_CHUNKS // 2
    for c0 in (0, half):
        chains = [(c, s) for c in range(c0, c0 + half) for s in range(SLOTS)]

        def cols(s):
            base = s * 2 * V7X_LANES
            return slice(base, base + V7X_LANES), slice(base + V7X_LANES, base + 2 * V7X_LANES)

        lams = [(lam_ref[c, :, cols(s)[0]], lam_ref[c, :, cols(s)[1]]) for c, s in chains]
        init = tuple((st_scr[c, :, cols(s)[0]], st_scr[c, :, cols(s)[1]]) for c, s in chains)

        def body(t, carry, chains=chains, lams=lams, cols=cols, h_new=h_new):
            rows = pl.ds(t * V7X_SUBLANES, V7X_SUBLANES)
            out = []
            for (c, s), (l_re, l_im), (s_re, s_im) in zip(chains, lams, carry):
                cr, ci = cols(s)
                n_re = (l_re * s_re - l_im * s_im) + bu_scr[c, rows, cr]
                n_im = (l_re * s_im + l_im * s_re) + bu_scr[c, rows, ci]
                h_new[c, rows, cr] = n_re
                h_new[c, rows, ci] = n_im
                out.append((n_re, n_im))
            return tuple(out)

        fin = init
        for t in range(SSM_T):
            fin = body(t, fin)
        for (c, s), (s_re, s_im) in zip(chains, fin):
            st_scr[c, :, cols(s)[0]] = s_re
            st_scr[c, :, cols(s)[1]] = s_im

    for c in range(LANE_CHUNKS):
        cols_c = slice(c * V7X_LANES, (c + 1) * V7X_LANES)
        y2 = _dot(h_old[c].astype(bf16), wc_ref[c])
        y2_scr[2 * c] = y2[:, :V7X_LANES]
        y2_scr[2 * c + 1] = y2[:, V7X_LANES:]
        for b in range(BATCH):
            yc = (y2_scr.at[2 * c][pl.ds(2 * b, SSM_T, stride=V7X_SUBLANES), :]
                  + y2_scr.at[2 * c + 1][pl.ds(2 * b + 1, SSM_T, stride=V7X_SUBLANES), :])
            yc = yc + d_ref[:, cols_c] * uprev_ref[b, :, cols_c]
            y_scr[b * SSM_T:(b + 1) * SSM_T, cols_c] = _gelu_tanh(yc)
    y = y_scr[...]
    out = y * _sigmoid(_dot(y.astype(bf16), wglu_ref[...]))
    out = (out * _rms_scale(out) * g_ref[...]).astype(bf16)
    o_ref[...] = out.reshape(BATCH, SSM_T, SSM_WIDTH)


def _ssm(u, wb, lam, wc, d, wglu, ssm_out_g):
    n_chunks = SEQ // SSM_T
    return pl.pallas_call(
        _ssm_kernel,
        grid=(n_chunks + 1,),
        in_specs=[
            pl.BlockSpec((BATCH, SSM_T, SSM_WIDTH), lambda i: (0, jnp.minimum(i, n_chunks - 1), 0)),
            pl.BlockSpec((BATCH, SSM_T, SSM_WIDTH), lambda i: (0, jnp.maximum(i - 1, 0), 0)),
            _const_spec((LANE_CHUNKS, V7X_LANES, SLOT_COLS)),
            _const_spec((LANE_CHUNKS, V7X_SUBLANES, SLOT_COLS)),
            _const_spec((LANE_CHUNKS, SLOT_COLS, 2 * V7X_LANES)),
            _const_spec((1, SSM_WIDTH)),
            _const_spec((SSM_WIDTH, SSM_WIDTH)),
            _const_spec((1, SSM_WIDTH)),
        ],
        out_specs=pl.BlockSpec((BATCH, SSM_T, SSM_WIDTH), lambda i: (0, jnp.maximum(i - 1, 0), 0)),
        out_shape=jax.ShapeDtypeStruct((BATCH, SEQ, SSM_WIDTH), bf16),
        scratch_shapes=[
            pltpu.VMEM((LANE_CHUNKS, SSM_ROWS2, V7X_LANES), f32),
            pltpu.VMEM((LANE_CHUNKS, SSM_ROWS2, SLOT_COLS), f32),
            pltpu.VMEM((2, LANE_CHUNKS, SSM_ROWS2, SLOT_COLS), f32),
            pltpu.VMEM((2 * LANE_CHUNKS, SSM_ROWS2, V7X_LANES), f32),
            pltpu.VMEM((SSM_ROWS, SSM_WIDTH), f32),
            pltpu.VMEM((LANE_CHUNKS, V7X_SUBLANES, SLOT_COLS), f32),
        ],
        compiler_params=pltpu.CompilerParams(
            dimension_semantics=("arbitrary",), vmem_limit_bytes=VMEM_LIMIT),
        name="ssm",
    )(u, u, wb, lam, wc, d, wglu, ssm_out_g)


def _out_proj_kernel(a_ref, s_ref, x_ref, wa_ref, ws_ref, g_ref, o_ref, h_ref):
    x1 = x_ref[...] + _dot(a_ref[...], wa_ref[...]) + _dot(s_ref[...], ws_ref[...])
    o_ref[...] = x1
    h_ref[...] = (x1 * _rms_scale(x1) * g_ref[...]).astype(bf16)


def _out_proj(a, s, x, w, ln2_g):
    nt = SEQ // IN_TM
    row_spec = lambda w: pl.BlockSpec((None, IN_TM, w), lambda b, t: (b, t, 0))
    return pl.pallas_call(
        _out_proj_kernel,
        grid=(BATCH, nt),
        in_specs=[
            row_spec(ATTN_WIDTH),
            row_spec(SSM_WIDTH),
            row_spec(D_MODEL),
            pl.BlockSpec((ATTN_WIDTH, D_MODEL), lambda b, t: (0, 0)),
            pl.BlockSpec((SSM_WIDTH, D_MODEL), lambda b, t: (1, 0)),
            _const_spec((1, D_MODEL)),
        ],
        out_specs=[row_spec(D_MODEL), row_spec(D_MODEL)],
        out_shape=[jax.ShapeDtypeStruct((BATCH, SEQ, D_MODEL), f32),
                   jax.ShapeDtypeStruct((BATCH, SEQ, D_MODEL), bf16)],
        compiler_params=pltpu.CompilerParams(
            dimension_semantics=("arbitrary", "arbitrary"), vmem_limit_bytes=VMEM_LIMIT),
        name="out_proj",
    )(a, s, x, w, w, ln2_g)


def _ffn_kernel(x_hbm, h_ref, wg_ref, wu_ref, wd_ref, o_ref, act_scr, res_scr, sem):
    f = pl.program_id(1)
    rows = pl.ds(pl.multiple_of(pl.program_id(0) * FFN_TM, FFN_TM), FFN_TM)
    residual_copy = pltpu.make_async_copy(x_hbm.at[rows, :], res_scr, sem)

    def activations(dst):
        h = h_ref[...]
        gate = _dot(h, wg_ref[...])
        up = _dot(h, wu_ref[...])
        dst[...] = (gate * _sigmoid(gate) * up).astype(bf16)

    def accumulate(src):
        o_ref[...] += _dot(src[...], wd_ref[...])

    @pl.when(f == 0)
    def _():
        activations(act_scr.at[0])

    @pl.when(f == 1)
    def _():
        activations(act_scr.at[1])
        o_ref[...] = _dot(act_scr[0], wd_ref[...])

    @pl.when(f == FFN_STEPS // 2)
    def _():
        residual_copy.start()

    for parity in range(2):
        @pl.when((f > 1) & (f < FFN_STEPS) & (f % 2 == parity))
        def _(parity=parity):
            activations(act_scr.at[parity])
            accumulate(act_scr.at[1 - parity])

    @pl.when(f == FFN_STEPS)
    def _():
        residual_copy.wait()
        o_ref[...] += _dot(act_scr[(FFN_STEPS - 1) % 2], wd_ref[...]) + res_scr[...]


def _ffn(x, h, wg, wu, wd):
    n_rows = BATCH * SEQ
    last = FFN_STEPS - 1
    return pl.pallas_call(
        _ffn_kernel,
        grid=(n_rows // FFN_TM, FFN_STEPS + 1),
        in_specs=[
            pl.BlockSpec(memory_space=pl.ANY),
            pl.BlockSpec((FFN_TM, D_MODEL), lambda i, f: (i, 0)),
            pl.BlockSpec((D_MODEL, FFN_TF), lambda i, f: (0, jnp.minimum(f, last))),
            pl.BlockSpec((D_MODEL, FFN_TF), lambda i, f: (0, jnp.minimum(f, last))),
            pl.BlockSpec((FFN_TF, D_MODEL), lambda i, f: (jnp.maximum(f - 1, 0), 0)),
        ],
        out_specs=pl.BlockSpec((FFN_TM, D_MODEL), lambda i, f: (i, 0)),
        out_shape=jax.ShapeDtypeStruct((n_rows, D_MODEL), f32),
        scratch_shapes=[pltpu.VMEM((2, FFN_TM, FFN_TF), bf16), pltpu.VMEM((FFN_TM, D_MODEL), f32),
                        pltpu.SemaphoreType.DMA(())],
        compiler_params=pltpu.CompilerParams(
            dimension_semantics=("arbitrary", "arbitrary"), vmem_limit_bytes=VMEM_LIMIT),
        name="ffn",
    )(x, h, wg, wu, wd)


def _t5_bucket(dist):
    n = np.maximum(dist, 0)
    max_exact = REL_BUCKETS // 2
    nf = np.maximum(n, 1).astype(np.float32)
    large = max_exact + (np.log(nf / max_exact) / math.log(REL_MAX_DISTANCE / max_exact)
                         * (REL_BUCKETS - max_exact)).astype(np.int32)
    large = np.minimum(large, REL_BUCKETS - 1)
    return np.where(n < max_exact, n, large).astype(np.int32)


def _bias_kernel(rb_ref, bucket_ref, o_ref):
    bucket = bucket_ref[...]
    from_prev = (lax.broadcasted_iota(jnp.int32, (BLOCK, BLOCK), 1)
                 > lax.broadcasted_iota(jnp.int32, (BLOCK, BLOCK), 0))
    for head in range(N_Q_HEADS):
        acc = jnp.zeros((BLOCK, BLOCK), f32)
        for k in range(REL_BUCKETS):
            acc = jnp.where(bucket == k, rb_ref[k, head], acc)
        o_ref[1, head] = acc
        o_ref[0, head] = jnp.where(from_prev, MASK_VALUE, acc)


def _bias_tables(rel_bias):
    qi = np.arange(BLOCK)[:, None]
    kj = np.arange(BLOCK)[None, :]
    from_prev = kj > qi
    dist = np.where(from_prev, qi + BLOCK - kj, qi - kj)
    assert ((dist >= 0) & (dist < WINDOW)).all()
    return pl.pallas_call(
        _bias_kernel,
        in_specs=[
            pl.BlockSpec(memory_space=pltpu.SMEM),
            pl.BlockSpec(memory_space=pltpu.VMEM),
        ],
        out_specs=pl.BlockSpec(memory_space=pltpu.VMEM),
        out_shape=jax.ShapeDtypeStruct((2, N_Q_HEADS, BLOCK, BLOCK), f32),
        name="bias_table",
    )(rel_bias.astype(f32), jnp.asarray(_t5_bucket(dist)))


def _ssm_params(a_re, a_im, log_dt, b_re, b_im, c_re, c_im):
    dt = jnp.exp(log_dt)[:, None]
    mag = jnp.exp(a_re * dt)
    ang = a_im * dt
    lb_re, lb_im = mag * jnp.cos(ang), mag * jnp.sin(ang)
    nr, ni = lb_re - 1.0, lb_im
    den = a_re * a_re + a_im * a_im
    f_re = (nr * a_re + ni * a_im) / den
    f_im = (ni * a_re - nr * a_im) / den
    bb_re = f_re[..., None] * b_re - f_im[..., None] * b_im
    bb_im = f_re[..., None] * b_im + f_im[..., None] * b_re

    split = lambda t: t.reshape((2, LANE_CHUNKS, SLOTS, 2, 2) + t.shape[2:])
    two = np.arange(2)
    bbs = split(jnp.stack([bb_re, bb_im]))
    k = np.arange(V7X_LANES)
    k_u, k_g = k // (V7X_LANES // SLOTS), (k // SSM_GROUP) % 2
    wb_mask = ((k_u[:, None, None] == two[None, :, None]) & (k_g[:, None, None] == two[None, None, :]))
    bt = jnp.transpose(bbs, (1, 2, 3, 4, 6, 0, 5)).reshape(LANE_CHUNKS, V7X_LANES, 1, 2, 1, SSM_STATE)
    wb = (bt * wb_mask.astype(np.float32)[:, :, None, :, None]).reshape(LANE_CHUNKS, V7X_LANES, SLOT_COLS)
    cs = split(jnp.stack([c_re, -c_im]))
    r = np.arange(SLOT_COLS)
    r_s, r_g = r // (SLOT_COLS // SLOTS), (r // SSM_STATE) % 2
    wc_mask = ((two[:, None, None, None] == two[None, None, :, None])[None]
               & (r_s[:, None, None, None, None] == two[None, None, :, None, None])
               & (r_g[:, None, None, None, None] == two[None, None, None, None, :]))
    ct = jnp.transpose(cs, (1, 2, 0, 4, 6, 3, 5)).reshape(LANE_CHUNKS, SLOT_COLS, 1, 1, 2, 1, SSM_GROUP)
    wc = (ct * wc_mask.astype(np.float32)[..., None]).reshape(LANE_CHUNKS, SLOT_COLS, 2 * V7X_LANES)
    lam = jnp.transpose(split(jnp.stack([lb_re, lb_im])), (1, 3, 2, 0, 4, 5)).reshape(LANE_CHUNKS, 1, 2, SLOT_COLS)
    lam = jnp.broadcast_to(lam, (LANE_CHUNKS, BATCH, 2, SLOT_COLS)).reshape(LANE_CHUNKS, V7X_SUBLANES, SLOT_COLS)
    return wb.astype(bf16), wc.astype(bf16), lam


def _layer(x, rel_bias, ln1_g, w_in, q_norm_g, k_norm_g, attn_sinks, ssm_a_re, ssm_a_im,
           ssm_log_dt, ssm_b_re, ssm_b_im, ssm_c_re, ssm_c_im, ssm_d, w_glu,
           attn_out_g, ssm_out_g, w_out, ln2_g, w_ff_gate, w_ff_up, w_ff_down):
    row = lambda v: v.reshape(1, -1).astype(f32)

    def regroup(t, axis):
        shape = t.shape[:axis] + (N_KV_HEADS, Q_PER_KV, HEAD_DIM) + t.shape[axis + 1:]
        return jnp.swapaxes(t.reshape(shape), axis, axis + 1).reshape(t.shape)

    gqk = row(jnp.concatenate([jnp.tile(q_norm_g, N_Q_HEADS) * (HEAD_DIM ** -0.5),
                               jnp.tile(k_norm_g, N_KV_HEADS)]))
    kv_mask = jnp.asarray(np.broadcast_to(
        (np.arange(KV_WIDTH)[None, None, :] // HEAD_DIM) == np.arange(N_KV_HEADS)[:, None, None],
        (N_KV_HEADS, BLOCK, KV_WIDTH)), bf16)

    q, k, v, u = _in_proj(x, row(ln1_g), w_in.astype(bf16), gqk)

    pos = np.arange(BLOCK)
    from_prev = pos[None, :] > pos[:, None]
    tri = jnp.asarray(np.stack([from_prev, ~from_prev]), bf16)
    y_attn, wg, wu, wd, wo = _attention(
        attn_sinks.astype(f32), q, k, v, _bias_tables(rel_bias), kv_mask, tri, row(regroup(attn_out_g, 0)),
        w_ff_gate.astype(f32), w_ff_up.astype(f32), w_ff_down.astype(f32), w_out.astype(f32))

    wb, wc, lam = _ssm_params(ssm_a_re.astype(f32), ssm_a_im.astype(f32), ssm_log_dt.astype(f32),
                                     ssm_b_re.astype(f32), ssm_b_im.astype(f32),
                                     ssm_c_re.astype(f32), ssm_c_im.astype(f32))
    y_ssm = _ssm(u, wb, lam, wc, row(ssm_d), w_glu.astype(bf16), row(ssm_out_g))

    x1, h2 = _out_proj(y_attn, y_ssm, x, wo, row(ln2_g))

    out = _ffn(x1.reshape(BATCH * SEQ, D_MODEL), h2.reshape(BATCH * SEQ, D_MODEL), wg, wu, wd)
    return out.reshape(BATCH, SEQ, D_MODEL)


def kernel(x, rel_bias, ln1_g, w_in, q_norm_g, k_norm_g, attn_sinks, ssm_a_re, ssm_a_im, ssm_log_dt, ssm_b_re, ssm_b_im, ssm_c_re, ssm_c_im, ssm_d, w_glu, attn_out_g, ssm_out_g, w_out, ln2_g, w_ff_gate, w_ff_up, w_ff_down):
    for l in range(ln1_g.shape[0]):
        x = _layer(x, rel_bias, ln1_g[l], w_in[l], q_norm_g[l], k_norm_g[l], attn_sinks[l],
                   ssm_a_re[l], ssm_a_im[l], ssm_log_dt[l], ssm_b_re[l], ssm_b_im[l],
                   ssm_c_re[l], ssm_c_im[l], ssm_d[l], w_glu[l], attn_out_g[l], ssm_out_g[l],
                   w_out[l], ln2_g[l], w_ff_gate[l], w_ff_up[l], w_ff_down[l])
    return x
```

```python
import functools
import math

import jax
import jax.numpy as jnp
import numpy as np
from jax import lax
from jax.experimental import pallas as pl
from jax.experimental.pallas import tpu as pltpu

D_MODEL = 2048
BATCH = 4
SEQ = 4096
HEAD_DIM = 64
N_Q_HEADS = 16
N_KV_HEADS = 4
Q_PER_KV = N_Q_HEADS // N_KV_HEADS
ATTN_WIDTH = N_Q_HEADS * HEAD_DIM
KV_WIDTH = N_KV_HEADS * HEAD_DIM
WINDOW = 128
BLOCK = 128
SSM_WIDTH = D_MODEL - ATTN_WIDTH
IN_WIDTH = ATTN_WIDTH + 2 * KV_WIDTH + SSM_WIDTH
SSM_GROUP = 16
SSM_GROUPS = SSM_WIDTH // SSM_GROUP
SSM_STATE = 64
FF_HIDDEN = 5632
REL_BUCKETS = 32
REL_MAX_DISTANCE = 128
EPS = 1e-6

V7X_LANES = 128
V7X_SUBLANES = 8
V7X_VMEM_BYTES = 64 * 1024 * 1024
VMEM_LIMIT = V7X_VMEM_BYTES - 8 * 1024 * 1024

MASK_VALUE = -1e30
KV_SLAB = Q_PER_KV * HEAD_DIM
LANE_CHUNKS = SSM_WIDTH // V7X_LANES
SLOTS = 2
SLOT_COLS = SLOTS * 2 * V7X_LANES

ATTN_BLOCKS = 8
ATTN_SLOTS = 4
IN_TM = 512
FFN_TM = 1024
FFN_TF = 512
FFN_STEPS = FF_HIDDEN // FFN_TF
SSM_T = 64
SSM_ROWS = SSM_T * BATCH
SSM_ROWS2 = 2 * SSM_ROWS

f32 = jnp.float32
bf16 = jnp.bfloat16


def _dot(a, b):
    return jnp.dot(a, b, preferred_element_type=f32)


def _rms_scale(x):
    return lax.rsqrt(jnp.mean(x * x, axis=-1, keepdims=True) + EPS)


def _in_proj_kernel(x_ref, g_ref, w_ref, gqk_ref, q_ref, k_ref, v_ref, u_ref, proj_scr):
    xf = x_ref[...]
    h = (xf * _rms_scale(xf) * g_ref[...]).astype(bf16)
    proj_scr[...] = _dot(h, w_ref[...])

    low = lax.broadcasted_iota(jnp.int32, (IN_TM, V7X_LANES), 1) < HEAD_DIM

    def head_norm(b):
        cols = slice(b * V7X_LANES, (b + 1) * V7X_LANES)
        xb = proj_scr[:, cols]
        sq = xb * xb
        r_lo = lax.rsqrt(jnp.sum(jnp.where(low, sq, 0.0), axis=-1, keepdims=True) * (1.0 / HEAD_DIM) + EPS)
        r_hi = lax.rsqrt(jnp.sum(jnp.where(low, 0.0, sq), axis=-1, keepdims=True) * (1.0 / HEAD_DIM) + EPS)
        return xb * jnp.where(low, r_lo, r_hi) * gqk_ref[:, cols]

    blocks_per_kv = KV_SLAB // V7X_LANES
    qn = [head_norm(b) for b in range(ATTN_WIDTH // V7X_LANES)]
    for g in range(Q_PER_KV):
        half = slice((g % 2) * HEAD_DIM, (g % 2 + 1) * HEAD_DIM)
        q_ref[:, g * KV_WIDTH:(g + 1) * KV_WIDTH] = jnp.concatenate(
            [qn[kh * blocks_per_kv + g // 2][:, half] for kh in range(N_KV_HEADS)], axis=1).astype(bf16)
    for b in range(KV_WIDTH // V7X_LANES):
        k_ref[:, b * V7X_LANES:(b + 1) * V7X_LANES] = head_norm(ATTN_WIDTH // V7X_LANES + b).astype(bf16)
    v_ref[...] = proj_scr[:, ATTN_WIDTH + KV_WIDTH:ATTN_WIDTH + 2 * KV_WIDTH].astype(bf16)
    u_ref[...] = proj_scr[:, ATTN_WIDTH + 2 * KV_WIDTH:]


def _const_spec(shape):
    nd = len(shape)
    return pl.BlockSpec(shape, lambda *_: (0,) * nd)


def _in_proj(x, ln1_g, w, gqk):
    nt = SEQ // IN_TM
    row_spec = lambda w: pl.BlockSpec((None, IN_TM, w), lambda b, t: (b, t, 0))
    return pl.pallas_call(
        _in_proj_kernel,
        grid=(BATCH, nt),
        in_specs=[
            row_spec(D_MODEL),
            _const_spec((1, D_MODEL)),
            _const_spec((D_MODEL, IN_WIDTH)),
            _const_spec((1, ATTN_WIDTH + KV_WIDTH)),
        ],
        out_specs=[row_spec(ATTN_WIDTH), row_spec(KV_WIDTH), row_spec(KV_WIDTH), row_spec(SSM_WIDTH)],
        out_shape=[
            jax.ShapeDtypeStruct((BATCH, SEQ, ATTN_WIDTH), bf16),
            jax.ShapeDtypeStruct((BATCH, SEQ, KV_WIDTH), bf16),
            jax.ShapeDtypeStruct((BATCH, SEQ, KV_WIDTH), bf16),
            jax.ShapeDtypeStruct((BATCH, SEQ, SSM_WIDTH), f32),
        ],
        scratch_shapes=[pltpu.VMEM((IN_TM, IN_WIDTH), f32)],
        compiler_params=pltpu.CompilerParams(
            dimension_semantics=("arbitrary", "arbitrary"), vmem_limit_bytes=VMEM_LIMIT),
        name="in_proj",
    )(x, ln1_g, w, gqk)


def _attn_kernel(n_heads, sink_ref, q_ref, kc_ref, kp_ref, vc_ref, vp_ref, bias_ref, kv_mask_ref, tri_ref,
                 g_ref, w0_ref, w1_ref, w2_ref, *refs):
    wo_refs, (o_ref, c0_ref, c1_ref, c2_ref, co_ref, y_scr, s_scr, p_scr) = refs[:n_heads], refs[n_heads:]
    for w_ref, c_ref in ((w0_ref, c0_ref), (w1_ref, c1_ref), (w2_ref, c2_ref)):
        c_ref[...] = w_ref[...].astype(bf16)
    for r, w_ref in enumerate(wo_refs):
        co_ref[r * HEAD_DIM:(r + 1) * HEAD_DIM, :] = w_ref[...].astype(bf16)

    qi = lax.broadcasted_iota(jnp.int32, (BLOCK, BLOCK), 0)
    kj = lax.broadcasted_iota(jnp.int32, (BLOCK, BLOCK), 1)
    from_prev = kj > qi
    first_table = jnp.minimum(pl.program_id(1), 1)
    for j in range(ATTN_BLOCKS):
        rows = slice(j * BLOCK, (j + 1) * BLOCK)
        prev_rows = slice((j - 1) * BLOCK, j * BLOCK)
        k_prev = kp_ref[...] if j == 0 else kc_ref[prev_rows, :]
        v_prev = vp_ref[...] if j == 0 else vc_ref[prev_rows, :]
        keys = jnp.concatenate([k_prev, kc_ref[rows, :]], axis=0)
        vals = jnp.concatenate([v_prev, vc_ref[rows, :]], axis=0)
        q16 = jnp.concatenate(
            [q_ref[rows, g * KV_WIDTH:(g + 1) * KV_WIDTH] * kv_mask_ref[kh]
             for kh in range(N_KV_HEADS) for g in range(Q_PER_KV)], axis=0)
        slot = j % ATTN_SLOTS
        s_scr[slot] = lax.dot_general(q16, keys, (((1,), (1,)), ((), ())), preferred_element_type=f32)
        for head in range(N_Q_HEADS):
            head_rows = slice(head * BLOCK, (head + 1) * BLOCK)
            bias = bias_ref[first_table, head] if j == 0 else bias_ref[1, head]
            s = jnp.where(from_prev, s_scr[slot, head_rows, :BLOCK], s_scr[slot, head_rows, BLOCK:]) + bias
            sink = sink_ref[head]
            m = jnp.max(s, axis=-1, keepdims=True)
            p = jnp.exp(s - m)
            den = jnp.sum(p, axis=-1, keepdims=True) + jnp.exp(sink - m)
            w = (p * (1.0 / den)).astype(bf16)
            p_scr[slot, head_rows, :BLOCK] = w * tri_ref[0]
            p_scr[slot, head_rows, BLOCK:] = w * tri_ref[1]
        o16 = _dot(p_scr[slot], vals)
        for head in range(N_Q_HEADS):
            kh, g = divmod(head, Q_PER_KV)
            y_scr[rows, g * KV_WIDTH + kh * HEAD_DIM:g * KV_WIDTH + (kh + 1) * HEAD_DIM] = (
                o16[head * BLOCK:(head + 1) * BLOCK, kh * HEAD_DIM:(kh + 1) * HEAD_DIM])
    y = y_scr[...]
    o_ref[...] = (y * _rms_scale(y) * g_ref[...]).astype(bf16)


def _attention(sinks, q, k, v, bias2, kv_mask, tri, attn_out_g, w_gate, w_up, w_down, w_out):
    rows = ATTN_BLOCKS * BLOCK
    n_blk = SEQ // rows
    steps = BATCH * n_blk
    step = lambda b, n: b * n_blk + n

    def slab(w):
        return pl.BlockSpec((w.shape[0] // steps, w.shape[1]), lambda b, n: (step(b, n), 0))

    heads_per_step = w_out.shape[0] // HEAD_DIM // steps

    def w_out_piece(r):
        def index(b, n):
            t = step(b, n) * heads_per_step + r
            return (jnp.where(t < N_Q_HEADS, (t % N_KV_HEADS) * Q_PER_KV + t // N_KV_HEADS, t), 0)
        return pl.BlockSpec((HEAD_DIM, D_MODEL), index)

    cast_in = [slab(w_gate), slab(w_up), slab(w_down)] + [w_out_piece(r) for r in range(heads_per_step)]
    cast_out = [slab(w_gate), slab(w_up), slab(w_down), slab(w_out)]
    cast_args = [w_gate, w_up, w_down] + [w_out] * heads_per_step
    cur = pl.BlockSpec((None, rows, ATTN_WIDTH), lambda b, n: (b, n, 0))
    kv_cur = pl.BlockSpec((None, rows, KV_WIDTH), lambda b, n: (b, n, 0))
    kv_prev = pl.BlockSpec((None, BLOCK, KV_WIDTH), lambda b, n: (b, jnp.maximum(n * ATTN_BLOCKS - 1, 0), 0))
    return pl.pallas_call(
        functools.partial(_attn_kernel, heads_per_step),
        grid=(BATCH, SEQ // rows),
        in_specs=[
            pl.BlockSpec(memory_space=pltpu.SMEM),
            cur, kv_cur, kv_prev, kv_cur, kv_prev,
            _const_spec((2, N_Q_HEADS, BLOCK, BLOCK)),
            _const_spec((N_KV_HEADS, BLOCK, KV_WIDTH)),
            _const_spec((2, BLOCK, BLOCK)),
            _const_spec((1, ATTN_WIDTH)),
        ] + cast_in,
        out_specs=[cur] + cast_out,
        out_shape=[jax.ShapeDtypeStruct((BATCH, SEQ, ATTN_WIDTH), bf16)]
        + [jax.ShapeDtypeStruct(w.shape, bf16) for w in (w_gate, w_up, w_down, w_out)],
        scratch_shapes=[
            pltpu.VMEM((rows, ATTN_WIDTH), f32),
            pltpu.VMEM((ATTN_SLOTS, N_Q_HEADS * BLOCK, 2 * BLOCK), f32),
            pltpu.VMEM((ATTN_SLOTS, N_Q_HEADS * BLOCK, 2 * BLOCK), bf16),
        ],
        compiler_params=pltpu.CompilerParams(
            dimension_semantics=("arbitrary", "arbitrary"), vmem_limit_bytes=VMEM_LIMIT),
        name="attn",
    )(sinks, q, k, k, v, v, bias2, kv_mask, tri, attn_out_g, *cast_args)


def _gelu_tanh(x):
    return 0.5 * x * (1.0 + jnp.tanh(math.sqrt(2.0 / math.pi) * (x + 0.044715 * (x * x * x))))


def _sigmoid(x):
    return 1.0 / (1.0 + jnp.exp(-x))


def _ssm_kernel(u_ref, uprev_ref, wb_ref, lam_ref, wc_ref, d_ref, wglu_ref, g_ref,
                o_ref, u2_scr, bu_scr, h_scr, y2_scr, y_scr, st_scr):
    step = pl.program_id(0)

    @pl.when(step == 0)
    def _():
        st_scr[...] = jnp.zeros_like(st_scr)
        h_scr[1] = jnp.zeros(h_scr.shape[1:], f32)

    for parity in range(2):
        pl.when(step % 2 == parity)(functools.partial(
            _ssm_step, u_ref, uprev_ref, wb_ref, lam_ref, wc_ref, d_ref, wglu_ref, g_ref, o_ref,
            u2_scr, bu_scr, h_scr.at[parity], h_scr.at[1 - parity], y2_scr, y_scr, st_scr))


def _ssm_step(u_ref, uprev_ref, wb_ref, lam_ref, wc_ref, d_ref, wglu_ref, g_ref, o_ref,
              u2_scr, bu_scr, h_new, h_old, y2_scr, y_scr, st_scr):
    for c in range(LANE_CHUNKS):
        for b in range(BATCH):
            for par in range(2):
                u2_scr.at[c][pl.ds(2 * b + par, SSM_T, stride=V7X_SUBLANES), :] = (
                    u_ref[b, :, c * V7X_LANES:(c + 1) * V7X_LANES])
    row_par = lax.broadcasted_iota(jnp.int32, (SSM_ROWS2, V7X_LANES), 0) % 2
    pair_par = (lax.broadcasted_iota(jnp.int32, (SSM_ROWS2, V7X_LANES), 1) // (2 * SSM_GROUP)) % 2
    own_pair = row_par == pair_par

    for c in range(LANE_CHUNKS):
        bu_scr[c] = _dot(jnp.where(own_pair, u2_scr[c], 0.0).astype(bf16), wb_ref[c])

    half = LANE_CHUNKS // 2
    for c0 in (0, half):
        chains = [(c, s) for c in range(c0, c0 + half) for s in range(SLOTS)]

        def cols(s):
            base = s * 2 * V7X_LANES
            return slice(base, base + V7X_LANES), slice(base + V7X_LANES, base + 2 * V7X_LANES)

        lams = [(lam_ref[c, :, cols(s)[0]], lam_ref[c, :, cols(s)[1]]) for c, s in chains]
        init = tuple((st_scr[c, :, cols(s)[0]], st_scr[c, :, cols(s)[1]]) for c, s in chains)

        def body(t, carry, chains=chains, lams=lams, cols=cols, h_new=h_new):
            rows = pl.ds(t * V7X_SUBLANES, V7X_SUBLANES)
            out = []
            for (c, s), (l_re, l_im), (s_re, s_im) in zip(chains, lams, carry):
                cr, ci = cols(s)
                n_re = (l_re * s_re - l_im * s_im) + bu_scr[c, rows, cr]
                n_im = (l_re * s_im + l_im * s_re) + bu_scr[c, rows, ci]
                h_new[c, rows, cr] = n_re
                h_new[c, rows, ci] = n_im
                out.append((n_re, n_im))
            return tuple(out)

        fin = init
        for t in range(SSM_T):
            fin = body(t, fin)
        for (c, s), (s_re, s_im) in zip(chains, fin):
            st_scr[c, :, cols(s)[0]] = s_re
            st_scr[c, :, cols(s)[1]] = s_im

    for c in range(LANE_CHUNKS):
        cols_c = slice(c * V7X_LANES, (c + 1) * V7X_LANES)
        y2 = _dot(h_old[c].astype(bf16), wc_ref[c])
        y2_scr[2 * c] = y2[:, :V7X_LANES]
        y2_scr[2 * c + 1] = y2[:, V7X_LANES:]
        for b in range(BATCH):
            yc = (y2_scr.at[2 * c][pl.ds(2 * b, SSM_T, stride=V7X_SUBLANES), :]
                  + y2_scr.at[2 * c + 1][pl.ds(2 * b + 1, SSM_T, stride=V7X_SUBLANES), :])
            yc = yc + d_ref[:, cols_c] * uprev_ref[b, :, cols_c]
            y_scr[b * SSM_T:(b + 1) * SSM_T, cols_c] = _gelu_tanh(yc)
    y = y_scr[...]
    out = y * _sigmoid(_dot(y.astype(bf16), wglu_ref[...]))
    out = (out * _rms_scale(out) * g_ref[...]).astype(bf16)
    o_ref[...] = out.reshape(BATCH, SSM_T, SSM_WIDTH)


def _ssm(u, wb, lam, wc, d, wglu, ssm_out_g):
    n_chunks = SEQ // SSM_T
    return pl.pallas_call(
        _ssm_kernel,
        grid=(n_chunks + 1,),
        in_specs=[
            pl.BlockSpec((BATCH, SSM_T, SSM_WIDTH), lambda i: (0, jnp.minimum(i, n_chunks - 1), 0)),
            pl.BlockSpec((BATCH, SSM_T, SSM_WIDTH), lambda i: (0, jnp.maximum(i - 1, 0), 0)),
            _const_spec((LANE_CHUNKS, V7X_LANES, SLOT_COLS)),
            _const_spec((LANE_CHUNKS, V7X_SUBLANES, SLOT_COLS)),
            _const_spec((LANE_CHUNKS, SLOT_COLS, 2 * V7X_LANES)),
            _const_spec((1, SSM_WIDTH)),
            _const_spec((SSM_WIDTH, SSM_WIDTH)),
            _const_spec((1, SSM_WIDTH)),
        ],
        out_specs=pl.BlockSpec((BATCH, SSM_T, SSM_WIDTH), lambda i: (0, jnp.maximum(i - 1, 0), 0)),
        out_shape=jax.ShapeDtypeStruct((BATCH, SEQ, SSM_WIDTH), bf16),
        scratch_shapes=[
            pltpu.VMEM((LANE_CHUNKS, SSM_ROWS2, V7X_LANES), f32),
            pltpu.VMEM((LANE_CHUNKS, SSM_ROWS2, SLOT_COLS), f32),
            pltpu.VMEM((2, LANE_CHUNKS, SSM_ROWS2, SLOT_COLS), f32),
            pltpu.VMEM((2 * LANE_CHUNKS, SSM_ROWS2, V7X_LANES), f32),
            pltpu.VMEM((SSM_ROWS, SSM_WIDTH), f32),
            pltpu.VMEM((LANE_CHUNKS, V7X_SUBLANES, SLOT_COLS), f32),
        ],
        compiler_params=pltpu.CompilerParams(
            dimension_semantics=("arbitrary",), vmem_limit_bytes=VMEM_LIMIT),
        name="ssm",
    )(u, u, wb, lam, wc, d, wglu, ssm_out_g)


def _out_proj_kernel(a_ref, s_ref, x_ref, wa_ref, ws_ref, g_ref, o_ref, h_ref):
    x1 = x_ref[...] + _dot(a_ref[...], wa_ref[...]) + _dot(s_ref[...], ws_ref[...])
    o_ref[...] = x1
    h_ref[...] = (x1 * _rms_scale(x1) * g_ref[...]).astype(bf16)


def _out_proj(a, s, x, w, ln2_g):
    nt = SEQ // IN_TM
    row_spec = lambda w: pl.BlockSpec((None, IN_TM, w), lambda b, t: (b, t, 0))
    return pl.pallas_call(
        _out_proj_kernel,
        grid=(BATCH, nt),
        in_specs=[
            row_spec(ATTN_WIDTH),
            row_spec(SSM_WIDTH),
            row_spec(D_MODEL),
            pl.BlockSpec((ATTN_WIDTH, D_MODEL), lambda b, t: (0, 0)),
            pl.BlockSpec((SSM_WIDTH, D_MODEL), lambda b, t: (1, 0)),
            _const_spec((1, D_MODEL)),
        ],
        out_specs=[row_spec(D_MODEL), row_spec(D_MODEL)],
        out_shape=[jax.ShapeDtypeStruct((BATCH, SEQ, D_MODEL), f32),
                   jax.ShapeDtypeStruct((BATCH, SEQ, D_MODEL), bf16)],
        compiler_params=pltpu.CompilerParams(
            dimension_semantics=("arbitrary", "arbitrary"), vmem_limit_bytes=VMEM_LIMIT),
        name="out_proj",
    )(a, s, x, w, w, ln2_g)


def _ffn_kernel(x_hbm, h_hbm, wg_ref, wu_ref, wd_ref, o_ref, act_scr, res_scr, h_buf, sem, h_sem):
    f = pl.program_id(1)
    rows = pl.ds(pl.multiple_of(pl.program_id(0) * FFN_TM, FFN_TM), FFN_TM)
    residual_copy = pltpu.make_async_copy(x_hbm.at[rows, :], res_scr, sem)

    tile = pl.program_id(0)
    slot = tile % 2

    def h_copy(t, s):
        src = h_hbm.at[pl.ds(pl.multiple_of(t * FFN_TM, FFN_TM), FFN_TM), :]
        return pltpu.make_async_copy(src, h_buf.at[s], h_sem.at[s])

    def activations(dst):
        h = h_buf[slot]
        gate = _dot(h, wg_ref[...])
        up = _dot(h, wu_ref[...])
        dst[...] = (gate * _sigmoid(gate) * up).astype(bf16)

    def accumulate(src):
        o_ref[...] += _dot(src[...], wd_ref[...])

    @pl.when(f == 0)
    def _():
        @pl.when(tile == 0)
        def _():
            h_copy(tile, slot).start()

        h_copy(tile, slot).wait()
        activations(act_scr.at[0])

    @pl.when(f == 1)
    def _():
        activations(act_scr.at[1])
        o_ref[...] = _dot(act_scr[0], wd_ref[...])

    @pl.when(f == FFN_STEPS // 2)
    def _():
        residual_copy.start()

        @pl.when(tile + 1 < pl.num_programs(0))
        def _():
            h_copy(tile + 1, 1 - slot).start()

    for parity in range(2):
        @pl.when((f > 1) & (f < FFN_STEPS) & (f % 2 == parity))
        def _(parity=parity):
            activations(act_scr.at[parity])
            accumulate(act_scr.at[1 - parity])

    @pl.when(f == FFN_STEPS)
    def _():
        residual_copy.wait()
        o_ref[...] += _dot(act_scr[(FFN_STEPS - 1) % 2], wd_ref[...]) + res_scr[...]


def _ffn(x, h, wg, wu, wd):
    n_rows = BATCH * SEQ
    last = FFN_STEPS - 1
    return pl.pallas_call(
        _ffn_kernel,
        grid=(n_rows // FFN_TM, FFN_STEPS + 1),
        in_specs=[
            pl.BlockSpec(memory_space=pl.ANY),
            pl.BlockSpec(memory_space=pl.ANY),
            pl.BlockSpec((D_MODEL, FFN_TF), lambda i, f: (0, jnp.minimum(f, last))),
            pl.BlockSpec((D_MODEL, FFN_TF), lambda i, f: (0, jnp.minimum(f, last))),
            pl.BlockSpec((FFN_TF, D_MODEL), lambda i, f: (jnp.maximum(f - 1, 0), 0)),
        ],
        out_specs=pl.BlockSpec((FFN_TM, D_MODEL), lambda i, f: (i, 0)),
        out_shape=jax.ShapeDtypeStruct((n_rows, D_MODEL), f32),
        scratch_shapes=[pltpu.VMEM((2, FFN_TM, FFN_TF), bf16), pltpu.VMEM((FFN_TM, D_MODEL), f32),
                        pltpu.VMEM((2, FFN_TM, D_MODEL), bf16),
                        pltpu.SemaphoreType.DMA(()), pltpu.SemaphoreType.DMA((2,))],
        compiler_params=pltpu.CompilerParams(
            dimension_semantics=("arbitrary", "arbitrary"), vmem_limit_bytes=VMEM_LIMIT),
        name="ffn",
    )(x, h, wg, wu, wd)


def _t5_bucket(dist):
    n = np.maximum(dist, 0)
    max_exact = REL_BUCKETS // 2
    nf = np.maximum(n, 1).astype(np.float32)
    large = max_exact + (np.log(nf / max_exact) / math.log(REL_MAX_DISTANCE / max_exact)
                         * (REL_BUCKETS - max_exact)).astype(np.int32)
    large = np.minimum(large, REL_BUCKETS - 1)
    return np.where(n < max_exact, n, large).astype(np.int32)


def _bias_kernel(rb_ref, bucket_ref, o_ref):
    bucket = bucket_ref[...]
    from_prev = (lax.broadcasted_iota(jnp.int32, (BLOCK, BLOCK), 1)
                 > lax.broadcasted_iota(jnp.int32, (BLOCK, BLOCK), 0))
    for head in range(N_Q_HEADS):
        acc = jnp.zeros((BLOCK, BLOCK), f32)
        for k in range(REL_BUCKETS):
            acc = jnp.where(bucket == k, rb_ref[k, head], acc)
        o_ref[1, head] = acc
        o_ref[0, head] = jnp.where(from_prev, MASK_VALUE, acc)


def _bias_tables(rel_bias):
    qi = np.arange(BLOCK)[:, None]
    kj = np.arange(BLOCK)[None, :]
    from_prev = kj > qi
    dist = np.where(from_prev, qi + BLOCK - kj, qi - kj)
    assert ((dist >= 0) & (dist < WINDOW)).all()
    return pl.pallas_call(
        _bias_kernel,
        in_specs=[
            pl.BlockSpec(memory_space=pltpu.SMEM),
            pl.BlockSpec(memory_space=pltpu.VMEM),
        ],
        out_specs=pl.BlockSpec(memory_space=pltpu.VMEM),
        out_shape=jax.ShapeDtypeStruct((2, N_Q_HEADS, BLOCK, BLOCK), f32),
        name="bias_table",
    )(rel_bias.astype(f32), jnp.asarray(_t5_bucket(dist)))


def _ssm_params(a_re, a_im, log_dt, b_re, b_im, c_re, c_im):
    dt = jnp.exp(log_dt)[:, None]
    mag = jnp.exp(a_re * dt)
    ang = a_im * dt
    lb_re, lb_im = mag * jnp.cos(ang), mag * jnp.sin(ang)
    nr, ni = lb_re - 1.0, lb_im
    den = a_re * a_re + a_im * a_im
    f_re = (nr * a_re + ni * a_im) / den
    f_im = (ni * a_re - nr * a_im) / den
    bb_re = f_re[..., None] * b_re - f_im[..., None] * b_im
    bb_im = f_re[..., None] * b_im + f_im[..., None] * b_re

    split = lambda t: t.reshape((2, LANE_CHUNKS, SLOTS, 2, 2) + t.shape[2:])
    two = np.arange(2)
    bbs = split(jnp.stack([bb_re, bb_im]))
    k = np.arange(V7X_LANES)
    k_u, k_g = k // (V7X_LANES // SLOTS), (k // SSM_GROUP) % 2
    wb_mask = ((k_u[:, None, None] == two[None, :, None]) & (k_g[:, None, None] == two[None, None, :]))
    bt = jnp.transpose(bbs, (1, 2, 3, 4, 6, 0, 5)).reshape(LANE_CHUNKS, V7X_LANES, 1, 2, 1, SSM_STATE)
    wb = (bt * wb_mask.astype(np.float32)[:, :, None, :, None]).reshape(LANE_CHUNKS, V7X_LANES, SLOT_COLS)
    cs = split(jnp.stack([c_re, -c_im]))
    r = np.arange(SLOT_COLS)
    r_s, r_g = r // (SLOT_COLS // SLOTS), (r // SSM_STATE) % 2
    wc_mask = ((two[:, None, None, None] == two[None, None, :, None])[None]
               & (r_s[:, None, None, None, None] == two[None, None, :, None, None])
               & (r_g[:, None, None, None, None] == two[None, None, None, None, :]))
    ct = jnp.transpose(cs, (1, 2, 0, 4, 6, 3, 5)).reshape(LANE_CHUNKS, SLOT_COLS, 1, 1, 2, 1, SSM_GROUP)
    wc = (ct * wc_mask.astype(np.float32)[..., None]).reshape(LANE_CHUNKS, SLOT_COLS, 2 * V7X_LANES)
    lam = jnp.transpose(split(jnp.stack([lb_re, lb_im])), (1, 3, 2, 0, 4, 5)).reshape(LANE_CHUNKS, 1, 2, SLOT_COLS)
    lam = jnp.broadcast_to(lam, (LANE_CHUNKS, BATCH, 2, SLOT_COLS)).reshape(LANE_CHUNKS, V7X_SUBLANES, SLOT_COLS)
    return wb.astype(bf16), wc.astype(bf16), lam


def _layer(x, rel_bias, ln1_g, w_in, q_norm_g, k_norm_g, attn_sinks, ssm_a_re, ssm_a_im,
           ssm_log_dt, ssm_b_re, ssm_b_im, ssm_c_re, ssm_c_im, ssm_d, w_glu,
           attn_out_g, ssm_out_g, w_out, ln2_g, w_ff_gate, w_ff_up, w_ff_down):
    row = lambda v: v.reshape(1, -1).astype(f32)

    def regroup(t, axis):
        shape = t.shape[:axis] + (N_KV_HEADS, Q_PER_KV, HEAD_DIM) + t.shape[axis + 1:]
        return jnp.swapaxes(t.reshape(shape), axis, axis + 1).reshape(t.shape)

    gqk = row(jnp.concatenate([jnp.tile(q_norm_g, N_Q_HEADS) * (HEAD_DIM ** -0.5),
                               jnp.tile(k_norm_g, N_KV_HEADS)]))
    kv_mask = jnp.asarray(np.broadcast_to(
        (np.arange(KV_WIDTH)[None, None, :] // HEAD_DIM) == np.arange(N_KV_HEADS)[:, None, None],
        (N_KV_HEADS, BLOCK, KV_WIDTH)), bf16)

    q, k, v, u = _in_proj(x, row(ln1_g), w_in.astype(bf16), gqk)

    pos = np.arange(BLOCK)
    from_prev = pos[None, :] > pos[:, None]
    tri = jnp.asarray(np.stack([from_prev, ~from_prev]), bf16)
    y_attn, wg, wu, wd, wo = _attention(
        attn_sinks.astype(f32), q, k, v, _bias_tables(rel_bias), kv_mask, tri, row(regroup(attn_out_g, 0)),
        w_ff_gate.astype(f32), w_ff_up.astype(f32), w_ff_down.astype(f32), w_out.astype(f32))

    wb, wc, lam = _ssm_params(ssm_a_re.astype(f32), ssm_a_im.astype(f32), ssm_log_dt.astype(f32),
                                     ssm_b_re.astype(f32), ssm_b_im.astype(f32),
                                     ssm_c_re.astype(f32), ssm_c_im.astype(f32))
    y_ssm = _ssm(u, wb, lam, wc, row(ssm_d), w_glu.astype(bf16), row(ssm_out_g))

    x1, h2 = _out_proj(y_attn, y_ssm, x, wo, row(ln2_g))

    out = _ffn(x1.reshape(BATCH * SEQ, D_MODEL), h2.reshape(BATCH * SEQ, D_MODEL), wg, wu, wd)
    return out.reshape(BATCH, SEQ, D_MODEL)


def kernel(x, rel_bias, ln1_g, w_in, q_norm_g, k_norm_g, attn_sinks, ssm_a_re, ssm_a_im, ssm_log_dt, ssm_b_re, ssm_b_im, ssm_c_re, ssm_c_im, ssm_d, w_glu, attn_out_g, ssm_out_g, w_out, ln2_g, w_ff_gate, w_ff_up, w_ff_down):
    for l in range(ln1_g.shape[0]):
        x = _layer(x, rel_bias, ln1_g[l], w_in[l], q_norm_g[l], k_norm_g[l], attn_sinks[l],
                   ssm_a_re[l], ssm_a_im[l], ssm_log_dt[l], ssm_b_re[l], ssm_b_im[l],
                   ssm_c_re[l], ssm_c_im[l], ssm_d[l], w_glu[l], attn_out_g[l], ssm_out_g[l],
                   w_out[l], ln2_g[l], w_ff_gate[l], w_ff_up[l], w_ff_down[l])
    return x
```
